```python
import math
import jax
import jax.numpy as jnp
from jax import lax
import numpy as np

D_MODEL = 2048
BATCH = 2
SEQ = 4096
DEPTH = 1

CTX_LEN = 256
GRID_W = 64
D_MIX = D_MODEL
D_SSD = D_MIX // 2
D_HYENA = D_MIX - D_SSD
SSD_HEAD_DIM = 64
SSD_HEADS = D_SSD // SSD_HEAD_DIM
SSD_GROUPS = 2
SSD_HPG = SSD_HEADS // SSD_GROUPS
SSD_STATE = 128
SSD_CONV = 3
SSD_CHUNK = 128
D_XBC = D_SSD + 2 * SSD_GROUPS * SSD_STATE
HY_SHORT = 3
HY_EMB = 33
HY_BANDS = (HY_EMB - 1) // 2
HY_ORDER = 64
HY_TARGET = 1e-2
HY_FAST_PCT = 0.3
HY_SLOW_PCT = 1.5
D_FF = -(-8 * D_MODEL // (3 * 256)) * 256
N_IN = D_SSD + D_XBC + 2 * SSD_HEADS + 3 * D_HYENA
RMS_EPS = 1e-6
POS_THETA = 10000.0

kernel_name = 'hymba_ssd_hyena_diffusion_block'


def rmsnorm(x, w):
    xf = x.astype(jnp.float32)
    r = lax.rsqrt(jnp.mean(xf * xf, axis=-1, keepdims=True) + RMS_EPS)
    return (xf * r).astype(x.dtype) * w


def modulate(x, w, shift, scale):
    return rmsnorm(x, w) * (1 + scale) + shift


def flip_seq(t):
    return t[:, ::-1]


def dwconv_centred(u, w, b):
    k_w = w.shape[0]
    pad = k_w // 2
    L = u.shape[1]
    up = jnp.pad(u, ((0, 0), (pad, pad), (0, 0)))
    out = up[:, 0:L] * w[0]
    for k in range(1, k_w):
        out = out + up[:, k:k + L] * w[k]
    return out + b


def sincos_pos_2d(rows, cols, dim):
    q = dim // 4
    omega = 1.0 / (POS_THETA ** (jnp.arange(q, dtype=jnp.float32) / q))
    r = jnp.arange(rows, dtype=jnp.float32)[:, None] * omega
    cc = jnp.arange(cols, dtype=jnp.float32)[:, None] * omega
    r_emb = jnp.concatenate([jnp.sin(r), jnp.cos(r)], -1)
    c_emb = jnp.concatenate([jnp.sin(cc), jnp.cos(cc)], -1)
    emb = jnp.concatenate([jnp.broadcast_to(r_emb[:, None], (rows, cols, 2 * q)),
                           jnp.broadcast_to(c_emb[None], (rows, cols, 2 * q))], -1)
    return emb.reshape(rows * cols, 4 * q)


def segsum(a):
    T = a.shape[-1]
    ar = jnp.broadcast_to(a[..., None], a.shape + (T,))
    strict = jnp.tril(jnp.ones((T, T), dtype=bool), -1)
    cs = jnp.cumsum(jnp.where(strict, ar, 0.0), axis=-2)
    return jnp.where(jnp.tril(jnp.ones((T, T), dtype=bool), 0), cs, -jnp.inf)


def ssd_chunked(x, dt, A, B, C, h0, return_y):
    nb, nl, ng, nr, hp = x.shape
    nc = nl // SSD_CHUNK
    xf = x.astype(jnp.float32).reshape(nb, nc, SSD_CHUNK, ng, nr, hp)
    dtc = dt.reshape(nb, nc, SSD_CHUNK, ng, nr)
    Bf = B.astype(jnp.float32).reshape(nb, nc, SSD_CHUNK, ng, -1)
    Cf = C.astype(jnp.float32).reshape(nb, nc, SSD_CHUNK, ng, -1)
    xdt = xf * dtc[..., None]
    a = jnp.transpose(dtc * A, (0, 3, 4, 1, 2))
    a_cs = jnp.cumsum(a, axis=-1)
    decay_states = jnp.exp(a_cs[..., -1:] - a_cs)
    states = jnp.einsum('bcsgn,bgrcs,bcsgrp->bcgrpn', Bf, decay_states, xdt)
    states = jnp.concatenate([h0[:, None].astype(jnp.float32), states], axis=1)
    chunk_a = jnp.pad(a_cs[..., -1], ((0, 0), (0, 0), (0, 0), (1, 0)))
    decay_chunk = jnp.exp(segsum(chunk_a))
    new_states = jnp.einsum('bgrzc,bcgrpn->bzgrpn', decay_chunk, states)
    final = new_states[:, -1]
    if not return_y:
        return None, final
    Lmat = jnp.exp(segsum(a))
    scores = jnp.einsum('bclgn,bcsgn->bgcls', Cf, Bf)
    y_diag = jnp.einsum('bgcls,bgrcls,bcsgrp->bclgrp', scores, Lmat, xdt)
    y_off = jnp.einsum('bclgn,bcgrpn,bgrcl->bclgrp', Cf, new_states[:, :-1], jnp.exp(a_cs))
    y = (y_diag + y_off).reshape(nb, nl, ng, nr, hp).astype(x.dtype)
    return y, final


def ssd_mixer(z, xbc, dt_raw, zc, xbc_c, dt_raw_c, p, need_ctx):
    def prep(u, dr):
        u = jax.nn.silu(dwconv_centred(u, p['ssd_conv_w'], p['ssd_conv_b']))
        nb, nl = u.shape[:2]
        xs, Bm, Cm = jnp.split(u, [D_SSD, D_SSD + SSD_GROUPS * SSD_STATE], axis=-1)
        xs = xs.reshape(nb, nl, SSD_GROUPS, SSD_HPG, SSD_HEAD_DIM)
        Bm = Bm.reshape(nb, nl, SSD_GROUPS, SSD_STATE)
        Cm = Cm.reshape(nb, nl, SSD_GROUPS, SSD_STATE)
        dt = jax.nn.softplus(dr.astype(jnp.float32).reshape(nb, nl, 2, SSD_GROUPS, SSD_HPG)
                             + p['ssd_dt_bias'].astype(jnp.float32).reshape(2, SSD_GROUPS, SSD_HPG))
        return xs, Bm, Cm, dt

    A = -jnp.exp(p['ssd_a_log'].astype(jnp.float32)).reshape(2, SSD_GROUPS, SSD_HPG)
    Dskip = p['ssd_d'].reshape(SSD_GROUPS, SSD_HPG)[..., None]
    xs, Bm, Cm, dt = prep(xbc, dt_raw)
    xc, Bc, Cc, dtc = prep(xbc_c, dt_raw_c)
    h0 = jnp.zeros((xs.shape[0], SSD_GROUPS, SSD_HPG, SSD_HEAD_DIM, SSD_STATE), jnp.float32)
    yc_f, s_f = ssd_chunked(xc, dtc[:, :, 0], A[0], Bc, Cc, h0, need_ctx)
    yc_b, s_b = ssd_chunked(flip_seq(xc), flip_seq(dtc[:, :, 1]), A[1], flip_seq(Bc), flip_seq(Cc), h0, need_ctx)
    y_f, _ = ssd_chunked(xs, dt[:, :, 0], A[0], Bm, Cm, s_f, True)
    y_b, _ = ssd_chunked(flip_seq(xs), flip_seq(dt[:, :, 1]), A[1], flip_seq(Bm), flip_seq(Cm), s_b, True)

    def finish(yf, yb, xs_, z_):
        y = (yf + flip_seq(yb) + Dskip * xs_).reshape(z_.shape)
        return rmsnorm(y * jax.nn.silu(z_), p['ssd_norm'])

    y = finish(y_f, y_b, xs, z)
    yc = finish(yc_f, yc_b, xc, zc) if need_ctx else None
    return y, yc


def hyena_filters(L, p):
    f32 = jnp.float32
    t = jnp.linspace(0.0, 1.0, L, dtype=f32)[:, None]
    w = 2.0 * math.pi * jnp.arange(L, dtype=f32)[:, None] / L
    fb = jnp.linspace(1e-4, HY_BANDS - 1, HY_BANDS, dtype=f32)[None]
    zpos = jnp.concatenate([t, jnp.cos(fb * w), -jnp.sin(fb * w)], -1)
    fr = p['hy_freq'].astype(f32)
    h = jnp.sin(fr * (zpos @ p['hy_w1'].astype(f32) + p['hy_b1'].astype(f32)))
    h = jnp.sin(fr * (h @ p['hy_w2'].astype(f32) + p['hy_b2'].astype(f32)))
    h = jnp.sin(fr * (h @ p['hy_w3'].astype(f32) + p['hy_b3'].astype(f32)))
    h = (h @ p['hy_w4'].astype(f32)).reshape(L, 2, D_HYENA)
    max_decay = math.log(HY_TARGET) / HY_FAST_PCT
    min_decay = math.log(HY_TARGET) / HY_SLOW_PCT
    deltas = jnp.abs(jnp.linspace(min_decay, max_decay, D_HYENA, dtype=f32))
    h = h * jnp.exp(-t[:, :, None] * deltas)
    return h / (jnp.sum(jnp.abs(h), axis=(0, 1), keepdims=True) + 1e-6)


def long_conv_bidir(u, h, bias):
    L, C = u.shape[1], u.shape[2]
    k = jnp.concatenate([h[:, 0], jnp.zeros((1, C), jnp.float32), h[:0:-1, 1]], axis=0)
    uf = u.astype(jnp.float32)
    U = jnp.fft.rfft(uf, n=2 * L, axis=1)
    Kf = jnp.fft.rfft(k, n=2 * L, axis=0)
    y = jnp.fft.irfft(U * Kf[None], n=2 * L, axis=1)[:, :L]
    return (y + uf * bias.astype(jnp.float32)).astype(u.dtype)


def hyena_mixer(u, p):
    L = u.shape[1]
    u = dwconv_centred(u, p['hy_conv_w'], p['hy_conv_b'])
    x0, x1, v = jnp.split(u, 3, axis=-1)
    v = long_conv_bidir(v * x1, hyena_filters(L, p), p['hy_bias'])
    return x0 * v


def token_mix(h, hc, p, need_ctx):
    cuts = [D_SSD, D_SSD + D_XBC, D_SSD + D_XBC + 2 * SSD_HEADS]
    z, xbc, dtr, hy = jnp.split(h @ p['w_in'], cuts, axis=-1)
    if need_ctx:
        zc, xbcc, dtrc, hyc = jnp.split(hc @ p['w_in'], cuts, axis=-1)
    else:
        xbcc, dtrc = jnp.split(hc @ p['w_in'][:, cuts[0]:cuts[2]], [D_XBC], axis=-1)
        zc, hyc = None, None
    y_ssd, yc_ssd = ssd_mixer(z, xbc, dtr, zc, xbcc, dtrc, p, need_ctx)
    y = jnp.concatenate([y_ssd, hyena_mixer(hy, p)], axis=-1) @ p['w_out']
    if not need_ctx:
        return y, None
    yc = jnp.concatenate([yc_ssd, hyena_mixer(hyc, p)], axis=-1) @ p['w_out']
    return y, yc


def swiglu(h, wg, wu, wd):
    return (jax.nn.silu(h @ wg) * (h @ wu)) @ wd


def setup_inputs(seed: int = 0) -> dict:
    key = jax.random.key(seed)
    ks = jax.random.split(key, 32)
    f32 = jnp.float32

    def nrm(k, shape, scale=1.0):
        return jax.random.normal(k, shape, f32) * scale

    def gain(k, shape):
        return 1.0 + 0.02 * jax.random.normal(k, shape, f32)

    dt0 = jnp.exp(jax.random.uniform(ks[13], (DEPTH, 2, SSD_HEADS), f32, math.log(1e-3), math.log(1e-1)))
    return {
        'x': nrm(ks[0], (BATCH, SEQ, D_MODEL)),
        'c': nrm(ks[1], (BATCH, D_MODEL)),
        'ctx': nrm(ks[2], (BATCH, CTX_LEN, D_MODEL)),
        'c_ctx': nrm(ks[3], (D_MODEL,)),
        'w_ada': nrm(ks[4], (DEPTH, D_MODEL, 6 * D_MODEL), D_MODEL ** -0.5),
        'b_ada': nrm(ks[5], (DEPTH, 6 * D_MODEL), 0.02),
        'norm_mix_pre': gain(ks[6], (DEPTH, D_MODEL)),
        'norm_mix_post': gain(ks[7], (DEPTH, D_MODEL)),
        'norm_ffn_pre': gain(ks[8], (DEPTH, D_MODEL)),
        'norm_ffn_post': gain(ks[9], (DEPTH, D_MODEL)),
        'w_in': nrm(ks[10], (DEPTH, D_MODEL, N_IN), D_MODEL ** -0.5),
        'ssd_conv_w': nrm(ks[11], (DEPTH, SSD_CONV, D_XBC), SSD_CONV ** -0.5),
        'ssd_conv_b': nrm(ks[12], (DEPTH, D_XBC), 0.02),
        'ssd_a_log': jnp.log(jax.random.uniform(ks[14], (DEPTH, 2, SSD_HEADS), f32, 1.0, 16.0)),
        'ssd_dt_bias': dt0 + jnp.log(-jnp.expm1(-dt0)),
        'ssd_d': gain(ks[15], (DEPTH, SSD_HEADS)),
        'ssd_norm': gain(ks[16], (DEPTH, D_SSD)),
        'hy_conv_w': nrm(ks[17], (DEPTH, HY_SHORT, 3 * D_HYENA), HY_SHORT ** -0.5),
        'hy_conv_b': nrm(ks[18], (DEPTH, 3 * D_HYENA), 0.02),
        'hy_w1': nrm(ks[19], (DEPTH, HY_EMB, HY_ORDER), HY_EMB ** -0.5),
        'hy_b1': nrm(ks[20], (DEPTH, HY_ORDER), 0.02),
        'hy_w2': nrm(ks[21], (DEPTH, HY_ORDER, HY_ORDER), HY_ORDER ** -0.5),
        'hy_b2': nrm(ks[22], (DEPTH, HY_ORDER), 0.02),
        'hy_w3': nrm(ks[23], (DEPTH, HY_ORDER, HY_ORDER), HY_ORDER ** -0.5),
        'hy_b3': nrm(ks[24], (DEPTH, HY_ORDER), 0.02),
        'hy_w4': nrm(ks[25], (DEPTH, HY_ORDER, 2 * D_HYENA), HY_ORDER ** -0.5),
        'hy_freq': gain(ks[26], (DEPTH, HY_ORDER)),
        'hy_bias': nrm(ks[27], (DEPTH, D_HYENA)),
        'w_out': nrm(ks[28], (DEPTH, D_MIX, D_MODEL), D_MIX ** -0.5),
        'w_gate': nrm(ks[29], (DEPTH, D_MODEL, D_FF), D_MODEL ** -0.5),
        'w_up': nrm(ks[30], (DEPTH, D_MODEL, D_FF), D_MODEL ** -0.5),
        'w_down': nrm(ks[31], (DEPTH, D_FF, D_MODEL), D_FF ** -0.5),
    }


def reference(x, c, ctx, c_ctx, w_ada, b_ada, norm_mix_pre, norm_mix_post, norm_ffn_pre, norm_ffn_post,
              w_in, ssd_conv_w, ssd_conv_b, ssd_a_log, ssd_dt_bias, ssd_d, ssd_norm,
              hy_conv_w, hy_conv_b, hy_w1, hy_b1, hy_w2, hy_b2, hy_w3, hy_b3, hy_w4, hy_freq, hy_bias,
              w_out, w_gate, w_up, w_down):
    L = x.shape[1]
    ROWS = L // GRID_W
    x = x + sincos_pos_2d(ROWS, GRID_W, D_MODEL).astype(x.dtype)[None]
    xc = ctx
    for layer in range(DEPTH):
        need_ctx = layer < DEPTH - 1
        p = {
            'w_in': w_in[layer], 'w_out': w_out[layer],
            'ssd_conv_w': ssd_conv_w[layer], 'ssd_conv_b': ssd_conv_b[layer],
            'ssd_a_log': ssd_a_log[layer], 'ssd_dt_bias': ssd_dt_bias[layer],
            'ssd_d': ssd_d[layer], 'ssd_norm': ssd_norm[layer],
            'hy_conv_w': hy_conv_w[layer], 'hy_conv_b': hy_conv_b[layer],
            'hy_w1': hy_w1[layer], 'hy_b1': hy_b1[layer], 'hy_w2': hy_w2[layer], 'hy_b2': hy_b2[layer],
            'hy_w3': hy_w3[layer], 'hy_b3': hy_b3[layer], 'hy_w4': hy_w4[layer],
            'hy_freq': hy_freq[layer], 'hy_bias': hy_bias[layer],
        }
        mod = jax.nn.silu(c) @ w_ada[layer] + b_ada[layer]
        sh1, sc1, g1, sh2, sc2, g2 = jnp.split(mod[:, None, :], 6, axis=-1)
        mod_c = jax.nn.silu(c_ctx) @ w_ada[layer] + b_ada[layer]
        csh1, csc1, cg1, csh2, csc2, cg2 = jnp.split(mod_c, 6, axis=-1)
        h = modulate(x, norm_mix_pre[layer], sh1, sc1)
        hc = modulate(xc, norm_mix_pre[layer], csh1, csc1)
        y, yc = token_mix(h, hc, p, need_ctx)
        x = x + g1 * rmsnorm(y, norm_mix_post[layer])
        h = modulate(x, norm_ffn_pre[layer], sh2, sc2)
        x = x + g2 * rmsnorm(swiglu(h, w_gate[layer], w_up[layer], w_down[layer]), norm_ffn_post[layer])
        if need_ctx:
            xc = xc + cg1 * rmsnorm(yc, norm_mix_post[layer])
            hc = modulate(xc, norm_ffn_pre[layer], csh2, csc2)
            xc = xc + cg2 * rmsnorm(swiglu(hc, w_gate[layer], w_up[layer], w_down[layer]), norm_ffn_post[layer])
    return x
```

```python
import functools
import math

import numpy as np
import jax
import jax.numpy as jnp
from jax import lax
from jax.experimental import pallas as pl
from jax.experimental.pallas import tpu as pltpu

F32 = jnp.float32
BF16 = jnp.bfloat16
HIGHEST = lax.Precision.HIGHEST

RMS_EPS = 1e-6
POS_THETA = 10000.0
GRID_W = 64
SSD_HEAD_DIM = 64
SSD_GROUPS = 2
SSD_STATE = 128
SSD_CHUNK = 128
HY_TARGET = 1e-2
HY_FAST_PCT = 0.3
HY_SLOW_PCT = 1.5
LANES = 128
MIB = 1024 * 1024


def _params(n_axes, vmem_mib):
    return pltpu.CompilerParams(
        dimension_semantics=("arbitrary",) * n_axes,
        vmem_limit_bytes=vmem_mib * MIB,
    )


def _silu(v):
    return v * (1.0 / (1.0 + jnp.exp(-v)))


def _softplus(v):
    return jnp.maximum(v, 0.0) + jnp.log(1.0 + jnp.exp(-jnp.abs(v)))


def _rms(v):
    return v * lax.rsqrt(jnp.mean(v * v, axis=-1, keepdims=True) + RMS_EPS)


def _ada_kernel(c_ref, w_ref, b_ref, o_ref):
    s = _silu(c_ref[...])
    o_ref[...] = jnp.dot(s, w_ref[...], precision=HIGHEST, preferred_element_type=F32) + b_ref[...]


def _ada(crows, w_ada, b_ada):
    d, n = w_ada.shape
    tn = 1024
    return pl.pallas_call(
        _ada_kernel,
        grid=(n // tn,),
        in_specs=[
            pl.BlockSpec((8, d), lambda j: (0, 0)),
            pl.BlockSpec((d, tn), lambda j: (0, j)),
            pl.BlockSpec((1, tn), lambda j: (0, j)),
        ],
        out_specs=pl.BlockSpec((8, tn), lambda j: (0, j)),
        out_shape=jax.ShapeDtypeStruct((8, n), F32),
        compiler_params=_params(1, 40),
        name="ada",
    )(crows, w_ada, b_ada.reshape(1, n))


def _inproj_kernel(x_ref, pos_ref, nw_ref, sh_ref, sc_ref, w_ref, wdt_ref, o_ref, dt_ref, h_scr):
    @pl.when(pl.program_id(1) == 0)
    def _():
        xf = x_ref[...] + pos_ref[...]
        h = _rms(xf) * nw_ref[...] * (1.0 + sc_ref[...]) + sh_ref[...]
        h_scr[...] = h.astype(BF16)
        dt_ref[...] = jnp.dot(h, wdt_ref[...], precision=HIGHEST, preferred_element_type=F32)

    o_ref[...] = jnp.dot(h_scr[...], w_ref[...], preferred_element_type=F32).astype(o_ref.dtype)


def _in_proj(x2, pos, nw, sh, sc, w, wdt, seq_len, tm, tn):
    m, d = x2.shape
    n = w.shape[1]
    ndt = wdt.shape[1]
    tiles_per_seq = seq_len // tm
    return pl.pallas_call(
        _inproj_kernel,
        grid=(m // tm, n // tn),
        in_specs=[
            pl.BlockSpec((tm, d), lambda i, j: (i, 0)),
            pl.BlockSpec((tm, d), lambda i, j: (i % tiles_per_seq, 0)),
            pl.BlockSpec((1, d), lambda i, j: (0, 0)),
            pl.BlockSpec((None, 1, d), lambda i, j: (i // tiles_per_seq, 0, 0)),
            pl.BlockSpec((None, 1, d), lambda i, j: (i // tiles_per_seq, 0, 0)),
            pl.BlockSpec((d, tn), lambda i, j: (0, j)),
            pl.BlockSpec((d, ndt), lambda i, j: (0, 0)),
        ],
        out_specs=[
            pl.BlockSpec((tm, tn), lambda i, j: (i, j)),
            pl.BlockSpec((tm, ndt), lambda i, j: (i, 0)),
        ],
        out_shape=[
            jax.ShapeDtypeStruct((m, n), BF16),
            jax.ShapeDtypeStruct((m, ndt), F32),
        ],
        scratch_shapes=[pltpu.VMEM((tm, d), BF16)],
        compiler_params=_params(2, 56),
        name="in_proj",
    )(x2, pos, nw, sh, sc, w, wdt)


def _conv3_chunk(x_ref, r0, rows, seq_len, w, b):
    cur = x_ref[r0:r0 + rows, :].astype(F32)
    tc = cur.shape[1]
    rid = lax.broadcasted_iota(jnp.int32, (rows, tc), 0)
    if r0 > 0:
        prev_row = x_ref[r0 - 16:r0, :].astype(F32)[15:16, :]
    else:
        prev_row = jnp.zeros((1, tc), F32)
    if r0 + rows < seq_len:
        next_row = x_ref[r0 + rows:r0 + rows + 16, :].astype(F32)[0:1, :]
    else:
        next_row = jnp.zeros((1, tc), F32)
    up = jnp.where(rid == 0, prev_row, pltpu.roll(cur, 1, 0))
    down = jnp.where(rid == rows - 1, next_row, pltpu.roll(cur, rows - 1, 0))
    return up * w[0:1, :] + cur * w[1:2, :] + down * w[2:3, :] + b


def _ssd_prep_kernel(x_ref, w_ref, b_ref, o_ref, *, rows):
    seq_len = x_ref.shape[0]
    w = w_ref[...]
    b = b_ref[...]
    for r0 in range(0, seq_len, rows):
        v = _conv3_chunk(x_ref, r0, rows, seq_len, w, b)
        o_ref[r0:r0 + rows, :] = _silu(v).astype(o_ref.dtype)


def _ssd_prep(proj3, col0, width, w8, b1, tc):
    nb, seq_len, _ = proj3.shape
    rows = min(512, seq_len)
    j0 = col0 // tc
    return pl.pallas_call(
        functools.partial(_ssd_prep_kernel, rows=rows),
        grid=(nb, width // tc),
        in_specs=[
            pl.BlockSpec((None, seq_len, tc), lambda b, j: (b, 0, j0 + j)),
            pl.BlockSpec((8, tc), lambda b, j: (0, j)),
            pl.BlockSpec((1, tc), lambda b, j: (0, j)),
        ],
        out_specs=pl.BlockSpec((None, seq_len, tc), lambda b, j: (b, 0, j)),
        out_shape=jax.ShapeDtypeStruct((nb, seq_len, width), BF16),
        compiler_params=_params(2, 40),
        name="ssd_prep",
    )(proj3, w8, b1)


def _hy_prep_kernel(x0_ref, x1_ref, v_ref, w0_ref, w1_ref, wv_ref, b0_ref, b1_ref, bv_ref,
                    x0c_ref, u_ref, *, rows):
    seq_len = x0_ref.shape[0]
    w0, w1, wv = w0_ref[...], w1_ref[...], wv_ref[...]
    b0, b1, bv = b0_ref[...], b1_ref[...], bv_ref[...]
    for r0 in range(0, seq_len, rows):
        x0c_ref[r0:r0 + rows, :] = _conv3_chunk(x0_ref, r0, rows, seq_len, w0, b0).astype(x0c_ref.dtype)
        x1c = _conv3_chunk(x1_ref, r0, rows, seq_len, w1, b1)
        vc = _conv3_chunk(v_ref, r0, rows, seq_len, wv, bv)
        u_ref[r0:r0 + rows, :] = (vc * x1c).astype(u_ref.dtype)


def _hy_prep(proj3, col0, dh, w8, b1, tc):
    nb, seq_len, _ = proj3.shape
    rows = min(512, seq_len)
    j0 = col0 // tc
    nj = dh // tc
    x_spec = lambda k: pl.BlockSpec((None, seq_len, tc), lambda b, j: (b, 0, j0 + k * nj + j))
    w_spec = lambda k: pl.BlockSpec((8, tc), lambda b, j: (0, k * nj + j))
    b_spec = lambda k: pl.BlockSpec((1, tc), lambda b, j: (0, k * nj + j))
    o_spec = pl.BlockSpec((None, seq_len, tc), lambda b, j: (b, 0, j))
    return pl.pallas_call(
        functools.partial(_hy_prep_kernel, rows=rows),
        grid=(nb, nj),
        in_specs=[x_spec(0), x_spec(1), x_spec(2), w_spec(0), w_spec(1), w_spec(2),
                  b_spec(0), b_spec(1), b_spec(2)],
        out_specs=[o_spec, o_spec],
        out_shape=[jax.ShapeDtypeStruct((nb, seq_len, dh), BF16)] * 2,
        compiler_params=_params(2, 48),
        name="hy_prep",
    )(proj3, proj3, proj3, w8, w8, w8, b1, b1, b1)


def _ssd_kernel(u_ref, dt_ref, uc_ref, dtc_ref, z_ref, alog_ref, dtb_ref, dsk_ref, nw_ref, e_ref,
                o_ref, h_scr, yf_scr, *, n_ctx, n_lat):
    q = SSD_CHUNK
    d_ssd = h_scr.shape[1]
    gw = d_ssd // SSD_GROUPS
    d = pl.program_id(1)
    s = pl.program_id(2)
    is_ctx = s < n_ctx
    t = jnp.maximum(s - n_ctx, 0)
    cidx = jnp.where(d == 0, t, n_lat - 1 - t)

    @pl.when(s == 0)
    def _():
        h_scr[...] = jnp.zeros_like(h_scr)

    u = jnp.where(is_ctx, uc_ref[...], u_ref[...])
    dtr = jnp.where(is_ctx, dtc_ref[...], dt_ref[...])

    dt = _softplus(dtr + dtb_ref[...])
    a = dt * (-jnp.exp(alog_ref[...]))
    row = lax.broadcasted_iota(jnp.int32, (q, q), 0)
    col = lax.broadcasted_iota(jnp.int32, (q, q), 1)
    fwd = d == 0
    tri = jnp.where(fwd, row, col) >= jnp.where(fwd, col, row)
    cum = jnp.dot(tri.astype(F32), a, precision=HIGHEST, preferred_element_type=F32)
    cum_t = cum.T
    tot = jnp.sum(a, axis=0, keepdims=True)

    stack = jnp.concatenate(
        [dt, jnp.exp(cum), jnp.exp(tot - cum), jnp.broadcast_to(jnp.exp(tot), (8, LANES))], axis=0)
    ex = jnp.dot(stack, e_ref[...], precision=HIGHEST, preferred_element_type=F32)
    dt_x = ex[0:q]
    ecum_x = ex[q:2 * q]
    edec_x = ex[2 * q:3 * q]
    etot_x = ex[3 * q:3 * q + 1]

    xs = u[:, :d_ssd].astype(F32)
    xdt = xs * dt_x
    xdt_b = xdt.astype(BF16)
    xdw_b = (xdt * edec_x).astype(BF16)
    lane_lo = lax.broadcasted_iota(jnp.int32, (q, LANES), 1) < SSD_HEAD_DIM
    heads_per_group = gw // SSD_HEAD_DIM

    y_parts = []
    for g in range(SSD_GROUPS):
        bg = u[:, d_ssd + g * SSD_STATE:d_ssd + (g + 1) * SSD_STATE]
        cg = u[:, d_ssd + (SSD_GROUPS + g) * SSD_STATE:d_ssd + (SSD_GROUPS + g + 1) * SSD_STATE]
        scores = lax.dot_general(cg, bg, (((1,), (1,)), ((), ())), preferred_element_type=F32)
        h_prev = h_scr[:, g * gw:(g + 1) * gw]
        y_off = jnp.dot(cg, h_prev.astype(BF16), preferred_element_type=F32)
        for j in range(heads_per_group // 2):
            c0 = g * gw + j * LANES
            xp = xdt_b[:, c0:c0 + LANES]
            acc = None
            for hh in range(2):
                h = g * heads_per_group + 2 * j + hh
                diff = cum[:, h:h + 1] - cum_t[h:h + 1, :]
                decay = jnp.exp(jnp.where(tri, diff, -1e30))
                m_h = (scores * decay).astype(BF16)
                x_h = jnp.where(lane_lo if hh == 0 else jnp.logical_not(lane_lo), xp, jnp.zeros_like(xp))
                part = jnp.dot(m_h, x_h, preferred_element_type=F32)
                acc = part if acc is None else acc + part
            y_parts.append(acc + y_off[:, j * LANES:(j + 1) * LANES] * ecum_x[:, c0:c0 + LANES])
        upd = lax.dot_general(bg, xdw_b[:, g * gw:(g + 1) * gw], (((0,), (0,)), ((), ())),
                              preferred_element_type=F32)
        h_scr[:, g * gw:(g + 1) * gw] = etot_x[:, g * gw:(g + 1) * gw] * h_prev + upd
    y = jnp.concatenate(y_parts, axis=1)

    @pl.when(jnp.logical_and(jnp.logical_not(is_ctx), fwd))
    def _():
        yf_scr[cidx] = y

    @pl.when(jnp.logical_and(jnp.logical_not(is_ctx), jnp.logical_not(fwd)))
    def _():
        ytot = yf_scr[cidx] + y + dsk_ref[...] * xs
        gated = ytot * _silu(z_ref[...].astype(F32))
        o_ref[...] = (_rms(gated) * nw_ref[...]).astype(o_ref.dtype)


def _ssd(u_lat, dt_lat, u_ctx, dt_ctx, proj3, alog, dtb, dsk, nw, expand):
    nb, seq_len, d_xbc = u_lat.shape
    ctx_len = u_ctx.shape[1]
    q = SSD_CHUNK
    n_lat = seq_len // q
    n_ctx = ctx_len // q
    d_ssd = dsk.shape[1]

    def lat_idx(d, s):
        t = jnp.maximum(s - n_ctx, 0)
        return jnp.where(d == 0, t, n_lat - 1 - t)

    def ctx_idx(d, s):
        t = jnp.minimum(s, n_ctx - 1)
        return jnp.where(d == 0, t, n_ctx - 1 - t)

    def out_idx(d, s):
        return jnp.where(d == 0, n_lat - 1, lat_idx(d, s))

    return pl.pallas_call(
        functools.partial(_ssd_kernel, n_ctx=n_ctx, n_lat=n_lat),
        grid=(nb, 2, n_ctx + n_lat),
        in_specs=[
            pl.BlockSpec((None, q, d_xbc), lambda b, d, s: (b, lat_idx(d, s), 0)),
            pl.BlockSpec((None, q, LANES), lambda b, d, s: (b, lat_idx(d, s), d)),
            pl.BlockSpec((None, q, d_xbc), lambda b, d, s: (b, ctx_idx(d, s), 0)),
            pl.BlockSpec((None, q, LANES), lambda b, d, s: (b, ctx_idx(d, s), d)),
            pl.BlockSpec((None, q, d_ssd), lambda b, d, s: (b, lat_idx(d, s), 0)),
            pl.BlockSpec((None, 1, LANES), lambda b, d, s: (d, 0, 0)),
            pl.BlockSpec((None, 1, LANES), lambda b, d, s: (d, 0, 0)),
            pl.BlockSpec((1, d_ssd), lambda b, d, s: (0, 0)),
            pl.BlockSpec((1, d_ssd), lambda b, d, s: (0, 0)),
            pl.BlockSpec((LANES, d_ssd), lambda b, d, s: (0, 0)),
        ],
        out_specs=pl.BlockSpec((None, q, d_ssd), lambda b, d, s: (b, out_idx(d, s), 0)),
        out_shape=jax.ShapeDtypeStruct((nb, seq_len, d_ssd), BF16),
        scratch_shapes=[
            pltpu.VMEM((SSD_STATE, d_ssd), F32),
            pltpu.VMEM((n_lat, q, d_ssd), F32),
        ],
        compiler_params=_params(3, 48),
        name="ssd",
    )(u_lat, dt_lat, u_ctx, dt_ctx, proj3, alog, dtb, dsk, nw, expand)


def _hy_mlp_kernel(zt_ref, w1_ref, w2_ref, w3_ref, b_ref, fr_ref, o_ref):
    fr = fr_ref[...]
    b = b_ref[...]
    h = jnp.sin(fr * (jnp.dot(w1_ref[...], zt_ref[...], precision=HIGHEST, preferred_element_type=F32)
                      + b[:, 0:1]))
    h = jnp.sin(fr * (jnp.dot(w2_ref[...], h, precision=HIGHEST, preferred_element_type=F32) + b[:, 1:2]))
    h = jnp.sin(fr * (jnp.dot(w3_ref[...], h, precision=HIGHEST, preferred_element_type=F32) + b[:, 2:3]))
    hid, seq_len = h.shape
    hp = jnp.concatenate([h, jnp.zeros((LANES - hid, seq_len), F32)], axis=0)
    o_ref[...] = hp.T


def _hy_mlp(zt, w1t, w2t, w3t, b3, fr):
    seq_len = zt.shape[1]
    return pl.pallas_call(
        _hy_mlp_kernel,
        out_shape=jax.ShapeDtypeStruct((seq_len, LANES), F32),
        compiler_params=pltpu.CompilerParams(vmem_limit_bytes=40 * MIB),
        name="hy_mlp",
    )(zt, w1t, w2t, w3t, b3, fr)


def _hy_filter_kernel(h_ref, wf_ref, wb_ref, dl_ref, ks_ref, kd_ref, ny_ref):
    seq_len = h_ref.shape[0]
    tc = wf_ref.shape[1]
    h3 = h_ref[...]
    rid = lax.broadcasted_iota(jnp.int32, (seq_len, tc), 0)
    decay = jnp.exp(-(rid.astype(F32) * (1.0 / (seq_len - 1))) * dl_ref[...])
    hf = jnp.dot(h3, wf_ref[...], precision=HIGHEST, preferred_element_type=F32) * decay
    hb = jnp.dot(h3, wb_ref[...], precision=HIGHEST, preferred_element_type=F32) * decay
    norm = jnp.sum(jnp.abs(hf) + jnp.abs(hb), axis=0, keepdims=True) + 1e-6
    inv = 1.0 / norm
    hf = hf * inv
    hb = jnp.where(rid == 0, 0.0, hb * inv)
    ksum = hf + hb
    ks_ref[...] = ksum.astype(ks_ref.dtype)
    kd_ref[...] = (hf - hb).astype(kd_ref.dtype)
    alt = jnp.where((rid & 1) == 0, ksum, -ksum)
    ny_ref[...] = jnp.broadcast_to(jnp.sum(alt, axis=0, keepdims=True), (8, tc))


def _hy_filters(h3, w4f, w4b, deltas, tc):
    seq_len = h3.shape[0]
    dh = w4f.shape[1]
    return pl.pallas_call(
        _hy_filter_kernel,
        grid=(dh // tc,),
        in_specs=[
            pl.BlockSpec((seq_len, LANES), lambda j: (0, 0)),
            pl.BlockSpec((LANES, tc), lambda j: (0, j)),
            pl.BlockSpec((LANES, tc), lambda j: (0, j)),
            pl.BlockSpec((1, tc), lambda j: (0, j)),
        ],
        out_specs=[
            pl.BlockSpec((seq_len, tc), lambda j: (0, j)),
            pl.BlockSpec((seq_len, tc), lambda j: (0, j)),
            pl.BlockSpec((8, tc), lambda j: (0, j)),
        ],
        out_shape=[
            jax.ShapeDtypeStruct((seq_len, dh), BF16),
            jax.ShapeDtypeStruct((seq_len, dh), BF16),
            jax.ShapeDtypeStruct((8, dh), F32),
        ],
        compiler_params=_params(1, 56),
        name="hy_filters",
    )(h3, w4f, w4b, deltas)


def _dft_filter_kernel(c_ref, s_ref, ks_ref, kd_ref, ny_ref, kp_ref, kq_ref):
    tm = c_ref.shape[0]
    kp_ref[...] = jnp.dot(c_ref[...], ks_ref[...], preferred_element_type=F32)
    kq = jnp.dot(s_ref[...], kd_ref[...], preferred_element_type=F32)
    grow = lax.broadcasted_iota(jnp.int32, kq.shape, 0) + pl.program_id(1) * tm
    kq_ref[...] = jnp.where(grow == 0, ny_ref[0:1, :], kq)


def _dft_filters(cm, sm, ksum, kdiff, knyq, tm, tn):
    seq_len, dh = ksum.shape
    return pl.pallas_call(
        _dft_filter_kernel,
        grid=(dh // tn, seq_len // tm),
        in_specs=[
            pl.BlockSpec((tm, seq_len), lambda j, i: (i, 0)),
            pl.BlockSpec((tm, seq_len), lambda j, i: (i, 0)),
            pl.BlockSpec((seq_len, tn), lambda j, i: (0, j)),
            pl.BlockSpec((seq_len, tn), lambda j, i: (0, j)),
            pl.BlockSpec((8, tn), lambda j, i: (0, j)),
        ],
        out_specs=[pl.BlockSpec((tm, tn), lambda j, i: (i, j))] * 2,
        out_shape=[jax.ShapeDtypeStruct((seq_len, dh), F32)] * 2,
        compiler_params=_params(2, 56),
        name="dft_filters",
    )(cm, sm, ksum, kdiff, knyq)


def _dft_fwd_kernel(c_ref, s_ref, u_ref, kp_ref, kq_ref, yp_ref, yq_ref, *, n_fft):
    tm = c_ref.shape[0]
    u = u_ref[...]
    p = jnp.dot(c_ref[...], u, preferred_element_type=F32)
    qv = jnp.dot(s_ref[...], u, preferred_element_type=F32)
    kp = kp_ref[...]
    kq = kq_ref[...]
    grow = lax.broadcasted_iota(jnp.int32, p.shape, 0) + pl.program_id(2) * tm
    first = grow == 0
    scale = jnp.where(first, 1.0 / n_fft, 2.0 / n_fft)
    yp = jnp.where(first, p * kp, p * kp - qv * kq)
    yq = jnp.where(first, qv * kq, p * kq + qv * kp)
    yp_ref[...] = (yp * scale).astype(yp_ref.dtype)
    yq_ref[...] = (yq * scale).astype(yq_ref.dtype)


def _dft_fwd(cm, sm, u3, kp, kq, tm, tn):
    nb, seq_len, dh = u3.shape
    return pl.pallas_call(
        functools.partial(_dft_fwd_kernel, n_fft=2 * seq_len),
        grid=(nb, dh // tn, seq_len // tm),
        in_specs=[
            pl.BlockSpec((tm, seq_len), lambda b, j, i: (i, 0)),
            pl.BlockSpec((tm, seq_len), lambda b, j, i: (i, 0)),
            pl.BlockSpec((None, seq_len, tn), lambda b, j, i: (b, 0, j)),
            pl.BlockSpec((tm, tn), lambda b, j, i: (i, j)),
            pl.BlockSpec((tm, tn), lambda b, j, i: (i, j)),
        ],
        out_specs=[pl.BlockSpec((None, tm, tn), lambda b, j, i: (b, i, j))] * 2,
        out_shape=[jax.ShapeDtypeStruct((nb, seq_len, dh), BF16)] * 2,
        compiler_params=_params(3, 56),
        name="dft_fwd",
    )(cm, sm, u3, kp, kq)


def _dft_inv_kernel(c_ref, st_ref, yp_ref, yq_ref, u_ref, x0_ref, bias_ref, o_ref):
    conv = jnp.dot(c_ref[...], yp_ref[...], preferred_element_type=F32)
    conv = conv + jnp.dot(st_ref[...], yq_ref[...], preferred_element_type=F32)
    v = conv + u_ref[...].astype(F32) * bias_ref[...]
    o_ref[...] = (x0_ref[...].astype(F32) * v).astype(o_ref.dtype)


def _dft_inv(cm, smt, yp, yq, u3, x0c, bias, tm, tn):
    nb, seq_len, dh = u3.shape
    return pl.pallas_call(
        _dft_inv_kernel,
        grid=(nb, dh // tn, seq_len // tm),
        in_specs=[
            pl.BlockSpec((tm, seq_len), lambda b, j, i: (i, 0)),
            pl.BlockSpec((tm, seq_len), lambda b, j, i: (i, 0)),
            pl.BlockSpec((None, seq_len, tn), lambda b, j, i: (b, 0, j)),
            pl.BlockSpec((None, seq_len, tn), lambda b, j, i: (b, 0, j)),
            pl.BlockSpec((None, tm, tn), lambda b, j, i: (b, i, j)),
            pl.BlockSpec((None, tm, tn), lambda b, j, i: (b, i, j)),
            pl.BlockSpec((1, tn), lambda b, j, i: (0, j)),
        ],
        out_specs=pl.BlockSpec((None, tm, tn), lambda b, j, i: (b, i, j)),
        out_shape=jax.ShapeDtypeStruct((nb, seq_len, dh), BF16),
        compiler_params=_params(3, 56),
        name="dft_inv",
    )(cm, smt, yp, yq, u3, x0c, bias)


def _dft_matrices(seq_len):
    n_fft = 2 * seq_len
    k = np.arange(seq_len, dtype=np.int64)
    ang = (2.0 * np.pi / n_fft) * ((k[:, None] * k[None, :]) % n_fft).astype(np.float64)
    cm = np.cos(ang)
    sm = np.sin(ang)
    sm[0, :] = np.where(k % 2 == 0, 1.0, -1.0)
    as_bf16 = lambda t: jnp.asarray(t.astype(np.float32)).astype(BF16)
    return as_bf16(cm), as_bf16(sm), as_bf16(np.ascontiguousarray(sm.T))


def _outproj_kernel(ys_ref, yh_ref, w_ref, x_ref, pos_ref, nwp_ref, g_ref, nwf_ref, sh_ref, sc_ref,
                    xo_ref, h_ref):
    ds = ys_ref.shape[1]
    y = jnp.dot(ys_ref[...], w_ref[0:ds, :], preferred_element_type=F32)
    y = y + jnp.dot(yh_ref[...], w_ref[ds:, :], preferred_element_type=F32)
    xn = x_ref[...] + pos_ref[...] + g_ref[...] * (_rms(y) * nwp_ref[...])
    xo_ref[...] = xn
    h_ref[...] = (_rms(xn) * nwf_ref[...] * (1.0 + sc_ref[...]) + sh_ref[...]).astype(h_ref.dtype)


def _out_proj(ys, yh, w, x2, pos, nwp, g1, nwf, sh2, sc2, seq_len, tm):
    m, d = x2.shape
    ds = ys.shape[1]
    dh = yh.shape[1]
    tiles_per_seq = seq_len // tm
    row = lambda i: (i, 0)
    fixed = lambda i: (0, 0)
    per_batch = lambda i: (i // tiles_per_seq, 0, 0)
    return pl.pallas_call(
        _outproj_kernel,
        grid=(m // tm,),
        in_specs=[
            pl.BlockSpec((tm, ds), row),
            pl.BlockSpec((tm, dh), row),
            pl.BlockSpec((ds + dh, d), fixed),
            pl.BlockSpec((tm, d), row),
            pl.BlockSpec((tm, d), lambda i: (i % tiles_per_seq, 0)),
            pl.BlockSpec((1, d), fixed),
            pl.BlockSpec((None, 1, d), per_batch),
            pl.BlockSpec((1, d), fixed),
            pl.BlockSpec((None, 1, d), per_batch),
            pl.BlockSpec((None, 1, d), per_batch),
        ],
        out_specs=[pl.BlockSpec((tm, d), row), pl.BlockSpec((tm, d), row)],
        out_shape=[jax.ShapeDtypeStruct((m, d), F32), jax.ShapeDtypeStruct((m, d), BF16)],
        compiler_params=_params(1, 56),
        name="out_proj",
    )(ys, yh, w, x2, pos, nwp, g1, nwf, sh2, sc2)


def _ffn_kernel(h_ref, wg_ref, wu_ref, wd_ref, x_ref, nw_ref, g_ref, o_ref, acc_ref):
    f = pl.program_id(1)
    h = h_ref[...]
    gate = jnp.dot(h, wg_ref[...], preferred_element_type=F32)
    up = jnp.dot(h, wu_ref[...], preferred_element_type=F32)
    act = (_silu(gate) * up).astype(BF16)
    part = jnp.dot(act, wd_ref[...], preferred_element_type=F32)

    @pl.when(f == 0)
    def _():
        acc_ref[...] = part

    @pl.when(f > 0)
    def _():
        acc_ref[...] += part

    @pl.when(f == pl.num_programs(1) - 1)
    def _():
        o_ref[...] = x_ref[...] + g_ref[...] * (_rms(acc_ref[...]) * nw_ref[...])


def _ffn(h2, wg, wu, wd, xn, nw, g2, seq_len, tm, tf):
    m, d = xn.shape
    dff = wg.shape[1]
    tiles_per_seq = seq_len // tm
    return pl.pallas_call(
        _ffn_kernel,
        grid=(m // tm, dff // tf),
        in_specs=[
            pl.BlockSpec((tm, d), lambda i, f: (i, 0)),
            pl.BlockSpec((d, tf), lambda i, f: (0, f)),
            pl.BlockSpec((d, tf), lambda i, f: (0, f)),
            pl.BlockSpec((tf, d), lambda i, f: (f, 0)),
            pl.BlockSpec((tm, d), lambda i, f: (i, 0)),
            pl.BlockSpec((1, d), lambda i, f: (0, 0)),
            pl.BlockSpec((None, 1, d), lambda i, f: (i // tiles_per_seq, 0, 0)),
        ],
        out_specs=pl.BlockSpec((tm, d), lambda i, f: (i, 0)),
        out_shape=jax.ShapeDtypeStruct((m, d), F32),
        scratch_shapes=[pltpu.VMEM((tm, d), F32)],
        compiler_params=_params(2, 56),
        name="ffn",
    )(h2, wg, wu, wd, xn, nw, g2)


def _sincos_pos_2d(rows, cols, dim):
    qd = dim // 4
    omega = 1.0 / (POS_THETA ** (jnp.arange(qd, dtype=F32) / qd))
    r = jnp.arange(rows, dtype=F32)[:, None] * omega
    cc = jnp.arange(cols, dtype=F32)[:, None] * omega
    r_emb = jnp.concatenate([jnp.sin(r), jnp.cos(r)], -1)
    c_emb = jnp.concatenate([jnp.sin(cc), jnp.cos(cc)], -1)
    emb = jnp.concatenate([jnp.broadcast_to(r_emb[:, None], (rows, cols, 2 * qd)),
                           jnp.broadcast_to(c_emb[None], (rows, cols, 2 * qd))], -1)
    return emb.reshape(rows * cols, 4 * qd)


def _filter_features_t(seq_len, n_bands):
    t = jnp.linspace(0.0, 1.0, seq_len, dtype=F32)[:, None]
    w = 2.0 * math.pi * jnp.arange(seq_len, dtype=F32)[:, None] / seq_len
    fb = jnp.linspace(1e-4, n_bands - 1, n_bands, dtype=F32)[None]
    zpos = jnp.concatenate([t, jnp.cos(fb * w), -jnp.sin(fb * w)], -1)
    emb = zpos.shape[1]
    return jnp.pad(zpos, ((0, 0), (0, LANES - emb))).T


def _pad_rows(a, rows):
    return jnp.pad(a, ((0, rows - a.shape[0]), (0, 0)))


def kernel(x, c, ctx, c_ctx, w_ada, b_ada, norm_mix_pre, norm_mix_post, norm_ffn_pre, norm_ffn_post,
           w_in, ssd_conv_w, ssd_conv_b, ssd_a_log, ssd_dt_bias, ssd_d, ssd_norm,
           hy_conv_w, hy_conv_b, hy_w1, hy_b1, hy_w2, hy_b2, hy_w3, hy_b3, hy_w4, hy_freq, hy_bias,
           w_out, w_gate, w_up, w_down):
    nb, seq_len, d = x.shape
    ctx_len = ctx.shape[1]
    assert w_ada.shape[0] == 1, "single layer"
    n_heads = ssd_d.shape[1]
    d_ssd = n_heads * SSD_HEAD_DIM
    d_xbc = d_ssd + 2 * SSD_GROUPS * SSD_STATE
    dh = hy_bias.shape[1]
    assert w_in.shape[2] == d_ssd + d_xbc + 2 * n_heads + 3 * dh
    assert n_heads <= LANES and nb + 1 <= 8
    m = nb * seq_len

    crows = _pad_rows(jnp.concatenate([c, c_ctx[None, :]], axis=0), 8)
    mod = _ada(crows, w_ada[0], b_ada[0])
    part = lambda r0, r1, k: mod[r0:r1, k * d:(k + 1) * d][:, None, :]
    sh1, sc1, g1, sh2, sc2, g2 = (part(0, nb, k) for k in range(6))
    csh1 = jnp.broadcast_to(part(nb, nb + 1, 0), (nb, 1, d))
    csc1 = jnp.broadcast_to(part(nb, nb + 1, 1), (nb, 1, d))

    wi = w_in[0]
    o_xbc = d_ssd
    o_dt = o_xbc + d_xbc
    o_hy = o_dt + 2 * n_heads
    w_main = jnp.concatenate([wi[:, :o_dt], wi[:, o_hy:]], axis=1).astype(BF16)
    w_ctx = wi[:, o_xbc:o_dt].astype(BF16)
    w_dt = jnp.concatenate([
        jnp.pad(wi[:, o_dt:o_dt + n_heads], ((0, 0), (0, LANES - n_heads))),
        jnp.pad(wi[:, o_dt + n_heads:o_hy], ((0, 0), (0, LANES - n_heads)))], axis=1)

    pos = _sincos_pos_2d(seq_len // GRID_W, GRID_W, d)
    nmp = norm_mix_pre[0][None, :]

    tm_in = min(512, seq_len)
    proj, dt_lat = _in_proj(x.reshape(m, d), pos, nmp, sh1, sc1, w_main, w_dt, seq_len, tm_in, 512)
    tm_ctx = min(256, ctx_len)
    xbc_ctx, dt_ctx = _in_proj(ctx.reshape(nb * ctx_len, d), jnp.zeros((ctx_len, d), F32), nmp, csh1, csc1,
                               w_ctx, w_dt, ctx_len, tm_ctx, 512)
    proj3 = proj.reshape(nb, seq_len, -1)

    cw8 = _pad_rows(ssd_conv_w[0], 8)
    cb1 = ssd_conv_b[0][None, :]
    u_lat = _ssd_prep(proj3, d_ssd, d_xbc, cw8, cb1, 512)
    u_ctx = _ssd_prep(xbc_ctx.reshape(nb, ctx_len, d_xbc), 0, d_xbc, cw8, cb1, 512)
    pad_heads = lambda a: jnp.pad(a, ((0, 0), (0, LANES - n_heads)))[:, None, :]
    expand = (jnp.arange(LANES)[:, None] == (jnp.arange(d_ssd)[None, :] // SSD_HEAD_DIM)).astype(F32)
    y_ssd = _ssd(u_lat, dt_lat.reshape(nb, seq_len, 2 * LANES), u_ctx, dt_ctx.reshape(nb, ctx_len, 2 * LANES),
                 proj3, pad_heads(ssd_a_log[0]), pad_heads(ssd_dt_bias[0]),
                 jnp.repeat(ssd_d[0], SSD_HEAD_DIM)[None, :], ssd_norm[0][None, :], expand)

    x0c, u_hy = _hy_prep(proj3, d_ssd + d_xbc, dh, _pad_rows(hy_conv_w[0], 8), hy_conv_b[0][None, :], 256)
    hid = hy_w2.shape[1]
    n_bands = (hy_w1.shape[1] - 1) // 2
    zt = _filter_features_t(seq_len, n_bands)
    w1t = jnp.pad(hy_w1[0].T, ((0, 0), (0, LANES - hy_w1.shape[1])))
    b3 = jnp.stack([hy_b1[0], hy_b2[0], hy_b3[0]], axis=1)
    h3 = _hy_mlp(zt, w1t, hy_w2[0].T, hy_w3[0].T, b3, hy_freq[0][:, None])
    w4 = _pad_rows(hy_w4[0], LANES)
    max_decay = math.log(HY_TARGET) / HY_FAST_PCT
    min_decay = math.log(HY_TARGET) / HY_SLOW_PCT
    deltas = jnp.abs(jnp.linspace(min_decay, max_decay, dh, dtype=F32))[None, :]
    ksum, kdiff, knyq = _hy_filters(h3, w4[:, :dh], w4[:, dh:], deltas, 256)
    cm, sm, smt = _dft_matrices(seq_len)
    tdm = min(512, seq_len)
    kp, kq = _dft_filters(cm, sm, ksum, kdiff, knyq, tdm, 512)
    yp, yq = _dft_fwd(cm, sm, u_hy, kp, kq, tdm, 512)
    y_hy = _dft_inv(cm, smt, yp, yq, u_hy, x0c, hy_bias[0][None, :], tdm, 512)

    xn, h2 = _out_proj(y_ssd.reshape(m, d_ssd), y_hy.reshape(m, dh), w_out[0].astype(BF16), x.reshape(m, d), pos,
                       norm_mix_post[0][None, :], g1, norm_ffn_pre[0][None, :], sh2, sc2, seq_len, 256)
    out = _ffn(h2, w_gate[0].astype(BF16), w_up[0].astype(BF16), w_down[0].astype(BF16), xn,
               norm_ffn_post[0][None, :], g2, seq_len, min(512, seq_len), 512)
    return out.reshape(nb, seq_len, d)
```

```python
import functools
import math

import numpy as np
import jax
import jax.numpy as jnp
from jax import lax
from jax.experimental import pallas as pl
from jax.experimental.pallas import tpu as pltpu

F32 = jnp.float32
BF16 = jnp.bfloat16
HIGHEST = lax.Precision.HIGHEST

RMS_EPS = 1e-6
POS_THETA = 10000.0
GRID_W = 64
SSD_HEAD_DIM = 64
SSD_GROUPS = 2
SSD_STATE = 128
SSD_CHUNK = 128
HY_TARGET = 1e-2
HY_FAST_PCT = 0.3
HY_SLOW_PCT = 1.5
LANES = 128
MIB = 1024 * 1024


def _params(n_axes, vmem_mib):
    return pltpu.CompilerParams(
        dimension_semantics=("arbitrary",) * n_axes,
        vmem_limit_bytes=vmem_mib * MIB,
    )


def _silu(v):
    return v * (1.0 / (1.0 + jnp.exp(-v)))


def _softplus(v):
    return jnp.maximum(v, 0.0) + jnp.log(1.0 + jnp.exp(-jnp.abs(v)))


def _rms(v):
    return v * lax.rsqrt(jnp.mean(v * v, axis=-1, keepdims=True) + RMS_EPS)


def _ada_kernel(c_ref, w_ref, b_ref, o_ref):
    s = _silu(c_ref[...])
    o_ref[...] = jnp.dot(s, w_ref[...], precision=HIGHEST, preferred_element_type=F32) + b_ref[...]


def _ada(crows, w_ada, b_ada):
    d, n = w_ada.shape
    tn = 1024
    return pl.pallas_call(
        _ada_kernel,
        grid=(n // tn,),
        in_specs=[
            pl.BlockSpec((8, d), lambda j: (0, 0)),
            pl.BlockSpec((d, tn), lambda j: (0, j)),
            pl.BlockSpec((1, tn), lambda j: (0, j)),
        ],
        out_specs=pl.BlockSpec((8, tn), lambda j: (0, j)),
        out_shape=jax.ShapeDtypeStruct((8, n), F32),
        compiler_params=_params(1, 40),
        name="ada",
    )(crows, w_ada, b_ada.reshape(1, n))


def _inproj_kernel(x_ref, pos_ref, nw_ref, sh_ref, sc_ref, w_ref, wdt_ref, o_ref, dt_ref, h_scr):
    @pl.when(pl.program_id(1) == 0)
    def _():
        xf = x_ref[...] + pos_ref[...]
        h = _rms(xf) * nw_ref[...] * (1.0 + sc_ref[...]) + sh_ref[...]
        hb = h.astype(BF16)
        h_scr[...] = hb
        dt_ref[...] = jnp.dot(hb, wdt_ref[...], preferred_element_type=F32)

    o_ref[...] = jnp.dot(h_scr[...], w_ref[...], preferred_element_type=F32).astype(o_ref.dtype)


def _in_proj(x2, pos, nw, sh, sc, w, wdt, seq_len, tm, tn):
    m, d = x2.shape
    n = w.shape[1]
    ndt = wdt.shape[1]
    tiles_per_seq = seq_len // tm
    return pl.pallas_call(
        _inproj_kernel,
        grid=(m // tm, n // tn),
        in_specs=[
            pl.BlockSpec((tm, d), lambda i, j: (i, 0)),
            pl.BlockSpec((tm, d), lambda i, j: (i % tiles_per_seq, 0)),
            pl.BlockSpec((1, d), lambda i, j: (0, 0)),
            pl.BlockSpec((None, 1, d), lambda i, j: (i // tiles_per_seq, 0, 0)),
            pl.BlockSpec((None, 1, d), lambda i, j: (i // tiles_per_seq, 0, 0)),
            pl.BlockSpec((d, tn), lambda i, j: (0, j)),
            pl.BlockSpec((d, ndt), lambda i, j: (0, 0)),
        ],
        out_specs=[
            pl.BlockSpec((tm, tn), lambda i, j: (i, j)),
            pl.BlockSpec((tm, ndt), lambda i, j: (i, 0)),
        ],
        out_shape=[
            jax.ShapeDtypeStruct((m, n), BF16),
            jax.ShapeDtypeStruct((m, ndt), F32),
        ],
        scratch_shapes=[pltpu.VMEM((tm, d), BF16)],
        compiler_params=_params(2, 56),
        name="in_proj",
    )(x2, pos, nw, sh, sc, w, wdt)


def _conv3_chunk(x_ref, r0, rows, seq_len, w, b):
    cur = x_ref[r0:r0 + rows, :].astype(F32)
    tc = cur.shape[1]
    rid = lax.broadcasted_iota(jnp.int32, (rows, tc), 0)
    if r0 > 0:
        prev_row = x_ref[r0 - 16:r0, :].astype(F32)[15:16, :]
    else:
        prev_row = jnp.zeros((1, tc), F32)
    if r0 + rows < seq_len:
        next_row = x_ref[r0 + rows:r0 + rows + 16, :].astype(F32)[0:1, :]
    else:
        next_row = jnp.zeros((1, tc), F32)
    up = jnp.where(rid == 0, prev_row, pltpu.roll(cur, 1, 0))
    down = jnp.where(rid == rows - 1, next_row, pltpu.roll(cur, rows - 1, 0))
    return up * w[0:1, :] + cur * w[1:2, :] + down * w[2:3, :] + b


def _ssd_prep_kernel(x_ref, w_ref, b_ref, o_ref, *, rows):
    seq_len = x_ref.shape[0]
    w = w_ref[...]
    b = b_ref[...]
    for r0 in range(0, seq_len, rows):
        v = _conv3_chunk(x_ref, r0, rows, seq_len, w, b)
        o_ref[r0:r0 + rows, :] = _silu(v).astype(o_ref.dtype)


def _ssd_prep(proj3, col0, width, w8, b1, tc):
    nb, seq_len, _ = proj3.shape
    rows = min(512, seq_len)
    j0 = col0 // tc
    return pl.pallas_call(
        functools.partial(_ssd_prep_kernel, rows=rows),
        grid=(nb, width // tc),
        in_specs=[
            pl.BlockSpec((None, seq_len, tc), lambda b, j: (b, 0, j0 + j)),
            pl.BlockSpec((8, tc), lambda b, j: (0, j)),
            pl.BlockSpec((1, tc), lambda b, j: (0, j)),
        ],
        out_specs=pl.BlockSpec((None, seq_len, tc), lambda b, j: (b, 0, j)),
        out_shape=jax.ShapeDtypeStruct((nb, seq_len, width), BF16),
        compiler_params=_params(2, 40),
        name="ssd_prep",
    )(proj3, w8, b1)


def _hy_prep_kernel(x0_ref, x1_ref, v_ref, w0_ref, w1_ref, wv_ref, b0_ref, b1_ref, bv_ref,
                    x0c_ref, u_ref, *, rows):
    seq_len = x0_ref.shape[0]
    w0, w1, wv = w0_ref[...], w1_ref[...], wv_ref[...]
    b0, b1, bv = b0_ref[...], b1_ref[...], bv_ref[...]
    for r0 in range(0, seq_len, rows):
        x0c_ref[r0:r0 + rows, :] = _conv3_chunk(x0_ref, r0, rows, seq_len, w0, b0).astype(x0c_ref.dtype)
        x1c = _conv3_chunk(x1_ref, r0, rows, seq_len, w1, b1)
        vc = _conv3_chunk(v_ref, r0, rows, seq_len, wv, bv)
        u_ref[r0:r0 + rows, :] = (vc * x1c).astype(u_ref.dtype)


def _hy_prep(proj3, col0, dh, w8, b1, tc):
    nb, seq_len, _ = proj3.shape
    rows = min(512, seq_len)
    j0 = col0 // tc
    nj = dh // tc
    x_spec = lambda k: pl.BlockSpec((None, seq_len, tc), lambda b, j: (b, 0, j0 + k * nj + j))
    w_spec = lambda k: pl.BlockSpec((8, tc), lambda b, j: (0, k * nj + j))
    b_spec = lambda k: pl.BlockSpec((1, tc), lambda b, j: (0, k * nj + j))
    o_spec = pl.BlockSpec((None, seq_len, tc), lambda b, j: (b, 0, j))
    return pl.pallas_call(
        functools.partial(_hy_prep_kernel, rows=rows),
        grid=(nb, nj),
        in_specs=[x_spec(0), x_spec(1), x_spec(2), w_spec(0), w_spec(1), w_spec(2),
                  b_spec(0), b_spec(1), b_spec(2)],
        out_specs=[o_spec, o_spec],
        out_shape=[jax.ShapeDtypeStruct((nb, seq_len, dh), BF16)] * 2,
        compiler_params=_params(2, 48),
        name="hy_prep",
    )(proj3, proj3, proj3, w8, w8, w8, b1, b1, b1)


def _ssd_kernel(u_ref, dt_ref, uc_ref, dtc_ref, z_ref, alog_ref, dtb_ref, dsk_ref, nw_ref, e_ref,
                o_ref, h_scr, yf_scr, *, n_ctx, n_lat):
    q = SSD_CHUNK
    d_ssd = h_scr.shape[1]
    gw = d_ssd // SSD_GROUPS
    d = pl.program_id(1)
    s = pl.program_id(2)
    is_ctx = s < n_ctx
    t = jnp.maximum(s - n_ctx, 0)
    cidx = jnp.where(d == 0, t, n_lat - 1 - t)

    @pl.when(s == 0)
    def _():
        h_scr[...] = jnp.zeros_like(h_scr)

    u = jnp.where(is_ctx, uc_ref[...], u_ref[...])
    dtr = jnp.where(is_ctx, dtc_ref[...], dt_ref[...])

    dt = _softplus(dtr + dtb_ref[...])
    a = dt * (-jnp.exp(alog_ref[...]))
    row = lax.broadcasted_iota(jnp.int32, (q, q), 0)
    col = lax.broadcasted_iota(jnp.int32, (q, q), 1)
    fwd = d == 0
    tri = jnp.where(fwd, row, col) >= jnp.where(fwd, col, row)
    cum = jnp.dot(tri.astype(F32), a, precision=HIGHEST, preferred_element_type=F32)
    cum_t = cum.T
    tot = jnp.sum(a, axis=0, keepdims=True)

    stack = jnp.concatenate(
        [dt, jnp.exp(cum), jnp.exp(tot - cum), jnp.broadcast_to(jnp.exp(tot), (8, LANES))], axis=0)
    ex = jnp.dot(stack.astype(BF16), e_ref[...], preferred_element_type=F32)
    dt_x = ex[0:q]
    ecum_x = ex[q:2 * q]
    edec_x = ex[2 * q:3 * q]
    etot_x = ex[3 * q:3 * q + 1]

    xs = u[:, :d_ssd].astype(F32)
    xdt = xs * dt_x
    xdt_b = xdt.astype(BF16)
    xdw_b = (xdt * edec_x).astype(BF16)
    lane_lo = lax.broadcasted_iota(jnp.int32, (q, LANES), 1) < SSD_HEAD_DIM
    heads_per_group = gw // SSD_HEAD_DIM

    y_parts = []
    for g in range(SSD_GROUPS):
        bg = u[:, d_ssd + g * SSD_STATE:d_ssd + (g + 1) * SSD_STATE]
        cg = u[:, d_ssd + (SSD_GROUPS + g) * SSD_STATE:d_ssd + (SSD_GROUPS + g + 1) * SSD_STATE]
        scores = lax.dot_general(cg, bg, (((1,), (1,)), ((), ())), preferred_element_type=F32)
        h_prev = h_scr[:, g * gw:(g + 1) * gw]
        y_off = jnp.dot(cg, h_prev.astype(BF16), preferred_element_type=F32)
        for j in range(heads_per_group // 2):
            c0 = g * gw + j * LANES
            xp = xdt_b[:, c0:c0 + LANES]
            acc = None
            for hh in range(2):
                h = g * heads_per_group + 2 * j + hh
                diff = cum[:, h:h + 1] - cum_t[h:h + 1, :]
                decay = jnp.exp(jnp.where(tri, diff, -1e30))
                m_h = (scores * decay).astype(BF16)
                x_h = jnp.where(lane_lo if hh == 0 else jnp.logical_not(lane_lo), xp, jnp.zeros_like(xp))
                part = jnp.dot(m_h, x_h, preferred_element_type=F32)
                acc = part if acc is None else acc + part
            y_parts.append(acc + y_off[:, j * LANES:(j + 1) * LANES] * ecum_x[:, c0:c0 + LANES])
        upd = lax.dot_general(bg, xdw_b[:, g * gw:(g + 1) * gw], (((0,), (0,)), ((), ())),
                              preferred_element_type=F32)
        h_scr[:, g * gw:(g + 1) * gw] = etot_x[:, g * gw:(g + 1) * gw] * h_prev + upd
    y = jnp.concatenate(y_parts, axis=1)

    @pl.when(jnp.logical_and(jnp.logical_not(is_ctx), fwd))
    def _():
        yf_scr[cidx] = y

    @pl.when(jnp.logical_and(jnp.logical_not(is_ctx), jnp.logical_not(fwd)))
    def _():
        ytot = yf_scr[cidx] + y + dsk_ref[...] * xs
        gated = ytot * _silu(z_ref[...].astype(F32))
        o_ref[...] = (_rms(gated) * nw_ref[...]).astype(o_ref.dtype)


def _ssd(u_lat, dt_lat, u_ctx, dt_ctx, proj3, alog, dtb, dsk, nw, expand):
    nb, seq_len, d_xbc = u_lat.shape
    ctx_len = u_ctx.shape[1]
    q = SSD_CHUNK
    n_lat = seq_len // q
    n_ctx = ctx_len // q
    d_ssd = dsk.shape[1]

    def lat_idx(d, s):
        t = jnp.maximum(s - n_ctx, 0)
        return jnp.where(d == 0, t, n_lat - 1 - t)

    def ctx_idx(d, s):
        t = jnp.minimum(s, n_ctx - 1)
        return jnp.where(d == 0, t, n_ctx - 1 - t)

    def out_idx(d, s):
        return jnp.where(d == 0, n_lat - 1, lat_idx(d, s))

    return pl.pallas_call(
        functools.partial(_ssd_kernel, n_ctx=n_ctx, n_lat=n_lat),
        grid=(nb, 2, n_ctx + n_lat),
        in_specs=[
            pl.BlockSpec((None, q, d_xbc), lambda b, d, s: (b, lat_idx(d, s), 0)),
            pl.BlockSpec((None, q, LANES), lambda b, d, s: (b, lat_idx(d, s), d)),
            pl.BlockSpec((None, q, d_xbc), lambda b, d, s: (b, ctx_idx(d, s), 0)),
            pl.BlockSpec((None, q, LANES), lambda b, d, s: (b, ctx_idx(d, s), d)),
            pl.BlockSpec((None, q, d_ssd), lambda b, d, s: (b, lat_idx(d, s), 0)),
            pl.BlockSpec((None, 1, LANES), lambda b, d, s: (d, 0, 0)),
            pl.BlockSpec((None, 1, LANES), lambda b, d, s: (d, 0, 0)),
            pl.BlockSpec((1, d_ssd), lambda b, d, s: (0, 0)),
            pl.BlockSpec((1, d_ssd), lambda b, d, s: (0, 0)),
            pl.BlockSpec((LANES, d_ssd), lambda b, d, s: (0, 0)),
        ],
        out_specs=pl.BlockSpec((None, q, d_ssd), lambda b, d, s: (b, out_idx(d, s), 0)),
        out_shape=jax.ShapeDtypeStruct((nb, seq_len, d_ssd), BF16),
        scratch_shapes=[
            pltpu.VMEM((SSD_STATE, d_ssd), F32),
            pltpu.VMEM((n_lat, q, d_ssd), F32),
        ],
        compiler_params=_params(3, 48),
        name="ssd",
    )(u_lat, dt_lat, u_ctx, dt_ctx, proj3, alog, dtb, dsk, nw, expand)


def _hy_mlp_kernel(zt_ref, w1_ref, w2_ref, w3_ref, b_ref, fr_ref, o_ref):
    fr = fr_ref[...]
    b = b_ref[...]
    h = jnp.sin(fr * (jnp.dot(w1_ref[...], zt_ref[...], precision=HIGHEST, preferred_element_type=F32)
                      + b[:, 0:1]))
    h = jnp.sin(fr * (jnp.dot(w2_ref[...], h, precision=HIGHEST, preferred_element_type=F32) + b[:, 1:2]))
    h = jnp.sin(fr * (jnp.dot(w3_ref[...], h, precision=HIGHEST, preferred_element_type=F32) + b[:, 2:3]))
    hid, seq_len = h.shape
    hp = jnp.concatenate([h, jnp.zeros((LANES - hid, seq_len), F32)], axis=0)
    o_ref[...] = hp.T


def _hy_mlp(zt, w1t, w2t, w3t, b3, fr):
    seq_len = zt.shape[1]
    return pl.pallas_call(
        _hy_mlp_kernel,
        out_shape=jax.ShapeDtypeStruct((seq_len, LANES), F32),
        compiler_params=pltpu.CompilerParams(vmem_limit_bytes=40 * MIB),
        name="hy_mlp",
    )(zt, w1t, w2t, w3t, b3, fr)


def _hy_filter_kernel(h_ref, wf_ref, wb_ref, dl_ref, ks_ref, kd_ref, ny_ref):
    seq_len = h_ref.shape[0]
    tc = wf_ref.shape[1]
    h3 = h_ref[...]
    rid = lax.broadcasted_iota(jnp.int32, (seq_len, tc), 0)
    decay = jnp.exp(-(rid.astype(F32) * (1.0 / (seq_len - 1))) * dl_ref[...])
    hf = jnp.dot(h3, wf_ref[...], precision=HIGHEST, preferred_element_type=F32) * decay
    hb = jnp.dot(h3, wb_ref[...], precision=HIGHEST, preferred_element_type=F32) * decay
    norm = jnp.sum(jnp.abs(hf) + jnp.abs(hb), axis=0, keepdims=True) + 1e-6
    inv = 1.0 / norm
    hf = hf * inv
    hb = jnp.where(rid == 0, 0.0, hb * inv)
    ksum = hf + hb
    ks_ref[...] = ksum.astype(ks_ref.dtype)
    kd_ref[...] = (hf - hb).astype(kd_ref.dtype)
    alt = jnp.where((rid & 1) == 0, ksum, -ksum)
    ny_ref[...] = jnp.broadcast_to(jnp.sum(alt, axis=0, keepdims=True), (8, tc))


def _hy_filters(h3, w4f, w4b, deltas, tc):
    seq_len = h3.shape[0]
    dh = w4f.shape[1]
    return pl.pallas_call(
        _hy_filter_kernel,
        grid=(dh // tc,),
        in_specs=[
            pl.BlockSpec((seq_len, LANES), lambda j: (0, 0)),
            pl.BlockSpec((LANES, tc), lambda j: (0, j)),
            pl.BlockSpec((LANES, tc), lambda j: (0, j)),
            pl.BlockSpec((1, tc), lambda j: (0, j)),
        ],
        out_specs=[
            pl.BlockSpec((seq_len, tc), lambda j: (0, j)),
            pl.BlockSpec((seq_len, tc), lambda j: (0, j)),
            pl.BlockSpec((8, tc), lambda j: (0, j)),
        ],
        out_shape=[
            jax.ShapeDtypeStruct((seq_len, dh), BF16),
            jax.ShapeDtypeStruct((seq_len, dh), BF16),
            jax.ShapeDtypeStruct((8, dh), F32),
        ],
        compiler_params=_params(1, 56),
        name="hy_filters",
    )(h3, w4f, w4b, deltas)


def _dft_filter_kernel(c_ref, s_ref, ks_ref, kd_ref, ny_ref, kp_ref, kq_ref):
    tm = c_ref.shape[0]
    kp_ref[...] = jnp.dot(c_ref[...], ks_ref[...], preferred_element_type=F32)
    kq = jnp.dot(s_ref[...], kd_ref[...], preferred_element_type=F32)
    grow = lax.broadcasted_iota(jnp.int32, kq.shape, 0) + pl.program_id(1) * tm
    kq_ref[...] = jnp.where(grow == 0, ny_ref[0:1, :], kq)


def _dft_filters(cm, sm, ksum, kdiff, knyq, tm, tn):
    seq_len, dh = ksum.shape
    return pl.pallas_call(
        _dft_filter_kernel,
        grid=(dh // tn, seq_len // tm),
        in_specs=[
            pl.BlockSpec((tm, seq_len), lambda j, i: (i, 0)),
            pl.BlockSpec((tm, seq_len), lambda j, i: (i, 0)),
            pl.BlockSpec((seq_len, tn), lambda j, i: (0, j)),
            pl.BlockSpec((seq_len, tn), lambda j, i: (0, j)),
            pl.BlockSpec((8, tn), lambda j, i: (0, j)),
        ],
        out_specs=[pl.BlockSpec((tm, tn), lambda j, i: (i, j))] * 2,
        out_shape=[jax.ShapeDtypeStruct((seq_len, dh), F32)] * 2,
        compiler_params=_params(2, 56),
        name="dft_filters",
    )(cm, sm, ksum, kdiff, knyq)


def _dft_fwd_kernel(c_ref, s_ref, u_ref, kp_ref, kq_ref, yp_ref, yq_ref, *, n_fft):
    tm = c_ref.shape[0]
    u = u_ref[...]
    p = jnp.dot(c_ref[...], u, preferred_element_type=F32)
    qv = jnp.dot(s_ref[...], u, preferred_element_type=F32)
    kp = kp_ref[...]
    kq = kq_ref[...]
    grow = lax.broadcasted_iota(jnp.int32, p.shape, 0) + pl.program_id(2) * tm
    first = grow == 0
    scale = jnp.where(first, 1.0 / n_fft, 2.0 / n_fft)
    yp = jnp.where(first, p * kp, p * kp - qv * kq)
    yq = jnp.where(first, qv * kq, p * kq + qv * kp)
    yp_ref[...] = (yp * scale).astype(yp_ref.dtype)
    yq_ref[...] = (yq * scale).astype(yq_ref.dtype)


def _dft_fwd(cm, sm, u3, kp, kq, tm, tn):
    nb, seq_len, dh = u3.shape
    return pl.pallas_call(
        functools.partial(_dft_fwd_kernel, n_fft=2 * seq_len),
        grid=(nb, dh // tn, seq_len // tm),
        in_specs=[
            pl.BlockSpec((tm, seq_len), lambda b, j, i: (i, 0)),
            pl.BlockSpec((tm, seq_len), lambda b, j, i: (i, 0)),
            pl.BlockSpec((None, seq_len, tn), lambda b, j, i: (b, 0, j)),
            pl.BlockSpec((tm, tn), lambda b, j, i: (i, j)),
            pl.BlockSpec((tm, tn), lambda b, j, i: (i, j)),
        ],
        out_specs=[pl.BlockSpec((None, tm, tn), lambda b, j, i: (b, i, j))] * 2,
        out_shape=[jax.ShapeDtypeStruct((nb, seq_len, dh), BF16)] * 2,
        compiler_params=_params(3, 56),
        name="dft_fwd",
    )(cm, sm, u3, kp, kq)


def _dft_inv_kernel(c_ref, st_ref, yp_ref, yq_ref, u_ref, x0_ref, bias_ref, o_ref):
    conv = jnp.dot(c_ref[...], yp_ref[...], preferred_element_type=F32)
    conv = conv + jnp.dot(st_ref[...], yq_ref[...], preferred_element_type=F32)
    v = conv + u_ref[...].astype(F32) * bias_ref[...]
    o_ref[...] = (x0_ref[...].astype(F32) * v).astype(o_ref.dtype)


def _dft_inv(cm, smt, yp, yq, u3, x0c, bias, tm, tn):
    nb, seq_len, dh = u3.shape
    return pl.pallas_call(
        _dft_inv_kernel,
        grid=(nb, dh // tn, seq_len // tm),
        in_specs=[
            pl.BlockSpec((tm, seq_len), lambda b, j, i: (i, 0)),
            pl.BlockSpec((tm, seq_len), lambda b, j, i: (i, 0)),
            pl.BlockSpec((None, seq_len, tn), lambda b, j, i: (b, 0, j)),
            pl.BlockSpec((None, seq_len, tn), lambda b, j, i: (b, 0, j)),
            pl.BlockSpec((None, tm, tn), lambda b, j, i: (b, i, j)),
            pl.BlockSpec((None, tm, tn), lambda b, j, i: (b, i, j)),
            pl.BlockSpec((1, tn), lambda b, j, i: (0, j)),
        ],
        out_specs=pl.BlockSpec((None, tm, tn), lambda b, j, i: (b, i, j)),
        out_shape=jax.ShapeDtypeStruct((nb, seq_len, dh), BF16),
        compiler_params=_params(3, 56),
        name="dft_inv",
    )(cm, smt, yp, yq, u3, x0c, bias)


def _dft_matrices(seq_len):
    n_fft = 2 * seq_len
    k = np.arange(seq_len, dtype=np.int64)
    ang = (2.0 * np.pi / n_fft) * ((k[:, None] * k[None, :]) % n_fft).astype(np.float64)
    cm = np.cos(ang)
    sm = np.sin(ang)
    sm[0, :] = np.where(k % 2 == 0, 1.0, -1.0)
    as_bf16 = lambda t: jnp.asarray(t.astype(np.float32)).astype(BF16)
    return as_bf16(cm), as_bf16(sm), as_bf16(np.ascontiguousarray(sm.T))


def _outproj_kernel(ys_ref, yh_ref, w_ref, x_ref, pos_ref, nwp_ref, g_ref, nwf_ref, sh_ref, sc_ref,
                    xo_ref, h_ref):
    ds = ys_ref.shape[1]
    y = jnp.dot(ys_ref[...], w_ref[0:ds, :], preferred_element_type=F32)
    y = y + jnp.dot(yh_ref[...], w_ref[ds:, :], preferred_element_type=F32)
    xn = x_ref[...] + pos_ref[...] + g_ref[...] * (_rms(y) * nwp_ref[...])
    xo_ref[...] = xn
    h_ref[...] = (_rms(xn) * nwf_ref[...] * (1.0 + sc_ref[...]) + sh_ref[...]).astype(h_ref.dtype)


def _out_proj(ys, yh, w, x2, pos, nwp, g1, nwf, sh2, sc2, seq_len, tm):
    m, d = x2.shape
    ds = ys.shape[1]
    dh = yh.shape[1]
    tiles_per_seq = seq_len // tm
    row = lambda i: (i, 0)
    fixed = lambda i: (0, 0)
    per_batch = lambda i: (i // tiles_per_seq, 0, 0)
    return pl.pallas_call(
        _outproj_kernel,
        grid=(m // tm,),
        in_specs=[
            pl.BlockSpec((tm, ds), row),
            pl.BlockSpec((tm, dh), row),
            pl.BlockSpec((ds + dh, d), fixed),
            pl.BlockSpec((tm, d), row),
            pl.BlockSpec((tm, d), lambda i: (i % tiles_per_seq, 0)),
            pl.BlockSpec((1, d), fixed),
            pl.BlockSpec((None, 1, d), per_batch),
            pl.BlockSpec((1, d), fixed),
            pl.BlockSpec((None, 1, d), per_batch),
            pl.BlockSpec((None, 1, d), per_batch),
        ],
        out_specs=[pl.BlockSpec((tm, d), row), pl.BlockSpec((tm, d), row)],
        out_shape=[jax.ShapeDtypeStruct((m, d), F32), jax.ShapeDtypeStruct((m, d), BF16)],
        compiler_params=_params(1, 56),
        name="out_proj",
    )(ys, yh, w, x2, pos, nwp, g1, nwf, sh2, sc2)


def _ffn_kernel(h_ref, wg_ref, wu_ref, wd_ref, x_ref, nw_ref, g_ref, o_ref, acc_ref):
    f = pl.program_id(1)
    h = h_ref[...]
    gate = jnp.dot(h, wg_ref[...], preferred_element_type=F32)
    up = jnp.dot(h, wu_ref[...], preferred_element_type=F32)
    act = (_silu(gate) * up).astype(BF16)

    @pl.when(f == 0)
    def _():
        acc_ref[...] = jnp.zeros_like(acc_ref)

    acc_ref[...] += jnp.dot(act, wd_ref[...], preferred_element_type=F32)

    @pl.when(f == pl.num_programs(1) - 1)
    def _():
        o_ref[...] = x_ref[...] + g_ref[...] * (_rms(acc_ref[...]) * nw_ref[...])


def _ffn(h2, wg, wu, wd, xn, nw, g2, seq_len, tm, tf):
    m, d = xn.shape
    dff = wg.shape[1]
    tiles_per_seq = seq_len // tm
    return pl.pallas_call(
        _ffn_kernel,
        grid=(m // tm, dff // tf),
        in_specs=[
            pl.BlockSpec((tm, d), lambda i, f: (i, 0)),
            pl.BlockSpec((d, tf), lambda i, f: (0, f)),
            pl.BlockSpec((d, tf), lambda i, f: (0, f)),
            pl.BlockSpec((tf, d), lambda i, f: (f, 0)),
            pl.BlockSpec((tm, d), lambda i, f: (i, 0)),
            pl.BlockSpec((1, d), lambda i, f: (0, 0)),
            pl.BlockSpec((None, 1, d), lambda i, f: (i // tiles_per_seq, 0, 0)),
        ],
        out_specs=pl.BlockSpec((tm, d), lambda i, f: (i, 0)),
        out_shape=jax.ShapeDtypeStruct((m, d), F32),
        scratch_shapes=[pltpu.VMEM((tm, d), F32)],
        compiler_params=_params(2, 56),
        name="ffn",
    )(h2, wg, wu, wd, xn, nw, g2)


def _sincos_pos_2d(rows, cols, dim):
    qd = dim // 4
    omega = 1.0 / (POS_THETA ** (jnp.arange(qd, dtype=F32) / qd))
    r = jnp.arange(rows, dtype=F32)[:, None] * omega
    cc = jnp.arange(cols, dtype=F32)[:, None] * omega
    r_emb = jnp.concatenate([jnp.sin(r), jnp.cos(r)], -1)
    c_emb = jnp.concatenate([jnp.sin(cc), jnp.cos(cc)], -1)
    emb = jnp.concatenate([jnp.broadcast_to(r_emb[:, None], (rows, cols, 2 * qd)),
                           jnp.broadcast_to(c_emb[None], (rows, cols, 2 * qd))], -1)
    return emb.reshape(rows * cols, 4 * qd)


def _filter_features_t(seq_len, n_bands):
    t = jnp.linspace(0.0, 1.0, seq_len, dtype=F32)[:, None]
    w = 2.0 * math.pi * jnp.arange(seq_len, dtype=F32)[:, None] / seq_len
    fb = jnp.linspace(1e-4, n_bands - 1, n_bands, dtype=F32)[None]
    zpos = jnp.concatenate([t, jnp.cos(fb * w), -jnp.sin(fb * w)], -1)
    emb = zpos.shape[1]
    return jnp.pad(zpos, ((0, 0), (0, LANES - emb))).T


def _pad_rows(a, rows):
    return jnp.pad(a, ((0, rows - a.shape[0]), (0, 0)))


def kernel(x, c, ctx, c_ctx, w_ada, b_ada, norm_mix_pre, norm_mix_post, norm_ffn_pre, norm_ffn_post,
           w_in, ssd_conv_w, ssd_conv_b, ssd_a_log, ssd_dt_bias, ssd_d, ssd_norm,
           hy_conv_w, hy_conv_b, hy_w1, hy_b1, hy_w2, hy_b2, hy_w3, hy_b3, hy_w4, hy_freq, hy_bias,
           w_out, w_gate, w_up, w_down):
    nb, seq_len, d = x.shape
    ctx_len = ctx.shape[1]
    assert w_ada.shape[0] == 1, "single layer"
    n_heads = ssd_d.shape[1]
    d_ssd = n_heads * SSD_HEAD_DIM
    d_xbc = d_ssd + 2 * SSD_GROUPS * SSD_STATE
    dh = hy_bias.shape[1]
    assert w_in.shape[2] == d_ssd + d_xbc + 2 * n_heads + 3 * dh
    assert n_heads <= LANES and nb + 1 <= 8
    m = nb * seq_len

    crows = _pad_rows(jnp.concatenate([c, c_ctx[None, :]], axis=0), 8)
    mod = _ada(crows, w_ada[0], b_ada[0])
    part = lambda r0, r1, k: mod[r0:r1, k * d:(k + 1) * d][:, None, :]
    sh1, sc1, g1, sh2, sc2, g2 = (part(0, nb, k) for k in range(6))
    csh1 = jnp.broadcast_to(part(nb, nb + 1, 0), (nb, 1, d))
    csc1 = jnp.broadcast_to(part(nb, nb + 1, 1), (nb, 1, d))

    wi = w_in[0]
    o_xbc = d_ssd
    o_dt = o_xbc + d_xbc
    o_hy = o_dt + 2 * n_heads
    w_main = jnp.concatenate([wi[:, :o_dt], wi[:, o_hy:]], axis=1).astype(BF16)
    w_ctx = wi[:, o_xbc:o_dt].astype(BF16)
    w_dt = jnp.concatenate([
        jnp.pad(wi[:, o_dt:o_dt + n_heads], ((0, 0), (0, LANES - n_heads))),
        jnp.pad(wi[:, o_dt + n_heads:o_hy], ((0, 0), (0, LANES - n_heads)))], axis=1).astype(BF16)

    pos = _sincos_pos_2d(seq_len // GRID_W, GRID_W, d)
    nmp = norm_mix_pre[0][None, :]

    tm_in = min(512, seq_len)
    proj, dt_lat = _in_proj(x.reshape(m, d), pos, nmp, sh1, sc1, w_main, w_dt, seq_len, tm_in,
                            w_main.shape[1] // 2)
    tm_ctx = min(256, ctx_len)
    xbc_ctx, dt_ctx = _in_proj(ctx.reshape(nb * ctx_len, d), jnp.zeros((ctx_len, d), F32), nmp, csh1, csc1,
                               w_ctx, w_dt, ctx_len, tm_ctx, 512)
    proj3 = proj.reshape(nb, seq_len, -1)

    cw8 = _pad_rows(ssd_conv_w[0], 8)
    cb1 = ssd_conv_b[0][None, :]
    u_lat = _ssd_prep(proj3, d_ssd, d_xbc, cw8, cb1, 512)
    u_ctx = _ssd_prep(xbc_ctx.reshape(nb, ctx_len, d_xbc), 0, d_xbc, cw8, cb1, 512)
    pad_heads = lambda a: jnp.pad(a, ((0, 0), (0, LANES - n_heads)))[:, None, :]
    expand = (jnp.arange(LANES)[:, None] == (jnp.arange(d_ssd)[None, :] // SSD_HEAD_DIM)).astype(BF16)
    y_ssd = _ssd(u_lat, dt_lat.reshape(nb, seq_len, 2 * LANES), u_ctx, dt_ctx.reshape(nb, ctx_len, 2 * LANES),
                 proj3, pad_heads(ssd_a_log[0]), pad_heads(ssd_dt_bias[0]),
                 jnp.repeat(ssd_d[0], SSD_HEAD_DIM)[None, :], ssd_norm[0][None, :], expand)

    x0c, u_hy = _hy_prep(proj3, d_ssd + d_xbc, dh, _pad_rows(hy_conv_w[0], 8), hy_conv_b[0][None, :], 256)
    hid = hy_w2.shape[1]
    n_bands = (hy_w1.shape[1] - 1) // 2
    zt = _filter_features_t(seq_len, n_bands)
    w1t = jnp.pad(hy_w1[0].T, ((0, 0), (0, LANES - hy_w1.shape[1])))
    b3 = jnp.stack([hy_b1[0], hy_b2[0], hy_b3[0]], axis=1)
    h3 = _hy_mlp(zt, w1t, hy_w2[0].T, hy_w3[0].T, b3, hy_freq[0][:, None])
    w4 = _pad_rows(hy_w4[0], LANES)
    max_decay = math.log(HY_TARGET) / HY_FAST_PCT
    min_decay = math.log(HY_TARGET) / HY_SLOW_PCT
    deltas = jnp.abs(jnp.linspace(min_decay, max_decay, dh, dtype=F32))[None, :]
    ksum, kdiff, knyq = _hy_filters(h3, w4[:, :dh], w4[:, dh:], deltas, 256)
    cm, sm, smt = _dft_matrices(seq_len)
    tdm = min(512, seq_len)
    kp, kq = _dft_filters(cm, sm, ksum, kdiff, knyq, tdm, 512)
    yp, yq = _dft_fwd(cm, sm, u_hy, kp, kq, tdm, 512)
    y_hy = _dft_inv(cm, smt, yp, yq, u_hy, x0c, hy_bias[0][None, :], tdm, 512)

    xn, h2 = _out_proj(y_ssd.reshape(m, d_ssd), y_hy.reshape(m, dh), w_out[0].astype(BF16), x.reshape(m, d), pos,
                       norm_mix_post[0][None, :], g1, norm_ffn_pre[0][None, :], sh2, sc2, seq_len,
                       min(512, seq_len))
    out = _ffn(h2, w_gate[0].astype(BF16), w_up[0].astype(BF16), w_down[0].astype(BF16), xn,
               norm_ffn_post[0][None, :], g2, seq_len, min(512, seq_len), 512)
    return out.reshape(nb, seq_len, d)
```

```python
import functools
import math

import numpy as np
import jax
import jax.numpy as jnp
from jax import lax
from jax.experimental import pallas as pl
from jax.experimental.pallas import tpu as pltpu

F32 = jnp.float32
BF16 = jnp.bfloat16
HIGHEST = lax.Precision.HIGHEST

RMS_EPS = 1e-6
POS_THETA = 10000.0
GRID_W = 64
SSD_HEAD_DIM = 64
SSD_GROUPS = 2
SSD_STATE = 128
SSD_CHUNK = 128
HY_TARGET = 1e-2
HY_FAST_PCT = 0.3
HY_SLOW_PCT = 1.5
FFT_N2 = 64
LANES = 128
MIB = 1024 * 1024
ONE_BUFFER = pl.Buffered(1)


def _params(n_axes, vmem_mib):
    return pltpu.CompilerParams(
        dimension_semantics=("arbitrary",) * n_axes,
        vmem_limit_bytes=vmem_mib * MIB,
    )


def _silu(v):
    return v * (1.0 / (1.0 + jnp.exp(-v)))


def _softplus(v):
    return jnp.maximum(v, 0.0) + jnp.log(1.0 + jnp.exp(-jnp.abs(v)))


def _rms(v):
    return v * lax.rsqrt(jnp.mean(v * v, axis=-1, keepdims=True) + RMS_EPS)


def _add_pos(x, posr, cemb):
    tm, d = x.shape
    half = d // 2
    lo = x[:, :half] + posr
    hi = (x[:, half:].reshape(tm // GRID_W, GRID_W, half) + cemb[None]).reshape(tm, half)
    return jnp.concatenate([lo, hi], axis=1)


def _ada_kernel(c_ref, w_ref, b_ref, o_ref):
    s = _silu(c_ref[...])
    o_ref[...] = jnp.dot(s, w_ref[...], precision=HIGHEST, preferred_element_type=F32) + b_ref[...]


def _ada(crows, w_ada, b_ada):
    d, n = w_ada.shape
    tn = 1024
    return pl.pallas_call(
        _ada_kernel,
        grid=(n // tn,),
        in_specs=[
            pl.BlockSpec((8, d), lambda j: (0, 0)),
            pl.BlockSpec((d, tn), lambda j: (0, j)),
            pl.BlockSpec((1, tn), lambda j: (0, j)),
        ],
        out_specs=pl.BlockSpec((8, tn), lambda j: (0, j)),
        out_shape=jax.ShapeDtypeStruct((8, n), F32),
        compiler_params=_params(1, 40),
        name="ada",
    )(crows, w_ada, b_ada.reshape(1, n))


def _inproj_kernel(x_ref, posr_ref, cemb_ref, nw_ref, sh_ref, sc_ref, w_ref, wdt_ref, o_ref, dt_ref, h_scr,
                   *, with_pos):
    @pl.when(pl.program_id(1) == 0)
    def _():
        tm = x_ref.shape[0]
        rows = min(256, tm)
        for r0 in range(0, tm, rows):
            xf = x_ref[r0:r0 + rows, :]
            if with_pos:
                xf = _add_pos(xf, posr_ref[r0:r0 + rows, :], cemb_ref[...])
            h = _rms(xf) * nw_ref[...] * (1.0 + sc_ref[...]) + sh_ref[...]
            hb = h.astype(BF16)
            h_scr[r0:r0 + rows, :] = hb
            dt_ref[r0:r0 + rows, :] = jnp.dot(hb, wdt_ref[...], preferred_element_type=F32)

    o_ref[...] = jnp.dot(h_scr[...], w_ref[...], preferred_element_type=F32).astype(o_ref.dtype)


def _in_proj(x2, posr, cemb, nw, sh, sc, w, wdt, seq_len, tm, tn, with_pos):
    m, d = x2.shape
    n = w.shape[1]
    ndt = wdt.shape[1]
    half = d // 2
    tiles_per_seq = seq_len // tm
    pos_rows = tm if with_pos else posr.shape[0]
    return pl.pallas_call(
        functools.partial(_inproj_kernel, with_pos=with_pos),
        grid=(m // tm, n // tn),
        in_specs=[
            pl.BlockSpec((tm, d), lambda i, j: (i, 0)),
            pl.BlockSpec((pos_rows, half), lambda i, j: ((i % tiles_per_seq) if with_pos else 0, 0)),
            pl.BlockSpec(cemb.shape, lambda i, j: (0, 0), pipeline_mode=ONE_BUFFER),
            pl.BlockSpec((1, d), lambda i, j: (0, 0)),
            pl.BlockSpec((None, 1, d), lambda i, j: (i // tiles_per_seq, 0, 0)),
            pl.BlockSpec((None, 1, d), lambda i, j: (i // tiles_per_seq, 0, 0)),
            pl.BlockSpec((d, tn), lambda i, j: (0, j)),
            pl.BlockSpec((d, ndt), lambda i, j: (0, 0), pipeline_mode=ONE_BUFFER),
        ],
        out_specs=[
            pl.BlockSpec((tm, tn), lambda i, j: (i, j)),
            pl.BlockSpec((tm, ndt), lambda i, j: (i, 0)),
        ],
        out_shape=[
            jax.ShapeDtypeStruct((m, n), BF16),
            jax.ShapeDtypeStruct((m, ndt), F32),
        ],
        scratch_shapes=[pltpu.VMEM((tm, d), BF16)],
        compiler_params=_params(2, 58),
        name="in_proj",
    )(x2, posr, cemb, nw, sh, sc, w, wdt)


def _conv3_chunk(x_ref, r0, rows, seq_len, w, b):
    cur = x_ref[r0:r0 + rows, :].astype(F32)
    tc = cur.shape[1]
    rid = lax.broadcasted_iota(jnp.int32, (rows, tc), 0)
    if r0 > 0:
        prev_row = x_ref[r0 - 16:r0, :].astype(F32)[15:16, :]
    else:
        prev_row = jnp.zeros((1, tc), F32)
    if r0 + rows < seq_len:
        next_row = x_ref[r0 + rows:r0 + rows + 16, :].astype(F32)[0:1, :]
    else:
        next_row = jnp.zeros((1, tc), F32)
    up = jnp.where(rid == 0, prev_row, pltpu.roll(cur, 1, 0))
    down = jnp.where(rid == rows - 1, next_row, pltpu.roll(cur, rows - 1, 0))
    return up * w[0:1, :] + cur * w[1:2, :] + down * w[2:3, :] + b


def _ssd_prep_kernel(x_ref, w_ref, b_ref, o_ref, *, rows):
    seq_len = x_ref.shape[0]
    w = w_ref[...]
    b = b_ref[...]
    for r0 in range(0, seq_len, rows):
        v = _conv3_chunk(x_ref, r0, rows, seq_len, w, b)
        o_ref[r0:r0 + rows, :] = _silu(v).astype(o_ref.dtype)


def _ssd_prep(proj3, col0, width, w8, b1, tc):
    nb, seq_len, _ = proj3.shape
    rows = min(512, seq_len)
    j0 = col0 // tc
    return pl.pallas_call(
        functools.partial(_ssd_prep_kernel, rows=rows),
        grid=(nb, width // tc),
        in_specs=[
            pl.BlockSpec((None, seq_len, tc), lambda b, j: (b, 0, j0 + j)),
            pl.BlockSpec((8, tc), lambda b, j: (0, j)),
            pl.BlockSpec((1, tc), lambda b, j: (0, j)),
        ],
        out_specs=pl.BlockSpec((None, seq_len, tc), lambda b, j: (b, 0, j)),
        out_shape=jax.ShapeDtypeStruct((nb, seq_len, width), BF16),
        compiler_params=_params(2, 40),
        name="ssd_prep",
    )(proj3, w8, b1)


def _hy_prep_kernel(x0_ref, x1_ref, v_ref, w0_ref, w1_ref, wv_ref, b0_ref, b1_ref, bv_ref,
                    x0c_ref, u_ref, *, rows):
    seq_len = x0_ref.shape[0]
    w0, w1, wv = w0_ref[...], w1_ref[...], wv_ref[...]
    b0, b1, bv = b0_ref[...], b1_ref[...], bv_ref[...]
    for r0 in range(0, seq_len, rows):
        x0c_ref[r0:r0 + rows, :] = _conv3_chunk(x0_ref, r0, rows, seq_len, w0, b0).astype(x0c_ref.dtype)
        x1c = _conv3_chunk(x1_ref, r0, rows, seq_len, w1, b1)
        vc = _conv3_chunk(v_ref, r0, rows, seq_len, wv, bv)
        u_ref[r0:r0 + rows, :] = (vc * x1c).astype(u_ref.dtype)


def _hy_prep(proj3, col0, dh, w8, b1, tc):
    nb, seq_len, _ = proj3.shape
    rows = min(512, seq_len)
    j0 = col0 // tc
    nj = dh // tc
    x_spec = lambda k: pl.BlockSpec((None, seq_len, tc), lambda b, j: (b, 0, j0 + k * nj + j))
    w_spec = lambda k: pl.BlockSpec((8, tc), lambda b, j: (0, k * nj + j))
    b_spec = lambda k: pl.BlockSpec((1, tc), lambda b, j: (0, k * nj + j))
    o_spec = pl.BlockSpec((None, seq_len, tc), lambda b, j: (b, 0, j))
    return pl.pallas_call(
        functools.partial(_hy_prep_kernel, rows=rows),
        grid=(nb, nj),
        in_specs=[x_spec(0), x_spec(1), x_spec(2), w_spec(0), w_spec(1), w_spec(2),
                  b_spec(0), b_spec(1), b_spec(2)],
        out_specs=[o_spec, o_spec],
        out_shape=[jax.ShapeDtypeStruct((nb, seq_len, dh), BF16)] * 2,
        compiler_params=_params(2, 48),
        name="hy_prep",
    )(proj3, proj3, proj3, w8, w8, w8, b1, b1, b1)


def _ssd_kernel(u_ref, dt_ref, uc_ref, dtc_ref, z_ref, alog_ref, dtb_ref, dsk_ref, nw_ref, e_ref,
                o_ref, h_scr, yf_scr, *, n_ctx, n_lat):
    q = SSD_CHUNK
    d_ssd = h_scr.shape[1]
    gw = d_ssd // SSD_GROUPS
    d = pl.program_id(1)
    s = pl.program_id(2)
    is_ctx = s < n_ctx
    t = jnp.maximum(s - n_ctx, 0)
    cidx = jnp.where(d == 0, t, n_lat - 1 - t)

    @pl.when(s == 0)
    def _():
        h_scr[...] = jnp.zeros_like(h_scr)

    u = jnp.where(is_ctx, uc_ref[...], u_ref[...])
    dtr = jnp.where(is_ctx, dtc_ref[...], dt_ref[...])

    dt = _softplus(dtr + dtb_ref[...])
    a = dt * (-jnp.exp(alog_ref[...]))
    row = lax.broadcasted_iota(jnp.int32, (q, q), 0)
    col = lax.broadcasted_iota(jnp.int32, (q, q), 1)
    fwd = d == 0
    tri = jnp.where(fwd, row, col) >= jnp.where(fwd, col, row)
    cum = jnp.dot(tri.astype(F32), a, precision=HIGHEST, preferred_element_type=F32)
    cum_t = cum.T
    tot = jnp.sum(a, axis=0, keepdims=True)

    stack = jnp.concatenate(
        [dt, jnp.exp(cum), jnp.exp(tot - cum), jnp.broadcast_to(jnp.exp(tot), (8, LANES))], axis=0)
    ex = jnp.dot(stack.astype(BF16), e_ref[...], preferred_element_type=F32)
    dt_x = ex[0:q]
    ecum_x = ex[q:2 * q]
    edec_x = ex[2 * q:3 * q]
    etot_x = ex[3 * q:3 * q + 1]

    xs = u[:, :d_ssd].astype(F32)
    xdt = xs * dt_x
    xdt_b = xdt.astype(BF16)
    xdw_b = (xdt * edec_x).astype(BF16)
    lane_lo = lax.broadcasted_iota(jnp.int32, (q, LANES), 1) < SSD_HEAD_DIM
    heads_per_group = gw // SSD_HEAD_DIM

    y_parts = []
    for g in range(SSD_GROUPS):
        bg = u[:, d_ssd + g * SSD_STATE:d_ssd + (g + 1) * SSD_STATE]
        cg = u[:, d_ssd + (SSD_GROUPS + g) * SSD_STATE:d_ssd + (SSD_GROUPS + g + 1) * SSD_STATE]
        scores = lax.dot_general(cg, bg, (((1,), (1,)), ((), ())), preferred_element_type=F32)
        h_prev = h_scr[:, g * gw:(g + 1) * gw]
        y_off = jnp.dot(cg, h_prev.astype(BF16), preferred_element_type=F32)
        for j in range(heads_per_group // 2):
            c0 = g * gw + j * LANES
            xp = xdt_b[:, c0:c0 + LANES]
            acc = None
            for hh in range(2):
                h = g * heads_per_group + 2 * j + hh
                diff = cum[:, h:h + 1] - cum_t[h:h + 1, :]
                decay = jnp.exp(jnp.where(tri, diff, -1e30))
                m_h = (scores * decay).astype(BF16)
                x_h = jnp.where(lane_lo if hh == 0 else jnp.logical_not(lane_lo), xp, jnp.zeros_like(xp))
                part = jnp.dot(m_h, x_h, preferred_element_type=F32)
                acc = part if acc is None else acc + part
            y_parts.append(acc + y_off[:, j * LANES:(j + 1) * LANES] * ecum_x[:, c0:c0 + LANES])
        upd = lax.dot_general(bg, xdw_b[:, g * gw:(g + 1) * gw], (((0,), (0,)), ((), ())),
                              preferred_element_type=F32)
        h_scr[:, g * gw:(g + 1) * gw] = etot_x[:, g * gw:(g + 1) * gw] * h_prev + upd
    y = jnp.concatenate(y_parts, axis=1)

    @pl.when(jnp.logical_and(jnp.logical_not(is_ctx), fwd))
    def _():
        yf_scr[cidx] = y

    @pl.when(jnp.logical_and(jnp.logical_not(is_ctx), jnp.logical_not(fwd)))
    def _():
        ytot = yf_scr[cidx] + y + dsk_ref[...] * xs
        gated = ytot * _silu(z_ref[...].astype(F32))
        o_ref[...] = (_rms(gated) * nw_ref[...]).astype(o_ref.dtype)


def _ssd(u_lat, dt_lat, u_ctx, dt_ctx, proj3, alog, dtb, dsk, nw, expand):
    nb, seq_len, d_xbc = u_lat.shape
    ctx_len = u_ctx.shape[1]
    q = SSD_CHUNK
    n_lat = seq_len // q
    n_ctx = ctx_len // q
    d_ssd = dsk.shape[1]

    def lat_idx(d, s):
        t = jnp.maximum(s - n_ctx, 0)
        return jnp.where(d == 0, t, n_lat - 1 - t)

    def ctx_idx(d, s):
        t = jnp.minimum(s, n_ctx - 1)
        return jnp.where(d == 0, t, n_ctx - 1 - t)

    def out_idx(d, s):
        return jnp.where(d == 0, n_lat - 1, lat_idx(d, s))

    return pl.pallas_call(
        functools.partial(_ssd_kernel, n_ctx=n_ctx, n_lat=n_lat),
        grid=(nb, 2, n_ctx + n_lat),
        in_specs=[
            pl.BlockSpec((None, q, d_xbc), lambda b, d, s: (b, lat_idx(d, s), 0)),
            pl.BlockSpec((None, q, LANES), lambda b, d, s: (b, lat_idx(d, s), d)),
            pl.BlockSpec((None, q, d_xbc), lambda b, d, s: (b, ctx_idx(d, s), 0)),
            pl.BlockSpec((None, q, LANES), lambda b, d, s: (b, ctx_idx(d, s), d)),
            pl.BlockSpec((None, q, d_ssd), lambda b, d, s: (b, lat_idx(d, s), 0)),
            pl.BlockSpec((None, 1, LANES), lambda b, d, s: (d, 0, 0)),
            pl.BlockSpec((None, 1, LANES), lambda b, d, s: (d, 0, 0)),
            pl.BlockSpec((1, d_ssd), lambda b, d, s: (0, 0)),
            pl.BlockSpec((1, d_ssd), lambda b, d, s: (0, 0)),
            pl.BlockSpec((LANES, d_ssd), lambda b, d, s: (0, 0)),
        ],
        out_specs=pl.BlockSpec((None, q, d_ssd), lambda b, d, s: (b, out_idx(d, s), 0)),
        out_shape=jax.ShapeDtypeStruct((nb, seq_len, d_ssd), BF16),
        scratch_shapes=[
            pltpu.VMEM((SSD_STATE, d_ssd), F32),
            pltpu.VMEM((n_lat, q, d_ssd), F32),
        ],
        compiler_params=_params(3, 48),
        name="ssd",
    )(u_lat, dt_lat, u_ctx, dt_ctx, proj3, alog, dtb, dsk, nw, expand)


def _hy_mlp_kernel(zt_ref, w1_ref, w2_ref, w3_ref, b_ref, fr_ref, o_ref):
    fr = fr_ref[...]
    b = b_ref[...]
    h = jnp.sin(fr * (jnp.dot(w1_ref[...], zt_ref[...], precision=HIGHEST, preferred_element_type=F32)
                      + b[:, 0:1]))
    h = jnp.sin(fr * (jnp.dot(w2_ref[...], h, precision=HIGHEST, preferred_element_type=F32) + b[:, 1:2]))
    h = jnp.sin(fr * (jnp.dot(w3_ref[...], h, precision=HIGHEST, preferred_element_type=F32) + b[:, 2:3]))
    hid, seq_len = h.shape
    hp = jnp.concatenate([h, jnp.zeros((LANES - hid, seq_len), F32)], axis=0)
    o_ref[...] = hp.T


def _hy_mlp(zt, w1t, w2t, w3t, b3, fr):
    seq_len = zt.shape[1]
    return pl.pallas_call(
        _hy_mlp_kernel,
        out_shape=jax.ShapeDtypeStruct((seq_len, LANES), F32),
        compiler_params=pltpu.CompilerParams(vmem_limit_bytes=40 * MIB),
        name="hy_mlp",
    )(zt, w1t, w2t, w3t, b3, fr)


def _hy_filter_kernel(h_ref, wf_ref, wb_ref, dl_ref, k_ref):
    seq_len = h_ref.shape[0]
    tc = wf_ref.shape[1]
    h3 = h_ref[...]
    rid = lax.broadcasted_iota(jnp.int32, (seq_len, tc), 0)
    decay = jnp.exp(-(rid.astype(F32) * (1.0 / (seq_len - 1))) * dl_ref[...])
    hf = jnp.dot(h3, wf_ref[...], precision=HIGHEST, preferred_element_type=F32) * decay
    hb = jnp.dot(h3, wb_ref[...], precision=HIGHEST, preferred_element_type=F32) * decay
    norm = jnp.sum(jnp.abs(hf) + jnp.abs(hb), axis=0, keepdims=True) + 1e-6
    inv = 1.0 / norm
    k_ref[0] = (hf * inv).astype(k_ref.dtype)
    k_ref[1] = jnp.where(rid == 0, 0.0, hb * inv).astype(k_ref.dtype)


def _hy_filters(h3, w4f, w4b, deltas, tc):
    seq_len = h3.shape[0]
    dh = w4f.shape[1]
    return pl.pallas_call(
        _hy_filter_kernel,
        grid=(dh // tc,),
        in_specs=[
            pl.BlockSpec((seq_len, LANES), lambda j: (0, 0)),
            pl.BlockSpec((LANES, tc), lambda j: (0, j)),
            pl.BlockSpec((LANES, tc), lambda j: (0, j)),
            pl.BlockSpec((1, tc), lambda j: (0, j)),
        ],
        out_specs=pl.BlockSpec((2, seq_len, tc), lambda j: (0, 0, j)),
        out_shape=jax.ShapeDtypeStruct((2, seq_len, dh), BF16),
        compiler_params=_params(1, 56),
        name="hy_filters",
    )(h3, w4f, w4b, deltas)


def _fft_tables(seq_len):
    n_fft = 2 * seq_len
    n2 = FFT_N2
    n1 = n_fft // n2
    n1h = n1 // 2
    k1n = n1h + 1
    k1p = -(-k1n // 4) * 4
    k1 = np.arange(k1n, dtype=np.int64)
    th = (2.0 * np.pi / n1) * ((k1[:, None] * np.arange(n1h, dtype=np.int64)[None, :]) % n1)
    f1 = np.zeros((2 * k1p, n1h))
    f1[0:2 * k1n:2] = np.cos(th)
    f1[1:2 * k1n:2] = -np.sin(th)
    idx = np.arange(n2, dtype=np.int64)
    kk = k1[:, None, None] + n1 * idx[None, :, None]
    ph = (2.0 * np.pi / n_fft) * ((kk * idx[None, None, :]) % n_fft)
    g_re, g_im = np.cos(ph), -np.sin(ph)

    def blocks(re, im):
        out = np.zeros((k1p, 2 * n2, 2 * n2))
        out[:k1n, :n2, :n2] = re
        out[:k1n, :n2, n2:] = -im
        out[:k1n, n2:, :n2] = im
        out[:k1n, n2:, n2:] = re
        return out

    wgt = np.where((k1 == 0) | (k1 == n1h), 1.0, 2.0) / n_fft
    gf = blocks(g_re, g_im)
    gi = blocks(np.transpose(g_re, (0, 2, 1)) * wgt[:, None, None],
                -np.transpose(g_im, (0, 2, 1)) * wgt[:, None, None])
    as_bf16 = lambda t: jnp.asarray(t.astype(np.float32)).astype(BF16)
    return as_bf16(f1), as_bf16(np.ascontiguousarray(f1.T)), as_bf16(gf), as_bf16(gi), k1p


def _fft1_kernel(f_ref, x_ref, o_ref):
    o_ref[...] = jnp.dot(f_ref[...], x_ref[...], preferred_element_type=F32).astype(o_ref.dtype)


def _fft1(f1, x3, tcol):
    ns, n1h, cols = x3.shape
    rows = f1.shape[0]
    return pl.pallas_call(
        _fft1_kernel,
        grid=(ns, cols // tcol),
        in_specs=[
            pl.BlockSpec((rows, n1h), lambda s, j: (0, 0)),
            pl.BlockSpec((None, n1h, tcol), lambda s, j: (s, 0, j)),
        ],
        out_specs=pl.BlockSpec((None, rows, tcol), lambda s, j: (s, 0, j)),
        out_shape=jax.ShapeDtypeStruct((ns, rows, cols), BF16),
        compiler_params=_params(2, 40),
        name="fft1",
    )(f1, x3)


def _fft2_filter_kernel(af_ref, ab_ref, g_ref, k_ref, *, kb):
    n2 = FFT_N2
    for t in range(kb):
        xf = jnp.dot(g_ref[t], af_ref[t], preferred_element_type=F32)
        xb = jnp.dot(g_ref[t], ab_ref[t], preferred_element_type=F32)
        k_ref[t] = jnp.concatenate([xf[:n2] + xb[:n2], xf[n2:] - xb[n2:]], axis=0)


def _fft2_filter(a4, gf, kb):
    _, k1p, r, c = a4.shape
    return pl.pallas_call(
        functools.partial(_fft2_filter_kernel, kb=kb),
        grid=(k1p // kb,),
        in_specs=[
            pl.BlockSpec((None, kb, r, c), lambda i: (0, i, 0, 0)),
            pl.BlockSpec((None, kb, r, c), lambda i: (1, i, 0, 0)),
            pl.BlockSpec((kb, r, r), lambda i: (i, 0, 0)),
        ],
        out_specs=pl.BlockSpec((kb, r, c), lambda i: (i, 0, 0)),
        out_shape=jax.ShapeDtypeStruct((k1p, r, c), F32),
        compiler_params=_params(1, 40),
        name="fft2_filter",
    )(a4, a4, gf)


def _fft2_kernel(a_ref, k_ref, gf_ref, gi_ref, o_ref, *, kb):
    n2 = FFT_N2
    for t in range(kb):
        x = jnp.dot(gf_ref[t], a_ref[t], preferred_element_type=F32)
        kk = k_ref[t]
        xr, xi, kr, ki = x[:n2], x[n2:], kk[:n2], kk[n2:]
        y = jnp.concatenate([xr * kr - xi * ki, xr * ki + xi * kr], axis=0).astype(BF16)
        o_ref[t] = jnp.dot(gi_ref[t], y, preferred_element_type=F32).astype(o_ref.dtype)


def _fft2(a4, kspec, gf, gi, kb):
    nb, k1p, r, c = a4.shape
    return pl.pallas_call(
        functools.partial(_fft2_kernel, kb=kb),
        grid=(k1p // kb, nb),
        in_specs=[
            pl.BlockSpec((None, kb, r, c), lambda i, b: (b, i, 0, 0)),
            pl.BlockSpec((kb, r, c), lambda i, b: (i, 0, 0)),
            pl.BlockSpec((kb, r, r), lambda i, b: (i, 0, 0)),
            pl.BlockSpec((kb, r, r), lambda i, b: (i, 0, 0)),
        ],
        out_specs=pl.BlockSpec((None, kb, r, c), lambda i, b: (b, i, 0, 0)),
        out_shape=jax.ShapeDtypeStruct((nb, k1p, r, c), BF16),
        compiler_params=_params(2, 40),
        name="fft2",
    )(a4, kspec, gf, gi)


def _fft3_kernel(ft_ref, bt_ref, u_ref, x0_ref, bias_ref, o_ref):
    conv = jnp.dot(ft_ref[...], bt_ref[...], preferred_element_type=F32)
    v = conv + u_ref[...].astype(F32) * bias_ref[...]
    o_ref[...] = (x0_ref[...].astype(F32) * v).astype(o_ref.dtype)


def _fft3(f1t, bt3, u3, x0c3, bias_cols, tcol):
    nb, rows, cols = bt3.shape
    n1h = f1t.shape[0]
    blk = pl.BlockSpec((None, n1h, tcol), lambda b, j: (b, 0, j))
    return pl.pallas_call(
        _fft3_kernel,
        grid=(nb, cols // tcol),
        in_specs=[
            pl.BlockSpec((n1h, rows), lambda b, j: (0, 0)),
            pl.BlockSpec((None, rows, tcol), lambda b, j: (b, 0, j)),
            blk,
            blk,
            pl.BlockSpec((1, tcol), lambda b, j: (0, 0)),
        ],
        out_specs=blk,
        out_shape=jax.ShapeDtypeStruct((nb, n1h, cols), BF16),
        compiler_params=_params(2, 40),
        name="fft3",
    )(f1t, bt3, u3, x0c3, bias_cols)


def _outproj_kernel(ys_ref, yh_ref, w_ref, x_ref, posr_ref, cemb_ref, nwp_ref, g_ref, nwf_ref, sh_ref, sc_ref,
                    xo_ref, h_ref):
    ds = ys_ref.shape[1]
    y = jnp.dot(ys_ref[...], w_ref[0:ds, :], preferred_element_type=F32)
    y = y + jnp.dot(yh_ref[...], w_ref[ds:, :], preferred_element_type=F32)
    xn = _add_pos(x_ref[...], posr_ref[...], cemb_ref[...]) + g_ref[...] * (_rms(y) * nwp_ref[...])
    xo_ref[...] = xn
    h_ref[...] = (_rms(xn) * nwf_ref[...] * (1.0 + sc_ref[...]) + sh_ref[...]).astype(h_ref.dtype)


def _out_proj(ys, yh, w, x2, posr, cemb, nwp, g1, nwf, sh2, sc2, seq_len, tm):
    m, d = x2.shape
    ds = ys.shape[1]
    dh = yh.shape[1]
    tiles_per_seq = seq_len // tm
    row = lambda i: (i, 0)
    fixed = lambda i: (0, 0)
    per_batch = lambda i: (i // tiles_per_seq, 0, 0)
    return pl.pallas_call(
        _outproj_kernel,
        grid=(m // tm,),
        in_specs=[
            pl.BlockSpec((tm, ds), row),
            pl.BlockSpec((tm, dh), row),
            pl.BlockSpec((ds + dh, d), fixed, pipeline_mode=ONE_BUFFER),
            pl.BlockSpec((tm, d), row),
            pl.BlockSpec((tm, d // 2), lambda i: (i % tiles_per_seq, 0)),
            pl.BlockSpec(cemb.shape, fixed),
            pl.BlockSpec((1, d), fixed),
            pl.BlockSpec((None, 1, d), per_batch),
            pl.BlockSpec((1, d), fixed),
            pl.BlockSpec((None, 1, d), per_batch),
            pl.BlockSpec((None, 1, d), per_batch),
        ],
        out_specs=[pl.BlockSpec((tm, d), row), pl.BlockSpec((tm, d), row)],
        out_shape=[jax.ShapeDtypeStruct((m, d), F32), jax.ShapeDtypeStruct((m, d), BF16)],
        compiler_params=_params(1, 56),
        name="out_proj",
    )(ys, yh, w, x2, posr, cemb, nwp, g1, nwf, sh2, sc2)


def _ffn_kernel(h_ref, wg_ref, wu_ref, wd_ref, x_ref, nw_ref, g_ref, o_ref):
    f = pl.program_id(1)
    h = h_ref[...]
    gate = jnp.dot(h, wg_ref[...], preferred_element_type=F32)
    up = jnp.dot(h, wu_ref[...], preferred_element_type=F32)
    act = (_silu(gate) * up).astype(BF16)

    @pl.when(f == 0)
    def _():
        o_ref[...] = jnp.zeros_like(o_ref)

    d = o_ref.shape[1]
    cw = min(512, d)
    for n0 in range(0, d, cw):
        o_ref[:, n0:n0 + cw] += jnp.dot(act, wd_ref[:, n0:n0 + cw], preferred_element_type=F32)

    @pl.when(f == pl.num_programs(1) - 1)
    def _():
        tm = o_ref.shape[0]
        rows = min(256, tm)
        for r0 in range(0, tm, rows):
            y = o_ref[r0:r0 + rows, :]
            o_ref[r0:r0 + rows, :] = x_ref[r0:r0 + rows, :] + g_ref[...] * (_rms(y) * nw_ref[...])


def _ffn(h2, wg, wu, wd, xn, nw, g2, seq_len, tm, tf):
    m, d = xn.shape
    dff = wg.shape[1]
    tiles_per_seq = seq_len // tm
    return pl.pallas_call(
        _ffn_kernel,
        grid=(m // tm, dff // tf),
        in_specs=[
            pl.BlockSpec((tm, d), lambda i, f: (i, 0)),
            pl.BlockSpec((d, tf), lambda i, f: (0, f)),
            pl.BlockSpec((d, tf), lambda i, f: (0, f)),
            pl.BlockSpec((tf, d), lambda i, f: (f, 0)),
            pl.BlockSpec((tm, d), lambda i, f: (i, 0), pipeline_mode=ONE_BUFFER),
            pl.BlockSpec((1, d), lambda i, f: (0, 0)),
            pl.BlockSpec((None, 1, d), lambda i, f: (i // tiles_per_seq, 0, 0)),
        ],
        out_specs=pl.BlockSpec((tm, d), lambda i, f: (i, 0)),
        out_shape=jax.ShapeDtypeStruct((m, d), F32),
        compiler_params=_params(2, 58),
        name="ffn",
    )(h2, wg, wu, wd, xn, nw, g2)


def _sincos_tables(rows, cols, dim):
    qd = dim // 4
    omega = 1.0 / (POS_THETA ** (jnp.arange(qd, dtype=F32) / qd))
    r = jnp.arange(rows, dtype=F32)[:, None] * omega
    cc = jnp.arange(cols, dtype=F32)[:, None] * omega
    r_emb = jnp.concatenate([jnp.sin(r), jnp.cos(r)], -1)
    c_emb = jnp.concatenate([jnp.sin(cc), jnp.cos(cc)], -1)
    return r_emb, c_emb


def _filter_features_t(seq_len, n_bands):
    t = jnp.linspace(0.0, 1.0, seq_len, dtype=F32)[:, None]
    w = 2.0 * math.pi * jnp.arange(seq_len, dtype=F32)[:, None] / seq_len
    fb = jnp.linspace(1e-4, n_bands - 1, n_bands, dtype=F32)[None]
    zpos = jnp.concatenate([t, jnp.cos(fb * w), -jnp.sin(fb * w)], -1)
    emb = zpos.shape[1]
    return jnp.pad(zpos, ((0, 0), (0, LANES - emb))).T


def _pad_rows(a, rows):
    return jnp.pad(a, ((0, rows - a.shape[0]), (0, 0)))


def kernel(x, c, ctx, c_ctx, w_ada, b_ada, norm_mix_pre, norm_mix_post, norm_ffn_pre, norm_ffn_post,
           w_in, ssd_conv_w, ssd_conv_b, ssd_a_log, ssd_dt_bias, ssd_d, ssd_norm,
           hy_conv_w, hy_conv_b, hy_w1, hy_b1, hy_w2, hy_b2, hy_w3, hy_b3, hy_w4, hy_freq, hy_bias,
           w_out, w_gate, w_up, w_down):
    nb, seq_len, d = x.shape
    ctx_len = ctx.shape[1]
    assert w_ada.shape[0] == 1, "single layer"
    n_heads = ssd_d.shape[1]
    d_ssd = n_heads * SSD_HEAD_DIM
    d_xbc = d_ssd + 2 * SSD_GROUPS * SSD_STATE
    dh = hy_bias.shape[1]
    assert w_in.shape[2] == d_ssd + d_xbc + 2 * n_heads + 3 * dh
    assert n_heads <= LANES and nb + 1 <= 8
    assert seq_len % (GRID_W * 8) == 0 and seq_len % FFT_N2 == 0
    m = nb * seq_len

    crows = _pad_rows(jnp.concatenate([c, c_ctx[None, :]], axis=0), 8)
    mod = _ada(crows, w_ada[0], b_ada[0])
    part = lambda r0, r1, k: mod[r0:r1, k * d:(k + 1) * d][:, None, :]
    sh1, sc1, g1, sh2, sc2, g2 = (part(0, nb, k) for k in range(6))
    csh1 = jnp.broadcast_to(part(nb, nb + 1, 0), (nb, 1, d))
    csc1 = jnp.broadcast_to(part(nb, nb + 1, 1), (nb, 1, d))

    wi = w_in[0]
    o_xbc = d_ssd
    o_dt = o_xbc + d_xbc
    o_hy = o_dt + 2 * n_heads
    w_main = jnp.concatenate([wi[:, :o_dt], wi[:, o_hy:]], axis=1).astype(BF16)
    w_ctx = wi[:, o_xbc:o_dt].astype(BF16)
    w_dt = jnp.concatenate([
        jnp.pad(wi[:, o_dt:o_dt + n_heads], ((0, 0), (0, LANES - n_heads))),
        jnp.pad(wi[:, o_dt + n_heads:o_hy], ((0, 0), (0, LANES - n_heads)))], axis=1).astype(BF16)

    r_emb, c_emb = _sincos_tables(seq_len // GRID_W, GRID_W, d)
    posr = jnp.repeat(r_emb, GRID_W, axis=0)
    nmp = norm_mix_pre[0][None, :]

    tm_in = min(1024, seq_len)
    n_main = w_main.shape[1]
    tn_in = n_main // 4 if (n_main // 4) % LANES == 0 else n_main // 2
    proj, dt_lat = _in_proj(x.reshape(m, d), posr, c_emb, nmp, sh1, sc1, w_main, w_dt, seq_len, tm_in, tn_in, True)
    tm_ctx = min(256, ctx_len)
    xbc_ctx, dt_ctx = _in_proj(ctx.reshape(nb * ctx_len, d), jnp.zeros((8, d // 2), F32), c_emb, nmp, csh1, csc1,
                               w_ctx, w_dt, ctx_len, tm_ctx, 512, False)
    proj3 = proj.reshape(nb, seq_len, -1)

    cw8 = _pad_rows(ssd_conv_w[0], 8)
    cb1 = ssd_conv_b[0][None, :]
    u_lat = _ssd_prep(proj3, d_ssd, d_xbc, cw8, cb1, 512)
    u_ctx = _ssd_prep(xbc_ctx.reshape(nb, ctx_len, d_xbc), 0, d_xbc, cw8, cb1, 512)
    pad_heads = lambda a: jnp.pad(a, ((0, 0), (0, LANES - n_heads)))[:, None, :]
    expand = (jnp.arange(LANES)[:, None] == (jnp.arange(d_ssd)[None, :] // SSD_HEAD_DIM)).astype(BF16)
    y_ssd = _ssd(u_lat, dt_lat.reshape(nb, seq_len, 2 * LANES), u_ctx, dt_ctx.reshape(nb, ctx_len, 2 * LANES),
                 proj3, pad_heads(ssd_a_log[0]), pad_heads(ssd_dt_bias[0]),
                 jnp.repeat(ssd_d[0], SSD_HEAD_DIM)[None, :], ssd_norm[0][None, :], expand)

    x0c, u_hy = _hy_prep(proj3, d_ssd + d_xbc, dh, _pad_rows(hy_conv_w[0], 8), hy_conv_b[0][None, :], 256)
    n_bands = (hy_w1.shape[1] - 1) // 2
    zt = _filter_features_t(seq_len, n_bands)
    w1t = jnp.pad(hy_w1[0].T, ((0, 0), (0, LANES - hy_w1.shape[1])))
    b3 = jnp.stack([hy_b1[0], hy_b2[0], hy_b3[0]], axis=1)
    h3 = _hy_mlp(zt, w1t, hy_w2[0].T, hy_w3[0].T, b3, hy_freq[0][:, None])
    w4 = _pad_rows(hy_w4[0], LANES)
    max_decay = math.log(HY_TARGET) / HY_FAST_PCT
    min_decay = math.log(HY_TARGET) / HY_SLOW_PCT
    deltas = jnp.abs(jnp.linspace(min_decay, max_decay, dh, dtype=F32))[None, :]
    filt = _hy_filters(h3, w4[:, :dh], w4[:, dh:], deltas, 256)

    f1, f1t, gf, gi, k1p = _fft_tables(seq_len)
    n1h = seq_len // FFT_N2
    cols = FFT_N2 * dh
    tcol = min(8192, cols)
    kb = 4
    a_filt = _fft1(f1, filt.reshape(2, n1h, cols), tcol)
    kspec = _fft2_filter(a_filt.reshape(2, k1p, 2 * FFT_N2, dh), gf, kb)
    a_u = _fft1(f1, u_hy.reshape(nb, n1h, cols), tcol)
    bt = _fft2(a_u.reshape(nb, k1p, 2 * FFT_N2, dh), kspec, gf, gi, kb)
    y_hy = _fft3(f1t, bt.reshape(nb, 2 * k1p, cols), u_hy.reshape(nb, n1h, cols), x0c.reshape(nb, n1h, cols),
                 jnp.tile(hy_bias[0], tcol // dh)[None, :], tcol)

    xn, h2 = _out_proj(y_ssd.reshape(m, d_ssd), y_hy.reshape(m, dh), w_out[0].astype(BF16), x.reshape(m, d),
                       posr, c_emb, norm_mix_post[0][None, :], g1, norm_ffn_pre[0][None, :], sh2, sc2, seq_len,
                       min(512, seq_len))
    out = _ffn(h2, w_gate[0].astype(BF16), w_up[0].astype(BF16), w_down[0].astype(BF16), xn,
               norm_ffn_post[0][None, :], g2, seq_len, min(1024, seq_len), 512)
    return out.reshape(nb, seq_len, d)
```

```python
import functools
import math

import numpy as np
import jax
import jax.numpy as jnp
from jax import lax
from jax.experimental import pallas as pl
from jax.experimental.pallas import tpu as pltpu

F32 = jnp.float32
BF16 = jnp.bfloat16
HIGHEST = lax.Precision.HIGHEST

RMS_EPS = 1e-6
POS_THETA = 10000.0
GRID_W = 64
SSD_HEAD_DIM = 64
SSD_GROUPS = 2
SSD_STATE = 128
SSD_CHUNK = 128
HY_TARGET = 1e-2
HY_FAST_PCT = 0.3
HY_SLOW_PCT = 1.5
FFT_N2 = 64
FFT_UNROLL = 8
LANES = 128
MIB = 1024 * 1024
ONE_BUFFER = pl.Buffered(1)


def _params(n_axes, vmem_mib):
    return pltpu.CompilerParams(
        dimension_semantics=("arbitrary",) * n_axes,
        vmem_limit_bytes=vmem_mib * MIB,
    )


def _silu(v):
    return v * (1.0 / (1.0 + jnp.exp(-v)))


def _softplus(v):
    return jnp.maximum(v, 0.0) + jnp.log(1.0 + jnp.exp(-jnp.abs(v)))


def _rms(v):
    return v * lax.rsqrt(jnp.mean(v * v, axis=-1, keepdims=True) + RMS_EPS)


def _add_pos(x, posr, cemb):
    tm, d = x.shape
    half = d // 2
    lo = x[:, :half] + posr
    hi = (x[:, half:].reshape(tm // GRID_W, GRID_W, half) + cemb[None]).reshape(tm, half)
    return jnp.concatenate([lo, hi], axis=1)


def _ada_kernel(c_ref, w_ref, b_ref, o_ref):
    s = _silu(c_ref[...])
    o_ref[...] = jnp.dot(s, w_ref[...], precision=HIGHEST, preferred_element_type=F32) + b_ref[...]


def _ada(crows, w_ada, b_ada):
    d, n = w_ada.shape
    tn = 1024
    return pl.pallas_call(
        _ada_kernel,
        grid=(n // tn,),
        in_specs=[
            pl.BlockSpec((8, d), lambda j: (0, 0)),
            pl.BlockSpec((d, tn), lambda j: (0, j)),
            pl.BlockSpec((1, tn), lambda j: (0, j)),
        ],
        out_specs=pl.BlockSpec((8, tn), lambda j: (0, j)),
        out_shape=jax.ShapeDtypeStruct((8, n), F32),
        compiler_params=_params(1, 40),
        name="ada",
    )(crows, w_ada, b_ada.reshape(1, n))


def _inproj_kernel(x_ref, posr_ref, cemb_ref, nw_ref, sh_ref, sc_ref, w_ref, wdt_ref, o_ref, dt_ref, h_scr,
                   *, with_pos):
    @pl.when(pl.program_id(1) == 0)
    def _():
        tm = x_ref.shape[0]
        rows = min(256, tm)
        for r0 in range(0, tm, rows):
            xf = x_ref[r0:r0 + rows, :]
            if with_pos:
                xf = _add_pos(xf, posr_ref[r0:r0 + rows, :], cemb_ref[...])
            h = _rms(xf) * nw_ref[...] * (1.0 + sc_ref[...]) + sh_ref[...]
            hb = h.astype(BF16)
            h_scr[r0:r0 + rows, :] = hb
            dt_ref[r0:r0 + rows, :] = jnp.dot(hb, wdt_ref[...], preferred_element_type=F32)

    o_ref[...] = jnp.dot(h_scr[...], w_ref[...], preferred_element_type=F32).astype(o_ref.dtype)


def _in_proj(x2, posr, cemb, nw, sh, sc, w, wdt, seq_len, tm, tn, with_pos):
    m, d = x2.shape
    n = w.shape[1]
    ndt = wdt.shape[1]
    half = d // 2
    tiles_per_seq = seq_len // tm
    pos_rows = tm if with_pos else posr.shape[0]
    return pl.pallas_call(
        functools.partial(_inproj_kernel, with_pos=with_pos),
        grid=(m // tm, n // tn),
        in_specs=[
            pl.BlockSpec((tm, d), lambda i, j: (i, 0)),
            pl.BlockSpec((pos_rows, half), lambda i, j: ((i % tiles_per_seq) if with_pos else 0, 0)),
            pl.BlockSpec(cemb.shape, lambda i, j: (0, 0), pipeline_mode=ONE_BUFFER),
            pl.BlockSpec((1, d), lambda i, j: (0, 0)),
            pl.BlockSpec((None, 1, d), lambda i, j: (i // tiles_per_seq, 0, 0)),
            pl.BlockSpec((None, 1, d), lambda i, j: (i // tiles_per_seq, 0, 0)),
            pl.BlockSpec((d, tn), lambda i, j: (0, j)),
            pl.BlockSpec((d, ndt), lambda i, j: (0, 0), pipeline_mode=ONE_BUFFER),
        ],
        out_specs=[
            pl.BlockSpec((tm, tn), lambda i, j: (i, j)),
            pl.BlockSpec((tm, ndt), lambda i, j: (i, 0)),
        ],
        out_shape=[
            jax.ShapeDtypeStruct((m, n), BF16),
            jax.ShapeDtypeStruct((m, ndt), F32),
        ],
        scratch_shapes=[pltpu.VMEM((tm, d), BF16)],
        compiler_params=_params(2, 58),
        name="in_proj",
    )(x2, posr, cemb, nw, sh, sc, w, wdt)


def _conv3_chunk(x_ref, r0, rows, seq_len, w, b):
    cur = x_ref[r0:r0 + rows, :].astype(F32)
    tc = cur.shape[1]
    rid = lax.broadcasted_iota(jnp.int32, (rows, tc), 0)
    if r0 > 0:
        prev_row = x_ref[r0 - 16:r0, :].astype(F32)[15:16, :]
    else:
        prev_row = jnp.zeros((1, tc), F32)
    if r0 + rows < seq_len:
        next_row = x_ref[r0 + rows:r0 + rows + 16, :].astype(F32)[0:1, :]
    else:
        next_row = jnp.zeros((1, tc), F32)
    up = jnp.where(rid == 0, prev_row, pltpu.roll(cur, 1, 0))
    down = jnp.where(rid == rows - 1, next_row, pltpu.roll(cur, rows - 1, 0))
    return up * w[0:1, :] + cur * w[1:2, :] + down * w[2:3, :] + b


def _ssd_prep_kernel(x_ref, w_ref, b_ref, o_ref, *, rows):
    seq_len = x_ref.shape[0]
    w = w_ref[...]
    b = b_ref[...]
    for r0 in range(0, seq_len, rows):
        v = _conv3_chunk(x_ref, r0, rows, seq_len, w, b)
        o_ref[r0:r0 + rows, :] = _silu(v).astype(o_ref.dtype)


def _ssd_prep(proj3, col0, width, w8, b1, tc):
    nb, seq_len, _ = proj3.shape
    rows = min(512, seq_len)
    j0 = col0 // tc
    return pl.pallas_call(
        functools.partial(_ssd_prep_kernel, rows=rows),
        grid=(nb, width // tc),
        in_specs=[
            pl.BlockSpec((None, seq_len, tc), lambda b, j: (b, 0, j0 + j)),
            pl.BlockSpec((8, tc), lambda b, j: (0, j)),
            pl.BlockSpec((1, tc), lambda b, j: (0, j)),
        ],
        out_specs=pl.BlockSpec((None, seq_len, tc), lambda b, j: (b, 0, j)),
        out_shape=jax.ShapeDtypeStruct((nb, seq_len, width), BF16),
        compiler_params=_params(2, 40),
        name="ssd_prep",
    )(proj3, w8, b1)


def _hy_prep_kernel(x0_ref, x1_ref, v_ref, w0_ref, w1_ref, wv_ref, b0_ref, b1_ref, bv_ref,
                    x0c_ref, u_ref, *, rows):
    seq_len = x0_ref.shape[0]
    w0, w1, wv = w0_ref[...], w1_ref[...], wv_ref[...]
    b0, b1, bv = b0_ref[...], b1_ref[...], bv_ref[...]
    for r0 in range(0, seq_len, rows):
        x0c_ref[r0:r0 + rows, :] = _conv3_chunk(x0_ref, r0, rows, seq_len, w0, b0).astype(x0c_ref.dtype)
        x1c = _conv3_chunk(x1_ref, r0, rows, seq_len, w1, b1)
        vc = _conv3_chunk(v_ref, r0, rows, seq_len, wv, bv)
        u_ref[r0:r0 + rows, :] = (vc * x1c).astype(u_ref.dtype)


def _hy_prep(proj3, col0, dh, w8, b1, tc):
    nb, seq_len, _ = proj3.shape
    rows = min(512, seq_len)
    j0 = col0 // tc
    nj = dh // tc
    x_spec = lambda k: pl.BlockSpec((None, seq_len, tc), lambda b, j: (b, 0, j0 + k * nj + j))
    w_spec = lambda k: pl.BlockSpec((8, tc), lambda b, j: (0, k * nj + j))
    b_spec = lambda k: pl.BlockSpec((1, tc), lambda b, j: (0, k * nj + j))
    o_spec = pl.BlockSpec((None, seq_len, tc), lambda b, j: (b, 0, j))
    return pl.pallas_call(
        functools.partial(_hy_prep_kernel, rows=rows),
        grid=(nb, nj),
        in_specs=[x_spec(0), x_spec(1), x_spec(2), w_spec(0), w_spec(1), w_spec(2),
                  b_spec(0), b_spec(1), b_spec(2)],
        out_specs=[o_spec, o_spec],
        out_shape=[jax.ShapeDtypeStruct((nb, seq_len, dh), BF16)] * 2,
        compiler_params=_params(2, 48),
        name="hy_prep",
    )(proj3, proj3, proj3, w8, w8, w8, b1, b1, b1)


def _ssd_kernel(u_ref, dt_ref, uc_ref, dtc_ref, z_ref, alog_ref, dtb_ref, dsk_ref, nw_ref, e_ref,
                o_ref, h_scr, yf_scr, *, n_ctx, n_lat):
    q = SSD_CHUNK
    d_ssd = h_scr.shape[1]
    gw = d_ssd // SSD_GROUPS
    d = pl.program_id(1)
    s = pl.program_id(2)
    is_ctx = s < n_ctx
    t = jnp.maximum(s - n_ctx, 0)
    cidx = jnp.where(d == 0, t, n_lat - 1 - t)

    @pl.when(s == 0)
    def _():
        h_scr[...] = jnp.zeros_like(h_scr)

    u = jnp.where(is_ctx, uc_ref[...], u_ref[...])
    dtr = jnp.where(is_ctx, dtc_ref[...], dt_ref[...])

    dt = _softplus(dtr + dtb_ref[...])
    a = dt * (-jnp.exp(alog_ref[...]))
    row = lax.broadcasted_iota(jnp.int32, (q, q), 0)
    col = lax.broadcasted_iota(jnp.int32, (q, q), 1)
    fwd = d == 0
    tri = jnp.where(fwd, row, col) >= jnp.where(fwd, col, row)
    cum = jnp.dot(tri.astype(F32), a, precision=HIGHEST, preferred_element_type=F32)
    cum_t = cum.T
    tot = jnp.sum(a, axis=0, keepdims=True)

    stack = jnp.concatenate(
        [dt, jnp.exp(cum), jnp.exp(tot - cum), jnp.broadcast_to(jnp.exp(tot), (8, LANES))], axis=0)
    ex = jnp.dot(stack.astype(BF16), e_ref[...], preferred_element_type=F32)
    dt_x = ex[0:q]
    ecum_x = ex[q:2 * q]
    edec_x = ex[2 * q:3 * q]
    etot_x = ex[3 * q:3 * q + 1]

    xs = u[:, :d_ssd].astype(F32)
    xdt = xs * dt_x
    xdt_b = xdt.astype(BF16)
    xdw_b = (xdt * edec_x).astype(BF16)
    lane_lo = lax.broadcasted_iota(jnp.int32, (q, LANES), 1) < SSD_HEAD_DIM
    heads_per_group = gw // SSD_HEAD_DIM

    y_parts = []
    for g in range(SSD_GROUPS):
        bg = u[:, d_ssd + g * SSD_STATE:d_ssd + (g + 1) * SSD_STATE]
        cg = u[:, d_ssd + (SSD_GROUPS + g) * SSD_STATE:d_ssd + (SSD_GROUPS + g + 1) * SSD_STATE]
        scores = lax.dot_general(cg, bg, (((1,), (1,)), ((), ())), preferred_element_type=F32)
        h_prev = h_scr[:, g * gw:(g + 1) * gw]
        y_off = jnp.dot(cg, h_prev.astype(BF16), preferred_element_type=F32)
        for j in range(heads_per_group // 2):
            c0 = g * gw + j * LANES
            xp = xdt_b[:, c0:c0 + LANES]
            acc = None
            for hh in range(2):
                h = g * heads_per_group + 2 * j + hh
                diff = cum[:, h:h + 1] - cum_t[h:h + 1, :]
                decay = jnp.exp(jnp.where(tri, diff, -1e30))
                m_h = (scores * decay).astype(BF16)
                x_h = jnp.where(lane_lo if hh == 0 else jnp.logical_not(lane_lo), xp, jnp.zeros_like(xp))
                part = jnp.dot(m_h, x_h, preferred_element_type=F32)
                acc = part if acc is None else acc + part
            y_parts.append(acc + y_off[:, j * LANES:(j + 1) * LANES] * ecum_x[:, c0:c0 + LANES])
        upd = lax.dot_general(bg, xdw_b[:, g * gw:(g + 1) * gw], (((0,), (0,)), ((), ())),
                              preferred_element_type=F32)
        h_scr[:, g * gw:(g + 1) * gw] = etot_x[:, g * gw:(g + 1) * gw] * h_prev + upd
    y = jnp.concatenate(y_parts, axis=1)

    @pl.when(jnp.logical_and(jnp.logical_not(is_ctx), fwd))
    def _():
        yf_scr[cidx] = y

    @pl.when(jnp.logical_and(jnp.logical_not(is_ctx), jnp.logical_not(fwd)))
    def _():
        ytot = yf_scr[cidx] + y + dsk_ref[...] * xs
        gated = ytot * _silu(z_ref[...].astype(F32))
        o_ref[...] = (_rms(gated) * nw_ref[...]).astype(o_ref.dtype)


def _ssd(u_lat, dt_lat, u_ctx, dt_ctx, proj3, alog, dtb, dsk, nw, expand):
    nb, seq_len, d_xbc = u_lat.shape
    ctx_len = u_ctx.shape[1]
    q = SSD_CHUNK
    n_lat = seq_len // q
    n_ctx = ctx_len // q
    d_ssd = dsk.shape[1]

    def lat_idx(d, s):
        t = jnp.maximum(s - n_ctx, 0)
        return jnp.where(d == 0, t, n_lat - 1 - t)

    def ctx_idx(d, s):
        t = jnp.minimum(s, n_ctx - 1)
        return jnp.where(d == 0, t, n_ctx - 1 - t)

    def out_idx(d, s):
        return jnp.where(d == 0, n_lat - 1, lat_idx(d, s))

    return pl.pallas_call(
        functools.partial(_ssd_kernel, n_ctx=n_ctx, n_lat=n_lat),
        grid=(nb, 2, n_ctx + n_lat),
        in_specs=[
            pl.BlockSpec((None, q, d_xbc), lambda b, d, s: (b, lat_idx(d, s), 0)),
            pl.BlockSpec((None, q, LANES), lambda b, d, s: (b, lat_idx(d, s), d)),
            pl.BlockSpec((None, q, d_xbc), lambda b, d, s: (b, ctx_idx(d, s), 0)),
            pl.BlockSpec((None, q, LANES), lambda b, d, s: (b, ctx_idx(d, s), d)),
            pl.BlockSpec((None, q, d_ssd), lambda b, d, s: (b, lat_idx(d, s), 0)),
            pl.BlockSpec((None, 1, LANES), lambda b, d, s: (d, 0, 0)),
            pl.BlockSpec((None, 1, LANES), lambda b, d, s: (d, 0, 0)),
            pl.BlockSpec((1, d_ssd), lambda b, d, s: (0, 0)),
            pl.BlockSpec((1, d_ssd), lambda b, d, s: (0, 0)),
            pl.BlockSpec((LANES, d_ssd), lambda b, d, s: (0, 0)),
        ],
        out_specs=pl.BlockSpec((None, q, d_ssd), lambda b, d, s: (b, out_idx(d, s), 0)),
        out_shape=jax.ShapeDtypeStruct((nb, seq_len, d_ssd), BF16),
        scratch_shapes=[
            pltpu.VMEM((SSD_STATE, d_ssd), F32),
            pltpu.VMEM((n_lat, q, d_ssd), F32),
        ],
        compiler_params=_params(3, 48),
        name="ssd",
    )(u_lat, dt_lat, u_ctx, dt_ctx, proj3, alog, dtb, dsk, nw, expand)


def _hy_mlp_kernel(zt_ref, w1_ref, w2_ref, w3_ref, b_ref, fr_ref, o_ref):
    fr = fr_ref[...]
    b = b_ref[...]
    h = jnp.sin(fr * (jnp.dot(w1_ref[...], zt_ref[...], precision=HIGHEST, preferred_element_type=F32)
                      + b[:, 0:1]))
    h = jnp.sin(fr * (jnp.dot(w2_ref[...], h, precision=HIGHEST, preferred_element_type=F32) + b[:, 1:2]))
    h = jnp.sin(fr * (jnp.dot(w3_ref[...], h, precision=HIGHEST, preferred_element_type=F32) + b[:, 2:3]))
    hid, seq_len = h.shape
    hp = jnp.concatenate([h, jnp.zeros((LANES - hid, seq_len), F32)], axis=0)
    o_ref[...] = hp.T


def _hy_mlp(zt, w1t, w2t, w3t, b3, fr):
    seq_len = zt.shape[1]
    return pl.pallas_call(
        _hy_mlp_kernel,
        out_shape=jax.ShapeDtypeStruct((seq_len, LANES), F32),
        compiler_params=pltpu.CompilerParams(vmem_limit_bytes=40 * MIB),
        name="hy_mlp",
    )(zt, w1t, w2t, w3t, b3, fr)


def _fft_tables(seq_len):
    n_fft = 2 * seq_len
    n2 = FFT_N2
    n1 = n_fft // n2
    n1h = n1 // 2
    k1n = n1h + 1
    k1p = -(-k1n // 4) * 4
    k1 = np.arange(k1n, dtype=np.int64)
    th = (2.0 * np.pi / n1) * ((k1[:, None] * np.arange(n1h, dtype=np.int64)[None, :]) % n1)
    f1 = np.zeros((2 * k1p, n1h))
    f1[0:2 * k1n:2] = np.cos(th)
    f1[1:2 * k1n:2] = -np.sin(th)
    idx = np.arange(n2, dtype=np.int64)
    kk = k1[:, None, None] + n1 * idx[None, :, None]
    ph = (2.0 * np.pi / n_fft) * ((kk * idx[None, None, :]) % n_fft)
    g_re, g_im = np.cos(ph), -np.sin(ph)

    def blocks(re, im):
        out = np.zeros((k1p, 2 * n2, 2 * n2))
        out[:k1n, :n2, :n2] = re
        out[:k1n, :n2, n2:] = -im
        out[:k1n, n2:, :n2] = im
        out[:k1n, n2:, n2:] = re
        return out

    wgt = np.where((k1 == 0) | (k1 == n1h), 1.0, 2.0) / n_fft
    gf = blocks(g_re, g_im)
    gi = blocks(np.transpose(g_re, (0, 2, 1)) * wgt[:, None, None],
                -np.transpose(g_im, (0, 2, 1)) * wgt[:, None, None])
    as_bf16 = lambda t: jnp.asarray(t.astype(np.float32)).astype(BF16)
    return as_bf16(f1), as_bf16(np.ascontiguousarray(f1.T)), as_bf16(gf), as_bf16(gi), k1n


def _ld(ref, start, size, stride=None):
    idx = pl.ds(start, size) if stride is None else pl.ds(start, size, stride=stride)
    return jnp.concatenate([ref[t, idx, :] for t in range(ref.shape[0])], axis=1)


def _st(ref, start, size, val, stride=None):
    idx = pl.ds(start, size) if stride is None else pl.ds(start, size, stride=stride)
    for t in range(ref.shape[0]):
        ref[t, idx, :] = val[:, t * LANES:(t + 1) * LANES]


def _fft_stage1(x_scr, a_scr, f1_ref):
    rows, n1h = f1_ref.shape

    def body(n2, _):
        xs = _ld(x_scr, n2, n1h, FFT_N2).astype(BF16)
        _st(a_scr, pl.multiple_of(n2 * rows, 8), rows, jnp.dot(f1_ref[...], xs, preferred_element_type=F32))
        return 0

    lax.fori_loop(0, FFT_N2, body, 0, unroll=FFT_UNROLL)


def _fft_stage2_in(a_scr, k1, rows):
    return jnp.concatenate([_ld(a_scr, 2 * k1, FFT_N2, rows), _ld(a_scr, 2 * k1 + 1, FFT_N2, rows)],
                           axis=0).astype(BF16)


def _hy_spec_kernel(h_ref, wf_ref, wb_ref, dl_ref, f1_ref, gf_ref, k_ref, x_scr, a_scr, *, k1n):
    n2 = FFT_N2
    seq_len = h_ref.shape[0]
    tc = wf_ref.shape[1]
    rows = f1_ref.shape[0]
    h3 = h_ref[...]
    rid = lax.broadcasted_iota(jnp.int32, (seq_len, tc), 0)
    decay = jnp.exp(-(rid.astype(F32) * (1.0 / (seq_len - 1))) * dl_ref[...])
    hf = jnp.dot(h3, wf_ref[...], precision=HIGHEST, preferred_element_type=F32) * decay
    hb = jnp.dot(h3, wb_ref[...], precision=HIGHEST, preferred_element_type=F32) * decay
    norm = jnp.sum(jnp.abs(hf) + jnp.abs(hb), axis=0, keepdims=True) + 1e-6
    inv = 1.0 / norm
    hb = jnp.where(rid == 0, 0.0, hb * inv)
    _st(x_scr, 0, seq_len, jnp.concatenate([hf * inv, hb], axis=1))
    _fft_stage1(x_scr, a_scr, f1_ref)

    def mid(k1, _):
        x = jnp.dot(gf_ref[k1], _fft_stage2_in(a_scr, k1, rows), preferred_element_type=F32)
        xf, xb = x[:, 0:tc], x[:, tc:2 * tc]
        k_ref[k1] = jnp.concatenate([xf[:n2] + xb[:n2], xf[n2:] - xb[n2:]], axis=0)
        return 0

    lax.fori_loop(0, k1n, mid, 0, unroll=FFT_UNROLL)
    for k1 in range(k1n, k_ref.shape[0]):
        k_ref[k1] = jnp.zeros((2 * n2, tc), F32)


def _hy_spec(h3, w4f, w4b, deltas, f1, gf, k1n, tc):
    seq_len = h3.shape[0]
    dh = w4f.shape[1]
    k1p = gf.shape[0]
    rows, n1h = f1.shape
    nt = 2 * tc // LANES
    return pl.pallas_call(
        functools.partial(_hy_spec_kernel, k1n=k1n),
        grid=(dh // tc,),
        in_specs=[
            pl.BlockSpec((seq_len, LANES), lambda j: (0, 0)),
            pl.BlockSpec((LANES, tc), lambda j: (0, j)),
            pl.BlockSpec((LANES, tc), lambda j: (0, j)),
            pl.BlockSpec((1, tc), lambda j: (0, j)),
            pl.BlockSpec((rows, n1h), lambda j: (0, 0)),
            pl.BlockSpec((k1p, 2 * FFT_N2, 2 * FFT_N2), lambda j: (0, 0, 0)),
        ],
        out_specs=pl.BlockSpec((k1p, 2 * FFT_N2, tc), lambda j: (0, 0, j)),
        out_shape=jax.ShapeDtypeStruct((k1p, 2 * FFT_N2, dh), F32),
        scratch_shapes=[pltpu.VMEM((nt, seq_len, LANES), F32), pltpu.VMEM((nt, FFT_N2 * rows, LANES), F32)],
        compiler_params=_params(1, 56),
        name="hy_spec",
    )(h3, w4f, w4b, deltas, f1, gf)


def _hy_conv_kernel(u_ref, x0_ref, k_ref, f1_ref, f1t_ref, gf_ref, gi_ref, bias_ref, o_ref, x_scr, a_scr, *, k1n):
    n2 = FFT_N2
    seq_len = u_ref.shape[0]
    rows, n1h = f1_ref.shape
    ch = min(512, seq_len)
    for r0 in range(0, seq_len, ch):
        _st(x_scr, r0, ch, u_ref[r0:r0 + ch, :].astype(F32))
    _fft_stage1(x_scr, a_scr, f1_ref)

    def mid(k1, _):
        x = jnp.dot(gf_ref[k1], _fft_stage2_in(a_scr, k1, rows), preferred_element_type=F32)
        kk = k_ref[k1]
        xr, xi, kr, ki = x[:n2], x[n2:], kk[:n2], kk[n2:]
        y = jnp.concatenate([xr * kr - xi * ki, xr * ki + xi * kr], axis=0).astype(BF16)
        bt = jnp.dot(gi_ref[k1], y, preferred_element_type=F32)
        _st(a_scr, 2 * k1, n2, bt[:n2], rows)
        _st(a_scr, 2 * k1 + 1, n2, bt[n2:], rows)
        return 0

    lax.fori_loop(0, k1n, mid, 0, unroll=FFT_UNROLL)

    def last(j, _):
        blk = _ld(a_scr, pl.multiple_of(j * rows, 8), rows).astype(BF16)
        _st(x_scr, j, n1h, jnp.dot(f1t_ref[...], blk, preferred_element_type=F32), n2)
        return 0

    lax.fori_loop(0, n2, last, 0, unroll=FFT_UNROLL)
    for r0 in range(0, seq_len, ch):
        v = _ld(x_scr, r0, ch) + u_ref[r0:r0 + ch, :].astype(F32) * bias_ref[...]
        o_ref[r0:r0 + ch, :] = (x0_ref[r0:r0 + ch, :].astype(F32) * v).astype(o_ref.dtype)


def _hy_conv(u3, x0c, kspec, f1, f1t, gf, gi, bias, k1n, tc):
    nb, seq_len, dh = u3.shape
    k1p = gf.shape[0]
    rows, n1h = f1.shape
    nt = tc // LANES
    blk = pl.BlockSpec((None, seq_len, tc), lambda j, b: (b, 0, j))
    const = lambda shape: pl.BlockSpec(shape, lambda j, b: (0,) * len(shape), pipeline_mode=ONE_BUFFER)
    return pl.pallas_call(
        functools.partial(_hy_conv_kernel, k1n=k1n),
        grid=(dh // tc, nb),
        in_specs=[
            blk, blk,
            pl.BlockSpec((k1p, 2 * FFT_N2, tc), lambda j, b: (0, 0, j), pipeline_mode=ONE_BUFFER),
            const((rows, n1h)), const((n1h, rows)),
            const((k1p, 2 * FFT_N2, 2 * FFT_N2)), const((k1p, 2 * FFT_N2, 2 * FFT_N2)),
            pl.BlockSpec((1, tc), lambda j, b: (0, j)),
        ],
        out_specs=blk,
        out_shape=jax.ShapeDtypeStruct((nb, seq_len, dh), BF16),
        scratch_shapes=[pltpu.VMEM((nt, seq_len, LANES), F32), pltpu.VMEM((nt, FFT_N2 * rows, LANES), F32)],
        compiler_params=_params(2, 56),
        name="hy_conv",
    )(u3, x0c, kspec, f1, f1t, gf, gi, bias)


def _outproj_kernel(ys_ref, yh_ref, w_ref, x_ref, posr_ref, cemb_ref, nwp_ref, g_ref, nwf_ref, sh_ref, sc_ref,
                    xo_ref, h_ref):
    ds = ys_ref.shape[1]
    y = jnp.dot(ys_ref[...], w_ref[0:ds, :], preferred_element_type=F32)
    y = y + jnp.dot(yh_ref[...], w_ref[ds:, :], preferred_element_type=F32)
    xn = _add_pos(x_ref[...], posr_ref[...], cemb_ref[...]) + g_ref[...] * (_rms(y) * nwp_ref[...])
    xo_ref[...] = xn
    h_ref[...] = (_rms(xn) * nwf_ref[...] * (1.0 + sc_ref[...]) + sh_ref[...]).astype(h_ref.dtype)


def _out_proj(ys, yh, w, x2, posr, cemb, nwp, g1, nwf, sh2, sc2, seq_len, tm):
    m, d = x2.shape
    ds = ys.shape[1]
    dh = yh.shape[1]
    tiles_per_seq = seq_len // tm
    row = lambda i: (i, 0)
    fixed = lambda i: (0, 0)
    per_batch = lambda i: (i // tiles_per_seq, 0, 0)
    return pl.pallas_call(
        _outproj_kernel,
        grid=(m // tm,),
        in_specs=[
            pl.BlockSpec((tm, ds), row),
            pl.BlockSpec((tm, dh), row),
            pl.BlockSpec((ds + dh, d), fixed, pipeline_mode=ONE_BUFFER),
            pl.BlockSpec((tm, d), row),
            pl.BlockSpec((tm, d // 2), lambda i: (i % tiles_per_seq, 0)),
            pl.BlockSpec(cemb.shape, fixed),
            pl.BlockSpec((1, d), fixed),
            pl.BlockSpec((None, 1, d), per_batch),
            pl.BlockSpec((1, d), fixed),
            pl.BlockSpec((None, 1, d), per_batch),
            pl.BlockSpec((None, 1, d), per_batch),
        ],
        out_specs=[pl.BlockSpec((tm, d), row), pl.BlockSpec((tm, d), row)],
        out_shape=[jax.ShapeDtypeStruct((m, d), F32), jax.ShapeDtypeStruct((m, d), BF16)],
        compiler_params=_params(1, 56),
        name="out_proj",
    )(ys, yh, w, x2, posr, cemb, nwp, g1, nwf, sh2, sc2)


def _ffn_kernel(h_ref, wg_ref, wu_ref, wd_ref, x_ref, nw_ref, g_ref, o_ref):
    f = pl.program_id(1)
    h = h_ref[...]
    gate = jnp.dot(h, wg_ref[...], preferred_element_type=F32)
    up = jnp.dot(h, wu_ref[...], preferred_element_type=F32)
    act = (_silu(gate) * up).astype(BF16)

    @pl.when(f == 0)
    def _():
        o_ref[...] = jnp.zeros_like(o_ref)

    d = o_ref.shape[1]
    cw = min(512, d)
    for n0 in range(0, d, cw):
        o_ref[:, n0:n0 + cw] += jnp.dot(act, wd_ref[:, n0:n0 + cw], preferred_element_type=F32)

    @pl.when(f == pl.num_programs(1) - 1)
    def _():
        tm = o_ref.shape[0]
        rows = min(256, tm)
        for r0 in range(0, tm, rows):
            y = o_ref[r0:r0 + rows, :]
            o_ref[r0:r0 + rows, :] = x_ref[r0:r0 + rows, :] + g_ref[...] * (_rms(y) * nw_ref[...])


def _ffn(h2, wg, wu, wd, xn, nw, g2, seq_len, tm, tf):
    m, d = xn.shape
    dff = wg.shape[1]
    tiles_per_seq = seq_len // tm
    return pl.pallas_call(
        _ffn_kernel,
        grid=(m // tm, dff // tf),
        in_specs=[
            pl.BlockSpec((tm, d), lambda i, f: (i, 0)),
            pl.BlockSpec((d, tf), lambda i, f: (0, f)),
            pl.BlockSpec((d, tf), lambda i, f: (0, f)),
            pl.BlockSpec((tf, d), lambda i, f: (f, 0)),
            pl.BlockSpec((tm, d), lambda i, f: (i, 0), pipeline_mode=ONE_BUFFER),
            pl.BlockSpec((1, d), lambda i, f: (0, 0)),
            pl.BlockSpec((None, 1, d), lambda i, f: (i // tiles_per_seq, 0, 0)),
        ],
        out_specs=pl.BlockSpec((tm, d), lambda i, f: (i, 0)),
        out_shape=jax.ShapeDtypeStruct((m, d), F32),
        compiler_params=_params(2, 58),
        name="ffn",
    )(h2, wg, wu, wd, xn, nw, g2)


def _sincos_tables(rows, cols, dim):
    qd = dim // 4
    omega = 1.0 / (POS_THETA ** (jnp.arange(qd, dtype=F32) / qd))
    r = jnp.arange(rows, dtype=F32)[:, None] * omega
    cc = jnp.arange(cols, dtype=F32)[:, None] * omega
    r_emb = jnp.concatenate([jnp.sin(r), jnp.cos(r)], -1)
    c_emb = jnp.concatenate([jnp.sin(cc), jnp.cos(cc)], -1)
    return r_emb, c_emb


def _filter_features_t(seq_len, n_bands):
    t = jnp.linspace(0.0, 1.0, seq_len, dtype=F32)[:, None]
    w = 2.0 * math.pi * jnp.arange(seq_len, dtype=F32)[:, None] / seq_len
    fb = jnp.linspace(1e-4, n_bands - 1, n_bands, dtype=F32)[None]
    zpos = jnp.concatenate([t, jnp.cos(fb * w), -jnp.sin(fb * w)], -1)
    emb = zpos.shape[1]
    return jnp.pad(zpos, ((0, 0), (0, LANES - emb))).T


def _pad_rows(a, rows):
    return jnp.pad(a, ((0, rows - a.shape[0]), (0, 0)))


def kernel(x, c, ctx, c_ctx, w_ada, b_ada, norm_mix_pre, norm_mix_post, norm_ffn_pre, norm_ffn_post,
           w_in, ssd_conv_w, ssd_conv_b, ssd_a_log, ssd_dt_bias, ssd_d, ssd_norm,
           hy_conv_w, hy_conv_b, hy_w1, hy_b1, hy_w2, hy_b2, hy_w3, hy_b3, hy_w4, hy_freq, hy_bias,
           w_out, w_gate, w_up, w_down):
    nb, seq_len, d = x.shape
    ctx_len = ctx.shape[1]
    assert w_ada.shape[0] == 1, "single layer"
    n_heads = ssd_d.shape[1]
    d_ssd = n_heads * SSD_HEAD_DIM
    d_xbc = d_ssd + 2 * SSD_GROUPS * SSD_STATE
    dh = hy_bias.shape[1]
    assert w_in.shape[2] == d_ssd + d_xbc + 2 * n_heads + 3 * dh
    assert n_heads <= LANES and nb + 1 <= 8
    assert seq_len % (GRID_W * 8) == 0 and seq_len % FFT_N2 == 0
    m = nb * seq_len

    crows = _pad_rows(jnp.concatenate([c, c_ctx[None, :]], axis=0), 8)
    mod = _ada(crows, w_ada[0], b_ada[0])
    part = lambda r0, r1, k: mod[r0:r1, k * d:(k + 1) * d][:, None, :]
    sh1, sc1, g1, sh2, sc2, g2 = (part(0, nb, k) for k in range(6))
    csh1 = jnp.broadcast_to(part(nb, nb + 1, 0), (nb, 1, d))
    csc1 = jnp.broadcast_to(part(nb, nb + 1, 1), (nb, 1, d))

    wi = w_in[0]
    o_xbc = d_ssd
    o_dt = o_xbc + d_xbc
    o_hy = o_dt + 2 * n_heads
    w_main = jnp.concatenate([wi[:, :o_dt], wi[:, o_hy:]], axis=1).astype(BF16)
    w_ctx = wi[:, o_xbc:o_dt].astype(BF16)
    w_dt = jnp.concatenate([
        jnp.pad(wi[:, o_dt:o_dt + n_heads], ((0, 0), (0, LANES - n_heads))),
        jnp.pad(wi[:, o_dt + n_heads:o_hy], ((0, 0), (0, LANES - n_heads)))], axis=1).astype(BF16)

    r_emb, c_emb = _sincos_tables(seq_len // GRID_W, GRID_W, d)
    posr = jnp.repeat(r_emb, GRID_W, axis=0)
    nmp = norm_mix_pre[0][None, :]

    tm_in = min(1024, seq_len)
    n_main = w_main.shape[1]
    tn_in = n_main // 4 if (n_main // 4) % LANES == 0 else n_main // 2
    proj, dt_lat = _in_proj(x.reshape(m, d), posr, c_emb, nmp, sh1, sc1, w_main, w_dt, seq_len, tm_in, tn_in, True)
    tm_ctx = min(256, ctx_len)
    xbc_ctx, dt_ctx = _in_proj(ctx.reshape(nb * ctx_len, d), jnp.zeros((8, d // 2), F32), c_emb, nmp, csh1, csc1,
                               w_ctx, w_dt, ctx_len, tm_ctx, 512, False)
    proj3 = proj.reshape(nb, seq_len, -1)

    cw8 = _pad_rows(ssd_conv_w[0], 8)
    cb1 = ssd_conv_b[0][None, :]
    u_lat = _ssd_prep(proj3, d_ssd, d_xbc, cw8, cb1, 512)
    u_ctx = _ssd_prep(xbc_ctx.reshape(nb, ctx_len, d_xbc), 0, d_xbc, cw8, cb1, 512)
    pad_heads = lambda a: jnp.pad(a, ((0, 0), (0, LANES - n_heads)))[:, None, :]
    expand = (jnp.arange(LANES)[:, None] == (jnp.arange(d_ssd)[None, :] // SSD_HEAD_DIM)).astype(BF16)
    y_ssd = _ssd(u_lat, dt_lat.reshape(nb, seq_len, 2 * LANES), u_ctx, dt_ctx.reshape(nb, ctx_len, 2 * LANES),
                 proj3, pad_heads(ssd_a_log[0]), pad_heads(ssd_dt_bias[0]),
                 jnp.repeat(ssd_d[0], SSD_HEAD_DIM)[None, :], ssd_norm[0][None, :], expand)

    x0c, u_hy = _hy_prep(proj3, d_ssd + d_xbc, dh, _pad_rows(hy_conv_w[0], 8), hy_conv_b[0][None, :], 256)
    n_bands = (hy_w1.shape[1] - 1) // 2
    zt = _filter_features_t(seq_len, n_bands)
    w1t = jnp.pad(hy_w1[0].T, ((0, 0), (0, LANES - hy_w1.shape[1])))
    b3 = jnp.stack([hy_b1[0], hy_b2[0], hy_b3[0]], axis=1)
    h3 = _hy_mlp(zt, w1t, hy_w2[0].T, hy_w3[0].T, b3, hy_freq[0][:, None])
    w4 = _pad_rows(hy_w4[0], LANES)
    max_decay = math.log(HY_TARGET) / HY_FAST_PCT
    min_decay = math.log(HY_TARGET) / HY_SLOW_PCT
    deltas = jnp.abs(jnp.linspace(min_decay, max_decay, dh, dtype=F32))[None, :]
    f1, f1t, gf, gi, k1n = _fft_tables(seq_len)
    kspec = _hy_spec(h3, w4[:, :dh], w4[:, dh:], deltas, f1, gf, k1n, 128)
    y_hy = _hy_conv(u_hy, x0c, kspec, f1, f1t, gf, gi, hy_bias[0][None, :], k1n, 256)

    xn, h2 = _out_proj(y_ssd.reshape(m, d_ssd), y_hy.reshape(m, dh), w_out[0].astype(BF16), x.reshape(m, d),
                       posr, c_emb, norm_mix_post[0][None, :], g1, norm_ffn_pre[0][None, :], sh2, sc2, seq_len,
                       min(512, seq_len))
    out = _ffn(h2, w_gate[0].astype(BF16), w_up[0].astype(BF16), w_down[0].astype(BF16), xn,
               norm_ffn_post[0][None, :], g2, seq_len, min(1024, seq_len), 512)
    return out.reshape(nb, seq_len, d)
```

```python
import functools
import math

import numpy as np
import jax
import jax.numpy as jnp
from jax import lax
from jax.experimental import pallas as pl
from jax.experimental.pallas import tpu as pltpu

F32 = jnp.float32
BF16 = jnp.bfloat16
HIGHEST = lax.Precision.HIGHEST

RMS_EPS = 1e-6
POS_THETA = 10000.0
GRID_W = 64
SSD_HEAD_DIM = 64
SSD_GROUPS = 2
SSD_STATE = 128
SSD_CHUNK = 128
HY_TARGET = 1e-2
HY_FAST_PCT = 0.3
HY_SLOW_PCT = 1.5
FFT_N2 = 64
FFT_UNROLL = 8
LANES = 128
MIB = 1024 * 1024
ONE_BUFFER = pl.Buffered(1)
_NT_DIMS = (((1,), (1,)), ((), ()))


def _params(n_axes, vmem_mib):
    return pltpu.CompilerParams(
        dimension_semantics=("arbitrary",) * n_axes,
        vmem_limit_bytes=vmem_mib * MIB,
    )


def _silu(v):
    return v * (1.0 / (1.0 + jnp.exp(-v)))


def _softplus(v):
    return jnp.maximum(v, 0.0) + jnp.log(1.0 + jnp.exp(-jnp.abs(v)))


def _rms(v):
    return v * lax.rsqrt(jnp.mean(v * v, axis=-1, keepdims=True) + RMS_EPS)


def _add_pos(x, posr, cemb):
    tm, d = x.shape
    half = d // 2
    lo = x[:, :half] + posr
    hi = (x[:, half:].reshape(tm // GRID_W, GRID_W, half) + cemb[None]).reshape(tm, half)
    return jnp.concatenate([lo, hi], axis=1)


def _ada_kernel(c_ref, w_ref, b_ref, o_ref):
    s = _silu(c_ref[...])
    o_ref[...] = jnp.dot(s, w_ref[...], precision=HIGHEST, preferred_element_type=F32) + b_ref[...]


def _ada(crows, w_ada, b_ada):
    d, n = w_ada.shape
    tn = 1024
    return pl.pallas_call(
        _ada_kernel,
        grid=(n // tn,),
        in_specs=[
            pl.BlockSpec((8, d), lambda j: (0, 0)),
            pl.BlockSpec((d, tn), lambda j: (0, j)),
            pl.BlockSpec((1, tn), lambda j: (0, j)),
        ],
        out_specs=pl.BlockSpec((8, tn), lambda j: (0, j)),
        out_shape=jax.ShapeDtypeStruct((8, n), F32),
        compiler_params=_params(1, 40),
        name="ada",
    )(crows, w_ada, b_ada.reshape(1, n))


def _inproj_kernel(x_ref, posr_ref, cemb_ref, nw_ref, sh_ref, sc_ref, w_ref, wdt_ref, o_ref, dt_ref, h_scr,
                   *, with_pos):
    @pl.when(pl.program_id(1) == 0)
    def _():
        tm = x_ref.shape[0]
        rows = min(256, tm)
        for r0 in range(0, tm, rows):
            xf = x_ref[r0:r0 + rows, :]
            if with_pos:
                xf = _add_pos(xf, posr_ref[r0:r0 + rows, :], cemb_ref[...])
            h = _rms(xf) * nw_ref[...] * (1.0 + sc_ref[...]) + sh_ref[...]
            hb = h.astype(BF16)
            h_scr[r0:r0 + rows, :] = hb
            dt_ref[r0:r0 + rows, :] = lax.dot_general(hb, wdt_ref[...], _NT_DIMS, preferred_element_type=F32)

    w = w_ref[...].astype(BF16)
    o_ref[...] = lax.dot_general(h_scr[...], w, _NT_DIMS, preferred_element_type=F32).astype(o_ref.dtype)


def _in_proj(x2, posr, cemb, nw, sh, sc, w_t, row_off, n_tiles, wdt_t, seq_len, tm, tn, with_pos):
    m, d = x2.shape
    n = n_tiles * tn
    ndt = wdt_t.shape[0]
    half = d // 2
    tiles_per_seq = seq_len // tm
    pos_rows = tm if with_pos else posr.shape[0]
    return pl.pallas_call(
        functools.partial(_inproj_kernel, with_pos=with_pos),
        grid=(m // tm, n_tiles),
        in_specs=[
            pl.BlockSpec((tm, d), lambda i, j: (i, 0)),
            pl.BlockSpec((pos_rows, half), lambda i, j: ((i % tiles_per_seq) if with_pos else 0, 0)),
            pl.BlockSpec(cemb.shape, lambda i, j: (0, 0), pipeline_mode=ONE_BUFFER),
            pl.BlockSpec((1, d), lambda i, j: (0, 0)),
            pl.BlockSpec((None, 1, d), lambda i, j: (i // tiles_per_seq, 0, 0)),
            pl.BlockSpec((None, 1, d), lambda i, j: (i // tiles_per_seq, 0, 0)),
            pl.BlockSpec((pl.Element(tn), pl.Element(d)), lambda i, j: (pl.multiple_of(row_off(j), 8), 0)),
            pl.BlockSpec((ndt, d), lambda i, j: (0, 0), pipeline_mode=ONE_BUFFER),
        ],
        out_specs=[
            pl.BlockSpec((tm, tn), lambda i, j: (i, j)),
            pl.BlockSpec((tm, ndt), lambda i, j: (i, 0)),
        ],
        out_shape=[
            jax.ShapeDtypeStruct((m, n), BF16),
            jax.ShapeDtypeStruct((m, ndt), F32),
        ],
        scratch_shapes=[pltpu.VMEM((tm, d), BF16)],
        compiler_params=_params(2, 58),
        name="in_proj",
    )(x2, posr, cemb, nw, sh, sc, w_t, wdt_t)


def _conv3_chunk(x_ref, r0, rows, seq_len, w, b):
    cur = x_ref[r0:r0 + rows, :].astype(F32)
    tc = cur.shape[1]
    rid = lax.broadcasted_iota(jnp.int32, (rows, tc), 0)
    if r0 > 0:
        prev_row = x_ref[r0 - 16:r0, :].astype(F32)[15:16, :]
    else:
        prev_row = jnp.zeros((1, tc), F32)
    if r0 + rows < seq_len:
        next_row = x_ref[r0 + rows:r0 + rows + 16, :].astype(F32)[0:1, :]
    else:
        next_row = jnp.zeros((1, tc), F32)
    up = jnp.where(rid == 0, prev_row, pltpu.roll(cur, 1, 0))
    down = jnp.where(rid == rows - 1, next_row, pltpu.roll(cur, rows - 1, 0))
    return up * w[0:1, :] + cur * w[1:2, :] + down * w[2:3, :] + b


def _ssd_prep_kernel(x_ref, w_ref, b_ref, o_ref, *, rows):
    seq_len = x_ref.shape[0]
    w = w_ref[...]
    b = b_ref[...]
    for r0 in range(0, seq_len, rows):
        v = _conv3_chunk(x_ref, r0, rows, seq_len, w, b)
        o_ref[r0:r0 + rows, :] = _silu(v).astype(o_ref.dtype)


def _ssd_prep(proj3, col0, width, w8, b1, tc):
    nb, seq_len, _ = proj3.shape
    rows = min(512, seq_len)
    j0 = col0 // tc
    return pl.pallas_call(
        functools.partial(_ssd_prep_kernel, rows=rows),
        grid=(nb, width // tc),
        in_specs=[
            pl.BlockSpec((None, seq_len, tc), lambda b, j: (b, 0, j0 + j)),
            pl.BlockSpec((8, tc), lambda b, j: (0, j)),
            pl.BlockSpec((1, tc), lambda b, j: (0, j)),
        ],
        out_specs=pl.BlockSpec((None, seq_len, tc), lambda b, j: (b, 0, j)),
        out_shape=jax.ShapeDtypeStruct((nb, seq_len, width), BF16),
        compiler_params=_params(2, 40),
        name="ssd_prep",
    )(proj3, w8, b1)


def _hy_prep_kernel(x0_ref, x1_ref, v_ref, w0_ref, w1_ref, wv_ref, b0_ref, b1_ref, bv_ref,
                    x0c_ref, u_ref, *, rows):
    seq_len = x0_ref.shape[0]
    w0, w1, wv = w0_ref[...], w1_ref[...], wv_ref[...]
    b0, b1, bv = b0_ref[...], b1_ref[...], bv_ref[...]
    for r0 in range(0, seq_len, rows):
        x0c_ref[r0:r0 + rows, :] = _conv3_chunk(x0_ref, r0, rows, seq_len, w0, b0).astype(x0c_ref.dtype)
        x1c = _conv3_chunk(x1_ref, r0, rows, seq_len, w1, b1)
        vc = _conv3_chunk(v_ref, r0, rows, seq_len, wv, bv)
        u_ref[r0:r0 + rows, :] = (vc * x1c).astype(u_ref.dtype)


def _hy_prep(proj3, col0, dh, w8, b1, tc):
    nb, seq_len, _ = proj3.shape
    rows = min(512, seq_len)
    j0 = col0 // tc
    nj = dh // tc
    x_spec = lambda k: pl.BlockSpec((None, seq_len, tc), lambda b, j: (b, 0, j0 + k * nj + j))
    w_spec = lambda k: pl.BlockSpec((8, tc), lambda b, j: (0, k * nj + j))
    b_spec = lambda k: pl.BlockSpec((1, tc), lambda b, j: (0, k * nj + j))
    o_spec = pl.BlockSpec((None, seq_len, tc), lambda b, j: (b, 0, j))
    return pl.pallas_call(
        functools.partial(_hy_prep_kernel, rows=rows),
        grid=(nb, nj),
        in_specs=[x_spec(0), x_spec(1), x_spec(2), w_spec(0), w_spec(1), w_spec(2),
                  b_spec(0), b_spec(1), b_spec(2)],
        out_specs=[o_spec, o_spec],
        out_shape=[jax.ShapeDtypeStruct((nb, seq_len, dh), BF16)] * 2,
        compiler_params=_params(2, 48),
        name="hy_prep",
    )(proj3, proj3, proj3, w8, w8, w8, b1, b1, b1)


def _ssd_kernel(u_ref, dt_ref, uc_ref, dtc_ref, z_ref, alog_ref, dtb_ref, dsk_ref, nw_ref, e_ref,
                o_ref, h_scr, yf_scr, *, n_ctx, n_lat):
    q = SSD_CHUNK
    d_ssd = h_scr.shape[1]
    gw = d_ssd // SSD_GROUPS
    d = pl.program_id(1)
    s = pl.program_id(2)
    is_ctx = s < n_ctx
    t = jnp.maximum(s - n_ctx, 0)
    cidx = jnp.where(d == 0, t, n_lat - 1 - t)

    @pl.when(s == 0)
    def _():
        h_scr[...] = jnp.zeros_like(h_scr)

    u = jnp.where(is_ctx, uc_ref[...], u_ref[...])
    dtr = jnp.where(is_ctx, dtc_ref[...], dt_ref[...])

    dt = _softplus(dtr + dtb_ref[...])
    a = dt * (-jnp.exp(alog_ref[...]))
    row = lax.broadcasted_iota(jnp.int32, (q, q), 0)
    col = lax.broadcasted_iota(jnp.int32, (q, q), 1)
    fwd = d == 0
    tri = jnp.where(fwd, row, col) >= jnp.where(fwd, col, row)
    cum = jnp.dot(tri.astype(F32), a, precision=HIGHEST, preferred_element_type=F32)
    cum_t = cum.T
    tot = jnp.sum(a, axis=0, keepdims=True)

    stack = jnp.concatenate(
        [dt, jnp.exp(cum), jnp.exp(tot - cum), jnp.broadcast_to(jnp.exp(tot), (8, LANES))], axis=0)
    ex = jnp.dot(stack.astype(BF16), e_ref[...], preferred_element_type=F32)
    dt_x = ex[0:q]
    ecum_x = ex[q:2 * q]
    edec_x = ex[2 * q:3 * q]
    etot_x = ex[3 * q:3 * q + 1]

    xs = u[:, :d_ssd].astype(F32)
    xdt = xs * dt_x
    xdt_b = xdt.astype(BF16)
    xdw_b = (xdt * edec_x).astype(BF16)
    lane_lo = lax.broadcasted_iota(jnp.int32, (q, LANES), 1) < SSD_HEAD_DIM
    heads_per_group = gw // SSD_HEAD_DIM

    y_parts = []
    for g in range(SSD_GROUPS):
        bg = u[:, d_ssd + g * SSD_STATE:d_ssd + (g + 1) * SSD_STATE]
        cg = u[:, d_ssd + (SSD_GROUPS + g) * SSD_STATE:d_ssd + (SSD_GROUPS + g + 1) * SSD_STATE]
        scores = lax.dot_general(cg, bg, (((1,), (1,)), ((), ())), preferred_element_type=F32)
        h_prev = h_scr[:, g * gw:(g + 1) * gw]
        y_off = jnp.dot(cg, h_prev.astype(BF16), preferred_element_type=F32)
        for j in range(heads_per_group // 2):
            c0 = g * gw + j * LANES
            xp = xdt_b[:, c0:c0 + LANES]
            acc = None
            for hh in range(2):
                h = g * heads_per_group + 2 * j + hh
                diff = cum[:, h:h + 1] - cum_t[h:h + 1, :]
                decay = jnp.exp(jnp.where(tri, diff, -1e30))
                m_h = (scores * decay).astype(BF16)
                x_h = jnp.where(lane_lo if hh == 0 else jnp.logical_not(lane_lo), xp, jnp.zeros_like(xp))
                part = jnp.dot(m_h, x_h, preferred_element_type=F32)
                acc = part if acc is None else acc + part
            y_parts.append(acc + y_off[:, j * LANES:(j + 1) * LANES] * ecum_x[:, c0:c0 + LANES])
        upd = lax.dot_general(bg, xdw_b[:, g * gw:(g + 1) * gw], (((0,), (0,)), ((), ())),
                              preferred_element_type=F32)
        h_scr[:, g * gw:(g + 1) * gw] = etot_x[:, g * gw:(g + 1) * gw] * h_prev + upd
    y = jnp.concatenate(y_parts, axis=1)

    @pl.when(jnp.logical_and(jnp.logical_not(is_ctx), fwd))
    def _():
        yf_scr[cidx] = y

    @pl.when(jnp.logical_and(jnp.logical_not(is_ctx), jnp.logical_not(fwd)))
    def _():
        ytot = yf_scr[cidx] + y + dsk_ref[...] * xs
        gated = ytot * _silu(z_ref[...].astype(F32))
        o_ref[...] = (_rms(gated) * nw_ref[...]).astype(o_ref.dtype)


def _ssd(u_lat, dt_lat, u_ctx, dt_ctx, proj3, alog, dtb, dsk, nw, expand):
    nb, seq_len, d_xbc = u_lat.shape
    ctx_len = u_ctx.shape[1]
    q = SSD_CHUNK
    n_lat = seq_len // q
    n_ctx = ctx_len // q
    d_ssd = dsk.shape[1]

    def lat_idx(d, s):
        t = jnp.maximum(s - n_ctx, 0)
        return jnp.where(d == 0, t, n_lat - 1 - t)

    def ctx_idx(d, s):
        t = jnp.minimum(s, n_ctx - 1)
        return jnp.where(d == 0, t, n_ctx - 1 - t)

    def out_idx(d, s):
        return jnp.where(d == 0, n_lat - 1, lat_idx(d, s))

    return pl.pallas_call(
        functools.partial(_ssd_kernel, n_ctx=n_ctx, n_lat=n_lat),
        grid=(nb, 2, n_ctx + n_lat),
        in_specs=[
            pl.BlockSpec((None, q, d_xbc), lambda b, d, s: (b, lat_idx(d, s), 0)),
            pl.BlockSpec((None, q, LANES), lambda b, d, s: (b, lat_idx(d, s), d)),
            pl.BlockSpec((None, q, d_xbc), lambda b, d, s: (b, ctx_idx(d, s), 0)),
            pl.BlockSpec((None, q, LANES), lambda b, d, s: (b, ctx_idx(d, s), d)),
            pl.BlockSpec((None, q, d_ssd), lambda b, d, s: (b, lat_idx(d, s), 0)),
            pl.BlockSpec((None, 1, LANES), lambda b, d, s: (d, 0, 0)),
            pl.BlockSpec((None, 1, LANES), lambda b, d, s: (d, 0, 0)),
            pl.BlockSpec((1, d_ssd), lambda b, d, s: (0, 0)),
            pl.BlockSpec((1, d_ssd), lambda b, d, s: (0, 0)),
            pl.BlockSpec((LANES, d_ssd), lambda b, d, s: (0, 0)),
        ],
        out_specs=pl.BlockSpec((None, q, d_ssd), lambda b, d, s: (b, out_idx(d, s), 0)),
        out_shape=jax.ShapeDtypeStruct((nb, seq_len, d_ssd), BF16),
        scratch_shapes=[
            pltpu.VMEM((SSD_STATE, d_ssd), F32),
            pltpu.VMEM((n_lat, q, d_ssd), F32),
        ],
        compiler_params=_params(3, 48),
        name="ssd",
    )(u_lat, dt_lat, u_ctx, dt_ctx, proj3, alog, dtb, dsk, nw, expand)


def _hy_mlp_kernel(zt_ref, w1_ref, w2_ref, w3_ref, b_ref, fr_ref, o_ref):
    fr = fr_ref[...]
    b = b_ref[...]
    h = jnp.sin(fr * (jnp.dot(w1_ref[...], zt_ref[...], precision=HIGHEST, preferred_element_type=F32)
                      + b[:, 0:1]))
    h = jnp.sin(fr * (jnp.dot(w2_ref[...], h, precision=HIGHEST, preferred_element_type=F32) + b[:, 1:2]))
    h = jnp.sin(fr * (jnp.dot(w3_ref[...], h, precision=HIGHEST, preferred_element_type=F32) + b[:, 2:3]))
    hid, seq_len = h.shape
    hp = jnp.concatenate([h, jnp.zeros((LANES - hid, seq_len), F32)], axis=0)
    o_ref[...] = hp.T


def _hy_mlp(zt, w1t, w2t, w3t, b3, fr):
    seq_len = zt.shape[1]
    return pl.pallas_call(
        _hy_mlp_kernel,
        out_shape=jax.ShapeDtypeStruct((seq_len, LANES), F32),
        compiler_params=pltpu.CompilerParams(vmem_limit_bytes=40 * MIB),
        name="hy_mlp",
    )(zt, w1t, w2t, w3t, b3, fr)


def _fft_tables(seq_len):
    n_fft = 2 * seq_len
    n2 = FFT_N2
    n1 = n_fft // n2
    n1h = n1 // 2
    k1n = n1h + 1
    k1p = -(-k1n // 4) * 4
    k1 = np.arange(k1n, dtype=np.int64)
    th = (2.0 * np.pi / n1) * ((k1[:, None] * np.arange(n1h, dtype=np.int64)[None, :]) % n1)
    f1 = np.zeros((2 * k1p, n1h))
    f1[0:2 * k1n:2] = np.cos(th)
    f1[1:2 * k1n:2] = -np.sin(th)
    idx = np.arange(n2, dtype=np.int64)
    kk = k1[:, None, None] + n1 * idx[None, :, None]
    ph = (2.0 * np.pi / n_fft) * ((kk * idx[None, None, :]) % n_fft)
    g_re, g_im = np.cos(ph), -np.sin(ph)

    def blocks(re, im):
        out = np.zeros((k1p, 2 * n2, 2 * n2))
        out[:k1n, :n2, :n2] = re
        out[:k1n, :n2, n2:] = -im
        out[:k1n, n2:, :n2] = im
        out[:k1n, n2:, n2:] = re
        return out

    wgt = np.where((k1 == 0) | (k1 == n1h), 1.0, 2.0) / n_fft
    gf = blocks(g_re, g_im)
    gi = blocks(np.transpose(g_re, (0, 2, 1)) * wgt[:, None, None],
                -np.transpose(g_im, (0, 2, 1)) * wgt[:, None, None])
    as_bf16 = lambda t: jnp.asarray(t.astype(np.float32)).astype(BF16)
    return as_bf16(f1), as_bf16(np.ascontiguousarray(f1.T)), as_bf16(gf), as_bf16(gi), k1n


def _ld(ref, start, size, stride=None):
    idx = pl.ds(start, size) if stride is None else pl.ds(start, size, stride=stride)
    return jnp.concatenate([ref[t, idx, :] for t in range(ref.shape[0])], axis=1)


def _st(ref, start, size, val, stride=None):
    idx = pl.ds(start, size) if stride is None else pl.ds(start, size, stride=stride)
    for t in range(ref.shape[0]):
        ref[t, idx, :] = val[:, t * LANES:(t + 1) * LANES]


def _fft_stage1(x_scr, a_scr, f1_ref):
    rows, n1h = f1_ref.shape

    def body(n2, _):
        xs = _ld(x_scr, n2, n1h, FFT_N2).astype(BF16)
        _st(a_scr, pl.multiple_of(n2 * rows, 8), rows, jnp.dot(f1_ref[...], xs, preferred_element_type=F32))
        return 0

    lax.fori_loop(0, FFT_N2, body, 0, unroll=FFT_UNROLL)


def _fft_stage2_in(a_scr, k1, rows):
    return jnp.concatenate([_ld(a_scr, 2 * k1, FFT_N2, rows), _ld(a_scr, 2 * k1 + 1, FFT_N2, rows)],
                           axis=0).astype(BF16)


def _hy_spec_kernel(h_ref, wf_ref, wb_ref, dl_ref, f1_ref, gf_ref, k_ref, x_scr, a_scr, *, k1n):
    n2 = FFT_N2
    seq_len = h_ref.shape[0]
    tc = wf_ref.shape[1]
    rows = f1_ref.shape[0]
    h3 = h_ref[...].astype(BF16)
    rid = lax.broadcasted_iota(jnp.int32, (seq_len, tc), 0)
    decay = jnp.exp(-(rid.astype(F32) * (1.0 / (seq_len - 1))) * dl_ref[...])
    hf = jnp.dot(h3, wf_ref[...].astype(BF16), preferred_element_type=F32) * decay
    hb = jnp.dot(h3, wb_ref[...].astype(BF16), preferred_element_type=F32) * decay
    norm = jnp.sum(jnp.abs(hf) + jnp.abs(hb), axis=0, keepdims=True) + 1e-6
    inv = 1.0 / norm
    hb = jnp.where(rid == 0, 0.0, hb * inv)
    _st(x_scr, 0, seq_len, jnp.concatenate([hf * inv, hb], axis=1))
    _fft_stage1(x_scr, a_scr, f1_ref)

    def mid(k1, _):
        x = jnp.dot(gf_ref[k1], _fft_stage2_in(a_scr, k1, rows), preferred_element_type=F32)
        xf, xb = x[:, 0:tc], x[:, tc:2 * tc]
        k_ref[k1] = jnp.concatenate([xf[:n2] + xb[:n2], xf[n2:] - xb[n2:]], axis=0)
        return 0

    lax.fori_loop(0, k1n, mid, 0, unroll=FFT_UNROLL)
    for k1 in range(k1n, k_ref.shape[0]):
        k_ref[k1] = jnp.zeros((2 * n2, tc), F32)


def _hy_spec(h3, w4f, w4b, deltas, f1, gf, k1n, tc):
    seq_len = h3.shape[0]
    dh = w4f.shape[1]
    k1p = gf.shape[0]
    rows, n1h = f1.shape
    nt = 2 * tc // LANES
    return pl.pallas_call(
        functools.partial(_hy_spec_kernel, k1n=k1n),
        grid=(dh // tc,),
        in_specs=[
            pl.BlockSpec((seq_len, LANES), lambda j: (0, 0)),
            pl.BlockSpec((LANES, tc), lambda j: (0, j)),
            pl.BlockSpec((LANES, tc), lambda j: (0, j)),
            pl.BlockSpec((1, tc), lambda j: (0, j)),
            pl.BlockSpec((rows, n1h), lambda j: (0, 0)),
            pl.BlockSpec((k1p, 2 * FFT_N2, 2 * FFT_N2), lambda j: (0, 0, 0)),
        ],
        out_specs=pl.BlockSpec((k1p, 2 * FFT_N2, tc), lambda j: (0, 0, j)),
        out_shape=jax.ShapeDtypeStruct((k1p, 2 * FFT_N2, dh), F32),
        scratch_shapes=[pltpu.VMEM((nt, seq_len, LANES), F32), pltpu.VMEM((nt, FFT_N2 * rows, LANES), F32)],
        compiler_params=_params(1, 56),
        name="hy_spec",
    )(h3, w4f, w4b, deltas, f1, gf)


def _hy_conv_kernel(u_ref, x0_ref, k_ref, f1_ref, f1t_ref, gf_ref, gi_ref, bias_ref, o_ref, x_scr, a_scr, *, k1n):
    n2 = FFT_N2
    seq_len = u_ref.shape[0]
    rows, n1h = f1_ref.shape
    ch = min(512, seq_len)
    for r0 in range(0, seq_len, ch):
        _st(x_scr, r0, ch, u_ref[r0:r0 + ch, :].astype(F32))
    _fft_stage1(x_scr, a_scr, f1_ref)

    def mid(k1, _):
        x = jnp.dot(gf_ref[k1], _fft_stage2_in(a_scr, k1, rows), preferred_element_type=F32)
        kk = k_ref[k1]
        xr, xi, kr, ki = x[:n2], x[n2:], kk[:n2], kk[n2:]
        y = jnp.concatenate([xr * kr - xi * ki, xr * ki + xi * kr], axis=0).astype(BF16)
        bt = jnp.dot(gi_ref[k1], y, preferred_element_type=F32)
        _st(a_scr, 2 * k1, n2, bt[:n2], rows)
        _st(a_scr, 2 * k1 + 1, n2, bt[n2:], rows)
        return 0

    lax.fori_loop(0, k1n, mid, 0, unroll=FFT_UNROLL)

    def last(j, _):
        blk = _ld(a_scr, pl.multiple_of(j * rows, 8), rows).astype(BF16)
        _st(x_scr, j, n1h, jnp.dot(f1t_ref[...], blk, preferred_element_type=F32), n2)
        return 0

    lax.fori_loop(0, n2, last, 0, unroll=FFT_UNROLL)
    for r0 in range(0, seq_len, ch):
        v = _ld(x_scr, r0, ch) + u_ref[r0:r0 + ch, :].astype(F32) * bias_ref[...]
        o_ref[r0:r0 + ch, :] = (x0_ref[r0:r0 + ch, :].astype(F32) * v).astype(o_ref.dtype)


def _hy_conv(u3, x0c, kspec, f1, f1t, gf, gi, bias, k1n, tc):
    nb, seq_len, dh = u3.shape
    k1p = gf.shape[0]
    rows, n1h = f1.shape
    nt = tc // LANES
    blk = pl.BlockSpec((None, seq_len, tc), lambda j, b: (b, 0, j))
    const = lambda shape: pl.BlockSpec(shape, lambda j, b: (0,) * len(shape), pipeline_mode=ONE_BUFFER)
    return pl.pallas_call(
        functools.partial(_hy_conv_kernel, k1n=k1n),
        grid=(dh // tc, nb),
        in_specs=[
            blk, blk,
            pl.BlockSpec((k1p, 2 * FFT_N2, tc), lambda j, b: (0, 0, j), pipeline_mode=ONE_BUFFER),
            const((rows, n1h)), const((n1h, rows)),
            const((k1p, 2 * FFT_N2, 2 * FFT_N2)), const((k1p, 2 * FFT_N2, 2 * FFT_N2)),
            pl.BlockSpec((1, tc), lambda j, b: (0, j)),
        ],
        out_specs=blk,
        out_shape=jax.ShapeDtypeStruct((nb, seq_len, dh), BF16),
        scratch_shapes=[pltpu.VMEM((nt, seq_len, LANES), F32), pltpu.VMEM((nt, FFT_N2 * rows, LANES), F32)],
        compiler_params=_params(2, 56),
        name="hy_conv",
    )(u3, x0c, kspec, f1, f1t, gf, gi, bias)


def _outproj_kernel(ys_ref, yh_ref, w_ref, x_ref, posr_ref, cemb_ref, nwp_ref, g_ref, nwf_ref, sh_ref, sc_ref,
                    xo_ref, h_ref):
    ds = ys_ref.shape[1]
    y = jnp.dot(ys_ref[...], w_ref[0:ds, :], preferred_element_type=F32)
    y = y + jnp.dot(yh_ref[...], w_ref[ds:, :], preferred_element_type=F32)
    xn = _add_pos(x_ref[...], posr_ref[...], cemb_ref[...]) + g_ref[...] * (_rms(y) * nwp_ref[...])
    xo_ref[...] = xn
    h_ref[...] = (_rms(xn) * nwf_ref[...] * (1.0 + sc_ref[...]) + sh_ref[...]).astype(h_ref.dtype)


def _out_proj(ys, yh, w, x2, posr, cemb, nwp, g1, nwf, sh2, sc2, seq_len, tm):
    m, d = x2.shape
    ds = ys.shape[1]
    dh = yh.shape[1]
    tiles_per_seq = seq_len // tm
    row = lambda i: (i, 0)
    fixed = lambda i: (0, 0)
    per_batch = lambda i: (i // tiles_per_seq, 0, 0)
    return pl.pallas_call(
        _outproj_kernel,
        grid=(m // tm,),
        in_specs=[
            pl.BlockSpec((tm, ds), row),
            pl.BlockSpec((tm, dh), row),
            pl.BlockSpec((ds + dh, d), fixed, pipeline_mode=ONE_BUFFER),
            pl.BlockSpec((tm, d), row),
            pl.BlockSpec((tm, d // 2), lambda i: (i % tiles_per_seq, 0)),
            pl.BlockSpec(cemb.shape, fixed),
            pl.BlockSpec((1, d), fixed),
            pl.BlockSpec((None, 1, d), per_batch),
            pl.BlockSpec((1, d), fixed),
            pl.BlockSpec((None, 1, d), per_batch),
            pl.BlockSpec((None, 1, d), per_batch),
        ],
        out_specs=[pl.BlockSpec((tm, d), row), pl.BlockSpec((tm, d), row)],
        out_shape=[jax.ShapeDtypeStruct((m, d), F32), jax.ShapeDtypeStruct((m, d), BF16)],
        compiler_params=_params(1, 56),
        name="out_proj",
    )(ys, yh, w, x2, posr, cemb, nwp, g1, nwf, sh2, sc2)


def _ffn_kernel(h_ref, wg_ref, wu_ref, wd_ref, x_ref, nw_ref, g_ref, o_ref):
    f = pl.program_id(1)
    h = h_ref[...]
    gate = jnp.dot(h, wg_ref[...], preferred_element_type=F32)
    up = jnp.dot(h, wu_ref[...], preferred_element_type=F32)
    act = (_silu(gate) * up).astype(BF16)

    @pl.when(f == 0)
    def _():
        o_ref[...] = jnp.zeros_like(o_ref)

    d = o_ref.shape[1]
    cw = min(512, d)
    for n0 in range(0, d, cw):
        o_ref[:, n0:n0 + cw] += jnp.dot(act, wd_ref[:, n0:n0 + cw], preferred_element_type=F32)

    @pl.when(f == pl.num_programs(1) - 1)
    def _():
        tm = o_ref.shape[0]
        rows = min(256, tm)
        for r0 in range(0, tm, rows):
            y = o_ref[r0:r0 + rows, :]
            o_ref[r0:r0 + rows, :] = x_ref[r0:r0 + rows, :] + g_ref[...] * (_rms(y) * nw_ref[...])


def _ffn(h2, wg, wu, wd, xn, nw, g2, seq_len, tm, tf):
    m, d = xn.shape
    dff = wg.shape[1]
    tiles_per_seq = seq_len // tm
    return pl.pallas_call(
        _ffn_kernel,
        grid=(m // tm, dff // tf),
        in_specs=[
            pl.BlockSpec((tm, d), lambda i, f: (i, 0)),
            pl.BlockSpec((d, tf), lambda i, f: (0, f)),
            pl.BlockSpec((d, tf), lambda i, f: (0, f)),
            pl.BlockSpec((tf, d), lambda i, f: (f, 0)),
            pl.BlockSpec((tm, d), lambda i, f: (i, 0), pipeline_mode=ONE_BUFFER),
            pl.BlockSpec((1, d), lambda i, f: (0, 0)),
            pl.BlockSpec((None, 1, d), lambda i, f: (i // tiles_per_seq, 0, 0)),
        ],
        out_specs=pl.BlockSpec((tm, d), lambda i, f: (i, 0)),
        out_shape=jax.ShapeDtypeStruct((m, d), F32),
        compiler_params=_params(2, 58),
        name="ffn",
    )(h2, wg, wu, wd, xn, nw, g2)


def _sincos_tables(rows, cols, dim):
    qd = dim // 4
    omega = 1.0 / (POS_THETA ** (jnp.arange(qd, dtype=F32) / qd))
    r = jnp.arange(rows, dtype=F32)[:, None] * omega
    cc = jnp.arange(cols, dtype=F32)[:, None] * omega
    r_emb = jnp.concatenate([jnp.sin(r), jnp.cos(r)], -1)
    c_emb = jnp.concatenate([jnp.sin(cc), jnp.cos(cc)], -1)
    return r_emb, c_emb


def _filter_features_t(seq_len, n_bands):
    t = jnp.linspace(0.0, 1.0, seq_len, dtype=F32)[:, None]
    w = 2.0 * math.pi * jnp.arange(seq_len, dtype=F32)[:, None] / seq_len
    fb = jnp.linspace(1e-4, n_bands - 1, n_bands, dtype=F32)[None]
    zpos = jnp.concatenate([t, jnp.cos(fb * w), -jnp.sin(fb * w)], -1)
    emb = zpos.shape[1]
    return jnp.pad(zpos, ((0, 0), (0, LANES - emb))).T


def _pad_rows(a, rows):
    return jnp.pad(a, ((0, rows - a.shape[0]), (0, 0)))


def kernel(x, c, ctx, c_ctx, w_ada, b_ada, norm_mix_pre, norm_mix_post, norm_ffn_pre, norm_ffn_post,
           w_in, ssd_conv_w, ssd_conv_b, ssd_a_log, ssd_dt_bias, ssd_d, ssd_norm,
           hy_conv_w, hy_conv_b, hy_w1, hy_b1, hy_w2, hy_b2, hy_w3, hy_b3, hy_w4, hy_freq, hy_bias,
           w_out, w_gate, w_up, w_down):
    nb, seq_len, d = x.shape
    ctx_len = ctx.shape[1]
    assert w_ada.shape[0] == 1, "single layer"
    n_heads = ssd_d.shape[1]
    d_ssd = n_heads * SSD_HEAD_DIM
    d_xbc = d_ssd + 2 * SSD_GROUPS * SSD_STATE
    dh = hy_bias.shape[1]
    assert w_in.shape[2] == d_ssd + d_xbc + 2 * n_heads + 3 * dh
    assert n_heads <= LANES and nb + 1 <= 8
    assert seq_len % (GRID_W * 8) == 0 and seq_len % FFT_N2 == 0
    m = nb * seq_len

    crows = _pad_rows(jnp.concatenate([c, c_ctx[None, :]], axis=0), 8)
    mod = _ada(crows, w_ada[0], b_ada[0])
    part = lambda r0, r1, k: mod[r0:r1, k * d:(k + 1) * d][:, None, :]
    sh1, sc1, g1, sh2, sc2, g2 = (part(0, nb, k) for k in range(6))
    csh1 = jnp.broadcast_to(part(nb, nb + 1, 0), (nb, 1, d))
    csc1 = jnp.broadcast_to(part(nb, nb + 1, 1), (nb, 1, d))

    w_t = jnp.transpose(w_in[0])
    o_xbc = d_ssd
    o_dt = o_xbc + d_xbc
    o_hy = o_dt + 2 * n_heads
    tn_in = 512
    assert o_dt % tn_in == 0 and (3 * dh) % tn_in == 0 and o_xbc % tn_in == 0 and o_hy % 16 == 0
    n_left = o_dt // tn_in
    main_off = lambda j: jnp.where(j < n_left, j * tn_in, o_hy + (j - n_left) * tn_in)
    ctx_off = lambda j: o_xbc + j * tn_in
    pad_dt = lambda rows: jnp.pad(rows, ((0, LANES - n_heads), (0, 0)))
    w_dt_t = jnp.concatenate([pad_dt(w_t[o_dt:o_dt + n_heads]), pad_dt(w_t[o_dt + n_heads:o_hy])],
                             axis=0).astype(BF16)

    r_emb, c_emb = _sincos_tables(seq_len // GRID_W, GRID_W, d)
    posr = jnp.repeat(r_emb, GRID_W, axis=0)
    nmp = norm_mix_pre[0][None, :]

    tm_in = min(1024, seq_len)
    proj, dt_lat = _in_proj(x.reshape(m, d), posr, c_emb, nmp, sh1, sc1, w_t, main_off, (o_dt + 3 * dh) // tn_in,
                            w_dt_t, seq_len, tm_in, tn_in, True)
    tm_ctx = min(256, ctx_len)
    xbc_ctx, dt_ctx = _in_proj(ctx.reshape(nb * ctx_len, d), jnp.zeros((8, d // 2), F32), c_emb, nmp, csh1, csc1,
                               w_t, ctx_off, d_xbc // tn_in, w_dt_t, ctx_len, tm_ctx, tn_in, False)
    proj3 = proj.reshape(nb, seq_len, -1)

    cw8 = _pad_rows(ssd_conv_w[0], 8)
    cb1 = ssd_conv_b[0][None, :]
    u_lat = _ssd_prep(proj3, d_ssd, d_xbc, cw8, cb1, 512)
    u_ctx = _ssd_prep(xbc_ctx.reshape(nb, ctx_len, d_xbc), 0, d_xbc, cw8, cb1, 512)
    pad_heads = lambda a: jnp.pad(a, ((0, 0), (0, LANES - n_heads)))[:, None, :]
    expand = (jnp.arange(LANES)[:, None] == (jnp.arange(d_ssd)[None, :] // SSD_HEAD_DIM)).astype(BF16)
    y_ssd = _ssd(u_lat, dt_lat.reshape(nb, seq_len, 2 * LANES), u_ctx, dt_ctx.reshape(nb, ctx_len, 2 * LANES),
                 proj3, pad_heads(ssd_a_log[0]), pad_heads(ssd_dt_bias[0]),
                 jnp.repeat(ssd_d[0], SSD_HEAD_DIM)[None, :], ssd_norm[0][None, :], expand)

    x0c, u_hy = _hy_prep(proj3, d_ssd + d_xbc, dh, _pad_rows(hy_conv_w[0], 8), hy_conv_b[0][None, :], 256)
    n_bands = (hy_w1.shape[1] - 1) // 2
    zt = _filter_features_t(seq_len, n_bands)
    w1t = jnp.pad(hy_w1[0].T, ((0, 0), (0, LANES - hy_w1.shape[1])))
    b3 = jnp.stack([hy_b1[0], hy_b2[0], hy_b3[0]], axis=1)
    h3 = _hy_mlp(zt, w1t, hy_w2[0].T, hy_w3[0].T, b3, hy_freq[0][:, None])
    w4 = _pad_rows(hy_w4[0], LANES)
    max_decay = math.log(HY_TARGET) / HY_FAST_PCT
    min_decay = math.log(HY_TARGET) / HY_SLOW_PCT
    deltas = jnp.abs(jnp.linspace(min_decay, max_decay, dh, dtype=F32))[None, :]
    f1, f1t, gf, gi, k1n = _fft_tables(seq_len)
    kspec = _hy_spec(h3, w4[:, :dh], w4[:, dh:], deltas, f1, gf, k1n, 128)
    y_hy = _hy_conv(u_hy, x0c, kspec, f1, f1t, gf, gi, hy_bias[0][None, :], k1n, 256)

    xn, h2 = _out_proj(y_ssd.reshape(m, d_ssd), y_hy.reshape(m, dh), w_out[0].astype(BF16), x.reshape(m, d),
                       posr, c_emb, norm_mix_post[0][None, :], g1, norm_ffn_pre[0][None, :], sh2, sc2, seq_len,
                       min(512, seq_len))
    out = _ffn(h2, w_gate[0].astype(BF16), w_up[0].astype(BF16), w_down[0].astype(BF16), xn,
               norm_ffn_post[0][None, :], g2, seq_len, min(1024, seq_len), 512)
    return out.reshape(nb, seq_len, d)
```

```python
import functools
import math

import numpy as np
import jax
import jax.numpy as jnp
from jax import lax
from jax.experimental import pallas as pl
from jax.experimental.pallas import tpu as pltpu

F32 = jnp.float32
BF16 = jnp.bfloat16
HIGHEST = lax.Precision.HIGHEST

RMS_EPS = 1e-6
POS_THETA = 10000.0
GRID_W = 64
SSD_HEAD_DIM = 64
SSD_GROUPS = 2
SSD_STATE = 128
SSD_CHUNK = 128
HY_TARGET = 1e-2
HY_FAST_PCT = 0.3
HY_SLOW_PCT = 1.5
FFT_N2 = 64
FFT_UNROLL = 8
LANES = 128
MIB = 1024 * 1024
ONE_BUFFER = pl.Buffered(1)
_NT_DIMS = (((1,), (1,)), ((), ()))


def _params(n_axes, vmem_mib):
    return pltpu.CompilerParams(
        dimension_semantics=("arbitrary",) * n_axes,
        vmem_limit_bytes=vmem_mib * MIB,
    )


def _silu(v):
    return v * (1.0 / (1.0 + jnp.exp(-v)))


def _softplus(v):
    return jnp.maximum(v, 0.0) + jnp.log(1.0 + jnp.exp(-jnp.abs(v)))


def _rms(v):
    return v * lax.rsqrt(jnp.mean(v * v, axis=-1, keepdims=True) + RMS_EPS)


def _add_pos(x, posr, cemb):
    tm, d = x.shape
    half = d // 2
    lo = x[:, :half] + posr
    hi = (x[:, half:].reshape(tm // GRID_W, GRID_W, half) + cemb[None]).reshape(tm, half)
    return jnp.concatenate([lo, hi], axis=1)


def _ada_kernel(c_ref, w_ref, b_ref, o_ref):
    s = _silu(c_ref[...])
    o_ref[...] = jnp.dot(s, w_ref[...], precision=HIGHEST, preferred_element_type=F32) + b_ref[...]


def _ada(crows, w_ada, b_ada):
    d, n = w_ada.shape
    tn = 1024
    return pl.pallas_call(
        _ada_kernel,
        grid=(n // tn,),
        in_specs=[
            pl.BlockSpec((8, d), lambda j: (0, 0)),
            pl.BlockSpec((d, tn), lambda j: (0, j)),
            pl.BlockSpec((1, tn), lambda j: (0, j)),
        ],
        out_specs=pl.BlockSpec((8, tn), lambda j: (0, j)),
        out_shape=jax.ShapeDtypeStruct((8, n), F32),
        compiler_params=_params(1, 40),
        name="ada",
    )(crows, w_ada, b_ada.reshape(1, n))


def _inproj_kernel(x_ref, posr_ref, cemb_ref, nw_ref, sh_ref, sc_ref, w_ref, wdt_ref, o_ref, dt_ref, h_scr,
                   *, with_pos):
    @pl.when(pl.program_id(1) == 0)
    def _():
        tm = x_ref.shape[0]
        rows = min(256, tm)
        for r0 in range(0, tm, rows):
            xf = x_ref[r0:r0 + rows, :]
            if with_pos:
                xf = _add_pos(xf, posr_ref[r0:r0 + rows, :], cemb_ref[...])
            h = _rms(xf) * nw_ref[...] * (1.0 + sc_ref[...]) + sh_ref[...]
            hb = h.astype(BF16)
            h_scr[r0:r0 + rows, :] = hb
            dt_ref[r0:r0 + rows, :] = lax.dot_general(hb, wdt_ref[...], _NT_DIMS, preferred_element_type=F32)

    w = w_ref[...].astype(BF16)
    o_ref[...] = lax.dot_general(h_scr[...], w, _NT_DIMS, preferred_element_type=F32).astype(o_ref.dtype)


def _in_proj(x2, posr, cemb, nw, sh, sc, w_t, row_off, n_tiles, wdt_t, seq_len, tm, tn, with_pos):
    m, d = x2.shape
    n = n_tiles * tn
    ndt = wdt_t.shape[0]
    half = d // 2
    tiles_per_seq = seq_len // tm
    pos_rows = tm if with_pos else posr.shape[0]
    return pl.pallas_call(
        functools.partial(_inproj_kernel, with_pos=with_pos),
        grid=(m // tm, n_tiles),
        in_specs=[
            pl.BlockSpec((tm, d), lambda i, j: (i, 0)),
            pl.BlockSpec((pos_rows, half), lambda i, j: ((i % tiles_per_seq) if with_pos else 0, 0)),
            pl.BlockSpec(cemb.shape, lambda i, j: (0, 0), pipeline_mode=ONE_BUFFER),
            pl.BlockSpec((1, d), lambda i, j: (0, 0)),
            pl.BlockSpec((None, 1, d), lambda i, j: (i // tiles_per_seq, 0, 0)),
            pl.BlockSpec((None, 1, d), lambda i, j: (i // tiles_per_seq, 0, 0)),
            pl.BlockSpec((pl.Element(tn), pl.Element(d)), lambda i, j: (pl.multiple_of(row_off(j), 8), 0)),
            pl.BlockSpec((ndt, d), lambda i, j: (0, 0), pipeline_mode=ONE_BUFFER),
        ],
        out_specs=[
            pl.BlockSpec((tm, tn), lambda i, j: (i, j)),
            pl.BlockSpec((tm, ndt), lambda i, j: (i, 0)),
        ],
        out_shape=[
            jax.ShapeDtypeStruct((m, n), BF16),
            jax.ShapeDtypeStruct((m, ndt), F32),
        ],
        scratch_shapes=[pltpu.VMEM((tm, d), BF16)],
        compiler_params=_params(2, 58),
        name="in_proj",
    )(x2, posr, cemb, nw, sh, sc, w_t, wdt_t)


def _conv3_chunk(x_ref, r0, rows, seq_len, w, b):
    cur = x_ref[r0:r0 + rows, :].astype(F32)
    tc = cur.shape[1]
    rid = lax.broadcasted_iota(jnp.int32, (rows, tc), 0)
    if r0 > 0:
        prev_row = x_ref[r0 - 16:r0, :].astype(F32)[15:16, :]
    else:
        prev_row = jnp.zeros((1, tc), F32)
    if r0 + rows < seq_len:
        next_row = x_ref[r0 + rows:r0 + rows + 16, :].astype(F32)[0:1, :]
    else:
        next_row = jnp.zeros((1, tc), F32)
    up = jnp.where(rid == 0, prev_row, pltpu.roll(cur, 1, 0))
    down = jnp.where(rid == rows - 1, next_row, pltpu.roll(cur, rows - 1, 0))
    return up * w[0:1, :] + cur * w[1:2, :] + down * w[2:3, :] + b


def _ssd_prep_kernel(x_ref, w_ref, b_ref, o_ref, *, rows):
    seq_len = x_ref.shape[0]
    w = w_ref[...]
    b = b_ref[...]
    for r0 in range(0, seq_len, rows):
        v = _conv3_chunk(x_ref, r0, rows, seq_len, w, b)
        o_ref[r0:r0 + rows, :] = _silu(v).astype(o_ref.dtype)


def _ssd_prep(proj3, col0, width, w8, b1, tc):
    nb, seq_len, _ = proj3.shape
    rows = min(512, seq_len)
    j0 = col0 // tc
    return pl.pallas_call(
        functools.partial(_ssd_prep_kernel, rows=rows),
        grid=(nb, width // tc),
        in_specs=[
            pl.BlockSpec((None, seq_len, tc), lambda b, j: (b, 0, j0 + j)),
            pl.BlockSpec((8, tc), lambda b, j: (0, j)),
            pl.BlockSpec((1, tc), lambda b, j: (0, j)),
        ],
        out_specs=pl.BlockSpec((None, seq_len, tc), lambda b, j: (b, 0, j)),
        out_shape=jax.ShapeDtypeStruct((nb, seq_len, width), BF16),
        compiler_params=_params(2, 40),
        name="ssd_prep",
    )(proj3, w8, b1)


def _hy_prep_kernel(x0_ref, x1_ref, v_ref, w0_ref, w1_ref, wv_ref, b0_ref, b1_ref, bv_ref,
                    x0c_ref, u_ref, *, rows):
    seq_len = x0_ref.shape[0]
    w0, w1, wv = w0_ref[...], w1_ref[...], wv_ref[...]
    b0, b1, bv = b0_ref[...], b1_ref[...], bv_ref[...]
    for r0 in range(0, seq_len, rows):
        x0c_ref[r0:r0 + rows, :] = _conv3_chunk(x0_ref, r0, rows, seq_len, w0, b0).astype(x0c_ref.dtype)
        x1c = _conv3_chunk(x1_ref, r0, rows, seq_len, w1, b1)
        vc = _conv3_chunk(v_ref, r0, rows, seq_len, wv, bv)
        u_ref[r0:r0 + rows, :] = (vc * x1c).astype(u_ref.dtype)


def _hy_prep(proj3, col0, dh, w8, b1, tc):
    nb, seq_len, _ = proj3.shape
    rows = min(512, seq_len)
    j0 = col0 // tc
    nj = dh // tc
    x_spec = lambda k: pl.BlockSpec((None, seq_len, tc), lambda b, j: (b, 0, j0 + k * nj + j))
    w_spec = lambda k: pl.BlockSpec((8, tc), lambda b, j: (0, k * nj + j))
    b_spec = lambda k: pl.BlockSpec((1, tc), lambda b, j: (0, k * nj + j))
    o_spec = pl.BlockSpec((None, seq_len, tc), lambda b, j: (b, 0, j))
    return pl.pallas_call(
        functools.partial(_hy_prep_kernel, rows=rows),
        grid=(nb, nj),
        in_specs=[x_spec(0), x_spec(1), x_spec(2), w_spec(0), w_spec(1), w_spec(2),
                  b_spec(0), b_spec(1), b_spec(2)],
        out_specs=[o_spec, o_spec],
        out_shape=[jax.ShapeDtypeStruct((nb, seq_len, dh), BF16)] * 2,
        compiler_params=_params(2, 48),
        name="hy_prep",
    )(proj3, proj3, proj3, w8, w8, w8, b1, b1, b1)


def _ssd_kernel(u_ref, dt_ref, uc_ref, dtc_ref, z_ref, alog_ref, dtb_ref, dsk_ref, nw_ref, e_ref,
                o_ref, h_scr, yf_scr, *, n_ctx, n_lat):
    q = SSD_CHUNK
    d_ssd = h_scr.shape[1]
    gw = d_ssd // SSD_GROUPS
    d = pl.program_id(1)
    s = pl.program_id(2)
    is_ctx = s < n_ctx
    t = jnp.maximum(s - n_ctx, 0)
    cidx = jnp.where(d == 0, t, n_lat - 1 - t)

    @pl.when(s == 0)
    def _():
        h_scr[...] = jnp.zeros_like(h_scr)

    u = jnp.where(is_ctx, uc_ref[...], u_ref[...])
    dtr = jnp.where(is_ctx, dtc_ref[...], dt_ref[...])

    dt = _softplus(dtr + dtb_ref[...])
    a = dt * (-jnp.exp(alog_ref[...]))
    row = lax.broadcasted_iota(jnp.int32, (q, q), 0)
    col = lax.broadcasted_iota(jnp.int32, (q, q), 1)
    fwd = d == 0
    tri = jnp.where(fwd, row, col) >= jnp.where(fwd, col, row)
    cum = jnp.dot(tri.astype(F32), a, precision=HIGHEST, preferred_element_type=F32)
    cum_t = cum.T
    tot = jnp.sum(a, axis=0, keepdims=True)

    stack = jnp.concatenate(
        [dt, jnp.exp(cum), jnp.exp(tot - cum), jnp.broadcast_to(jnp.exp(tot), (8, LANES))], axis=0)
    ex = jnp.dot(stack.astype(BF16), e_ref[...], preferred_element_type=F32)
    dt_x = ex[0:q]
    ecum_x = ex[q:2 * q]
    edec_x = ex[2 * q:3 * q]
    etot_x = ex[3 * q:3 * q + 1]

    xs = u[:, :d_ssd].astype(F32)
    xdt = xs * dt_x
    xdt_b = xdt.astype(BF16)
    xdw_b = (xdt * edec_x).astype(BF16)
    lane_lo = lax.broadcasted_iota(jnp.int32, (q, LANES), 1) < SSD_HEAD_DIM
    heads_per_group = gw // SSD_HEAD_DIM

    y_parts = []
    for g in range(SSD_GROUPS):
        bg = u[:, d_ssd + g * SSD_STATE:d_ssd + (g + 1) * SSD_STATE]
        cg = u[:, d_ssd + (SSD_GROUPS + g) * SSD_STATE:d_ssd + (SSD_GROUPS + g + 1) * SSD_STATE]
        scores = lax.dot_general(cg, bg, (((1,), (1,)), ((), ())), preferred_element_type=F32)
        h_prev = h_scr[:, g * gw:(g + 1) * gw]
        y_off = jnp.dot(cg, h_prev.astype(BF16), preferred_element_type=F32)
        for j in range(heads_per_group // 2):
            c0 = g * gw + j * LANES
            xp = xdt_b[:, c0:c0 + LANES]
            acc = None
            for hh in range(2):
                h = g * heads_per_group + 2 * j + hh
                diff = cum[:, h:h + 1] - cum_t[h:h + 1, :]
                decay = jnp.exp(jnp.where(tri, diff, -1e30))
                m_h = (scores * decay).astype(BF16)
                x_h = jnp.where(lane_lo if hh == 0 else jnp.logical_not(lane_lo), xp, jnp.zeros_like(xp))
                part = jnp.dot(m_h, x_h, preferred_element_type=F32)
                acc = part if acc is None else acc + part
            y_parts.append(acc + y_off[:, j * LANES:(j + 1) * LANES] * ecum_x[:, c0:c0 + LANES])
        upd = lax.dot_general(bg, xdw_b[:, g * gw:(g + 1) * gw], (((0,), (0,)), ((), ())),
                              preferred_element_type=F32)
        h_scr[:, g * gw:(g + 1) * gw] = etot_x[:, g * gw:(g + 1) * gw] * h_prev + upd
    y = jnp.concatenate(y_parts, axis=1)

    @pl.when(jnp.logical_and(jnp.logical_not(is_ctx), fwd))
    def _():
        yf_scr[cidx] = y

    @pl.when(jnp.logical_and(jnp.logical_not(is_ctx), jnp.logical_not(fwd)))
    def _():
        ytot = yf_scr[cidx] + y + dsk_ref[...] * xs
        gated = ytot * _silu(z_ref[...].astype(F32))
        o_ref[...] = (_rms(gated) * nw_ref[...]).astype(o_ref.dtype)


def _ssd(u_lat, dt_lat, u_ctx, dt_ctx, proj3, alog, dtb, dsk, nw, expand):
    nb, seq_len, d_xbc = u_lat.shape
    ctx_len = u_ctx.shape[1]
    q = SSD_CHUNK
    n_lat = seq_len // q
    n_ctx = ctx_len // q
    d_ssd = dsk.shape[1]

    def lat_idx(d, s):
        t = jnp.maximum(s - n_ctx, 0)
        return jnp.where(d == 0, t, n_lat - 1 - t)

    def ctx_idx(d, s):
        t = jnp.minimum(s, n_ctx - 1)
        return jnp.where(d == 0, t, n_ctx - 1 - t)

    def out_idx(d, s):
        return jnp.where(d == 0, n_lat - 1, lat_idx(d, s))

    return pl.pallas_call(
        functools.partial(_ssd_kernel, n_ctx=n_ctx, n_lat=n_lat),
        grid=(nb, 2, n_ctx + n_lat),
        in_specs=[
            pl.BlockSpec((None, q, d_xbc), lambda b, d, s: (b, lat_idx(d, s), 0)),
            pl.BlockSpec((None, q, LANES), lambda b, d, s: (b, lat_idx(d, s), d)),
            pl.BlockSpec((None, q, d_xbc), lambda b, d, s: (b, ctx_idx(d, s), 0)),
            pl.BlockSpec((None, q, LANES), lambda b, d, s: (b, ctx_idx(d, s), d)),
            pl.BlockSpec((None, q, d_ssd), lambda b, d, s: (b, lat_idx(d, s), 0)),
            pl.BlockSpec((None, 1, LANES), lambda b, d, s: (d, 0, 0)),
            pl.BlockSpec((None, 1, LANES), lambda b, d, s: (d, 0, 0)),
            pl.BlockSpec((1, d_ssd), lambda b, d, s: (0, 0)),
            pl.BlockSpec((1, d_ssd), lambda b, d, s: (0, 0)),
            pl.BlockSpec((LANES, d_ssd), lambda b, d, s: (0, 0)),
        ],
        out_specs=pl.BlockSpec((None, q, d_ssd), lambda b, d, s: (b, out_idx(d, s), 0)),
        out_shape=jax.ShapeDtypeStruct((nb, seq_len, d_ssd), BF16),
        scratch_shapes=[
            pltpu.VMEM((SSD_STATE, d_ssd), F32),
            pltpu.VMEM((n_lat, q, d_ssd), F32),
        ],
        compiler_params=_params(3, 48),
        name="ssd",
    )(u_lat, dt_lat, u_ctx, dt_ctx, proj3, alog, dtb, dsk, nw, expand)


def _hy_mlp_kernel(zt_ref, w1_ref, w2_ref, w3_ref, b_ref, fr_ref, o_ref):
    fr = fr_ref[...]
    b = b_ref[...]
    h = jnp.sin(fr * (jnp.dot(w1_ref[...], zt_ref[...], precision=HIGHEST, preferred_element_type=F32)
                      + b[:, 0:1]))
    h = jnp.sin(fr * (jnp.dot(w2_ref[...], h, precision=HIGHEST, preferred_element_type=F32) + b[:, 1:2]))
    h = jnp.sin(fr * (jnp.dot(w3_ref[...], h, precision=HIGHEST, preferred_element_type=F32) + b[:, 2:3]))
    hid, seq_len = h.shape
    hp = jnp.concatenate([h, jnp.zeros((LANES - hid, seq_len), F32)], axis=0)
    o_ref[...] = hp.T


def _hy_mlp(zt, w1t, w2t, w3t, b3, fr):
    seq_len = zt.shape[1]
    return pl.pallas_call(
        _hy_mlp_kernel,
        out_shape=jax.ShapeDtypeStruct((seq_len, LANES), F32),
        compiler_params=pltpu.CompilerParams(vmem_limit_bytes=40 * MIB),
        name="hy_mlp",
    )(zt, w1t, w2t, w3t, b3, fr)


def _fft_tables(seq_len):
    n_fft = 2 * seq_len
    n2 = FFT_N2
    n1 = n_fft // n2
    n1h = n1 // 2
    k1n = n1h + 1
    k1p = -(-k1n // 4) * 4
    k1 = np.arange(k1n, dtype=np.int64)
    th = (2.0 * np.pi / n1) * ((k1[:, None] * np.arange(n1h, dtype=np.int64)[None, :]) % n1)
    f1 = np.zeros((2 * k1p, n1h))
    f1[0:2 * k1n:2] = np.cos(th)
    f1[1:2 * k1n:2] = -np.sin(th)
    idx = np.arange(n2, dtype=np.int64)
    kk = k1[:, None, None] + n1 * idx[None, :, None]
    ph = (2.0 * np.pi / n_fft) * ((kk * idx[None, None, :]) % n_fft)
    g_re, g_im = np.cos(ph), -np.sin(ph)

    def blocks(re, im):
        out = np.zeros((k1p, 2 * n2, 2 * n2))
        out[:k1n, :n2, :n2] = re
        out[:k1n, :n2, n2:] = -im
        out[:k1n, n2:, :n2] = im
        out[:k1n, n2:, n2:] = re
        return out

    wgt = np.where((k1 == 0) | (k1 == n1h), 1.0, 2.0) / n_fft
    gf = blocks(g_re, g_im)
    gi = blocks(np.transpose(g_re, (0, 2, 1)) * wgt[:, None, None],
                -np.transpose(g_im, (0, 2, 1)) * wgt[:, None, None])
    as_bf16 = lambda t: jnp.asarray(t.astype(np.float32)).astype(BF16)
    return as_bf16(f1), as_bf16(np.ascontiguousarray(f1.T)), as_bf16(gf), as_bf16(gi), k1n


def _ld(ref, start, size, stride=None):
    idx = pl.ds(start, size) if stride is None else pl.ds(start, size, stride=stride)
    return jnp.concatenate([ref[t, idx, :] for t in range(ref.shape[0])], axis=1)


def _st(ref, start, size, val, stride=None):
    idx = pl.ds(start, size) if stride is None else pl.ds(start, size, stride=stride)
    for t in range(ref.shape[0]):
        ref[t, idx, :] = val[:, t * LANES:(t + 1) * LANES]


def _fft_stage1(x_scr, a_scr, f1_ref):
    rows, n1h = f1_ref.shape

    def body(n2, _):
        xs = _ld(x_scr, n2, n1h, FFT_N2).astype(BF16)
        _st(a_scr, pl.multiple_of(n2 * rows, 8), rows, jnp.dot(f1_ref[...], xs, preferred_element_type=F32))
        return 0

    lax.fori_loop(0, FFT_N2, body, 0, unroll=FFT_UNROLL)


def _fft_stage2_in(a_scr, k1, rows):
    return jnp.concatenate([_ld(a_scr, 2 * k1, FFT_N2, rows), _ld(a_scr, 2 * k1 + 1, FFT_N2, rows)],
                           axis=0).astype(BF16)


def _hy_spec_kernel(h_ref, wf_ref, wb_ref, dl_ref, f1_ref, gf_ref, k_ref, x_scr, a_scr, *, k1n):
    n2 = FFT_N2
    seq_len = h_ref.shape[0]
    tc = wf_ref.shape[1]
    rows = f1_ref.shape[0]
    h3 = h_ref[...].astype(BF16)
    rid = lax.broadcasted_iota(jnp.int32, (seq_len, tc), 0)
    decay = jnp.exp(-(rid.astype(F32) * (1.0 / (seq_len - 1))) * dl_ref[...])
    hf = jnp.dot(h3, wf_ref[...].astype(BF16), preferred_element_type=F32) * decay
    hb = jnp.dot(h3, wb_ref[...].astype(BF16), preferred_element_type=F32) * decay
    norm = jnp.sum(jnp.abs(hf) + jnp.abs(hb), axis=0, keepdims=True) + 1e-6
    inv = 1.0 / norm
    hb = jnp.where(rid == 0, 0.0, hb * inv)
    _st(x_scr, 0, seq_len, jnp.concatenate([hf * inv, hb], axis=1))
    _fft_stage1(x_scr, a_scr, f1_ref)

    def mid(k1, _):
        x = jnp.dot(gf_ref[k1], _fft_stage2_in(a_scr, k1, rows), preferred_element_type=F32)
        xf, xb = x[:, 0:tc], x[:, tc:2 * tc]
        k_ref[k1] = jnp.concatenate([xf[:n2] + xb[:n2], xf[n2:] - xb[n2:]], axis=0)
        return 0

    lax.fori_loop(0, k1n, mid, 0, unroll=FFT_UNROLL)
    for k1 in range(k1n, k_ref.shape[0]):
        k_ref[k1] = jnp.zeros((2 * n2, tc), F32)


def _hy_spec(h3, w4f, w4b, deltas, f1, gf, k1n, tc):
    seq_len = h3.shape[0]
    dh = w4f.shape[1]
    k1p = gf.shape[0]
    rows, n1h = f1.shape
    nt = 2 * tc // LANES
    return pl.pallas_call(
        functools.partial(_hy_spec_kernel, k1n=k1n),
        grid=(dh // tc,),
        in_specs=[
            pl.BlockSpec((seq_len, LANES), lambda j: (0, 0)),
            pl.BlockSpec((LANES, tc), lambda j: (0, j)),
            pl.BlockSpec((LANES, tc), lambda j: (0, j)),
            pl.BlockSpec((1, tc), lambda j: (0, j)),
            pl.BlockSpec((rows, n1h), lambda j: (0, 0)),
            pl.BlockSpec((k1p, 2 * FFT_N2, 2 * FFT_N2), lambda j: (0, 0, 0)),
        ],
        out_specs=pl.BlockSpec((k1p, 2 * FFT_N2, tc), lambda j: (0, 0, j)),
        out_shape=jax.ShapeDtypeStruct((k1p, 2 * FFT_N2, dh), F32),
        scratch_shapes=[pltpu.VMEM((nt, seq_len, LANES), F32), pltpu.VMEM((nt, FFT_N2 * rows, LANES), F32)],
        compiler_params=_params(1, 56),
        name="hy_spec",
    )(h3, w4f, w4b, deltas, f1, gf)


def _hy_conv_kernel(u_ref, x0_ref, k_ref, f1_ref, f1t_ref, gf_ref, gi_ref, bias_ref, o_ref, x_scr, a_scr, *, k1n):
    n2 = FFT_N2
    seq_len = u_ref.shape[0]
    rows, n1h = f1_ref.shape
    ch = min(512, seq_len)
    for r0 in range(0, seq_len, ch):
        _st(x_scr, r0, ch, u_ref[r0:r0 + ch, :].astype(F32))
    _fft_stage1(x_scr, a_scr, f1_ref)

    def mid(k1, _):
        x = jnp.dot(gf_ref[k1], _fft_stage2_in(a_scr, k1, rows), preferred_element_type=F32)
        kk = k_ref[k1]
        xr, xi, kr, ki = x[:n2], x[n2:], kk[:n2], kk[n2:]
        y = jnp.concatenate([xr * kr - xi * ki, xr * ki + xi * kr], axis=0).astype(BF16)
        bt = jnp.dot(gi_ref[k1], y, preferred_element_type=F32)
        _st(a_scr, 2 * k1, n2, bt[:n2], rows)
        _st(a_scr, 2 * k1 + 1, n2, bt[n2:], rows)
        return 0

    lax.fori_loop(0, k1n, mid, 0, unroll=FFT_UNROLL)

    def last(j, _):
        blk = _ld(a_scr, pl.multiple_of(j * rows, 8), rows).astype(BF16)
        _st(x_scr, j, n1h, jnp.dot(f1t_ref[...], blk, preferred_element_type=F32), n2)
        return 0

    lax.fori_loop(0, n2, last, 0, unroll=FFT_UNROLL)
    for r0 in range(0, seq_len, ch):
        v = _ld(x_scr, r0, ch) + u_ref[r0:r0 + ch, :].astype(F32) * bias_ref[...]
        o_ref[r0:r0 + ch, :] = (x0_ref[r0:r0 + ch, :].astype(F32) * v).astype(o_ref.dtype)


def _hy_conv(u3, x0c, kspec, f1, f1t, gf, gi, bias, k1n, tc):
    nb, seq_len, dh = u3.shape
    k1p = gf.shape[0]
    rows, n1h = f1.shape
    nt = tc // LANES
    blk = pl.BlockSpec((None, seq_len, tc), lambda j, b: (b, 0, j))
    const = lambda shape: pl.BlockSpec(shape, lambda j, b: (0,) * len(shape), pipeline_mode=ONE_BUFFER)
    return pl.pallas_call(
        functools.partial(_hy_conv_kernel, k1n=k1n),
        grid=(dh // tc, nb),
        in_specs=[
            blk, blk,
            pl.BlockSpec((k1p, 2 * FFT_N2, tc), lambda j, b: (0, 0, j), pipeline_mode=ONE_BUFFER),
            const((rows, n1h)), const((n1h, rows)),
            const((k1p, 2 * FFT_N2, 2 * FFT_N2)), const((k1p, 2 * FFT_N2, 2 * FFT_N2)),
            pl.BlockSpec((1, tc), lambda j, b: (0, j)),
        ],
        out_specs=blk,
        out_shape=jax.ShapeDtypeStruct((nb, seq_len, dh), BF16),
        scratch_shapes=[pltpu.VMEM((nt, seq_len, LANES), F32), pltpu.VMEM((nt, FFT_N2 * rows, LANES), F32)],
        compiler_params=_params(2, 56),
        name="hy_conv",
    )(u3, x0c, kspec, f1, f1t, gf, gi, bias)


def _outproj_kernel(ys_ref, yh_ref, w_ref, x_ref, posr_ref, cemb_ref, nwp_ref, g_ref, nwf_ref, sh_ref, sc_ref,
                    xo_ref, h_ref):
    ds = ys_ref.shape[1]
    y = jnp.dot(ys_ref[...], w_ref[0:ds, :], preferred_element_type=F32)
    y = y + jnp.dot(yh_ref[...], w_ref[ds:, :], preferred_element_type=F32)
    xn = _add_pos(x_ref[...], posr_ref[...], cemb_ref[...]) + g_ref[...] * (_rms(y) * nwp_ref[...])
    xo_ref[...] = xn
    h_ref[...] = (_rms(xn) * nwf_ref[...] * (1.0 + sc_ref[...]) + sh_ref[...]).astype(h_ref.dtype)


def _out_proj(ys, yh, w, x2, posr, cemb, nwp, g1, nwf, sh2, sc2, seq_len, tm):
    m, d = x2.shape
    ds = ys.shape[1]
    dh = yh.shape[1]
    tiles_per_seq = seq_len // tm
    row = lambda i: (i, 0)
    fixed = lambda i: (0, 0)
    per_batch = lambda i: (i // tiles_per_seq, 0, 0)
    return pl.pallas_call(
        _outproj_kernel,
        grid=(m // tm,),
        in_specs=[
            pl.BlockSpec((tm, ds), row),
            pl.BlockSpec((tm, dh), row),
            pl.BlockSpec((ds + dh, d), fixed, pipeline_mode=ONE_BUFFER),
            pl.BlockSpec((tm, d), row),
            pl.BlockSpec((tm, d // 2), lambda i: (i % tiles_per_seq, 0)),
            pl.BlockSpec(cemb.shape, fixed),
            pl.BlockSpec((1, d), fixed),
            pl.BlockSpec((None, 1, d), per_batch),
            pl.BlockSpec((1, d), fixed),
            pl.BlockSpec((None, 1, d), per_batch),
            pl.BlockSpec((None, 1, d), per_batch),
        ],
        out_specs=[pl.BlockSpec((tm, d), row), pl.BlockSpec((tm, d), row)],
        out_shape=[jax.ShapeDtypeStruct((m, d), F32), jax.ShapeDtypeStruct((m, d), BF16)],
        compiler_params=_params(1, 56),
        name="out_proj",
    )(ys, yh, w, x2, posr, cemb, nwp, g1, nwf, sh2, sc2)


def _ffn_step(h_ref, wg, wu, wd, x_ref, nw_ref, g_ref, o_ref):
    f = pl.program_id(1)
    h = h_ref[...]
    gate = jnp.dot(h, wg, preferred_element_type=F32)
    up = jnp.dot(h, wu, preferred_element_type=F32)
    act = (_silu(gate) * up).astype(BF16)

    @pl.when(f == 0)
    def _():
        o_ref[...] = jnp.zeros_like(o_ref)

    d = o_ref.shape[1]
    cw = min(512, d)
    for n0 in range(0, d, cw):
        o_ref[:, n0:n0 + cw] += jnp.dot(act, wd[:, n0:n0 + cw], preferred_element_type=F32)

    @pl.when(f == pl.num_programs(1) - 1)
    def _():
        tm = o_ref.shape[0]
        rows = min(256, tm)
        for r0 in range(0, tm, rows):
            y = o_ref[r0:r0 + rows, :]
            o_ref[r0:r0 + rows, :] = x_ref[r0:r0 + rows, :] + g_ref[...] * (_rms(y) * nw_ref[...])


def _ffn_first_kernel(h_ref, wg_ref, wu_ref, wd_ref, x_ref, nw_ref, g_ref, o_ref, wgb_ref, wub_ref, wdb_ref):
    wg = wg_ref[...].astype(BF16)
    wu = wu_ref[...].astype(BF16)
    wd = wd_ref[...].astype(BF16)
    wgb_ref[...] = wg
    wub_ref[...] = wu
    wdb_ref[...] = wd
    _ffn_step(h_ref, wg, wu, wd, x_ref, nw_ref, g_ref, o_ref)


def _ffn_rest_kernel(h_ref, wg_ref, wu_ref, wd_ref, x_ref, nw_ref, g_ref, o_ref):
    _ffn_step(h_ref, wg_ref[...], wu_ref[...], wd_ref[...], x_ref, nw_ref, g_ref, o_ref)


def _ffn(h2, wg, wu, wd, xn, nw, g2, seq_len, tm, tf_first, tf):
    m, d = xn.shape
    dff = wg.shape[1]
    tiles_per_seq = seq_len // tm
    n_rows = m // tm
    out, wgb, wub, wdb = pl.pallas_call(
        _ffn_first_kernel,
        grid=(1, dff // tf_first),
        in_specs=[
            pl.BlockSpec((tm, d), lambda i, f: (0, 0), pipeline_mode=ONE_BUFFER),
            pl.BlockSpec((d, tf_first), lambda i, f: (0, f)),
            pl.BlockSpec((d, tf_first), lambda i, f: (0, f)),
            pl.BlockSpec((tf_first, d), lambda i, f: (f, 0)),
            pl.BlockSpec((tm, d), lambda i, f: (0, 0), pipeline_mode=ONE_BUFFER),
            pl.BlockSpec((1, d), lambda i, f: (0, 0)),
            pl.BlockSpec((None, 1, d), lambda i, f: (0, 0, 0)),
        ],
        out_specs=[
            pl.BlockSpec((tm, d), lambda i, f: (0, 0)),
            pl.BlockSpec((d, tf_first), lambda i, f: (0, f)),
            pl.BlockSpec((d, tf_first), lambda i, f: (0, f)),
            pl.BlockSpec((tf_first, d), lambda i, f: (f, 0)),
        ],
        out_shape=[
            jax.ShapeDtypeStruct((m, d), F32),
            jax.ShapeDtypeStruct((d, dff), BF16),
            jax.ShapeDtypeStruct((d, dff), BF16),
            jax.ShapeDtypeStruct((dff, d), BF16),
        ],
        input_output_aliases={4: 0},
        compiler_params=_params(2, 58),
        name="ffn_first",
    )(h2, wg, wu, wd, xn, nw, g2)
    if n_rows == 1:
        return out
    return pl.pallas_call(
        _ffn_rest_kernel,
        grid=(n_rows - 1, dff // tf),
        in_specs=[
            pl.BlockSpec((tm, d), lambda i, f: (i + 1, 0)),
            pl.BlockSpec((d, tf), lambda i, f: (0, f)),
            pl.BlockSpec((d, tf), lambda i, f: (0, f)),
            pl.BlockSpec((tf, d), lambda i, f: (f, 0)),
            pl.BlockSpec((tm, d), lambda i, f: (i + 1, 0), pipeline_mode=ONE_BUFFER),
            pl.BlockSpec((1, d), lambda i, f: (0, 0)),
            pl.BlockSpec((None, 1, d), lambda i, f: ((i + 1) // tiles_per_seq, 0, 0)),
        ],
        out_specs=pl.BlockSpec((tm, d), lambda i, f: (i + 1, 0)),
        out_shape=jax.ShapeDtypeStruct((m, d), F32),
        input_output_aliases={4: 0},
        compiler_params=_params(2, 58),
        name="ffn",
    )(h2, wgb, wub, wdb, out, nw, g2)


def _sincos_tables(rows, cols, dim):
    qd = dim // 4
    omega = 1.0 / (POS_THETA ** (jnp.arange(qd, dtype=F32) / qd))
    r = jnp.arange(rows, dtype=F32)[:, None] * omega
    cc = jnp.arange(cols, dtype=F32)[:, None] * omega
    r_emb = jnp.concatenate([jnp.sin(r), jnp.cos(r)], -1)
    c_emb = jnp.concatenate([jnp.sin(cc), jnp.cos(cc)], -1)
    return r_emb, c_emb


def _filter_features_t(seq_len, n_bands):
    t = jnp.linspace(0.0, 1.0, seq_len, dtype=F32)[:, None]
    w = 2.0 * math.pi * jnp.arange(seq_len, dtype=F32)[:, None] / seq_len
    fb = jnp.linspace(1e-4, n_bands - 1, n_bands, dtype=F32)[None]
    zpos = jnp.concatenate([t, jnp.cos(fb * w), -jnp.sin(fb * w)], -1)
    emb = zpos.shape[1]
    return jnp.pad(zpos, ((0, 0), (0, LANES - emb))).T


def _pad_rows(a, rows):
    return jnp.pad(a, ((0, rows - a.shape[0]), (0, 0)))


def kernel(x, c, ctx, c_ctx, w_ada, b_ada, norm_mix_pre, norm_mix_post, norm_ffn_pre, norm_ffn_post,
           w_in, ssd_conv_w, ssd_conv_b, ssd_a_log, ssd_dt_bias, ssd_d, ssd_norm,
           hy_conv_w, hy_conv_b, hy_w1, hy_b1, hy_w2, hy_b2, hy_w3, hy_b3, hy_w4, hy_freq, hy_bias,
           w_out, w_gate, w_up, w_down):
    nb, seq_len, d = x.shape
    ctx_len = ctx.shape[1]
    assert w_ada.shape[0] == 1, "single layer"
    n_heads = ssd_d.shape[1]
    d_ssd = n_heads * SSD_HEAD_DIM
    d_xbc = d_ssd + 2 * SSD_GROUPS * SSD_STATE
    dh = hy_bias.shape[1]
    assert w_in.shape[2] == d_ssd + d_xbc + 2 * n_heads + 3 * dh
    assert n_heads <= LANES and nb + 1 <= 8
    assert seq_len % (GRID_W * 8) == 0 and seq_len % FFT_N2 == 0
    m = nb * seq_len

    crows = _pad_rows(jnp.concatenate([c, c_ctx[None, :]], axis=0), 8)
    mod = _ada(crows, w_ada[0], b_ada[0])
    part = lambda r0, r1, k: mod[r0:r1, k * d:(k + 1) * d][:, None, :]
    sh1, sc1, g1, sh2, sc2, g2 = (part(0, nb, k) for k in range(6))
    csh1 = jnp.broadcast_to(part(nb, nb + 1, 0), (nb, 1, d))
    csc1 = jnp.broadcast_to(part(nb, nb + 1, 1), (nb, 1, d))

    w_t = jnp.transpose(w_in[0])
    o_xbc = d_ssd
    o_dt = o_xbc + d_xbc
    o_hy = o_dt + 2 * n_heads
    tn_in = 512
    assert o_dt % tn_in == 0 and (3 * dh) % tn_in == 0 and o_xbc % tn_in == 0 and o_hy % 16 == 0
    n_left = o_dt // tn_in
    main_off = lambda j: jnp.where(j < n_left, j * tn_in, o_hy + (j - n_left) * tn_in)
    ctx_off = lambda j: o_xbc + j * tn_in
    pad_dt = lambda rows: jnp.pad(rows, ((0, LANES - n_heads), (0, 0)))
    w_dt_t = jnp.concatenate([pad_dt(w_t[o_dt:o_dt + n_heads]), pad_dt(w_t[o_dt + n_heads:o_hy])],
                             axis=0).astype(BF16)

    r_emb, c_emb = _sincos_tables(seq_len // GRID_W, GRID_W, d)
    posr = jnp.repeat(r_emb, GRID_W, axis=0)
    nmp = norm_mix_pre[0][None, :]

    tm_in = min(1024, seq_len)
    proj, dt_lat = _in_proj(x.reshape(m, d), posr, c_emb, nmp, sh1, sc1, w_t, main_off, (o_dt + 3 * dh) // tn_in,
                            w_dt_t, seq_len, tm_in, tn_in, True)
    tm_ctx = min(256, ctx_len)
    xbc_ctx, dt_ctx = _in_proj(ctx.reshape(nb * ctx_len, d), jnp.zeros((8, d // 2), F32), c_emb, nmp, csh1, csc1,
                               w_t, ctx_off, d_xbc // tn_in, w_dt_t, ctx_len, tm_ctx, tn_in, False)
    proj3 = proj.reshape(nb, seq_len, -1)

    cw8 = _pad_rows(ssd_conv_w[0], 8)
    cb1 = ssd_conv_b[0][None, :]
    u_lat = _ssd_prep(proj3, d_ssd, d_xbc, cw8, cb1, 512)
    u_ctx = _ssd_prep(xbc_ctx.reshape(nb, ctx_len, d_xbc), 0, d_xbc, cw8, cb1, 512)
    pad_heads = lambda a: jnp.pad(a, ((0, 0), (0, LANES - n_heads)))[:, None, :]
    expand = (jnp.arange(LANES)[:, None] == (jnp.arange(d_ssd)[None, :] // SSD_HEAD_DIM)).astype(BF16)
    y_ssd = _ssd(u_lat, dt_lat.reshape(nb, seq_len, 2 * LANES), u_ctx, dt_ctx.reshape(nb, ctx_len, 2 * LANES),
                 proj3, pad_heads(ssd_a_log[0]), pad_heads(ssd_dt_bias[0]),
                 jnp.repeat(ssd_d[0], SSD_HEAD_DIM)[None, :], ssd_norm[0][None, :], expand)

    x0c, u_hy = _hy_prep(proj3, d_ssd + d_xbc, dh, _pad_rows(hy_conv_w[0], 8), hy_conv_b[0][None, :], 256)
    n_bands = (hy_w1.shape[1] - 1) // 2
    zt = _filter_features_t(seq_len, n_bands)
    w1t = jnp.pad(hy_w1[0].T, ((0, 0), (0, LANES - hy_w1.shape[1])))
    b3 = jnp.stack([hy_b1[0], hy_b2[0], hy_b3[0]], axis=1)
    h3 = _hy_mlp(zt, w1t, hy_w2[0].T, hy_w3[0].T, b3, hy_freq[0][:, None])
    w4 = _pad_rows(hy_w4[0], LANES)
    max_decay = math.log(HY_TARGET) / HY_FAST_PCT
    min_decay = math.log(HY_TARGET) / HY_SLOW_PCT
    deltas = jnp.abs(jnp.linspace(min_decay, max_decay, dh, dtype=F32))[None, :]
    f1, f1t, gf, gi, k1n = _fft_tables(seq_len)
    kspec = _hy_spec(h3, w4[:, :dh], w4[:, dh:], deltas, f1, gf, k1n, 128)
    y_hy = _hy_conv(u_hy, x0c, kspec, f1, f1t, gf, gi, hy_bias[0][None, :], k1n, 256)

    xn, h2 = _out_proj(y_ssd.reshape(m, d_ssd), y_hy.reshape(m, dh), w_out[0].astype(BF16), x.reshape(m, d),
                       posr, c_emb, norm_mix_post[0][None, :], g1, norm_ffn_pre[0][None, :], sh2, sc2, seq_len,
                       min(512, seq_len))
    out = _ffn(h2, w_gate[0], w_up[0], w_down[0], xn, norm_ffn_post[0][None, :], g2, seq_len,
               min(1024, seq_len), 256, 512)
    return out.reshape(nb, seq_len, d)
```

```python
import functools
import math

import numpy as np
import jax
import jax.numpy as jnp
from jax import lax
from jax.experimental import pallas as pl
from jax.experimental.pallas import tpu as pltpu

F32 = jnp.float32
BF16 = jnp.bfloat16
HIGHEST = lax.Precision.HIGHEST

RMS_EPS = 1e-6
POS_THETA = 10000.0
GRID_W = 64
SSD_HEAD_DIM = 64
SSD_GROUPS = 2
SSD_STATE = 128
SSD_CHUNK = 128
HY_TARGET = 1e-2
HY_FAST_PCT = 0.3
HY_SLOW_PCT = 1.5
FFT_N2 = 64
FFT_UNROLL = 8
LANES = 128
MIB = 1024 * 1024
ONE_BUFFER = pl.Buffered(1)
_NT_DIMS = (((1,), (1,)), ((), ()))


def _params(n_axes, vmem_mib):
    return pltpu.CompilerParams(
        dimension_semantics=("arbitrary",) * n_axes,
        vmem_limit_bytes=vmem_mib * MIB,
    )


def _silu(v):
    return v * (1.0 / (1.0 + jnp.exp(-v)))


def _softplus(v):
    return jnp.maximum(v, 0.0) + jnp.log(1.0 + jnp.exp(-jnp.abs(v)))


def _rms(v):
    return v * lax.rsqrt(jnp.mean(v * v, axis=-1, keepdims=True) + RMS_EPS)


def _add_pos(x, posr, cemb):
    tm, d = x.shape
    half = d // 2
    lo = x[:, :half] + posr
    hi = (x[:, half:].reshape(tm // GRID_W, GRID_W, half) + cemb[None]).reshape(tm, half)
    return jnp.concatenate([lo, hi], axis=1)


def _ada_kernel(c_ref, w_ref, b_ref, o_ref, s_scr):
    nr, d, _ = c_ref.shape
    tn = w_ref.shape[1]
    nt = tn // LANES
    kc = 64

    @pl.when(pl.program_id(0) == 0)
    def _():
        for r in range(nr):
            for k0 in range(0, d, 512):
                s_scr[r, k0:k0 + 512, :] = _silu(c_ref[r, k0:k0 + 512, :])

    acc = [[jnp.zeros((8, LANES), F32) for _ in range(nt)] for _ in range(nr)]
    for k0 in range(0, d, kc):
        s = [s_scr[r, k0:k0 + kc, :] for r in range(nr)]
        for t in range(nt):
            w = w_ref[k0:k0 + kc, t * LANES:(t + 1) * LANES]
            for r in range(nr):
                acc[r][t] = acc[r][t] + jnp.sum((w * s[r]).reshape(kc // 8, 8, LANES), axis=0)
    rows = [jnp.concatenate([jnp.sum(a, axis=0, keepdims=True) for a in acc[r]], axis=1) for r in range(nr)]
    rows.append(jnp.zeros((o_ref.shape[0] - nr, tn), F32))
    o_ref[...] = jnp.concatenate(rows, axis=0) + b_ref[...]


def _ada(crows, w_ada, b_ada):
    nr = crows.shape[0]
    d, n = w_ada.shape
    tn = 1024
    c_lanes = jnp.broadcast_to(crows[:, :, None], (nr, d, LANES))
    return pl.pallas_call(
        _ada_kernel,
        grid=(n // tn,),
        in_specs=[
            pl.BlockSpec((nr, d, LANES), lambda j: (0, 0, 0), pipeline_mode=ONE_BUFFER),
            pl.BlockSpec((d, tn), lambda j: (0, j)),
            pl.BlockSpec((1, tn), lambda j: (0, j)),
        ],
        out_specs=pl.BlockSpec((8, tn), lambda j: (0, j)),
        out_shape=jax.ShapeDtypeStruct((8, n), F32),
        scratch_shapes=[pltpu.VMEM((nr, d, LANES), F32)],
        compiler_params=_params(1, 40),
        name="ada",
    )(c_lanes, w_ada, b_ada.reshape(1, n))


def _inproj_kernel(x_ref, posr_ref, cemb_ref, nw_ref, sh_ref, sc_ref, w_ref, wdt_ref, o_ref, dt_ref, h_scr,
                   *, with_pos):
    first = pl.program_id(1) == 0
    w = w_ref[...].astype(BF16)

    @pl.when(first)
    def _():
        tm = x_ref.shape[0]
        rows = min(256, tm)
        wdt = wdt_ref[...].astype(BF16)
        for r0 in range(0, tm, rows):
            xf = x_ref[r0:r0 + rows, :]
            if with_pos:
                xf = _add_pos(xf, posr_ref[r0:r0 + rows, :], cemb_ref[...])
            h = _rms(xf) * nw_ref[...] * (1.0 + sc_ref[...]) + sh_ref[...]
            hb = h.astype(BF16)
            h_scr[r0:r0 + rows, :] = hb
            dt_ref[r0:r0 + rows, :] = lax.dot_general(hb, wdt, _NT_DIMS, preferred_element_type=F32)
            o_ref[r0:r0 + rows, :] = lax.dot_general(hb, w, _NT_DIMS, preferred_element_type=F32).astype(o_ref.dtype)

    @pl.when(jnp.logical_not(first))
    def _():
        o_ref[...] = lax.dot_general(h_scr[...], w, _NT_DIMS, preferred_element_type=F32).astype(o_ref.dtype)


def _in_proj(x2, posr, cemb, nw, sh, sc, w_t, row_off, n_tiles, wdt_t, seq_len, tm, tn, with_pos):
    m, d = x2.shape
    n = n_tiles * tn
    ndt = wdt_t.shape[0]
    half = d // 2
    tiles_per_seq = seq_len // tm
    pos_rows = tm if with_pos else posr.shape[0]
    return pl.pallas_call(
        functools.partial(_inproj_kernel, with_pos=with_pos),
        grid=(m // tm, n_tiles),
        in_specs=[
            pl.BlockSpec((tm, d), lambda i, j: (i, 0)),
            pl.BlockSpec((pos_rows, half), lambda i, j: ((i % tiles_per_seq) if with_pos else 0, 0)),
            pl.BlockSpec(cemb.shape, lambda i, j: (0, 0), pipeline_mode=ONE_BUFFER),
            pl.BlockSpec((1, d), lambda i, j: (0, 0)),
            pl.BlockSpec((None, 1, d), lambda i, j: (i // tiles_per_seq, 0, 0)),
            pl.BlockSpec((None, 1, d), lambda i, j: (i // tiles_per_seq, 0, 0)),
            pl.BlockSpec((pl.Element(tn), pl.Element(d)), lambda i, j: (pl.multiple_of(row_off(j), 8), 0)),
            pl.BlockSpec((ndt, d), lambda i, j: (0, 0), pipeline_mode=ONE_BUFFER),
        ],
        out_specs=[
            pl.BlockSpec((tm, tn), lambda i, j: (i, j)),
            pl.BlockSpec((tm, ndt), lambda i, j: (i, 0)),
        ],
        out_shape=[
            jax.ShapeDtypeStruct((m, n), BF16),
            jax.ShapeDtypeStruct((m, ndt), F32),
        ],
        scratch_shapes=[pltpu.VMEM((tm, d), BF16)],
        compiler_params=_params(2, 58),
        name="in_proj",
    )(x2, posr, cemb, nw, sh, sc, w_t, wdt_t)


def _conv3_chunk(x_ref, r0, rows, seq_len, w, b):
    cur = x_ref[r0:r0 + rows, :].astype(F32)
    tc = cur.shape[1]
    rid = lax.broadcasted_iota(jnp.int32, (rows, tc), 0)
    if r0 > 0:
        prev_row = x_ref[r0 - 16:r0, :].astype(F32)[15:16, :]
    else:
        prev_row = jnp.zeros((1, tc), F32)
    if r0 + rows < seq_len:
        next_row = x_ref[r0 + rows:r0 + rows + 16, :].astype(F32)[0:1, :]
    else:
        next_row = jnp.zeros((1, tc), F32)
    up = jnp.where(rid == 0, prev_row, pltpu.roll(cur, 1, 0))
    down = jnp.where(rid == rows - 1, next_row, pltpu.roll(cur, rows - 1, 0))
    return up * w[0:1, :] + cur * w[1:2, :] + down * w[2:3, :] + b


def _ssd_prep_kernel(x_ref, w_ref, b_ref, o_ref, *, rows):
    seq_len = x_ref.shape[0]
    w = w_ref[...]
    b = b_ref[...]
    for r0 in range(0, seq_len, rows):
        v = _conv3_chunk(x_ref, r0, rows, seq_len, w, b)
        o_ref[r0:r0 + rows, :] = _silu(v).astype(o_ref.dtype)


def _ssd_prep(proj3, col0, width, w8, b1, tc):
    nb, seq_len, _ = proj3.shape
    rows = min(512, seq_len)
    j0 = col0 // tc
    return pl.pallas_call(
        functools.partial(_ssd_prep_kernel, rows=rows),
        grid=(nb, width // tc),
        in_specs=[
            pl.BlockSpec((None, seq_len, tc), lambda b, j: (b, 0, j0 + j)),
            pl.BlockSpec((8, tc), lambda b, j: (0, j)),
            pl.BlockSpec((1, tc), lambda b, j: (0, j)),
        ],
        out_specs=pl.BlockSpec((None, seq_len, tc), lambda b, j: (b, 0, j)),
        out_shape=jax.ShapeDtypeStruct((nb, seq_len, width), BF16),
        compiler_params=_params(2, 40),
        name="ssd_prep",
    )(proj3, w8, b1)


def _hy_prep_kernel(x0_ref, x1_ref, v_ref, w0_ref, w1_ref, wv_ref, b0_ref, b1_ref, bv_ref,
                    x0c_ref, u_ref, *, rows):
    seq_len = x0_ref.shape[0]
    w0, w1, wv = w0_ref[...], w1_ref[...], wv_ref[...]
    b0, b1, bv = b0_ref[...], b1_ref[...], bv_ref[...]
    for r0 in range(0, seq_len, rows):
        x0c_ref[r0:r0 + rows, :] = _conv3_chunk(x0_ref, r0, rows, seq_len, w0, b0).astype(x0c_ref.dtype)
        x1c = _conv3_chunk(x1_ref, r0, rows, seq_len, w1, b1)
        vc = _conv3_chunk(v_ref, r0, rows, seq_len, wv, bv)
        u_ref[r0:r0 + rows, :] = (vc * x1c).astype(u_ref.dtype)


def _hy_prep(proj3, col0, dh, w8, b1, tc):
    nb, seq_len, _ = proj3.shape
    rows = min(512, seq_len)
    j0 = col0 // tc
    nj = dh // tc
    x_spec = lambda k: pl.BlockSpec((None, seq_len, tc), lambda b, j: (b, 0, j0 + k * nj + j))
    w_spec = lambda k: pl.BlockSpec((8, tc), lambda b, j: (0, k * nj + j))
    b_spec = lambda k: pl.BlockSpec((1, tc), lambda b, j: (0, k * nj + j))
    o_spec = pl.BlockSpec((None, seq_len, tc), lambda b, j: (b, 0, j))
    return pl.pallas_call(
        functools.partial(_hy_prep_kernel, rows=rows),
        grid=(nb, nj),
        in_specs=[x_spec(0), x_spec(1), x_spec(2), w_spec(0), w_spec(1), w_spec(2),
                  b_spec(0), b_spec(1), b_spec(2)],
        out_specs=[o_spec, o_spec],
        out_shape=[jax.ShapeDtypeStruct((nb, seq_len, dh), BF16)] * 2,
        compiler_params=_params(2, 48),
        name="hy_prep",
    )(proj3, proj3, proj3, w8, w8, w8, b1, b1, b1)


def _ssd_kernel(u_ref, dt_ref, uc_ref, dtc_ref, z_ref, alog_ref, dtb_ref, dsk_ref, nw_ref, e_ref,
                o_ref, h_scr, yf_scr, *, n_ctx, n_lat):
    q = SSD_CHUNK
    d_ssd = h_scr.shape[1]
    gw = d_ssd // SSD_GROUPS
    d = pl.program_id(1)
    s = pl.program_id(2)
    is_ctx = s < n_ctx
    t = jnp.maximum(s - n_ctx, 0)
    cidx = jnp.where(d == 0, t, n_lat - 1 - t)

    @pl.when(s == 0)
    def _():
        h_scr[...] = jnp.zeros_like(h_scr)

    u = jnp.where(is_ctx, uc_ref[...], u_ref[...])
    dtr = jnp.where(is_ctx, dtc_ref[...], dt_ref[...])

    dt = _softplus(dtr + dtb_ref[...])
    a = dt * (-jnp.exp(alog_ref[...]))
    row = lax.broadcasted_iota(jnp.int32, (q, q), 0)
    col = lax.broadcasted_iota(jnp.int32, (q, q), 1)
    fwd = d == 0
    tri = jnp.where(fwd, row, col) >= jnp.where(fwd, col, row)
    cum = jnp.dot(tri.astype(F32), a, precision=HIGHEST, preferred_element_type=F32)
    cum_t = cum.T
    tot = jnp.sum(a, axis=0, keepdims=True)

    stack = jnp.concatenate(
        [dt, jnp.exp(cum), jnp.exp(tot - cum), jnp.broadcast_to(jnp.exp(tot), (8, LANES))], axis=0)
    ex = jnp.dot(stack.astype(BF16), e_ref[...], preferred_element_type=F32)
    dt_x = ex[0:q]
    ecum_x = ex[q:2 * q]
    edec_x = ex[2 * q:3 * q]
    etot_x = ex[3 * q:3 * q + 1]

    xs = u[:, :d_ssd].astype(F32)
    xdt = xs * dt_x
    xdt_b = xdt.astype(BF16)
    xdw_b = (xdt * edec_x).astype(BF16)
    lane_lo = lax.broadcasted_iota(jnp.int32, (q, LANES), 1) < SSD_HEAD_DIM
    heads_per_group = gw // SSD_HEAD_DIM

    y_parts = []
    for g in range(SSD_GROUPS):
        bg = u[:, d_ssd + g * SSD_STATE:d_ssd + (g + 1) * SSD_STATE]
        cg = u[:, d_ssd + (SSD_GROUPS + g) * SSD_STATE:d_ssd + (SSD_GROUPS + g + 1) * SSD_STATE]
        scores = lax.dot_general(cg, bg, (((1,), (1,)), ((), ())), preferred_element_type=F32)
        h_prev = h_scr[:, g * gw:(g + 1) * gw]
        y_off = jnp.dot(cg, h_prev.astype(BF16), preferred_element_type=F32)
        for j in range(heads_per_group // 2):
            c0 = g * gw + j * LANES
            xp = xdt_b[:, c0:c0 + LANES]
            acc = None
            for hh in range(2):
                h = g * heads_per_group + 2 * j + hh
                diff = cum[:, h:h + 1] - cum_t[h:h + 1, :]
                decay = jnp.exp(jnp.where(tri, diff, -1e30))
                m_h = (scores * decay).astype(BF16)
                x_h = jnp.where(lane_lo if hh == 0 else jnp.logical_not(lane_lo), xp, jnp.zeros_like(xp))
                part = jnp.dot(m_h, x_h, preferred_element_type=F32)
                acc = part if acc is None else acc + part
            y_parts.append(acc + y_off[:, j * LANES:(j + 1) * LANES] * ecum_x[:, c0:c0 + LANES])
        upd = lax.dot_general(bg, xdw_b[:, g * gw:(g + 1) * gw], (((0,), (0,)), ((), ())),
                              preferred_element_type=F32)
        h_scr[:, g * gw:(g + 1) * gw] = etot_x[:, g * gw:(g + 1) * gw] * h_prev + upd
    y = jnp.concatenate(y_parts, axis=1)

    @pl.when(jnp.logical_and(jnp.logical_not(is_ctx), fwd))
    def _():
        yf_scr[cidx] = y

    @pl.when(jnp.logical_and(jnp.logical_not(is_ctx), jnp.logical_not(fwd)))
    def _():
        ytot = yf_scr[cidx] + y + dsk_ref[...] * xs
        gated = ytot * _silu(z_ref[...].astype(F32))
        o_ref[...] = (_rms(gated) * nw_ref[...]).astype(o_ref.dtype)


def _ssd(u_lat, dt_lat, u_ctx, dt_ctx, proj3, alog, dtb, dsk, nw, expand):
    nb, seq_len, d_xbc = u_lat.shape
    ctx_len = u_ctx.shape[1]
    q = SSD_CHUNK
    n_lat = seq_len // q
    n_ctx = ctx_len // q
    d_ssd = dsk.shape[1]

    def lat_idx(d, s):
        t = jnp.maximum(s - n_ctx, 0)
        return jnp.where(d == 0, t, n_lat - 1 - t)

    def ctx_idx(d, s):
        t = jnp.minimum(s, n_ctx - 1)
        return jnp.where(d == 0, t, n_ctx - 1 - t)

    def out_idx(d, s):
        return jnp.where(d == 0, n_lat - 1, lat_idx(d, s))

    return pl.pallas_call(
        functools.partial(_ssd_kernel, n_ctx=n_ctx, n_lat=n_lat),
        grid=(nb, 2, n_ctx + n_lat),
        in_specs=[
            pl.BlockSpec((None, q, d_xbc), lambda b, d, s: (b, lat_idx(d, s), 0)),
            pl.BlockSpec((None, q, LANES), lambda b, d, s: (b, lat_idx(d, s), d)),
            pl.BlockSpec((None, q, d_xbc), lambda b, d, s: (b, ctx_idx(d, s), 0)),
            pl.BlockSpec((None, q, LANES), lambda b, d, s: (b, ctx_idx(d, s), d)),
            pl.BlockSpec((None, q, d_ssd), lambda b, d, s: (b, lat_idx(d, s), 0)),
            pl.BlockSpec((None, 1, LANES), lambda b, d, s: (d, 0, 0)),
            pl.BlockSpec((None, 1, LANES), lambda b, d, s: (d, 0, 0)),
            pl.BlockSpec((1, d_ssd), lambda b, d, s: (0, 0)),
            pl.BlockSpec((1, d_ssd), lambda b, d, s: (0, 0)),
            pl.BlockSpec((LANES, d_ssd), lambda b, d, s: (0, 0)),
        ],
        out_specs=pl.BlockSpec((None, q, d_ssd), lambda b, d, s: (b, out_idx(d, s), 0)),
        out_shape=jax.ShapeDtypeStruct((nb, seq_len, d_ssd), BF16),
        scratch_shapes=[
            pltpu.VMEM((SSD_STATE, d_ssd), F32),
            pltpu.VMEM((n_lat, q, d_ssd), F32),
        ],
        compiler_params=_params(3, 48),
        name="ssd",
    )(u_lat, dt_lat, u_ctx, dt_ctx, proj3, alog, dtb, dsk, nw, expand)


def _hy_mlp_kernel(zt_ref, w1_ref, w2_ref, w3_ref, b_ref, fr_ref, o_ref):
    fr = fr_ref[...]
    b = b_ref[...]
    h = jnp.sin(fr * (jnp.dot(w1_ref[...], zt_ref[...], precision=HIGHEST, preferred_element_type=F32)
                      + b[:, 0:1]))
    h = jnp.sin(fr * (jnp.dot(w2_ref[...], h, precision=HIGHEST, preferred_element_type=F32) + b[:, 1:2]))
    h = jnp.sin(fr * (jnp.dot(w3_ref[...], h, precision=HIGHEST, preferred_element_type=F32) + b[:, 2:3]))
    hid, seq_len = h.shape
    hp = jnp.concatenate([h, jnp.zeros((LANES - hid, seq_len), F32)], axis=0)
    o_ref[...] = hp.T


def _hy_mlp(zt, w1t, w2t, w3t, b3, fr):
    seq_len = zt.shape[1]
    return pl.pallas_call(
        _hy_mlp_kernel,
        out_shape=jax.ShapeDtypeStruct((seq_len, LANES), F32),
        compiler_params=pltpu.CompilerParams(vmem_limit_bytes=40 * MIB),
        name="hy_mlp",
    )(zt, w1t, w2t, w3t, b3, fr)


def _fft_tables(seq_len):
    n_fft = 2 * seq_len
    n2 = FFT_N2
    n1 = n_fft // n2
    n1h = n1 // 2
    k1n = n1h + 1
    k1p = -(-k1n // 4) * 4
    k1 = np.arange(k1n, dtype=np.int64)
    th = (2.0 * np.pi / n1) * ((k1[:, None] * np.arange(n1h, dtype=np.int64)[None, :]) % n1)
    f1 = np.zeros((2 * k1p, n1h))
    f1[0:2 * k1n:2] = np.cos(th)
    f1[1:2 * k1n:2] = -np.sin(th)
    idx = np.arange(n2, dtype=np.int64)
    kk = k1[:, None, None] + n1 * idx[None, :, None]
    ph = (2.0 * np.pi / n_fft) * ((kk * idx[None, None, :]) % n_fft)
    g_re, g_im = np.cos(ph), -np.sin(ph)

    def blocks(re, im):
        out = np.zeros((k1p, 2 * n2, 2 * n2))
        out[:k1n, :n2, :n2] = re
        out[:k1n, :n2, n2:] = -im
        out[:k1n, n2:, :n2] = im
        out[:k1n, n2:, n2:] = re
        return out

    wgt = np.where((k1 == 0) | (k1 == n1h), 1.0, 2.0) / n_fft
    gf = blocks(g_re, g_im)
    gi = blocks(np.transpose(g_re, (0, 2, 1)) * wgt[:, None, None],
                -np.transpose(g_im, (0, 2, 1)) * wgt[:, None, None])
    as_bf16 = lambda t: jnp.asarray(t.astype(np.float32)).astype(BF16)
    return as_bf16(f1), as_bf16(np.ascontiguousarray(f1.T)), as_bf16(gf), as_bf16(gi), k1n


def _ld(ref, start, size, stride=None):
    idx = pl.ds(start, size) if stride is None else pl.ds(start, size, stride=stride)
    return jnp.concatenate([ref[t, idx, :] for t in range(ref.shape[0])], axis=1)


def _st(ref, start, size, val, stride=None):
    idx = pl.ds(start, size) if stride is None else pl.ds(start, size, stride=stride)
    for t in range(ref.shape[0]):
        ref[t, idx, :] = val[:, t * LANES:(t + 1) * LANES]


def _fft_stage1(x_scr, a_scr, f1_ref):
    rows, n1h = f1_ref.shape

    def body(n2, _):
        xs = _ld(x_scr, n2, n1h, FFT_N2).astype(BF16)
        _st(a_scr, pl.multiple_of(n2 * rows, 8), rows, jnp.dot(f1_ref[...], xs, preferred_element_type=F32))
        return 0

    lax.fori_loop(0, FFT_N2, body, 0, unroll=FFT_UNROLL)


def _fft_stage2_in(a_scr, k1, rows):
    return jnp.concatenate([_ld(a_scr, 2 * k1, FFT_N2, rows), _ld(a_scr, 2 * k1 + 1, FFT_N2, rows)],
                           axis=0).astype(BF16)


def _hy_spec_kernel(h_ref, wf_ref, wb_ref, dl_ref, f1_ref, gf_ref, k_ref, x_scr, a_scr, *, k1n):
    n2 = FFT_N2
    seq_len = h_ref.shape[0]
    tc = wf_ref.shape[1]
    rows = f1_ref.shape[0]
    h3 = h_ref[...].astype(BF16)
    rid = lax.broadcasted_iota(jnp.int32, (seq_len, tc), 0)
    decay = jnp.exp(-(rid.astype(F32) * (1.0 / (seq_len - 1))) * dl_ref[...])
    hf = jnp.dot(h3, wf_ref[...].astype(BF16), preferred_element_type=F32) * decay
    hb = jnp.dot(h3, wb_ref[...].astype(BF16), preferred_element_type=F32) * decay
    norm = jnp.sum(jnp.abs(hf) + jnp.abs(hb), axis=0, keepdims=True) + 1e-6
    inv = 1.0 / norm
    hb = jnp.where(rid == 0, 0.0, hb * inv)
    _st(x_scr, 0, seq_len, jnp.concatenate([hf * inv, hb], axis=1))
    _fft_stage1(x_scr, a_scr, f1_ref)

    def mid(k1, _):
        x = jnp.dot(gf_ref[k1], _fft_stage2_in(a_scr, k1, rows), preferred_element_type=F32)
        xf, xb = x[:, 0:tc], x[:, tc:2 * tc]
        k_ref[k1] = jnp.concatenate([xf[:n2] + xb[:n2], xf[n2:] - xb[n2:]], axis=0)
        return 0

    lax.fori_loop(0, k1n, mid, 0, unroll=FFT_UNROLL)
    for k1 in range(k1n, k_ref.shape[0]):
        k_ref[k1] = jnp.zeros((2 * n2, tc), F32)


def _hy_spec(h3, w4f, w4b, deltas, f1, gf, k1n, tc):
    seq_len = h3.shape[0]
    dh = w4f.shape[1]
    k1p = gf.shape[0]
    rows, n1h = f1.shape
    nt = 2 * tc // LANES
    return pl.pallas_call(
        functools.partial(_hy_spec_kernel, k1n=k1n),
        grid=(dh // tc,),
        in_specs=[
            pl.BlockSpec((seq_len, LANES), lambda j: (0, 0)),
            pl.BlockSpec((LANES, tc), lambda j: (0, j)),
            pl.BlockSpec((LANES, tc), lambda j: (0, j)),
            pl.BlockSpec((1, tc), lambda j: (0, j)),
            pl.BlockSpec((rows, n1h), lambda j: (0, 0)),
            pl.BlockSpec((k1p, 2 * FFT_N2, 2 * FFT_N2), lambda j: (0, 0, 0)),
        ],
        out_specs=pl.BlockSpec((k1p, 2 * FFT_N2, tc), lambda j: (0, 0, j)),
        out_shape=jax.ShapeDtypeStruct((k1p, 2 * FFT_N2, dh), F32),
        scratch_shapes=[pltpu.VMEM((nt, seq_len, LANES), F32), pltpu.VMEM((nt, FFT_N2 * rows, LANES), F32)],
        compiler_params=_params(1, 56),
        name="hy_spec",
    )(h3, w4f, w4b, deltas, f1, gf)


def _hy_conv_kernel(u_ref, x0_ref, k_ref, f1_ref, f1t_ref, gf_ref, gi_ref, bias_ref, o_ref, x_scr, a_scr, *, k1n):
    n2 = FFT_N2
    seq_len = u_ref.shape[0]
    rows, n1h = f1_ref.shape
    ch = min(512, seq_len)
    for r0 in range(0, seq_len, ch):
        _st(x_scr, r0, ch, u_ref[r0:r0 + ch, :].astype(F32))
    _fft_stage1(x_scr, a_scr, f1_ref)

    def mid(k1, _):
        x = jnp.dot(gf_ref[k1], _fft_stage2_in(a_scr, k1, rows), preferred_element_type=F32)
        kk = k_ref[k1]
        xr, xi, kr, ki = x[:n2], x[n2:], kk[:n2], kk[n2:]
        y = jnp.concatenate([xr * kr - xi * ki, xr * ki + xi * kr], axis=0).astype(BF16)
        bt = jnp.dot(gi_ref[k1], y, preferred_element_type=F32)
        _st(a_scr, 2 * k1, n2, bt[:n2], rows)
        _st(a_scr, 2 * k1 + 1, n2, bt[n2:], rows)
        return 0

    lax.fori_loop(0, k1n, mid, 0, unroll=FFT_UNROLL)

    def last(j, _):
        blk = _ld(a_scr, pl.multiple_of(j * rows, 8), rows).astype(BF16)
        _st(x_scr, j, n1h, jnp.dot(f1t_ref[...], blk, preferred_element_type=F32), n2)
        return 0

    lax.fori_loop(0, n2, last, 0, unroll=FFT_UNROLL)
    for r0 in range(0, seq_len, ch):
        v = _ld(x_scr, r0, ch) + u_ref[r0:r0 + ch, :].astype(F32) * bias_ref[...]
        o_ref[r0:r0 + ch, :] = (x0_ref[r0:r0 + ch, :].astype(F32) * v).astype(o_ref.dtype)


def _hy_conv(u3, x0c, kspec, f1, f1t, gf, gi, bias, k1n, tc):
    nb, seq_len, dh = u3.shape
    k1p = gf.shape[0]
    rows, n1h = f1.shape
    nt = tc // LANES
    blk = pl.BlockSpec((None, seq_len, tc), lambda j, b: (b, 0, j))
    const = lambda shape: pl.BlockSpec(shape, lambda j, b: (0,) * len(shape), pipeline_mode=ONE_BUFFER)
    return pl.pallas_call(
        functools.partial(_hy_conv_kernel, k1n=k1n),
        grid=(dh // tc, nb),
        in_specs=[
            blk, blk,
            pl.BlockSpec((k1p, 2 * FFT_N2, tc), lambda j, b: (0, 0, j), pipeline_mode=ONE_BUFFER),
            const((rows, n1h)), const((n1h, rows)),
            const((k1p, 2 * FFT_N2, 2 * FFT_N2)), const((k1p, 2 * FFT_N2, 2 * FFT_N2)),
            pl.BlockSpec((1, tc), lambda j, b: (0, j)),
        ],
        out_specs=blk,
        out_shape=jax.ShapeDtypeStruct((nb, seq_len, dh), BF16),
        scratch_shapes=[pltpu.VMEM((nt, seq_len, LANES), F32), pltpu.VMEM((nt, FFT_N2 * rows, LANES), F32)],
        compiler_params=_params(2, 56),
        name="hy_conv",
    )(u3, x0c, kspec, f1, f1t, gf, gi, bias)


def _outproj_kernel(ys_ref, yh_ref, w_ref, x_ref, posr_ref, cemb_ref, nwp_ref, g_ref, nwf_ref, sh_ref, sc_ref,
                    xo_ref, h_ref):
    tm, ds = ys_ref.shape
    rows = min(256, tm)
    for r0 in range(0, tm, rows):
        rs = slice(r0, r0 + rows)
        y = jnp.dot(ys_ref[rs, :], w_ref[0:ds, :], preferred_element_type=F32)
        y = y + jnp.dot(yh_ref[rs, :], w_ref[ds:, :], preferred_element_type=F32)
        xn = _add_pos(x_ref[rs, :], posr_ref[rs, :], cemb_ref[...]) + g_ref[...] * (_rms(y) * nwp_ref[...])
        xo_ref[rs, :] = xn
        h_ref[rs, :] = (_rms(xn) * nwf_ref[...] * (1.0 + sc_ref[...]) + sh_ref[...]).astype(h_ref.dtype)


def _out_proj(ys, yh, w, x2, posr, cemb, nwp, g1, nwf, sh2, sc2, seq_len, tm):
    m, d = x2.shape
    ds = ys.shape[1]
    dh = yh.shape[1]
    tiles_per_seq = seq_len // tm
    row = lambda i: (i, 0)
    fixed = lambda i: (0, 0)
    per_batch = lambda i: (i // tiles_per_seq, 0, 0)
    return pl.pallas_call(
        _outproj_kernel,
        grid=(m // tm,),
        in_specs=[
            pl.BlockSpec((tm, ds), row),
            pl.BlockSpec((tm, dh), row),
            pl.BlockSpec((ds + dh, d), fixed, pipeline_mode=ONE_BUFFER),
            pl.BlockSpec((tm, d), row),
            pl.BlockSpec((tm, d // 2), lambda i: (i % tiles_per_seq, 0)),
            pl.BlockSpec(cemb.shape, fixed),
            pl.BlockSpec((1, d), fixed),
            pl.BlockSpec((None, 1, d), per_batch),
            pl.BlockSpec((1, d), fixed),
            pl.BlockSpec((None, 1, d), per_batch),
            pl.BlockSpec((None, 1, d), per_batch),
        ],
        out_specs=[pl.BlockSpec((tm, d), row), pl.BlockSpec((tm, d), row)],
        out_shape=[jax.ShapeDtypeStruct((m, d), F32), jax.ShapeDtypeStruct((m, d), BF16)],
        compiler_params=_params(1, 56),
        name="out_proj",
    )(ys, yh, w, x2, posr, cemb, nwp, g1, nwf, sh2, sc2)


def _ffn_step(h_ref, wg, wu, wd, x_ref, nw_ref, g_ref, o_ref):
    f = pl.program_id(1)
    h = h_ref[...]
    gate = jnp.dot(h, wg, preferred_element_type=F32)
    up = jnp.dot(h, wu, preferred_element_type=F32)
    act = (_silu(gate) * up).astype(BF16)

    @pl.when(f == 0)
    def _():
        o_ref[...] = jnp.zeros_like(o_ref)

    d = o_ref.shape[1]
    cw = min(512, d)
    for n0 in range(0, d, cw):
        o_ref[:, n0:n0 + cw] += jnp.dot(act, wd[:, n0:n0 + cw], preferred_element_type=F32)

    @pl.when(f == pl.num_programs(1) - 1)
    def _():
        tm = o_ref.shape[0]
        rows = min(256, tm)
        for r0 in range(0, tm, rows):
            y = o_ref[r0:r0 + rows, :]
            o_ref[r0:r0 + rows, :] = x_ref[r0:r0 + rows, :] + g_ref[...] * (_rms(y) * nw_ref[...])


def _ffn_first_kernel(h_ref, wg_ref, wu_ref, wd_ref, x_ref, nw_ref, g_ref, o_ref, wgb_ref, wub_ref, wdb_ref):
    wg = wg_ref[...].astype(BF16)
    wu = wu_ref[...].astype(BF16)
    wd = wd_ref[...].astype(BF16)
    wgb_ref[...] = wg
    wub_ref[...] = wu
    wdb_ref[...] = wd
    _ffn_step(h_ref, wg, wu, wd, x_ref, nw_ref, g_ref, o_ref)


def _ffn_rest_kernel(h_ref, wg_ref, wu_ref, wd_ref, x_ref, nw_ref, g_ref, o_ref):
    _ffn_step(h_ref, wg_ref[...], wu_ref[...], wd_ref[...], x_ref, nw_ref, g_ref, o_ref)


def _ffn(h2, wg, wu, wd, xn, nw, g2, seq_len, tm, tf_first, tf):
    m, d = xn.shape
    dff = wg.shape[1]
    tiles_per_seq = seq_len // tm
    n_rows = m // tm
    out, wgb, wub, wdb = pl.pallas_call(
        _ffn_first_kernel,
        grid=(1, dff // tf_first),
        in_specs=[
            pl.BlockSpec((tm, d), lambda i, f: (0, 0), pipeline_mode=ONE_BUFFER),
            pl.BlockSpec((d, tf_first), lambda i, f: (0, f)),
            pl.BlockSpec((d, tf_first), lambda i, f: (0, f)),
            pl.BlockSpec((tf_first, d), lambda i, f: (f, 0)),
            pl.BlockSpec((tm, d), lambda i, f: (0, 0), pipeline_mode=ONE_BUFFER),
            pl.BlockSpec((1, d), lambda i, f: (0, 0)),
            pl.BlockSpec((None, 1, d), lambda i, f: (0, 0, 0)),
        ],
        out_specs=[
            pl.BlockSpec((tm, d), lambda i, f: (0, 0)),
            pl.BlockSpec((d, tf_first), lambda i, f: (0, f)),
            pl.BlockSpec((d, tf_first), lambda i, f: (0, f)),
            pl.BlockSpec((tf_first, d), lambda i, f: (f, 0)),
        ],
        out_shape=[
            jax.ShapeDtypeStruct((m, d), F32),
            jax.ShapeDtypeStruct((d, dff), BF16),
            jax.ShapeDtypeStruct((d, dff), BF16),
            jax.ShapeDtypeStruct((dff, d), BF16),
        ],
        input_output_aliases={4: 0},
        compiler_params=_params(2, 58),
        name="ffn_first",
    )(h2, wg, wu, wd, xn, nw, g2)
    if n_rows == 1:
        return out
    return pl.pallas_call(
        _ffn_rest_kernel,
        grid=(n_rows - 1, dff // tf),
        in_specs=[
            pl.BlockSpec((tm, d), lambda i, f: (i + 1, 0)),
            pl.BlockSpec((d, tf), lambda i, f: (0, f)),
            pl.BlockSpec((d, tf), lambda i, f: (0, f)),
            pl.BlockSpec((tf, d), lambda i, f: (f, 0)),
            pl.BlockSpec((tm, d), lambda i, f: (i + 1, 0), pipeline_mode=ONE_BUFFER),
            pl.BlockSpec((1, d), lambda i, f: (0, 0)),
            pl.BlockSpec((None, 1, d), lambda i, f: ((i + 1) // tiles_per_seq, 0, 0)),
        ],
        out_specs=pl.BlockSpec((tm, d), lambda i, f: (i + 1, 0)),
        out_shape=jax.ShapeDtypeStruct((m, d), F32),
        input_output_aliases={4: 0},
        compiler_params=_params(2, 58),
        name="ffn",
    )(h2, wgb, wub, wdb, out, nw, g2)


def _sincos_tables(rows, cols, dim):
    qd = dim // 4
    omega = 1.0 / (POS_THETA ** (jnp.arange(qd, dtype=F32) / qd))
    r = jnp.arange(rows, dtype=F32)[:, None] * omega
    cc = jnp.arange(cols, dtype=F32)[:, None] * omega
    r_emb = jnp.concatenate([jnp.sin(r), jnp.cos(r)], -1)
    c_emb = jnp.concatenate([jnp.sin(cc), jnp.cos(cc)], -1)
    return r_emb, c_emb


def _filter_features_t(seq_len, n_bands):
    t = jnp.linspace(0.0, 1.0, seq_len, dtype=F32)[:, None]
    w = 2.0 * math.pi * jnp.arange(seq_len, dtype=F32)[:, None] / seq_len
    fb = jnp.linspace(1e-4, n_bands - 1, n_bands, dtype=F32)[None]
    zpos = jnp.concatenate([t, jnp.cos(fb * w), -jnp.sin(fb * w)], -1)
    emb = zpos.shape[1]
    return jnp.pad(zpos, ((0, 0), (0, LANES - emb))).T


def _pad_rows(a, rows):
    return jnp.pad(a, ((0, rows - a.shape[0]), (0, 0)))


def kernel(x, c, ctx, c_ctx, w_ada, b_ada, norm_mix_pre, norm_mix_post, norm_ffn_pre, norm_ffn_post,
           w_in, ssd_conv_w, ssd_conv_b, ssd_a_log, ssd_dt_bias, ssd_d, ssd_norm,
           hy_conv_w, hy_conv_b, hy_w1, hy_b1, hy_w2, hy_b2, hy_w3, hy_b3, hy_w4, hy_freq, hy_bias,
           w_out, w_gate, w_up, w_down):
    nb, seq_len, d = x.shape
    ctx_len = ctx.shape[1]
    assert w_ada.shape[0] == 1, "single layer"
    n_heads = ssd_d.shape[1]
    d_ssd = n_heads * SSD_HEAD_DIM
    d_xbc = d_ssd + 2 * SSD_GROUPS * SSD_STATE
    dh = hy_bias.shape[1]
    assert w_in.shape[2] == d_ssd + d_xbc + 2 * n_heads + 3 * dh
    assert n_heads <= LANES and nb + 1 <= 8
    assert seq_len % (GRID_W * 8) == 0 and seq_len % FFT_N2 == 0
    m = nb * seq_len

    crows = jnp.concatenate([c, c_ctx[None, :]], axis=0)
    mod = _ada(crows, w_ada[0], b_ada[0])
    part = lambda r0, r1, k: mod[r0:r1, k * d:(k + 1) * d][:, None, :]
    sh1, sc1, g1, sh2, sc2, g2 = (part(0, nb, k) for k in range(6))
    csh1 = jnp.broadcast_to(part(nb, nb + 1, 0), (nb, 1, d))
    csc1 = jnp.broadcast_to(part(nb, nb + 1, 1), (nb, 1, d))

    w_t = jnp.transpose(w_in[0])
    o_xbc = d_ssd
    o_dt = o_xbc + d_xbc
    o_hy = o_dt + 2 * n_heads
    tn_in = 512
    assert o_dt % tn_in == 0 and (3 * dh) % tn_in == 0 and o_xbc % tn_in == 0 and o_hy % 16 == 0
    n_left = o_dt // tn_in
    main_off = lambda j: jnp.where(j < n_left, j * tn_in, o_hy + (j - n_left) * tn_in)
    ctx_off = lambda j: o_xbc + j * tn_in
    pad_dt = lambda rows: jnp.pad(rows, ((0, LANES - n_heads), (0, 0)))
    w_dt_t = jnp.concatenate([pad_dt(w_t[o_dt:o_dt + n_heads]), pad_dt(w_t[o_dt + n_heads:o_hy])],
                             axis=0)

    r_emb, c_emb = _sincos_tables(seq_len // GRID_W, GRID_W, d)
    posr = jnp.repeat(r_emb, GRID_W, axis=0)
    nmp = norm_mix_pre[0][None, :]

    tm_in = min(1024, seq_len)
    proj, dt_lat = _in_proj(x.reshape(m, d), posr, c_emb, nmp, sh1, sc1, w_t, main_off, (o_dt + 3 * dh) // tn_in,
                            w_dt_t, seq_len, tm_in, tn_in, True)
    tm_ctx = min(256, ctx_len)
    xbc_ctx, dt_ctx = _in_proj(ctx.reshape(nb * ctx_len, d), jnp.zeros((8, d // 2), F32), c_emb, nmp, csh1, csc1,
                               w_t, ctx_off, d_xbc // tn_in, w_dt_t, ctx_len, tm_ctx, tn_in, False)
    proj3 = proj.reshape(nb, seq_len, -1)

    cw8 = _pad_rows(ssd_conv_w[0], 8)
    cb1 = ssd_conv_b[0][None, :]
    u_lat = _ssd_prep(proj3, d_ssd, d_xbc, cw8, cb1, 512)
    u_ctx = _ssd_prep(xbc_ctx.reshape(nb, ctx_len, d_xbc), 0, d_xbc, cw8, cb1, 512)
    pad_heads = lambda a: jnp.pad(a, ((0, 0), (0, LANES - n_heads)))[:, None, :]
    expand = (jnp.arange(LANES)[:, None] == (jnp.arange(d_ssd)[None, :] // SSD_HEAD_DIM)).astype(BF16)
    y_ssd = _ssd(u_lat, dt_lat.reshape(nb, seq_len, 2 * LANES), u_ctx, dt_ctx.reshape(nb, ctx_len, 2 * LANES),
                 proj3, pad_heads(ssd_a_log[0]), pad_heads(ssd_dt_bias[0]),
                 jnp.repeat(ssd_d[0], SSD_HEAD_DIM)[None, :], ssd_norm[0][None, :], expand)

    x0c, u_hy = _hy_prep(proj3, d_ssd + d_xbc, dh, _pad_rows(hy_conv_w[0], 8), hy_conv_b[0][None, :], 256)
    n_bands = (hy_w1.shape[1] - 1) // 2
    zt = _filter_features_t(seq_len, n_bands)
    w1t = jnp.pad(hy_w1[0].T, ((0, 0), (0, LANES - hy_w1.shape[1])))
    b3 = jnp.stack([hy_b1[0], hy_b2[0], hy_b3[0]], axis=1)
    h3 = _hy_mlp(zt, w1t, hy_w2[0].T, hy_w3[0].T, b3, hy_freq[0][:, None])
    w4 = _pad_rows(hy_w4[0], LANES)
    max_decay = math.log(HY_TARGET) / HY_FAST_PCT
    min_decay = math.log(HY_TARGET) / HY_SLOW_PCT
    deltas = jnp.abs(jnp.linspace(min_decay, max_decay, dh, dtype=F32))[None, :]
    f1, f1t, gf, gi, k1n = _fft_tables(seq_len)
    kspec = _hy_spec(h3, w4[:, :dh], w4[:, dh:], deltas, f1, gf, k1n, 128)
    y_hy = _hy_conv(u_hy, x0c, kspec, f1, f1t, gf, gi, hy_bias[0][None, :], k1n, 256)

    xn, h2 = _out_proj(y_ssd.reshape(m, d_ssd), y_hy.reshape(m, dh), w_out[0].astype(BF16), x.reshape(m, d),
                       posr, c_emb, norm_mix_post[0][None, :], g1, norm_ffn_pre[0][None, :], sh2, sc2, seq_len,
                       min(512, seq_len))
    out = _ffn(h2, w_gate[0], w_up[0], w_down[0], xn, norm_ffn_post[0][None, :], g2, seq_len,
               min(1024, seq_len), 256, 512)
    return out.reshape(nb, seq_len, d)
```

```python
import functools
import math

import numpy as np
import jax
import jax.numpy as jnp
from jax import lax
from jax.experimental import pallas as pl
from jax.experimental.pallas import tpu as pltpu

F32 = jnp.float32
BF16 = jnp.bfloat16
HIGHEST = lax.Precision.HIGHEST

RMS_EPS = 1e-6
POS_THETA = 10000.0
GRID_W = 64
SSD_HEAD_DIM = 64
SSD_GROUPS = 2
SSD_STATE = 128
SSD_CHUNK = 128
HY_TARGET = 1e-2
HY_FAST_PCT = 0.3
HY_SLOW_PCT = 1.5
FFT_N2 = 64
FFT_UNROLL = 8
LANES = 128
MIB = 1024 * 1024
ONE_BUFFER = pl.Buffered(1)
_NT_DIMS = (((1,), (1,)), ((), ()))


def _params(n_axes, vmem_mib):
    return pltpu.CompilerParams(
        dimension_semantics=("arbitrary",) * n_axes,
        vmem_limit_bytes=vmem_mib * MIB,
    )


def _silu(v):
    return v * (1.0 / (1.0 + jnp.exp(-v)))


def _softplus(v):
    return jnp.maximum(v, 0.0) + jnp.log(1.0 + jnp.exp(-jnp.abs(v)))


def _rms(v):
    return v * lax.rsqrt(jnp.mean(v * v, axis=-1, keepdims=True) + RMS_EPS)


def _add_pos(x, posr, cemb):
    tm, d = x.shape
    half = d // 2
    lo = x[:, :half] + posr
    hi = (x[:, half:].reshape(tm // GRID_W, GRID_W, half) + cemb[None]).reshape(tm, half)
    return jnp.concatenate([lo, hi], axis=1)


def _ada_kernel(c_ref, w_ref, b_ref, o_ref, s_scr):
    nr, d, _ = c_ref.shape
    tn = w_ref.shape[1]
    nt = tn // LANES
    kc = 64

    @pl.when(pl.program_id(0) == 0)
    def _():
        for r in range(nr):
            for k0 in range(0, d, 512):
                s_scr[r, k0:k0 + 512, :] = _silu(c_ref[r, k0:k0 + 512, :])

    acc = [[jnp.zeros((8, LANES), F32) for _ in range(nt)] for _ in range(nr)]
    for k0 in range(0, d, kc):
        s = [s_scr[r, k0:k0 + kc, :] for r in range(nr)]
        for t in range(nt):
            w = w_ref[k0:k0 + kc, t * LANES:(t + 1) * LANES]
            for r in range(nr):
                acc[r][t] = acc[r][t] + jnp.sum((w * s[r]).reshape(kc // 8, 8, LANES), axis=0)
    rows = [jnp.concatenate([jnp.sum(a, axis=0, keepdims=True) for a in acc[r]], axis=1) for r in range(nr)]
    rows.append(jnp.zeros((o_ref.shape[0] - nr, tn), F32))
    o_ref[...] = jnp.concatenate(rows, axis=0) + b_ref[...]


def _ada(crows, w_ada, b_ada):
    nr = crows.shape[0]
    d, n = w_ada.shape
    tn = 1024
    c_lanes = jnp.broadcast_to(crows[:, :, None], (nr, d, LANES))
    return pl.pallas_call(
        _ada_kernel,
        grid=(n // tn,),
        in_specs=[
            pl.BlockSpec((nr, d, LANES), lambda j: (0, 0, 0), pipeline_mode=ONE_BUFFER),
            pl.BlockSpec((d, tn), lambda j: (0, j)),
            pl.BlockSpec((1, tn), lambda j: (0, j)),
        ],
        out_specs=pl.BlockSpec((8, tn), lambda j: (0, j)),
        out_shape=jax.ShapeDtypeStruct((8, n), F32),
        scratch_shapes=[pltpu.VMEM((nr, d, LANES), F32)],
        compiler_params=_params(1, 40),
        name="ada",
    )(c_lanes, w_ada, b_ada.reshape(1, n))


def _inproj_body(x_ref, posr_ref, cemb_ref, nw_ref, sh_ref, sc_ref, get_w, wdt_ref, o_ref, dt_ref, h_scr, with_pos):
    first = pl.program_id(1) == 0

    @pl.when(first)
    def _():
        tm = x_ref.shape[0]
        rows = min(256, tm)
        wdt = wdt_ref[...].astype(BF16)
        w = get_w()
        for r0 in range(0, tm, rows):
            xf = x_ref[r0:r0 + rows, :]
            if with_pos:
                xf = _add_pos(xf, posr_ref[r0:r0 + rows, :], cemb_ref[...])
            h = _rms(xf) * nw_ref[...] * (1.0 + sc_ref[...]) + sh_ref[...]
            hb = h.astype(BF16)
            h_scr[r0:r0 + rows, :] = hb
            dt_ref[r0:r0 + rows, :] = lax.dot_general(hb, wdt, _NT_DIMS, preferred_element_type=F32)
            o_ref[r0:r0 + rows, :] = jnp.dot(hb, w, preferred_element_type=F32).astype(o_ref.dtype)

    @pl.when(jnp.logical_not(first))
    def _():
        o_ref[...] = jnp.dot(h_scr[...], get_w(), preferred_element_type=F32).astype(o_ref.dtype)


def _inproj_t_kernel(x_ref, posr_ref, cemb_ref, nw_ref, sh_ref, sc_ref, w_ref, wdt_ref, *rest, with_pos, emit_w):
    w = w_ref[...].T.astype(BF16)
    if emit_w:
        o_ref, dt_ref, wb_ref, h_scr = rest
        wb_ref[...] = w
    else:
        o_ref, dt_ref, h_scr = rest
    _inproj_body(x_ref, posr_ref, cemb_ref, nw_ref, sh_ref, sc_ref, lambda: w, wdt_ref, o_ref, dt_ref, h_scr,
                 with_pos)


def _inproj_rest_kernel(x_ref, posr_ref, cemb_ref, nw_ref, sh_ref, sc_ref, w_ref, wdt_ref, o_prev, dt_prev,
                        o_ref, dt_ref, h_scr):
    del o_prev, dt_prev
    _inproj_body(x_ref, posr_ref, cemb_ref, nw_ref, sh_ref, sc_ref, lambda: w_ref[...], wdt_ref, o_ref, dt_ref,
                 h_scr, True)


def _in_proj_specs(tm, d, half, pos_rows, cemb, ndt, tiles_per_seq, with_pos, row0, one_row_block):
    blk = lambda i: i + row0
    mode = ONE_BUFFER if one_row_block else None
    return [
        pl.BlockSpec((tm, d), lambda i, j: (blk(i), 0), pipeline_mode=mode),
        pl.BlockSpec((pos_rows, half), lambda i, j: ((blk(i) % tiles_per_seq) if with_pos else 0, 0),
                     pipeline_mode=mode),
        pl.BlockSpec(cemb.shape, lambda i, j: (0, 0), pipeline_mode=ONE_BUFFER),
        pl.BlockSpec((1, d), lambda i, j: (0, 0)),
        pl.BlockSpec((None, 1, d), lambda i, j: (blk(i) // tiles_per_seq, 0, 0)),
        pl.BlockSpec((None, 1, d), lambda i, j: (blk(i) // tiles_per_seq, 0, 0)),
    ], pl.BlockSpec((ndt, d), lambda i, j: (0, 0), pipeline_mode=ONE_BUFFER)


def _in_proj(x2, posr, cemb, nw, sh, sc, w_t, row_off, n_tiles, wdt_t, seq_len, tm, tn, with_pos, tn_rest=None):
    m, d = x2.shape
    n = n_tiles * tn
    ndt = wdt_t.shape[0]
    half = d // 2
    tiles_per_seq = seq_len // tm
    pos_rows = tm if with_pos else posr.shape[0]
    split = tn_rest is not None and m // tm > 1
    common, wdt_spec = _in_proj_specs(tm, d, half, pos_rows, cemb, ndt, tiles_per_seq, with_pos, 0, split)
    w_spec = pl.BlockSpec((pl.Element(tn), pl.Element(d)), lambda i, j: (pl.multiple_of(row_off(j), 8), 0))
    out_specs = [pl.BlockSpec((tm, tn), lambda i, j: (i, j)), pl.BlockSpec((tm, ndt), lambda i, j: (i, 0))]
    out_shape = [jax.ShapeDtypeStruct((m, n), BF16), jax.ShapeDtypeStruct((m, ndt), F32)]
    if split:
        out_specs.append(pl.BlockSpec((d, tn), lambda i, j: (0, j)))
        out_shape.append(jax.ShapeDtypeStruct((d, n), BF16))
    res = pl.pallas_call(
        functools.partial(_inproj_t_kernel, with_pos=with_pos, emit_w=split),
        grid=(1 if split else m // tm, n_tiles),
        in_specs=common + [w_spec, wdt_spec],
        out_specs=out_specs,
        out_shape=out_shape,
        scratch_shapes=[pltpu.VMEM((tm, d), BF16)],
        compiler_params=_params(2, 58),
        name="in_proj_t",
    )(x2, posr, cemb, nw, sh, sc, w_t, wdt_t)
    if not split:
        return res
    proj, dt, w_bf = res
    common, wdt_spec = _in_proj_specs(tm, d, half, pos_rows, cemb, ndt, tiles_per_seq, with_pos, 1, False)
    return pl.pallas_call(
        _inproj_rest_kernel,
        grid=(m // tm - 1, n // tn_rest),
        in_specs=common + [
            pl.BlockSpec((d, tn_rest), lambda i, j: (0, j)),
            wdt_spec,
            pl.BlockSpec(memory_space=pl.ANY),
            pl.BlockSpec(memory_space=pl.ANY),
        ],
        out_specs=[pl.BlockSpec((tm, tn_rest), lambda i, j: (i + 1, j)),
                   pl.BlockSpec((tm, ndt), lambda i, j: (i + 1, 0))],
        out_shape=[jax.ShapeDtypeStruct((m, n), BF16), jax.ShapeDtypeStruct((m, ndt), F32)],
        input_output_aliases={8: 0, 9: 1},
        scratch_shapes=[pltpu.VMEM((tm, d), BF16)],
        compiler_params=_params(2, 58),
        name="in_proj",
    )(x2, posr, cemb, nw, sh, sc, w_bf, wdt_t, proj, dt)


def _conv3_chunk(x_ref, r0, rows, seq_len, w, b):
    cur = x_ref[r0:r0 + rows, :].astype(F32)
    tc = cur.shape[1]
    rid = lax.broadcasted_iota(jnp.int32, (rows, tc), 0)
    if r0 > 0:
        prev_row = x_ref[r0 - 16:r0, :].astype(F32)[15:16, :]
    else:
        prev_row = jnp.zeros((1, tc), F32)
    if r0 + rows < seq_len:
        next_row = x_ref[r0 + rows:r0 + rows + 16, :].astype(F32)[0:1, :]
    else:
        next_row = jnp.zeros((1, tc), F32)
    up = jnp.where(rid == 0, prev_row, pltpu.roll(cur, 1, 0))
    down = jnp.where(rid == rows - 1, next_row, pltpu.roll(cur, rows - 1, 0))
    return up * w[0:1, :] + cur * w[1:2, :] + down * w[2:3, :] + b


def _ssd_prep_kernel(x_ref, w_ref, b_ref, o_ref, *, rows):
    seq_len = x_ref.shape[0]
    w = w_ref[...]
    b = b_ref[...]
    for r0 in range(0, seq_len, rows):
        v = _conv3_chunk(x_ref, r0, rows, seq_len, w, b)
        o_ref[r0:r0 + rows, :] = _silu(v).astype(o_ref.dtype)


def _ssd_prep(proj3, col0, width, w8, b1, tc):
    nb, seq_len, _ = proj3.shape
    rows = min(512, seq_len)
    j0 = col0 // tc
    return pl.pallas_call(
        functools.partial(_ssd_prep_kernel, rows=rows),
        grid=(nb, width // tc),
        in_specs=[
            pl.BlockSpec((None, seq_len, tc), lambda b, j: (b, 0, j0 + j)),
            pl.BlockSpec((8, tc), lambda b, j: (0, j)),
            pl.BlockSpec((1, tc), lambda b, j: (0, j)),
        ],
        out_specs=pl.BlockSpec((None, seq_len, tc), lambda b, j: (b, 0, j)),
        out_shape=jax.ShapeDtypeStruct((nb, seq_len, width), BF16),
        compiler_params=_params(2, 40),
        name="ssd_prep",
    )(proj3, w8, b1)


def _hy_prep_kernel(x0_ref, x1_ref, v_ref, w0_ref, w1_ref, wv_ref, b0_ref, b1_ref, bv_ref,
                    x0c_ref, u_ref, *, rows):
    seq_len = x0_ref.shape[0]
    w0, w1, wv = w0_ref[...], w1_ref[...], wv_ref[...]
    b0, b1, bv = b0_ref[...], b1_ref[...], bv_ref[...]
    for r0 in range(0, seq_len, rows):
        x0c_ref[r0:r0 + rows, :] = _conv3_chunk(x0_ref, r0, rows, seq_len, w0, b0).astype(x0c_ref.dtype)
        x1c = _conv3_chunk(x1_ref, r0, rows, seq_len, w1, b1)
        vc = _conv3_chunk(v_ref, r0, rows, seq_len, wv, bv)
        u_ref[r0:r0 + rows, :] = (vc * x1c).astype(u_ref.dtype)


def _hy_prep(proj3, col0, dh, w8, b1, tc):
    nb, seq_len, _ = proj3.shape
    rows = min(512, seq_len)
    j0 = col0 // tc
    nj = dh // tc
    x_spec = lambda k: pl.BlockSpec((None, seq_len, tc), lambda b, j: (b, 0, j0 + k * nj + j))
    w_spec = lambda k: pl.BlockSpec((8, tc), lambda b, j: (0, k * nj + j))
    b_spec = lambda k: pl.BlockSpec((1, tc), lambda b, j: (0, k * nj + j))
    o_spec = pl.BlockSpec((None, seq_len, tc), lambda b, j: (b, 0, j))
    return pl.pallas_call(
        functools.partial(_hy_prep_kernel, rows=rows),
        grid=(nb, nj),
        in_specs=[x_spec(0), x_spec(1), x_spec(2), w_spec(0), w_spec(1), w_spec(2),
                  b_spec(0), b_spec(1), b_spec(2)],
        out_specs=[o_spec, o_spec],
        out_shape=[jax.ShapeDtypeStruct((nb, seq_len, dh), BF16)] * 2,
        compiler_params=_params(2, 48),
        name="hy_prep",
    )(proj3, proj3, proj3, w8, w8, w8, b1, b1, b1)


def _ssd_kernel(u_ref, dt_ref, uc_ref, dtc_ref, z_ref, alog_ref, dtb_ref, dsk_ref, nw_ref, e_ref,
                o_ref, h_scr, yf_scr, *, n_ctx, n_lat):
    q = SSD_CHUNK
    d_ssd = h_scr.shape[1]
    gw = d_ssd // SSD_GROUPS
    d = pl.program_id(1)
    s = pl.program_id(2)
    is_ctx = s < n_ctx
    t = jnp.maximum(s - n_ctx, 0)
    cidx = jnp.where(d == 0, t, n_lat - 1 - t)

    @pl.when(s == 0)
    def _():
        h_scr[...] = jnp.zeros_like(h_scr)

    u = jnp.where(is_ctx, uc_ref[...], u_ref[...])
    dtr = jnp.where(is_ctx, dtc_ref[...], dt_ref[...])

    dt = _softplus(dtr + dtb_ref[...])
    a = dt * (-jnp.exp(alog_ref[...]))
    row = lax.broadcasted_iota(jnp.int32, (q, q), 0)
    col = lax.broadcasted_iota(jnp.int32, (q, q), 1)
    fwd = d == 0
    tri = jnp.where(fwd, row, col) >= jnp.where(fwd, col, row)
    cum = jnp.dot(tri.astype(F32), a, precision=HIGHEST, preferred_element_type=F32)
    cum_t = cum.T
    tot = jnp.sum(a, axis=0, keepdims=True)

    stack = jnp.concatenate(
        [dt, jnp.exp(cum), jnp.exp(tot - cum), jnp.broadcast_to(jnp.exp(tot), (8, LANES))], axis=0)
    ex = jnp.dot(stack.astype(BF16), e_ref[...], preferred_element_type=F32)
    dt_x = ex[0:q]
    ecum_x = ex[q:2 * q]
    edec_x = ex[2 * q:3 * q]
    etot_x = ex[3 * q:3 * q + 1]

    xs = u[:, :d_ssd].astype(F32)
    xdt = xs * dt_x
    xdt_b = xdt.astype(BF16)
    xdw_b = (xdt * edec_x).astype(BF16)
    lane_lo = lax.broadcasted_iota(jnp.int32, (q, LANES), 1) < SSD_HEAD_DIM
    heads_per_group = gw // SSD_HEAD_DIM

    y_parts = []
    for g in range(SSD_GROUPS):
        bg = u[:, d_ssd + g * SSD_STATE:d_ssd + (g + 1) * SSD_STATE]
        cg = u[:, d_ssd + (SSD_GROUPS + g) * SSD_STATE:d_ssd + (SSD_GROUPS + g + 1) * SSD_STATE]
        scores = lax.dot_general(cg, bg, (((1,), (1,)), ((), ())), preferred_element_type=F32)
        h_prev = h_scr[:, g * gw:(g + 1) * gw]
        y_off = jnp.dot(cg, h_prev.astype(BF16), preferred_element_type=F32)
        for j in range(heads_per_group // 2):
            c0 = g * gw + j * LANES
            xp = xdt_b[:, c0:c0 + LANES]
            acc = None
            for hh in range(2):
                h = g * heads_per_group + 2 * j + hh
                diff = cum[:, h:h + 1] - cum_t[h:h + 1, :]
                decay = jnp.exp(jnp.where(tri, diff, -1e30))
                m_h = (scores * decay).astype(BF16)
                x_h = jnp.where(lane_lo if hh == 0 else jnp.logical_not(lane_lo), xp, jnp.zeros_like(xp))
                part = jnp.dot(m_h, x_h, preferred_element_type=F32)
                acc = part if acc is None else acc + part
            y_parts.append(acc + y_off[:, j * LANES:(j + 1) * LANES] * ecum_x[:, c0:c0 + LANES])
        upd = lax.dot_general(bg, xdw_b[:, g * gw:(g + 1) * gw], (((0,), (0,)), ((), ())),
                              preferred_element_type=F32)
        h_scr[:, g * gw:(g + 1) * gw] = etot_x[:, g * gw:(g + 1) * gw] * h_prev + upd
    y = jnp.concatenate(y_parts, axis=1)

    @pl.when(jnp.logical_and(jnp.logical_not(is_ctx), fwd))
    def _():
        yf_scr[cidx] = y

    @pl.when(jnp.logical_and(jnp.logical_not(is_ctx), jnp.logical_not(fwd)))
    def _():
        ytot = yf_scr[cidx] + y + dsk_ref[...] * xs
        gated = ytot * _silu(z_ref[...].astype(F32))
        o_ref[...] = (_rms(gated) * nw_ref[...]).astype(o_ref.dtype)


def _ssd(u_lat, dt_lat, u_ctx, dt_ctx, proj3, alog, dtb, dsk, nw, expand):
    nb, seq_len, d_xbc = u_lat.shape
    ctx_len = u_ctx.shape[1]
    q = SSD_CHUNK
    n_lat = seq_len // q
    n_ctx = ctx_len // q
    d_ssd = dsk.shape[1]

    def lat_idx(d, s):
        t = jnp.maximum(s - n_ctx, 0)
        return jnp.where(d == 0, t, n_lat - 1 - t)

    def ctx_idx(d, s):
        t = jnp.minimum(s, n_ctx - 1)
        return jnp.where(d == 0, t, n_ctx - 1 - t)

    def out_idx(d, s):
        return jnp.where(d == 0, n_lat - 1, lat_idx(d, s))

    return pl.pallas_call(
        functools.partial(_ssd_kernel, n_ctx=n_ctx, n_lat=n_lat),
        grid=(nb, 2, n_ctx + n_lat),
        in_specs=[
            pl.BlockSpec((None, q, d_xbc), lambda b, d, s: (b, lat_idx(d, s), 0)),
            pl.BlockSpec((None, q, LANES), lambda b, d, s: (b, lat_idx(d, s), d)),
            pl.BlockSpec((None, q, d_xbc), lambda b, d, s: (b, ctx_idx(d, s), 0)),
            pl.BlockSpec((None, q, LANES), lambda b, d, s: (b, ctx_idx(d, s), d)),
            pl.BlockSpec((None, q, d_ssd), lambda b, d, s: (b, lat_idx(d, s), 0)),
            pl.BlockSpec((None, 1, LANES), lambda b, d, s: (d, 0, 0)),
            pl.BlockSpec((None, 1, LANES), lambda b, d, s: (d, 0, 0)),
            pl.BlockSpec((1, d_ssd), lambda b, d, s: (0, 0)),
            pl.BlockSpec((1, d_ssd), lambda b, d, s: (0, 0)),
            pl.BlockSpec((LANES, d_ssd), lambda b, d, s: (0, 0)),
        ],
        out_specs=pl.BlockSpec((None, q, d_ssd), lambda b, d, s: (b, out_idx(d, s), 0)),
        out_shape=jax.ShapeDtypeStruct((nb, seq_len, d_ssd), BF16),
        scratch_shapes=[
            pltpu.VMEM((SSD_STATE, d_ssd), F32),
            pltpu.VMEM((n_lat, q, d_ssd), F32),
        ],
        compiler_params=_params(3, 48),
        name="ssd",
    )(u_lat, dt_lat, u_ctx, dt_ctx, proj3, alog, dtb, dsk, nw, expand)


def _hy_mlp_kernel(zt_ref, w1_ref, w2_ref, w3_ref, b_ref, fr_ref, o_ref):
    fr = fr_ref[...]
    b = b_ref[...]
    h = jnp.sin(fr * (jnp.dot(w1_ref[...], zt_ref[...], precision=HIGHEST, preferred_element_type=F32)
                      + b[:, 0:1]))
    h = jnp.sin(fr * (jnp.dot(w2_ref[...], h, precision=HIGHEST, preferred_element_type=F32) + b[:, 1:2]))
    h = jnp.sin(fr * (jnp.dot(w3_ref[...], h, precision=HIGHEST, preferred_element_type=F32) + b[:, 2:3]))
    hid, seq_len = h.shape
    hp = jnp.concatenate([h, jnp.zeros((LANES - hid, seq_len), F32)], axis=0)
    o_ref[...] = hp.T


def _hy_mlp(zt, w1t, w2t, w3t, b3, fr):
    seq_len = zt.shape[1]
    return pl.pallas_call(
        _hy_mlp_kernel,
        out_shape=jax.ShapeDtypeStruct((seq_len, LANES), F32),
        compiler_params=pltpu.CompilerParams(vmem_limit_bytes=40 * MIB),
        name="hy_mlp",
    )(zt, w1t, w2t, w3t, b3, fr)


def _fft_tables(seq_len):
    n_fft = 2 * seq_len
    n2 = FFT_N2
    n1 = n_fft // n2
    n1h = n1 // 2
    k1n = n1h + 1
    k1p = -(-k1n // 4) * 4
    k1 = np.arange(k1n, dtype=np.int64)
    th = (2.0 * np.pi / n1) * ((k1[:, None] * np.arange(n1h, dtype=np.int64)[None, :]) % n1)
    f1 = np.zeros((2 * k1p, n1h))
    f1[0:2 * k1n:2] = np.cos(th)
    f1[1:2 * k1n:2] = -np.sin(th)
    idx = np.arange(n2, dtype=np.int64)
    kk = k1[:, None, None] + n1 * idx[None, :, None]
    ph = (2.0 * np.pi / n_fft) * ((kk * idx[None, None, :]) % n_fft)
    g_re, g_im = np.cos(ph), -np.sin(ph)

    def blocks(re, im):
        out = np.zeros((k1p, 2 * n2, 2 * n2))
        out[:k1n, :n2, :n2] = re
        out[:k1n, :n2, n2:] = -im
        out[:k1n, n2:, :n2] = im
        out[:k1n, n2:, n2:] = re
        return out

    wgt = np.where((k1 == 0) | (k1 == n1h), 1.0, 2.0) / n_fft
    gf = blocks(g_re, g_im)
    gi = blocks(np.transpose(g_re, (0, 2, 1)) * wgt[:, None, None],
                -np.transpose(g_im, (0, 2, 1)) * wgt[:, None, None])
    as_bf16 = lambda t: jnp.asarray(t.astype(np.float32)).astype(BF16)
    return as_bf16(f1), as_bf16(np.ascontiguousarray(f1.T)), as_bf16(gf), as_bf16(gi), k1n


def _ld(ref, start, size, stride=None):
    idx = pl.ds(start, size) if stride is None else pl.ds(start, size, stride=stride)
    return jnp.concatenate([ref[t, idx, :] for t in range(ref.shape[0])], axis=1)


def _st(ref, start, size, val, stride=None):
    idx = pl.ds(start, size) if stride is None else pl.ds(start, size, stride=stride)
    for t in range(ref.shape[0]):
        ref[t, idx, :] = val[:, t * LANES:(t + 1) * LANES]


def _fft_stage1(x_scr, a_scr, f1_ref):
    rows, n1h = f1_ref.shape

    def body(n2, _):
        xs = _ld(x_scr, n2, n1h, FFT_N2).astype(BF16)
        _st(a_scr, pl.multiple_of(n2 * rows, 8), rows, jnp.dot(f1_ref[...], xs, preferred_element_type=F32))
        return 0

    lax.fori_loop(0, FFT_N2, body, 0, unroll=FFT_UNROLL)


def _fft_stage2_in(a_scr, k1, rows):
    return jnp.concatenate([_ld(a_scr, 2 * k1, FFT_N2, rows), _ld(a_scr, 2 * k1 + 1, FFT_N2, rows)],
                           axis=0).astype(BF16)


def _hy_spec_kernel(h_ref, wf_ref, wb_ref, dl_ref, f1_ref, gf_ref, k_ref, x_scr, a_scr, *, k1n):
    n2 = FFT_N2
    seq_len = h_ref.shape[0]
    tc = wf_ref.shape[1]
    rows = f1_ref.shape[0]
    h3 = h_ref[...].astype(BF16)
    rid = lax.broadcasted_iota(jnp.int32, (seq_len, tc), 0)
    decay = jnp.exp(-(rid.astype(F32) * (1.0 / (seq_len - 1))) * dl_ref[...])
    hf = jnp.dot(h3, wf_ref[...].astype(BF16), preferred_element_type=F32) * decay
    hb = jnp.dot(h3, wb_ref[...].astype(BF16), preferred_element_type=F32) * decay
    norm = jnp.sum(jnp.abs(hf) + jnp.abs(hb), axis=0, keepdims=True) + 1e-6
    inv = 1.0 / norm
    hb = jnp.where(rid == 0, 0.0, hb * inv)
    _st(x_scr, 0, seq_len, jnp.concatenate([hf * inv, hb], axis=1))
    _fft_stage1(x_scr, a_scr, f1_ref)

    def mid(k1, _):
        x = jnp.dot(gf_ref[k1], _fft_stage2_in(a_scr, k1, rows), preferred_element_type=F32)
        xf, xb = x[:, 0:tc], x[:, tc:2 * tc]
        k_ref[k1] = jnp.concatenate([xf[:n2] + xb[:n2], xf[n2:] - xb[n2:]], axis=0)
        return 0

    lax.fori_loop(0, k1n, mid, 0, unroll=FFT_UNROLL)
    for k1 in range(k1n, k_ref.shape[0]):
        k_ref[k1] = jnp.zeros((2 * n2, tc), F32)


def _hy_spec(h3, w4f, w4b, deltas, f1, gf, k1n, tc):
    seq_len = h3.shape[0]
    dh = w4f.shape[1]
    k1p = gf.shape[0]
    rows, n1h = f1.shape
    nt = 2 * tc // LANES
    return pl.pallas_call(
        functools.partial(_hy_spec_kernel, k1n=k1n),
        grid=(dh // tc,),
        in_specs=[
            pl.BlockSpec((seq_len, LANES), lambda j: (0, 0)),
            pl.BlockSpec((LANES, tc), lambda j: (0, j)),
            pl.BlockSpec((LANES, tc), lambda j: (0, j)),
            pl.BlockSpec((1, tc), lambda j: (0, j)),
            pl.BlockSpec((rows, n1h), lambda j: (0, 0)),
            pl.BlockSpec((k1p, 2 * FFT_N2, 2 * FFT_N2), lambda j: (0, 0, 0)),
        ],
        out_specs=pl.BlockSpec((k1p, 2 * FFT_N2, tc), lambda j: (0, 0, j)),
        out_shape=jax.ShapeDtypeStruct((k1p, 2 * FFT_N2, dh), F32),
        scratch_shapes=[pltpu.VMEM((nt, seq_len, LANES), F32), pltpu.VMEM((nt, FFT_N2 * rows, LANES), F32)],
        compiler_params=_params(1, 56),
        name="hy_spec",
    )(h3, w4f, w4b, deltas, f1, gf)


def _hy_conv_kernel(u_ref, x0_ref, k_ref, f1_ref, f1t_ref, gf_ref, gi_ref, bias_ref, o_ref, x_scr, a_scr, *, k1n):
    n2 = FFT_N2
    seq_len = u_ref.shape[0]
    rows, n1h = f1_ref.shape
    ch = min(512, seq_len)
    for r0 in range(0, seq_len, ch):
        _st(x_scr, r0, ch, u_ref[r0:r0 + ch, :].astype(F32))
    _fft_stage1(x_scr, a_scr, f1_ref)

    def mid(k1, _):
        x = jnp.dot(gf_ref[k1], _fft_stage2_in(a_scr, k1, rows), preferred_element_type=F32)
        kk = k_ref[k1]
        xr, xi, kr, ki = x[:n2], x[n2:], kk[:n2], kk[n2:]
        y = jnp.concatenate([xr * kr - xi * ki, xr * ki + xi * kr], axis=0).astype(BF16)
        bt = jnp.dot(gi_ref[k1], y, preferred_element_type=F32)
        _st(a_scr, 2 * k1, n2, bt[:n2], rows)
        _st(a_scr, 2 * k1 + 1, n2, bt[n2:], rows)
        return 0

    lax.fori_loop(0, k1n, mid, 0, unroll=FFT_UNROLL)

    def last(j, _):
        blk = _ld(a_scr, pl.multiple_of(j * rows, 8), rows).astype(BF16)
        _st(x_scr, j, n1h, jnp.dot(f1t_ref[...], blk, preferred_element_type=F32), n2)
        return 0

    lax.fori_loop(0, n2, last, 0, unroll=FFT_UNROLL)
    for r0 in range(0, seq_len, ch):
        v = _ld(x_scr, r0, ch) + u_ref[r0:r0 + ch, :].astype(F32) * bias_ref[...]
        o_ref[r0:r0 + ch, :] = (x0_ref[r0:r0 + ch, :].astype(F32) * v).astype(o_ref.dtype)


def _hy_conv(u3, x0c, kspec, f1, f1t, gf, gi, bias, k1n, tc):
    nb, seq_len, dh = u3.shape
    k1p = gf.shape[0]
    rows, n1h = f1.shape
    nt = tc // LANES
    blk = pl.BlockSpec((None, seq_len, tc), lambda j, b: (b, 0, j))
    const = lambda shape: pl.BlockSpec(shape, lambda j, b: (0,) * len(shape), pipeline_mode=ONE_BUFFER)
    return pl.pallas_call(
        functools.partial(_hy_conv_kernel, k1n=k1n),
        grid=(dh // tc, nb),
        in_specs=[
            blk, blk,
            pl.BlockSpec((k1p, 2 * FFT_N2, tc), lambda j, b: (0, 0, j), pipeline_mode=ONE_BUFFER),
            const((rows, n1h)), const((n1h, rows)),
            const((k1p, 2 * FFT_N2, 2 * FFT_N2)), const((k1p, 2 * FFT_N2, 2 * FFT_N2)),
            pl.BlockSpec((1, tc), lambda j, b: (0, j)),
        ],
        out_specs=blk,
        out_shape=jax.ShapeDtypeStruct((nb, seq_len, dh), BF16),
        scratch_shapes=[pltpu.VMEM((nt, seq_len, LANES), F32), pltpu.VMEM((nt, FFT_N2 * rows, LANES), F32)],
        compiler_params=_params(2, 56),
        name="hy_conv",
    )(u3, x0c, kspec, f1, f1t, gf, gi, bias)


def _outproj_kernel(ys_ref, yh_ref, w_ref, x_ref, posr_ref, cemb_ref, nwp_ref, g_ref, nwf_ref, sh_ref, sc_ref,
                    xo_ref, h_ref):
    tm, ds = ys_ref.shape
    rows = min(256, tm)
    for r0 in range(0, tm, rows):
        rs = slice(r0, r0 + rows)
        y = jnp.dot(ys_ref[rs, :], w_ref[0:ds, :], preferred_element_type=F32)
        y = y + jnp.dot(yh_ref[rs, :], w_ref[ds:, :], preferred_element_type=F32)
        xn = _add_pos(x_ref[rs, :], posr_ref[rs, :], cemb_ref[...]) + g_ref[...] * (_rms(y) * nwp_ref[...])
        xo_ref[rs, :] = xn
        h_ref[rs, :] = (_rms(xn) * nwf_ref[...] * (1.0 + sc_ref[...]) + sh_ref[...]).astype(h_ref.dtype)


def _out_proj(ys, yh, w, x2, posr, cemb, nwp, g1, nwf, sh2, sc2, seq_len, tm):
    m, d = x2.shape
    ds = ys.shape[1]
    dh = yh.shape[1]
    tiles_per_seq = seq_len // tm
    row = lambda i: (i, 0)
    fixed = lambda i: (0, 0)
    per_batch = lambda i: (i // tiles_per_seq, 0, 0)
    return pl.pallas_call(
        _outproj_kernel,
        grid=(m // tm,),
        in_specs=[
            pl.BlockSpec((tm, ds), row),
            pl.BlockSpec((tm, dh), row),
            pl.BlockSpec((ds + dh, d), fixed, pipeline_mode=ONE_BUFFER),
            pl.BlockSpec((tm, d), row),
            pl.BlockSpec((tm, d // 2), lambda i: (i % tiles_per_seq, 0)),
            pl.BlockSpec(cemb.shape, fixed),
            pl.BlockSpec((1, d), fixed),
            pl.BlockSpec((None, 1, d), per_batch),
            pl.BlockSpec((1, d), fixed),
            pl.BlockSpec((None, 1, d), per_batch),
            pl.BlockSpec((None, 1, d), per_batch),
        ],
        out_specs=[pl.BlockSpec((tm, d), row), pl.BlockSpec((tm, d), row)],
        out_shape=[jax.ShapeDtypeStruct((m, d), F32), jax.ShapeDtypeStruct((m, d), BF16)],
        compiler_params=_params(1, 56),
        name="out_proj",
    )(ys, yh, w, x2, posr, cemb, nwp, g1, nwf, sh2, sc2)


def _ffn_step(h_ref, wg, wu, wd, x_ref, nw_ref, g_ref, o_ref):
    f = pl.program_id(1)
    h = h_ref[...]
    gate = jnp.dot(h, wg, preferred_element_type=F32)
    up = jnp.dot(h, wu, preferred_element_type=F32)
    act = (_silu(gate) * up).astype(BF16)

    @pl.when(f == 0)
    def _():
        o_ref[...] = jnp.zeros_like(o_ref)

    d = o_ref.shape[1]
    cw = min(512, d)
    for n0 in range(0, d, cw):
        o_ref[:, n0:n0 + cw] += jnp.dot(act, wd[:, n0:n0 + cw], preferred_element_type=F32)

    @pl.when(f == pl.num_programs(1) - 1)
    def _():
        tm = o_ref.shape[0]
        rows = min(256, tm)
        for r0 in range(0, tm, rows):
            y = o_ref[r0:r0 + rows, :]
            o_ref[r0:r0 + rows, :] = x_ref[r0:r0 + rows, :] + g_ref[...] * (_rms(y) * nw_ref[...])


def _ffn_first_kernel(h_ref, wg_ref, wu_ref, wd_ref, x_ref, nw_ref, g_ref, o_ref, wgb_ref, wub_ref, wdb_ref):
    wg = wg_ref[...].astype(BF16)
    wu = wu_ref[...].astype(BF16)
    wd = wd_ref[...].astype(BF16)
    wgb_ref[...] = wg
    wub_ref[...] = wu
    wdb_ref[...] = wd
    _ffn_step(h_ref, wg, wu, wd, x_ref, nw_ref, g_ref, o_ref)


def _ffn_rest_kernel(h_ref, wg_ref, wu_ref, wd_ref, x_ref, nw_ref, g_ref, o_ref):
    _ffn_step(h_ref, wg_ref[...], wu_ref[...], wd_ref[...], x_ref, nw_ref, g_ref, o_ref)


def _ffn(h2, wg, wu, wd, xn, nw, g2, seq_len, tm, tf_first, tf):
    m, d = xn.shape
    dff = wg.shape[1]
    tiles_per_seq = seq_len // tm
    n_rows = m // tm
    out, wgb, wub, wdb = pl.pallas_call(
        _ffn_first_kernel,
        grid=(1, dff // tf_first),
        in_specs=[
            pl.BlockSpec((tm, d), lambda i, f: (0, 0), pipeline_mode=ONE_BUFFER),
            pl.BlockSpec((d, tf_first), lambda i, f: (0, f)),
            pl.BlockSpec((d, tf_first), lambda i, f: (0, f)),
            pl.BlockSpec((tf_first, d), lambda i, f: (f, 0)),
            pl.BlockSpec((tm, d), lambda i, f: (0, 0), pipeline_mode=ONE_BUFFER),
            pl.BlockSpec((1, d), lambda i, f: (0, 0)),
            pl.BlockSpec((None, 1, d), lambda i, f: (0, 0, 0)),
        ],
        out_specs=[
            pl.BlockSpec((tm, d), lambda i, f: (0, 0)),
            pl.BlockSpec((d, tf_first), lambda i, f: (0, f)),
            pl.BlockSpec((d, tf_first), lambda i, f: (0, f)),
            pl.BlockSpec((tf_first, d), lambda i, f: (f, 0)),
        ],
        out_shape=[
            jax.ShapeDtypeStruct((m, d), F32),
            jax.ShapeDtypeStruct((d, dff), BF16),
            jax.ShapeDtypeStruct((d, dff), BF16),
            jax.ShapeDtypeStruct((dff, d), BF16),
        ],
        input_output_aliases={4: 0},
        compiler_params=_params(2, 58),
        name="ffn_first",
    )(h2, wg, wu, wd, xn, nw, g2)
    if n_rows == 1:
        return out
    return pl.pallas_call(
        _ffn_rest_kernel,
        grid=(n_rows - 1, dff // tf),
        in_specs=[
            pl.BlockSpec((tm, d), lambda i, f: (i + 1, 0)),
            pl.BlockSpec((d, tf), lambda i, f: (0, f)),
            pl.BlockSpec((d, tf), lambda i, f: (0, f)),
            pl.BlockSpec((tf, d), lambda i, f: (f, 0)),
            pl.BlockSpec((tm, d), lambda i, f: (i + 1, 0), pipeline_mode=ONE_BUFFER),
            pl.BlockSpec((1, d), lambda i, f: (0, 0)),
            pl.BlockSpec((None, 1, d), lambda i, f: ((i + 1) // tiles_per_seq, 0, 0)),
        ],
        out_specs=pl.BlockSpec((tm, d), lambda i, f: (i + 1, 0)),
        out_shape=jax.ShapeDtypeStruct((m, d), F32),
        input_output_aliases={4: 0},
        compiler_params=_params(2, 58),
        name="ffn",
    )(h2, wgb, wub, wdb, out, nw, g2)


def _sincos_tables(rows, cols, dim):
    qd = dim // 4
    omega = 1.0 / (POS_THETA ** (jnp.arange(qd, dtype=F32) / qd))
    r = jnp.arange(rows, dtype=F32)[:, None] * omega
    cc = jnp.arange(cols, dtype=F32)[:, None] * omega
    r_emb = jnp.concatenate([jnp.sin(r), jnp.cos(r)], -1)
    c_emb = jnp.concatenate([jnp.sin(cc), jnp.cos(cc)], -1)
    return r_emb, c_emb


def _filter_features_t(seq_len, n_bands):
    t = jnp.linspace(0.0, 1.0, seq_len, dtype=F32)[:, None]
    w = 2.0 * math.pi * jnp.arange(seq_len, dtype=F32)[:, None] / seq_len
    fb = jnp.linspace(1e-4, n_bands - 1, n_bands, dtype=F32)[None]
    zpos = jnp.concatenate([t, jnp.cos(fb * w), -jnp.sin(fb * w)], -1)
    emb = zpos.shape[1]
    return jnp.pad(zpos, ((0, 0), (0, LANES - emb))).T


def _pad_rows(a, rows):
    return jnp.pad(a, ((0, rows - a.shape[0]), (0, 0)))


def kernel(x, c, ctx, c_ctx, w_ada, b_ada, norm_mix_pre, norm_mix_post, norm_ffn_pre, norm_ffn_post,
           w_in, ssd_conv_w, ssd_conv_b, ssd_a_log, ssd_dt_bias, ssd_d, ssd_norm,
           hy_conv_w, hy_conv_b, hy_w1, hy_b1, hy_w2, hy_b2, hy_w3, hy_b3, hy_w4, hy_freq, hy_bias,
           w_out, w_gate, w_up, w_down):
    nb, seq_len, d = x.shape
    ctx_len = ctx.shape[1]
    assert w_ada.shape[0] == 1, "single layer"
    n_heads = ssd_d.shape[1]
    d_ssd = n_heads * SSD_HEAD_DIM
    d_xbc = d_ssd + 2 * SSD_GROUPS * SSD_STATE
    dh = hy_bias.shape[1]
    assert w_in.shape[2] == d_ssd + d_xbc + 2 * n_heads + 3 * dh
    assert n_heads <= LANES and nb + 1 <= 8
    assert seq_len % (GRID_W * 8) == 0 and seq_len % FFT_N2 == 0
    m = nb * seq_len

    crows = jnp.concatenate([c, c_ctx[None, :]], axis=0)
    mod = _ada(crows, w_ada[0], b_ada[0])
    part = lambda r0, r1, k: mod[r0:r1, k * d:(k + 1) * d][:, None, :]
    sh1, sc1, g1, sh2, sc2, g2 = (part(0, nb, k) for k in range(6))
    csh1 = jnp.broadcast_to(part(nb, nb + 1, 0), (nb, 1, d))
    csc1 = jnp.broadcast_to(part(nb, nb + 1, 1), (nb, 1, d))

    w_t = jnp.transpose(w_in[0])
    o_xbc = d_ssd
    o_dt = o_xbc + d_xbc
    o_hy = o_dt + 2 * n_heads
    tn_in = 512
    assert o_dt % tn_in == 0 and (3 * dh) % tn_in == 0 and o_xbc % tn_in == 0 and o_hy % 16 == 0
    n_left = o_dt // tn_in
    main_off = lambda j: jnp.where(j < n_left, j * tn_in, o_hy + (j - n_left) * tn_in)
    ctx_off = lambda j: o_xbc + j * tn_in
    pad_dt = lambda rows: jnp.pad(rows, ((0, LANES - n_heads), (0, 0)))
    w_dt_t = jnp.concatenate([pad_dt(w_t[o_dt:o_dt + n_heads]), pad_dt(w_t[o_dt + n_heads:o_hy])],
                             axis=0)

    r_emb, c_emb = _sincos_tables(seq_len // GRID_W, GRID_W, d)
    posr = jnp.repeat(r_emb, GRID_W, axis=0)
    nmp = norm_mix_pre[0][None, :]

    tm_in = min(1024, seq_len)
    n_main = o_dt + 3 * dh
    tn_rest = n_main // 4 if n_main % (4 * LANES) == 0 else tn_in
    proj, dt_lat = _in_proj(x.reshape(m, d), posr, c_emb, nmp, sh1, sc1, w_t, main_off, n_main // tn_in,
                            w_dt_t, seq_len, tm_in, tn_in, True, tn_rest)
    tm_ctx = min(256, ctx_len)
    xbc_ctx, dt_ctx = _in_proj(ctx.reshape(nb * ctx_len, d), jnp.zeros((8, d // 2), F32), c_emb, nmp, csh1, csc1,
                               w_t, ctx_off, d_xbc // tn_in, w_dt_t, ctx_len, tm_ctx, tn_in, False)
    proj3 = proj.reshape(nb, seq_len, -1)

    cw8 = _pad_rows(ssd_conv_w[0], 8)
    cb1 = ssd_conv_b[0][None, :]
    u_lat = _ssd_prep(proj3, d_ssd, d_xbc, cw8, cb1, 512)
    u_ctx = _ssd_prep(xbc_ctx.reshape(nb, ctx_len, d_xbc), 0, d_xbc, cw8, cb1, 512)
    pad_heads = lambda a: jnp.pad(a, ((0, 0), (0, LANES - n_heads)))[:, None, :]
    expand = (jnp.arange(LANES)[:, None] == (jnp.arange(d_ssd)[None, :] // SSD_HEAD_DIM)).astype(BF16)
    y_ssd = _ssd(u_lat, dt_lat.reshape(nb, seq_len, 2 * LANES), u_ctx, dt_ctx.reshape(nb, ctx_len, 2 * LANES),
                 proj3, pad_heads(ssd_a_log[0]), pad_heads(ssd_dt_bias[0]),
                 jnp.repeat(ssd_d[0], SSD_HEAD_DIM)[None, :], ssd_norm[0][None, :], expand)

    x0c, u_hy = _hy_prep(proj3, d_ssd + d_xbc, dh, _pad_rows(hy_conv_w[0], 8), hy_conv_b[0][None, :], 256)
    n_bands = (hy_w1.shape[1] - 1) // 2
    zt = _filter_features_t(seq_len, n_bands)
    w1t = jnp.pad(hy_w1[0].T, ((0, 0), (0, LANES - hy_w1.shape[1])))
    b3 = jnp.stack([hy_b1[0], hy_b2[0], hy_b3[0]], axis=1)
    h3 = _hy_mlp(zt, w1t, hy_w2[0].T, hy_w3[0].T, b3, hy_freq[0][:, None])
    w4 = _pad_rows(hy_w4[0], LANES)
    max_decay = math.log(HY_TARGET) / HY_FAST_PCT
    min_decay = math.log(HY_TARGET) / HY_SLOW_PCT
    deltas = jnp.abs(jnp.linspace(min_decay, max_decay, dh, dtype=F32))[None, :]
    f1, f1t, gf, gi, k1n = _fft_tables(seq_len)
    kspec = _hy_spec(h3, w4[:, :dh], w4[:, dh:], deltas, f1, gf, k1n, 128)
    y_hy = _hy_conv(u_hy, x0c, kspec, f1, f1t, gf, gi, hy_bias[0][None, :], k1n, 256)

    xn, h2 = _out_proj(y_ssd.reshape(m, d_ssd), y_hy.reshape(m, dh), w_out[0].astype(BF16), x.reshape(m, d),
                       posr, c_emb, norm_mix_post[0][None, :], g1, norm_ffn_pre[0][None, :], sh2, sc2, seq_len,
                       min(512, seq_len))
    out = _ffn(h2, w_gate[0], w_up[0], w_down[0], xn, norm_ffn_post[0][None, :], g2, seq_len,
               min(1024, seq_len), 256, 512)
    return out.reshape(nb, seq_len, d)
```

```python
import functools
import math

import numpy as np
import jax
import jax.numpy as jnp
from jax import lax
from jax.experimental import pallas as pl
from jax.experimental.pallas import tpu as pltpu

F32 = jnp.float32
BF16 = jnp.bfloat16
HIGHEST = lax.Precision.HIGHEST

RMS_EPS = 1e-6
POS_THETA = 10000.0
GRID_W = 64
SSD_HEAD_DIM = 64
SSD_GROUPS = 2
SSD_STATE = 128
SSD_CHUNK = 128
HY_TARGET = 1e-2
HY_FAST_PCT = 0.3
HY_SLOW_PCT = 1.5
FFT_N2 = 64
FFT_UNROLL = 16
LANES = 128
MIB = 1024 * 1024
ONE_BUFFER = pl.Buffered(1)
_NT_DIMS = (((1,), (1,)), ((), ()))
_NEG_BIG = -1e30


def _params(n_axes, vmem_mib):
    return pltpu.CompilerParams(
        dimension_semantics=("arbitrary",) * n_axes,
        vmem_limit_bytes=vmem_mib * MIB,
    )


def _silu(v):
    return v * (1.0 / (1.0 + jnp.exp(-v)))


def _softplus(v):
    return jnp.maximum(v, 0.0) + jnp.log(1.0 + jnp.exp(-jnp.abs(v)))


def _rms(v):
    return v * lax.rsqrt(jnp.mean(v * v, axis=-1, keepdims=True) + RMS_EPS)


def _add_pos(x, remb_ref, g0, cemb):
    rows, d = x.shape
    half = d // 2
    lo = jnp.concatenate(
        [x[g * GRID_W:(g + 1) * GRID_W, :half] + remb_ref[g0 + g:g0 + g + 1, :] for g in range(rows // GRID_W)],
        axis=0)
    hi = (x[:, half:].reshape(rows // GRID_W, GRID_W, half) + cemb[None]).reshape(rows, half)
    return jnp.concatenate([lo, hi], axis=1)


def _ada_kernel(c_ref, w_ref, b_ref, o_ref, s_scr):
    nr, d, _ = c_ref.shape
    tn = w_ref.shape[1]
    nt = tn // LANES
    kc = 64

    @pl.when(pl.program_id(0) == 0)
    def _():
        for r in range(nr):
            for k0 in range(0, d, 512):
                s_scr[r, k0:k0 + 512, :] = _silu(c_ref[r, k0:k0 + 512, :])

    acc = [[jnp.zeros((8, LANES), F32) for _ in range(nt)] for _ in range(nr)]
    for k0 in range(0, d, kc):
        s = [s_scr[r, k0:k0 + kc, :] for r in range(nr)]
        for t in range(nt):
            w = w_ref[k0:k0 + kc, t * LANES:(t + 1) * LANES]
            for r in range(nr):
                acc[r][t] = acc[r][t] + jnp.sum((w * s[r]).reshape(kc // 8, 8, LANES), axis=0)
    rows = [jnp.concatenate([jnp.sum(a, axis=0, keepdims=True) for a in acc[r]], axis=1) for r in range(nr)]
    rows.append(jnp.zeros((o_ref.shape[0] - nr, tn), F32))
    o_ref[...] = jnp.concatenate(rows, axis=0) + b_ref[...]


def _ada(crows, w_ada, b_ada):
    nr = crows.shape[0]
    d, n = w_ada.shape
    tn = 1024
    c_lanes = jnp.broadcast_to(crows[:, :, None], (nr, d, LANES))
    return pl.pallas_call(
        _ada_kernel,
        grid=(n // tn,),
        in_specs=[
            pl.BlockSpec((nr, d, LANES), lambda j: (0, 0, 0), pipeline_mode=ONE_BUFFER),
            pl.BlockSpec((d, tn), lambda j: (0, j)),
            pl.BlockSpec((1, tn), lambda j: (0, j)),
        ],
        out_specs=pl.BlockSpec((8, tn), lambda j: (0, j)),
        out_shape=jax.ShapeDtypeStruct((8, n), F32),
        scratch_shapes=[pltpu.VMEM((nr, d, LANES), F32)],
        compiler_params=_params(1, 40),
        name="ada",
    )(c_lanes, w_ada, b_ada.reshape(1, n))


def _inproj_body(x_ref, posr_ref, cemb_ref, nw_ref, sh_ref, sc_ref, get_w, wdt_ref, o_ref, dt_ref, h_scr, with_pos):
    first = pl.program_id(1) == 0

    @pl.when(first)
    def _():
        tm = x_ref.shape[0]
        rows = min(256, tm)
        wdt = wdt_ref[...].astype(BF16)
        w = get_w()
        for r0 in range(0, tm, rows):
            xf = x_ref[r0:r0 + rows, :]
            if with_pos:
                xf = _add_pos(xf, posr_ref, r0 // GRID_W, cemb_ref[...])
            h = _rms(xf) * nw_ref[...] * (1.0 + sc_ref[...]) + sh_ref[...]
            hb = h.astype(BF16)
            h_scr[r0:r0 + rows, :] = hb
            dt_ref[r0:r0 + rows, :] = lax.dot_general(hb, wdt, _NT_DIMS, preferred_element_type=F32)
            o_ref[r0:r0 + rows, :] = jnp.dot(hb, w, preferred_element_type=F32).astype(o_ref.dtype)

    @pl.when(jnp.logical_not(first))
    def _():
        o_ref[...] = jnp.dot(h_scr[...], get_w(), preferred_element_type=F32).astype(o_ref.dtype)


def _inproj_t_kernel(x_ref, posr_ref, cemb_ref, nw_ref, sh_ref, sc_ref, w_ref, wdt_ref, *rest, with_pos, emit_w):
    w = w_ref[...].T.astype(BF16)
    if emit_w:
        o_ref, dt_ref, wb_ref, h_scr = rest
        wb_ref[...] = w
    else:
        o_ref, dt_ref, h_scr = rest
    _inproj_body(x_ref, posr_ref, cemb_ref, nw_ref, sh_ref, sc_ref, lambda: w, wdt_ref, o_ref, dt_ref, h_scr,
                 with_pos)


def _inproj_rest_kernel(x_ref, posr_ref, cemb_ref, nw_ref, sh_ref, sc_ref, w_ref, wdt_ref, p0_ref, dt0_ref,
                        o_ref, dt_ref, h_scr):
    row_block = pl.program_id(0)

    @pl.when(row_block == 0)
    def _():
        o_ref[...] = p0_ref[...]

        @pl.when(pl.program_id(1) == 0)
        def _():
            dt_ref[...] = dt0_ref[...]

    @pl.when(row_block > 0)
    def _():
        _inproj_body(x_ref, posr_ref, cemb_ref, nw_ref, sh_ref, sc_ref, lambda: w_ref[...], wdt_ref, o_ref, dt_ref,
                     h_scr, True)


def _in_proj_specs(tm, d, half, pos_rows, cemb, ndt, tiles_per_seq, with_pos, blk, one_row_block):
    mode = ONE_BUFFER if one_row_block else None
    return [
        pl.BlockSpec((tm, d), lambda i, j: (blk(i), 0), pipeline_mode=mode),
        pl.BlockSpec((pos_rows, half), lambda i, j: ((blk(i) % tiles_per_seq) if with_pos else 0, 0),
                     pipeline_mode=mode),
        pl.BlockSpec(cemb.shape, lambda i, j: (0, 0), pipeline_mode=ONE_BUFFER),
        pl.BlockSpec((1, d), lambda i, j: (0, 0)),
        pl.BlockSpec((None, 1, d), lambda i, j: (blk(i) // tiles_per_seq, 0, 0)),
        pl.BlockSpec((None, 1, d), lambda i, j: (blk(i) // tiles_per_seq, 0, 0)),
    ], pl.BlockSpec((ndt, d), lambda i, j: (0, 0), pipeline_mode=ONE_BUFFER)


def _in_proj(x2, posr, cemb, nw, sh, sc, w_t, row_off, n_tiles, wdt_t, seq_len, tm, tn, with_pos, tn_rest=None):
    m, d = x2.shape
    n = n_tiles * tn
    ndt = wdt_t.shape[0]
    half = d // 2
    tiles_per_seq = seq_len // tm
    pos_rows = tm // GRID_W if with_pos else posr.shape[0]
    split = tn_rest is not None and m // tm > 1
    common, wdt_spec = _in_proj_specs(tm, d, half, pos_rows, cemb, ndt, tiles_per_seq, with_pos, lambda i: i, split)
    w_spec = pl.BlockSpec((pl.Element(tn), pl.Element(d)), lambda i, j: (pl.multiple_of(row_off(j), 8), 0))
    out_specs = [pl.BlockSpec((tm, tn), lambda i, j: (i, j)), pl.BlockSpec((tm, ndt), lambda i, j: (i, 0))]
    m_first = tm if split else m
    out_shape = [jax.ShapeDtypeStruct((m_first, n), BF16), jax.ShapeDtypeStruct((m_first, ndt), F32)]
    if split:
        out_specs.append(pl.BlockSpec((d, tn), lambda i, j: (0, j)))
        out_shape.append(jax.ShapeDtypeStruct((d, n), BF16))
    res = pl.pallas_call(
        functools.partial(_inproj_t_kernel, with_pos=with_pos, emit_w=split),
        grid=(1 if split else m // tm, n_tiles),
        in_specs=common + [w_spec, wdt_spec],
        out_specs=out_specs,
        out_shape=out_shape,
        scratch_shapes=[pltpu.VMEM((tm, d), BF16)],
        compiler_params=_params(2, 58),
        name="in_proj_t",
    )(x2, posr, cemb, nw, sh, sc, w_t, wdt_t)
    if not split:
        return res
    proj0, dt0, w_bf = res
    nj = n // tn_rest
    common, wdt_spec = _in_proj_specs(tm, d, half, pos_rows, cemb, ndt, tiles_per_seq, with_pos,
                                      lambda i: jnp.maximum(i, 1), False)
    return pl.pallas_call(
        _inproj_rest_kernel,
        grid=(m // tm, nj),
        in_specs=common + [
            pl.BlockSpec((d, tn_rest), lambda i, j: (0, j)),
            wdt_spec,
            pl.BlockSpec((tm, tn_rest), lambda i, j: (0, jnp.where(i == 0, j, nj - 1))),
            pl.BlockSpec((tm, ndt), lambda i, j: (0, 0), pipeline_mode=ONE_BUFFER),
        ],
        out_specs=[pl.BlockSpec((tm, tn_rest), lambda i, j: (i, j)),
                   pl.BlockSpec((tm, ndt), lambda i, j: (i, 0))],
        out_shape=[jax.ShapeDtypeStruct((m, n), BF16), jax.ShapeDtypeStruct((m, ndt), F32)],
        scratch_shapes=[pltpu.VMEM((tm, d), BF16)],
        compiler_params=_params(2, 58),
        name="in_proj",
    )(x2, posr, cemb, nw, sh, sc, w_bf, wdt_t, proj0, dt0)


def _conv3_chunk(x_ref, r0, rows, seq_len, w, b):
    cur = x_ref[r0:r0 + rows, :].astype(F32)
    tc = cur.shape[1]
    rid = lax.broadcasted_iota(jnp.int32, (rows, tc), 0)
    if r0 > 0:
        prev_row = x_ref[r0 - 16:r0, :].astype(F32)[15:16, :]
    else:
        prev_row = jnp.zeros((1, tc), F32)
    if r0 + rows < seq_len:
        next_row = x_ref[r0 + rows:r0 + rows + 16, :].astype(F32)[0:1, :]
    else:
        next_row = jnp.zeros((1, tc), F32)
    up = jnp.where(rid == 0, prev_row, pltpu.roll(cur, 1, 0))
    down = jnp.where(rid == rows - 1, next_row, pltpu.roll(cur, rows - 1, 0))
    return up * w[0:1, :] + cur * w[1:2, :] + down * w[2:3, :] + b


def _ssd_prep_kernel(x_ref, w_ref, b_ref, o_ref, *, rows):
    seq_len = x_ref.shape[0]
    w = w_ref[...]
    b = b_ref[...]
    for r0 in range(0, seq_len, rows):
        v = _conv3_chunk(x_ref, r0, rows, seq_len, w, b)
        o_ref[r0:r0 + rows, :] = _silu(v).astype(o_ref.dtype)


def _ssd_prep(proj3, col0, width, w8, b1, tc):
    nb, seq_len, _ = proj3.shape
    rows = min(512, seq_len)
    j0 = col0 // tc
    return pl.pallas_call(
        functools.partial(_ssd_prep_kernel, rows=rows),
        grid=(nb, width // tc),
        in_specs=[
            pl.BlockSpec((None, seq_len, tc), lambda b, j: (b, 0, j0 + j)),
            pl.BlockSpec((8, tc), lambda b, j: (0, j)),
            pl.BlockSpec((1, tc), lambda b, j: (0, j)),
        ],
        out_specs=pl.BlockSpec((None, seq_len, tc), lambda b, j: (b, 0, j)),
        out_shape=jax.ShapeDtypeStruct((nb, seq_len, width), BF16),
        compiler_params=_params(2, 40),
        name="ssd_prep",
    )(proj3, w8, b1)


def _hy_prep_kernel(x0_ref, x1_ref, v_ref, w0_ref, w1_ref, wv_ref, b0_ref, b1_ref, bv_ref,
                    x0c_ref, u_ref, *, rows):
    seq_len = x0_ref.shape[0]
    w0, w1, wv = w0_ref[...], w1_ref[...], wv_ref[...]
    b0, b1, bv = b0_ref[...], b1_ref[...], bv_ref[...]
    for r0 in range(0, seq_len, rows):
        x0c_ref[r0:r0 + rows, :] = _conv3_chunk(x0_ref, r0, rows, seq_len, w0, b0).astype(x0c_ref.dtype)
        x1c = _conv3_chunk(x1_ref, r0, rows, seq_len, w1, b1)
        vc = _conv3_chunk(v_ref, r0, rows, seq_len, wv, bv)
        u_ref[r0:r0 + rows, :] = (vc * x1c).astype(u_ref.dtype)


def _hy_prep(proj3, col0, dh, w8, b1, tc):
    nb, seq_len, _ = proj3.shape
    rows = min(512, seq_len)
    j0 = col0 // tc
    nj = dh // tc
    x_spec = lambda k: pl.BlockSpec((None, seq_len, tc), lambda b, j: (b, 0, j0 + k * nj + j))
    w_spec = lambda k: pl.BlockSpec((8, tc), lambda b, j: (0, k * nj + j))
    b_spec = lambda k: pl.BlockSpec((1, tc), lambda b, j: (0, k * nj + j))
    o_spec = pl.BlockSpec((None, seq_len, tc), lambda b, j: (b, 0, j))
    return pl.pallas_call(
        functools.partial(_hy_prep_kernel, rows=rows),
        grid=(nb, nj),
        in_specs=[x_spec(0), x_spec(1), x_spec(2), w_spec(0), w_spec(1), w_spec(2),
                  b_spec(0), b_spec(1), b_spec(2)],
        out_specs=[o_spec, o_spec],
        out_shape=[jax.ShapeDtypeStruct((nb, seq_len, dh), BF16)] * 2,
        compiler_params=_params(2, 48),
        name="hy_prep",
    )(proj3, proj3, proj3, w8, w8, w8, b1, b1, b1)


def _ssd_kernel(u_ref, dt_ref, uc_ref, dtc_ref, z_ref, alog_ref, dtb_ref, dsk_ref, nw_ref, e_ref,
                o_ref, h_scr, yf_scr, *, n_ctx, n_lat):
    q = SSD_CHUNK
    d = pl.program_id(0)
    s = pl.program_id(1)
    is_ctx = s < n_ctx
    t = jnp.maximum(s - n_ctx, 0)
    cidx = jnp.where(d == 0, t, n_lat - 1 - t)
    fwd = d == 0

    @pl.when(s == 0)
    def _():
        h_scr[...] = jnp.zeros_like(h_scr)

    row = lax.broadcasted_iota(jnp.int32, (q, q), 0)
    col = lax.broadcasted_iota(jnp.int32, (q, q), 1)
    tri = (jnp.where(fwd, row, col) >= jnp.where(fwd, col, row)).astype(F32)
    nb = u_ref.shape[0]
    res = [_ssd_chunk(b, is_ctx, tri, u_ref, dt_ref, uc_ref, dtc_ref, alog_ref, dtb_ref, e_ref, h_scr)
           for b in range(nb)]

    @pl.when(jnp.logical_and(jnp.logical_not(is_ctx), fwd))
    def _():
        for b in range(nb):
            yf_scr[b, cidx] = res[b][0]

    @pl.when(jnp.logical_and(jnp.logical_not(is_ctx), jnp.logical_not(fwd)))
    def _():
        for b in range(nb):
            y, xs = res[b]
            ytot = yf_scr[b, cidx] + y + dsk_ref[...] * xs
            gated = ytot * _silu(z_ref[b].astype(F32))
            o_ref[b] = (_rms(gated) * nw_ref[...]).astype(o_ref.dtype)


def _ssd_chunk(b, is_ctx, tri, u_ref, dt_ref, uc_ref, dtc_ref, alog_ref, dtb_ref, e_ref, h_scr):
    q = SSD_CHUNK
    d_ssd = h_scr.shape[2]
    gw = d_ssd // SSD_GROUPS
    u = jnp.where(is_ctx, uc_ref[b], u_ref[b])
    dtr = jnp.where(is_ctx, dtc_ref[b], dt_ref[b])

    dt = _softplus(dtr + dtb_ref[...])
    a = dt * (-jnp.exp(alog_ref[...]))
    cum = jnp.dot(tri, a, precision=HIGHEST, preferred_element_type=F32)
    mask_add = ((1.0 - tri) * _NEG_BIG).astype(BF16)
    cum_t = cum.T
    tot = jnp.sum(a, axis=0, keepdims=True)

    stack = jnp.concatenate(
        [dt, jnp.exp(cum), jnp.exp(tot - cum), jnp.broadcast_to(jnp.exp(tot), (8, LANES))], axis=0)
    ex = jnp.dot(stack.astype(BF16), e_ref[...], preferred_element_type=F32)
    dt_x = ex[0:q]
    ecum_x = ex[q:2 * q]
    edec_x = ex[2 * q:3 * q]
    etot_x = ex[3 * q:3 * q + 1]

    xs = u[:, :d_ssd].astype(F32)
    xdt = xs * dt_x
    xdt_b = xdt.astype(BF16)
    xdw_b = (xdt * edec_x).astype(BF16)
    lane_lo = lax.broadcasted_iota(jnp.int32, (q, LANES), 1) < SSD_HEAD_DIM
    heads_per_group = gw // SSD_HEAD_DIM

    y_parts = []
    for g in range(SSD_GROUPS):
        bg = u[:, d_ssd + g * SSD_STATE:d_ssd + (g + 1) * SSD_STATE]
        cg = u[:, d_ssd + (SSD_GROUPS + g) * SSD_STATE:d_ssd + (SSD_GROUPS + g + 1) * SSD_STATE]
        scores = lax.dot_general(cg, bg, (((1,), (1,)), ((), ())), preferred_element_type=F32).astype(BF16)
        h_prev = h_scr[b, :, g * gw:(g + 1) * gw]
        y_off = jnp.dot(cg, h_prev.astype(BF16), preferred_element_type=F32)
        for j in range(heads_per_group // 2):
            c0 = g * gw + j * LANES
            xp = xdt_b[:, c0:c0 + LANES]
            acc = None
            for hh in range(2):
                h = g * heads_per_group + 2 * j + hh
                diff = (cum[:, h:h + 1] - cum_t[h:h + 1, :]).astype(BF16)
                m_h = scores * jnp.exp(diff + mask_add)
                x_h = jnp.where(lane_lo if hh == 0 else jnp.logical_not(lane_lo), xp, jnp.zeros_like(xp))
                part = jnp.dot(m_h, x_h, preferred_element_type=F32)
                acc = part if acc is None else acc + part
            y_parts.append(acc + y_off[:, j * LANES:(j + 1) * LANES] * ecum_x[:, c0:c0 + LANES])
        upd = lax.dot_general(bg, xdw_b[:, g * gw:(g + 1) * gw], (((0,), (0,)), ((), ())),
                              preferred_element_type=F32)
        h_scr[b, :, g * gw:(g + 1) * gw] = etot_x[:, g * gw:(g + 1) * gw] * h_prev + upd
    return jnp.concatenate(y_parts, axis=1), xs


def _ssd(u_lat, dt_lat, u_ctx, dt_ctx, proj3, alog, dtb, dsk, nw, expand):
    nb, seq_len, d_xbc = u_lat.shape
    ctx_len = u_ctx.shape[1]
    q = SSD_CHUNK
    n_lat = seq_len // q
    n_ctx = ctx_len // q
    d_ssd = dsk.shape[1]

    def lat_idx(d, s):
        t = jnp.maximum(s - n_ctx, 0)
        return jnp.where(d == 0, t, n_lat - 1 - t)

    def ctx_idx(d, s):
        t = jnp.minimum(s, n_ctx - 1)
        return jnp.where(d == 0, t, n_ctx - 1 - t)

    def out_idx(d, s):
        return jnp.where(d == 0, n_lat - 1, lat_idx(d, s))

    return pl.pallas_call(
        functools.partial(_ssd_kernel, n_ctx=n_ctx, n_lat=n_lat),
        grid=(2, n_ctx + n_lat),
        in_specs=[
            pl.BlockSpec((nb, q, d_xbc), lambda d, s: (0, lat_idx(d, s), 0)),
            pl.BlockSpec((nb, q, LANES), lambda d, s: (0, lat_idx(d, s), d)),
            pl.BlockSpec((nb, q, d_xbc), lambda d, s: (0, ctx_idx(d, s), 0)),
            pl.BlockSpec((nb, q, LANES), lambda d, s: (0, ctx_idx(d, s), d)),
            pl.BlockSpec((nb, q, d_ssd), lambda d, s: (0, lat_idx(d, s), 0)),
            pl.BlockSpec((None, 1, LANES), lambda d, s: (d, 0, 0)),
            pl.BlockSpec((None, 1, LANES), lambda d, s: (d, 0, 0)),
            pl.BlockSpec((1, d_ssd), lambda d, s: (0, 0)),
            pl.BlockSpec((1, d_ssd), lambda d, s: (0, 0)),
            pl.BlockSpec((LANES, d_ssd), lambda d, s: (0, 0)),
        ],
        out_specs=pl.BlockSpec((nb, q, d_ssd), lambda d, s: (0, out_idx(d, s), 0)),
        out_shape=jax.ShapeDtypeStruct((nb, seq_len, d_ssd), BF16),
        scratch_shapes=[
            pltpu.VMEM((nb, SSD_STATE, d_ssd), F32),
            pltpu.VMEM((nb, n_lat, q, d_ssd), F32),
        ],
        compiler_params=_params(2, 56),
        name="ssd",
    )(u_lat, dt_lat, u_ctx, dt_ctx, proj3, alog, dtb, dsk, nw, expand)


def _hy_mlp_kernel(zt_ref, w1_ref, w2_ref, w3_ref, b_ref, fr_ref, o_ref):
    fr = fr_ref[...]
    b = b_ref[...]
    h = jnp.sin(fr * (jnp.dot(w1_ref[...], zt_ref[...], precision=HIGHEST, preferred_element_type=F32)
                      + b[:, 0:1]))
    h = jnp.sin(fr * (jnp.dot(w2_ref[...], h, precision=HIGHEST, preferred_element_type=F32) + b[:, 1:2]))
    h = jnp.sin(fr * (jnp.dot(w3_ref[...], h, precision=HIGHEST, preferred_element_type=F32) + b[:, 2:3]))
    hid, seq_len = h.shape
    hp = jnp.concatenate([h, jnp.zeros((LANES - hid, seq_len), F32)], axis=0)
    o_ref[...] = hp.T


def _hy_mlp(zt, w1t, w2t, w3t, b3, fr):
    seq_len = zt.shape[1]
    return pl.pallas_call(
        _hy_mlp_kernel,
        out_shape=jax.ShapeDtypeStruct((seq_len, LANES), F32),
        compiler_params=pltpu.CompilerParams(vmem_limit_bytes=40 * MIB),
        name="hy_mlp",
    )(zt, w1t, w2t, w3t, b3, fr)


def _fft_tables(seq_len):
    n_fft = 2 * seq_len
    n2 = FFT_N2
    n1 = n_fft // n2
    n1h = n1 // 2
    k1n = n1h + 1
    k1p = -(-k1n // 4) * 4
    k1 = np.arange(k1n, dtype=np.int64)
    th = (2.0 * np.pi / n1) * ((k1[:, None] * np.arange(n1h, dtype=np.int64)[None, :]) % n1)
    f1 = np.zeros((2 * k1p, n1h))
    f1[0:2 * k1n:2] = np.cos(th)
    f1[1:2 * k1n:2] = -np.sin(th)
    idx = np.arange(n2, dtype=np.int64)
    kk = k1[:, None, None] + n1 * idx[None, :, None]
    ph = (2.0 * np.pi / n_fft) * ((kk * idx[None, None, :]) % n_fft)
    g_re, g_im = np.cos(ph), -np.sin(ph)

    def blocks(re, im):
        out = np.zeros((k1p, 2 * n2, 2 * n2))
        out[:k1n, :n2, :n2] = re
        out[:k1n, :n2, n2:] = -im
        out[:k1n, n2:, :n2] = im
        out[:k1n, n2:, n2:] = re
        return out

    wgt = np.where((k1 == 0) | (k1 == n1h), 1.0, 2.0) / n_fft
    gf = blocks(g_re, g_im)
    gi = blocks(np.transpose(g_re, (0, 2, 1)) * wgt[:, None, None],
                -np.transpose(g_im, (0, 2, 1)) * wgt[:, None, None])
    as_bf16 = lambda t: jnp.asarray(t.astype(np.float32)).astype(BF16)
    return as_bf16(f1), as_bf16(np.ascontiguousarray(f1.T)), as_bf16(gf), as_bf16(gi), k1n


def _ld(ref, start, size, stride=None):
    idx = pl.ds(start, size) if stride is None else pl.ds(start, size, stride=stride)
    return jnp.concatenate([ref[t, idx, :] for t in range(ref.shape[0])], axis=1)


def _st(ref, start, size, val, stride=None):
    idx = pl.ds(start, size) if stride is None else pl.ds(start, size, stride=stride)
    for t in range(ref.shape[0]):
        ref[t, idx, :] = val[:, t * LANES:(t + 1) * LANES]


def _fft_stage1(x_scr, a_scr, f1_ref):
    rows, n1h = f1_ref.shape

    def body(n2, _):
        xs = _ld(x_scr, n2, n1h, FFT_N2).astype(BF16)
        _st(a_scr, pl.multiple_of(n2 * rows, 8), rows, jnp.dot(f1_ref[...], xs, preferred_element_type=F32))
        return 0

    lax.fori_loop(0, FFT_N2, body, 0, unroll=FFT_UNROLL)


def _fft_stage2_in(a_scr, k1, rows):
    return jnp.concatenate([_ld(a_scr, 2 * k1, FFT_N2, rows), _ld(a_scr, 2 * k1 + 1, FFT_N2, rows)],
                           axis=0).astype(BF16)


def _hy_spec_kernel(h_ref, wf_ref, wb_ref, dl_ref, f1_ref, gf_ref, k_ref, x_scr, a_scr, *, k1n):
    n2 = FFT_N2
    seq_len = h_ref.shape[0]
    tc = wf_ref.shape[1]
    rows = f1_ref.shape[0]
    h3 = h_ref[...].astype(BF16)
    rid = lax.broadcasted_iota(jnp.int32, (seq_len, tc), 0)
    decay = jnp.exp(-(rid.astype(F32) * (1.0 / (seq_len - 1))) * dl_ref[...])
    hf = jnp.dot(h3, wf_ref[...].astype(BF16), preferred_element_type=F32) * decay
    hb = jnp.dot(h3, wb_ref[...].astype(BF16), preferred_element_type=F32) * decay
    norm = jnp.sum(jnp.abs(hf) + jnp.abs(hb), axis=0, keepdims=True) + 1e-6
    inv = 1.0 / norm
    hb = jnp.where(rid == 0, 0.0, hb * inv)
    _st(x_scr, 0, seq_len, jnp.concatenate([hf * inv, hb], axis=1))
    _fft_stage1(x_scr, a_scr, f1_ref)

    def mid(k1, _):
        x = jnp.dot(gf_ref[k1], _fft_stage2_in(a_scr, k1, rows), preferred_element_type=F32)
        xf, xb = x[:, 0:tc], x[:, tc:2 * tc]
        k_ref[k1] = jnp.concatenate([xf[:n2] + xb[:n2], xf[n2:] - xb[n2:]], axis=0)
        return 0

    lax.fori_loop(0, k1n, mid, 0, unroll=FFT_UNROLL)
    for k1 in range(k1n, k_ref.shape[0]):
        k_ref[k1] = jnp.zeros((2 * n2, tc), F32)


def _hy_spec(h3, w4f, w4b, deltas, f1, gf, k1n, tc):
    seq_len = h3.shape[0]
    dh = w4f.shape[1]
    k1p = gf.shape[0]
    rows, n1h = f1.shape
    nt = 2 * tc // LANES
    return pl.pallas_call(
        functools.partial(_hy_spec_kernel, k1n=k1n),
        grid=(dh // tc,),
        in_specs=[
            pl.BlockSpec((seq_len, LANES), lambda j: (0, 0)),
            pl.BlockSpec((LANES, tc), lambda j: (0, j)),
            pl.BlockSpec((LANES, tc), lambda j: (0, j)),
            pl.BlockSpec((1, tc), lambda j: (0, j)),
            pl.BlockSpec((rows, n1h), lambda j: (0, 0)),
            pl.BlockSpec((k1p, 2 * FFT_N2, 2 * FFT_N2), lambda j: (0, 0, 0)),
        ],
        out_specs=pl.BlockSpec((k1p, 2 * FFT_N2, tc), lambda j: (0, 0, j)),
        out_shape=jax.ShapeDtypeStruct((k1p, 2 * FFT_N2, dh), F32),
        scratch_shapes=[pltpu.VMEM((nt, seq_len, LANES), F32), pltpu.VMEM((nt, FFT_N2 * rows, LANES), F32)],
        compiler_params=_params(1, 56),
        name="hy_spec",
    )(h3, w4f, w4b, deltas, f1, gf)


def _hy_conv_kernel(u_ref, x0_ref, k_ref, f1_ref, f1t_ref, gf_ref, gi_ref, bias_ref, o_ref, x_scr, a_scr, *, k1n):
    n2 = FFT_N2
    seq_len = u_ref.shape[0]
    rows, n1h = f1_ref.shape
    ch = min(512, seq_len)
    for r0 in range(0, seq_len, ch):
        _st(x_scr, r0, ch, u_ref[r0:r0 + ch, :].astype(F32))
    _fft_stage1(x_scr, a_scr, f1_ref)

    def mid(k1, _):
        x = jnp.dot(gf_ref[k1], _fft_stage2_in(a_scr, k1, rows), preferred_element_type=F32)
        kk = k_ref[k1]
        xr, xi, kr, ki = x[:n2], x[n2:], kk[:n2], kk[n2:]
        y = jnp.concatenate([xr * kr - xi * ki, xr * ki + xi * kr], axis=0).astype(BF16)
        bt = jnp.dot(gi_ref[k1], y, preferred_element_type=F32)
        _st(a_scr, 2 * k1, n2, bt[:n2], rows)
        _st(a_scr, 2 * k1 + 1, n2, bt[n2:], rows)
        return 0

    lax.fori_loop(0, k1n, mid, 0, unroll=FFT_UNROLL)

    def last(j, _):
        blk = _ld(a_scr, pl.multiple_of(j * rows, 8), rows).astype(BF16)
        _st(x_scr, j, n1h, jnp.dot(f1t_ref[...], blk, preferred_element_type=F32), n2)
        return 0

    lax.fori_loop(0, n2, last, 0, unroll=FFT_UNROLL)
    for r0 in range(0, seq_len, ch):
        v = _ld(x_scr, r0, ch) + u_ref[r0:r0 + ch, :].astype(F32) * bias_ref[...]
        o_ref[r0:r0 + ch, :] = (x0_ref[r0:r0 + ch, :].astype(F32) * v).astype(o_ref.dtype)


def _hy_conv(u3, x0c, kspec, f1, f1t, gf, gi, bias, k1n, tc):
    nb, seq_len, dh = u3.shape
    k1p = gf.shape[0]
    rows, n1h = f1.shape
    nt = tc // LANES
    blk = pl.BlockSpec((None, seq_len, tc), lambda j, b: (b, 0, j))
    const = lambda shape: pl.BlockSpec(shape, lambda j, b: (0,) * len(shape), pipeline_mode=ONE_BUFFER)
    return pl.pallas_call(
        functools.partial(_hy_conv_kernel, k1n=k1n),
        grid=(dh // tc, nb),
        in_specs=[
            blk, blk,
            pl.BlockSpec((k1p, 2 * FFT_N2, tc), lambda j, b: (0, 0, j), pipeline_mode=ONE_BUFFER),
            const((rows, n1h)), const((n1h, rows)),
            const((k1p, 2 * FFT_N2, 2 * FFT_N2)), const((k1p, 2 * FFT_N2, 2 * FFT_N2)),
            pl.BlockSpec((1, tc), lambda j, b: (0, j)),
        ],
        out_specs=blk,
        out_shape=jax.ShapeDtypeStruct((nb, seq_len, dh), BF16),
        scratch_shapes=[pltpu.VMEM((nt, seq_len, LANES), F32), pltpu.VMEM((nt, FFT_N2 * rows, LANES), F32)],
        compiler_params=_params(2, 56),
        name="hy_conv",
    )(u3, x0c, kspec, f1, f1t, gf, gi, bias)


def _outproj_kernel(ys_ref, yh_ref, w_ref, x_ref, posr_ref, cemb_ref, nwp_ref, g_ref, nwf_ref, sh_ref, sc_ref,
                    xo_ref, h_ref):
    tm, ds = ys_ref.shape
    rows = min(256, tm)
    for r0 in range(0, tm, rows):
        rs = slice(r0, r0 + rows)
        y = jnp.dot(ys_ref[rs, :], w_ref[0:ds, :], preferred_element_type=F32)
        y = y + jnp.dot(yh_ref[rs, :], w_ref[ds:, :], preferred_element_type=F32)
        xn = _add_pos(x_ref[rs, :], posr_ref, r0 // GRID_W, cemb_ref[...]) + g_ref[...] * (_rms(y) * nwp_ref[...])
        xo_ref[rs, :] = xn
        h_ref[rs, :] = (_rms(xn) * nwf_ref[...] * (1.0 + sc_ref[...]) + sh_ref[...]).astype(h_ref.dtype)


def _out_proj(ys, yh, w, x2, posr, cemb, nwp, g1, nwf, sh2, sc2, seq_len, tm):
    m, d = x2.shape
    ds = ys.shape[1]
    dh = yh.shape[1]
    tiles_per_seq = seq_len // tm
    row = lambda i: (i, 0)
    fixed = lambda i: (0, 0)
    per_batch = lambda i: (i // tiles_per_seq, 0, 0)
    return pl.pallas_call(
        _outproj_kernel,
        grid=(m // tm,),
        in_specs=[
            pl.BlockSpec((tm, ds), row),
            pl.BlockSpec((tm, dh), row),
            pl.BlockSpec((ds + dh, d), fixed, pipeline_mode=ONE_BUFFER),
            pl.BlockSpec((tm, d), row),
            pl.BlockSpec((tm // GRID_W, d // 2), lambda i: (i % tiles_per_seq, 0)),
            pl.BlockSpec(cemb.shape, fixed),
            pl.BlockSpec((1, d), fixed),
            pl.BlockSpec((None, 1, d), per_batch),
            pl.BlockSpec((1, d), fixed),
            pl.BlockSpec((None, 1, d), per_batch),
            pl.BlockSpec((None, 1, d), per_batch),
        ],
        out_specs=[pl.BlockSpec((tm, d), row), pl.BlockSpec((tm, d), row)],
        out_shape=[jax.ShapeDtypeStruct((m, d), F32), jax.ShapeDtypeStruct((m, d), BF16)],
        compiler_params=_params(1, 56),
        name="out_proj",
    )(ys, yh, w, x2, posr, cemb, nwp, g1, nwf, sh2, sc2)


def _ffn_step(h_ref, wg, wu, wd, x_ref, nw_ref, g_ref, o_ref):
    f = pl.program_id(1)
    h = h_ref[...]
    gate = jnp.dot(h, wg, preferred_element_type=F32)
    up = jnp.dot(h, wu, preferred_element_type=F32)
    act = (_silu(gate) * up).astype(BF16)

    @pl.when(f == 0)
    def _():
        o_ref[...] = jnp.zeros_like(o_ref)

    d = o_ref.shape[1]
    cw = min(512, d)
    for n0 in range(0, d, cw):
        o_ref[:, n0:n0 + cw] += jnp.dot(act, wd[:, n0:n0 + cw], preferred_element_type=F32)

    @pl.when(f == pl.num_programs(1) - 1)
    def _():
        tm = o_ref.shape[0]
        rows = min(256, tm)
        for r0 in range(0, tm, rows):
            y = o_ref[r0:r0 + rows, :]
            o_ref[r0:r0 + rows, :] = x_ref[r0:r0 + rows, :] + g_ref[...] * (_rms(y) * nw_ref[...])


def _ffn_first_kernel(h_ref, wg_ref, wu_ref, wd_ref, x_ref, nw_ref, g_ref, o_ref, wgb_ref, wub_ref, wdb_ref):
    wg = wg_ref[...].astype(BF16)
    wu = wu_ref[...].astype(BF16)
    wd = wd_ref[...].astype(BF16)
    wgb_ref[...] = wg
    wub_ref[...] = wu
    wdb_ref[...] = wd
    _ffn_step(h_ref, wg, wu, wd, x_ref, nw_ref, g_ref, o_ref)


def _ffn_rest_kernel(h_ref, wg_ref, wu_ref, wd_ref, x_ref, nw_ref, g_ref, o_ref):
    _ffn_step(h_ref, wg_ref[...], wu_ref[...], wd_ref[...], x_ref, nw_ref, g_ref, o_ref)


def _ffn(h2, wg, wu, wd, xn, nw, g2, seq_len, tm, tf_first, tf):
    m, d = xn.shape
    dff = wg.shape[1]
    tiles_per_seq = seq_len // tm
    n_rows = m // tm
    out, wgb, wub, wdb = pl.pallas_call(
        _ffn_first_kernel,
        grid=(1, dff // tf_first),
        in_specs=[
            pl.BlockSpec((tm, d), lambda i, f: (0, 0), pipeline_mode=ONE_BUFFER),
            pl.BlockSpec((d, tf_first), lambda i, f: (0, f)),
            pl.BlockSpec((d, tf_first), lambda i, f: (0, f)),
            pl.BlockSpec((tf_first, d), lambda i, f: (f, 0)),
            pl.BlockSpec((tm, d), lambda i, f: (0, 0), pipeline_mode=ONE_BUFFER),
            pl.BlockSpec((1, d), lambda i, f: (0, 0)),
            pl.BlockSpec((None, 1, d), lambda i, f: (0, 0, 0)),
        ],
        out_specs=[
            pl.BlockSpec((tm, d), lambda i, f: (0, 0)),
            pl.BlockSpec((d, tf_first), lambda i, f: (0, f)),
            pl.BlockSpec((d, tf_first), lambda i, f: (0, f)),
            pl.BlockSpec((tf_first, d), lambda i, f: (f, 0)),
        ],
        out_shape=[
            jax.ShapeDtypeStruct((m, d), F32),
            jax.ShapeDtypeStruct((d, dff), BF16),
            jax.ShapeDtypeStruct((d, dff), BF16),
            jax.ShapeDtypeStruct((dff, d), BF16),
        ],
        input_output_aliases={4: 0},
        compiler_params=_params(2, 58),
        name="ffn_first",
    )(h2, wg, wu, wd, xn, nw, g2)
    if n_rows == 1:
        return out
    return pl.pallas_call(
        _ffn_rest_kernel,
        grid=(n_rows - 1, dff // tf),
        in_specs=[
            pl.BlockSpec((tm, d), lambda i, f: (i + 1, 0)),
            pl.BlockSpec((d, tf), lambda i, f: (0, f)),
            pl.BlockSpec((d, tf), lambda i, f: (0, f)),
            pl.BlockSpec((tf, d), lambda i, f: (f, 0)),
            pl.BlockSpec((tm, d), lambda i, f: (i + 1, 0), pipeline_mode=ONE_BUFFER),
            pl.BlockSpec((1, d), lambda i, f: (0, 0)),
            pl.BlockSpec((None, 1, d), lambda i, f: ((i + 1) // tiles_per_seq, 0, 0)),
        ],
        out_specs=pl.BlockSpec((tm, d), lambda i, f: (i + 1, 0)),
        out_shape=jax.ShapeDtypeStruct((m, d), F32),
        input_output_aliases={4: 0},
        compiler_params=_params(2, 58),
        name="ffn",
    )(h2, wgb, wub, wdb, out, nw, g2)


def _sincos_tables(rows, cols, dim):
    qd = dim // 4
    omega = 1.0 / (POS_THETA ** (jnp.arange(qd, dtype=F32) / qd))
    r = jnp.arange(rows, dtype=F32)[:, None] * omega
    cc = jnp.arange(cols, dtype=F32)[:, None] * omega
    r_emb = jnp.concatenate([jnp.sin(r), jnp.cos(r)], -1)
    c_emb = jnp.concatenate([jnp.sin(cc), jnp.cos(cc)], -1)
    return r_emb, c_emb


def _filter_features_t(seq_len, n_bands):
    t = jnp.linspace(0.0, 1.0, seq_len, dtype=F32)[:, None]
    w = 2.0 * math.pi * jnp.arange(seq_len, dtype=F32)[:, None] / seq_len
    fb = jnp.linspace(1e-4, n_bands - 1, n_bands, dtype=F32)[None]
    zpos = jnp.concatenate([t, jnp.cos(fb * w), -jnp.sin(fb * w)], -1)
    emb = zpos.shape[1]
    return jnp.pad(zpos, ((0, 0), (0, LANES - emb))).T


def _pad_rows(a, rows):
    return jnp.pad(a, ((0, rows - a.shape[0]), (0, 0)))


def kernel(x, c, ctx, c_ctx, w_ada, b_ada, norm_mix_pre, norm_mix_post, norm_ffn_pre, norm_ffn_post,
           w_in, ssd_conv_w, ssd_conv_b, ssd_a_log, ssd_dt_bias, ssd_d, ssd_norm,
           hy_conv_w, hy_conv_b, hy_w1, hy_b1, hy_w2, hy_b2, hy_w3, hy_b3, hy_w4, hy_freq, hy_bias,
           w_out, w_gate, w_up, w_down):
    nb, seq_len, d = x.shape
    ctx_len = ctx.shape[1]
    assert w_ada.shape[0] == 1, "single layer"
    n_heads = ssd_d.shape[1]
    d_ssd = n_heads * SSD_HEAD_DIM
    d_xbc = d_ssd + 2 * SSD_GROUPS * SSD_STATE
    dh = hy_bias.shape[1]
    assert w_in.shape[2] == d_ssd + d_xbc + 2 * n_heads + 3 * dh
    assert n_heads <= LANES and nb + 1 <= 8
    assert seq_len % (GRID_W * 8) == 0 and seq_len % FFT_N2 == 0
    m = nb * seq_len

    crows = jnp.concatenate([c, c_ctx[None, :]], axis=0)
    mod = _ada(crows, w_ada[0], b_ada[0])
    part = lambda r0, r1, k: mod[r0:r1, k * d:(k + 1) * d][:, None, :]
    sh1, sc1, g1, sh2, sc2, g2 = (part(0, nb, k) for k in range(6))
    csh1 = jnp.broadcast_to(part(nb, nb + 1, 0), (nb, 1, d))
    csc1 = jnp.broadcast_to(part(nb, nb + 1, 1), (nb, 1, d))

    w_t = jnp.transpose(w_in[0])
    o_xbc = d_ssd
    o_dt = o_xbc + d_xbc
    o_hy = o_dt + 2 * n_heads
    tn_in = 512
    assert o_dt % tn_in == 0 and (3 * dh) % tn_in == 0 and o_xbc % tn_in == 0 and o_hy % 16 == 0
    n_left = o_dt // tn_in
    main_off = lambda j: jnp.where(j < n_left, j * tn_in, o_hy + (j - n_left) * tn_in)
    ctx_off = lambda j: o_xbc + j * tn_in
    pad_dt = lambda rows: jnp.pad(rows, ((0, LANES - n_heads), (0, 0)))
    w_dt_t = jnp.concatenate([pad_dt(w_t[o_dt:o_dt + n_heads]), pad_dt(w_t[o_dt + n_heads:o_hy])],
                             axis=0)

    r_emb, c_emb = _sincos_tables(seq_len // GRID_W, GRID_W, d)
    posr = r_emb
    nmp = norm_mix_pre[0][None, :]

    tm_in = min(1024, seq_len)
    n_main = o_dt + 3 * dh
    tn_rest = n_main // 4 if n_main % (4 * LANES) == 0 else tn_in
    proj, dt_lat = _in_proj(x.reshape(m, d), posr, c_emb, nmp, sh1, sc1, w_t, main_off, n_main // tn_in,
                            w_dt_t, seq_len, tm_in, tn_in, True, tn_rest)
    tm_ctx = min(256, ctx_len)
    xbc_ctx, dt_ctx = _in_proj(ctx.reshape(nb * ctx_len, d), jnp.zeros((8, d // 2), F32), c_emb, nmp, csh1, csc1,
                               w_t, ctx_off, d_xbc // tn_in, w_dt_t, ctx_len, tm_ctx, tn_in, False)
    proj3 = proj.reshape(nb, seq_len, -1)

    cw8 = _pad_rows(ssd_conv_w[0], 8)
    cb1 = ssd_conv_b[0][None, :]
    u_lat = _ssd_prep(proj3, d_ssd, d_xbc, cw8, cb1, 512)
    u_ctx = _ssd_prep(xbc_ctx.reshape(nb, ctx_len, d_xbc), 0, d_xbc, cw8, cb1, 512)
    pad_heads = lambda a: jnp.pad(a, ((0, 0), (0, LANES - n_heads)))[:, None, :]
    expand = (jnp.arange(LANES)[:, None] == (jnp.arange(d_ssd)[None, :] // SSD_HEAD_DIM)).astype(BF16)
    y_ssd = _ssd(u_lat, dt_lat.reshape(nb, seq_len, 2 * LANES), u_ctx, dt_ctx.reshape(nb, ctx_len, 2 * LANES),
                 proj3, pad_heads(ssd_a_log[0]), pad_heads(ssd_dt_bias[0]),
                 jnp.repeat(ssd_d[0], SSD_HEAD_DIM)[None, :], ssd_norm[0][None, :], expand)

    x0c, u_hy = _hy_prep(proj3, d_ssd + d_xbc, dh, _pad_rows(hy_conv_w[0], 8), hy_conv_b[0][None, :], 256)
    n_bands = (hy_w1.shape[1] - 1) // 2
    zt = _filter_features_t(seq_len, n_bands)
    w1t = jnp.pad(hy_w1[0].T, ((0, 0), (0, LANES - hy_w1.shape[1])))
    b3 = jnp.stack([hy_b1[0], hy_b2[0], hy_b3[0]], axis=1)
    h3 = _hy_mlp(zt, w1t, hy_w2[0].T, hy_w3[0].T, b3, hy_freq[0][:, None])
    w4 = _pad_rows(hy_w4[0], LANES)
    max_decay = math.log(HY_TARGET) / HY_FAST_PCT
    min_decay = math.log(HY_TARGET) / HY_SLOW_PCT
    deltas = jnp.abs(jnp.linspace(min_decay, max_decay, dh, dtype=F32))[None, :]
    f1, f1t, gf, gi, k1n = _fft_tables(seq_len)
    kspec = _hy_spec(h3, w4[:, :dh], w4[:, dh:], deltas, f1, gf, k1n, 128)
    y_hy = _hy_conv(u_hy, x0c, kspec, f1, f1t, gf, gi, hy_bias[0][None, :], k1n, 256)

    xn, h2 = _out_proj(y_ssd.reshape(m, d_ssd), y_hy.reshape(m, dh), w_out[0].astype(BF16), x.reshape(m, d),
                       posr, c_emb, norm_mix_post[0][None, :], g1, norm_ffn_pre[0][None, :], sh2, sc2, seq_len,
                       min(512, seq_len))
    out = _ffn(h2, w_gate[0], w_up[0], w_down[0], xn, norm_ffn_post[0][None, :], g2, seq_len,
               min(1024, seq_len), 256, 512)
    return out.reshape(nb, seq_len, d)
```

```python
import functools
import math

import numpy as np
import jax
import jax.numpy as jnp
from jax import lax
from jax.experimental import pallas as pl
from jax.experimental.pallas import tpu as pltpu

F32 = jnp.float32
BF16 = jnp.bfloat16
HIGHEST = lax.Precision.HIGHEST

RMS_EPS = 1e-6
POS_THETA = 10000.0
GRID_W = 64
SSD_HEAD_DIM = 64
SSD_GROUPS = 2
SSD_STATE = 128
SSD_CHUNK = 128
HY_TARGET = 1e-2
HY_FAST_PCT = 0.3
HY_SLOW_PCT = 1.5
FFT_N2 = 64
LANES = 128
MIB = 1024 * 1024
ONE_BUFFER = pl.Buffered(1)
_NT_DIMS = (((1,), (1,)), ((), ()))
_NEG_BIG = -1e30


def _params(n_axes, vmem_mib):
    return pltpu.CompilerParams(
        dimension_semantics=("arbitrary",) * n_axes,
        vmem_limit_bytes=vmem_mib * MIB,
    )


def _silu(v):
    return v * (1.0 / (1.0 + jnp.exp(-v)))


def _softplus(v):
    return jnp.maximum(v, 0.0) + jnp.log(1.0 + jnp.exp(-jnp.abs(v)))


def _rms(v):
    return v * lax.rsqrt(jnp.mean(v * v, axis=-1, keepdims=True) + RMS_EPS)


def _add_pos(x, remb_ref, g0, cemb):
    rows, d = x.shape
    half = d // 2
    lo = jnp.concatenate(
        [x[g * GRID_W:(g + 1) * GRID_W, :half] + remb_ref[g0 + g:g0 + g + 1, :] for g in range(rows // GRID_W)],
        axis=0)
    hi = (x[:, half:].reshape(rows // GRID_W, GRID_W, half) + cemb[None]).reshape(rows, half)
    return jnp.concatenate([lo, hi], axis=1)


def _ada_kernel(c_ref, w_ref, b_ref, o_ref, s_scr):
    nr, d, _ = c_ref.shape
    tn = w_ref.shape[1]
    nt = tn // LANES
    kc = 64

    @pl.when(pl.program_id(0) == 0)
    def _():
        for r in range(nr):
            for k0 in range(0, d, 512):
                s_scr[r, k0:k0 + 512, :] = _silu(c_ref[r, k0:k0 + 512, :])

    acc = [[jnp.zeros((8, LANES), F32) for _ in range(nt)] for _ in range(nr)]
    for k0 in range(0, d, kc):
        s = [s_scr[r, k0:k0 + kc, :] for r in range(nr)]
        for t in range(nt):
            w = w_ref[k0:k0 + kc, t * LANES:(t + 1) * LANES]
            for r in range(nr):
                acc[r][t] = acc[r][t] + jnp.sum((w * s[r]).reshape(kc // 8, 8, LANES), axis=0)
    rows = [jnp.concatenate([jnp.sum(a, axis=0, keepdims=True) for a in acc[r]], axis=1) for r in range(nr)]
    rows.append(jnp.zeros((o_ref.shape[0] - nr, tn), F32))
    o_ref[...] = jnp.concatenate(rows, axis=0) + b_ref[...]


def _ada(crows, w_ada, b_ada):
    nr = crows.shape[0]
    d, n = w_ada.shape
    tn = 1024
    c_lanes = jnp.broadcast_to(crows[:, :, None], (nr, d, LANES))
    return pl.pallas_call(
        _ada_kernel,
        grid=(n // tn,),
        in_specs=[
            pl.BlockSpec((nr, d, LANES), lambda j: (0, 0, 0), pipeline_mode=ONE_BUFFER),
            pl.BlockSpec((d, tn), lambda j: (0, j)),
            pl.BlockSpec((1, tn), lambda j: (0, j)),
        ],
        out_specs=pl.BlockSpec((8, tn), lambda j: (0, j)),
        out_shape=jax.ShapeDtypeStruct((8, n), F32),
        scratch_shapes=[pltpu.VMEM((nr, d, LANES), F32)],
        compiler_params=_params(1, 40),
        name="ada",
    )(c_lanes, w_ada, b_ada.reshape(1, n))


def _inproj_body(x_ref, posr_ref, cemb_ref, nw_ref, sh_ref, sc_ref, get_w, wdt_ref, o_ref, dt_ref, h_scr, with_pos):
    first = pl.program_id(1) == 0

    @pl.when(first)
    def _():
        tm = x_ref.shape[0]
        rows = min(256, tm)
        wdt = wdt_ref[...].astype(BF16)
        w = get_w()
        for r0 in range(0, tm, rows):
            xf = x_ref[r0:r0 + rows, :]
            if with_pos:
                xf = _add_pos(xf, posr_ref, r0 // GRID_W, cemb_ref[...])
            h = _rms(xf) * nw_ref[...] * (1.0 + sc_ref[...]) + sh_ref[...]
            hb = h.astype(BF16)
            h_scr[r0:r0 + rows, :] = hb
            dt_ref[r0:r0 + rows, :] = lax.dot_general(hb, wdt, _NT_DIMS, preferred_element_type=F32)
            o_ref[r0:r0 + rows, :] = jnp.dot(hb, w, preferred_element_type=F32).astype(o_ref.dtype)

    @pl.when(jnp.logical_not(first))
    def _():
        o_ref[...] = jnp.dot(h_scr[...], get_w(), preferred_element_type=F32).astype(o_ref.dtype)


def _inproj_t_kernel(x_ref, posr_ref, cemb_ref, nw_ref, sh_ref, sc_ref, w_ref, wdt_ref, *rest, with_pos, emit_w):
    w = w_ref[...].T.astype(BF16)
    if emit_w:
        o_ref, dt_ref, wb_ref, h_scr = rest
        wb_ref[...] = w
    else:
        o_ref, dt_ref, h_scr = rest
    _inproj_body(x_ref, posr_ref, cemb_ref, nw_ref, sh_ref, sc_ref, lambda: w, wdt_ref, o_ref, dt_ref, h_scr,
                 with_pos)


def _inproj_rest_kernel(x_ref, posr_ref, cemb_ref, nw_ref, sh_ref, sc_ref, w_ref, wdt_ref, p0_ref, dt0_ref,
                        o_ref, dt_ref, h_scr):
    row_block = pl.program_id(0)

    @pl.when(row_block == 0)
    def _():
        o_ref[...] = p0_ref[...]

        @pl.when(pl.program_id(1) == 0)
        def _():
            dt_ref[...] = dt0_ref[...]

    @pl.when(row_block > 0)
    def _():
        _inproj_body(x_ref, posr_ref, cemb_ref, nw_ref, sh_ref, sc_ref, lambda: w_ref[...], wdt_ref, o_ref, dt_ref,
                     h_scr, True)


def _in_proj_specs(tm, d, half, pos_rows, cemb, ndt, tiles_per_seq, with_pos, blk, one_row_block):
    mode = ONE_BUFFER if one_row_block else None
    return [
        pl.BlockSpec((tm, d), lambda i, j: (blk(i), 0), pipeline_mode=mode),
        pl.BlockSpec((pos_rows, half), lambda i, j: ((blk(i) % tiles_per_seq) if with_pos else 0, 0),
                     pipeline_mode=mode),
        pl.BlockSpec(cemb.shape, lambda i, j: (0, 0), pipeline_mode=ONE_BUFFER),
        pl.BlockSpec((1, d), lambda i, j: (0, 0)),
        pl.BlockSpec((None, 1, d), lambda i, j: (blk(i) // tiles_per_seq, 0, 0)),
        pl.BlockSpec((None, 1, d), lambda i, j: (blk(i) // tiles_per_seq, 0, 0)),
    ], pl.BlockSpec((ndt, d), lambda i, j: (0, 0), pipeline_mode=ONE_BUFFER)


def _in_proj(x2, posr, cemb, nw, sh, sc, w_t, row_off, n_tiles, wdt_t, seq_len, tm, tn, with_pos, tn_rest=None):
    m, d = x2.shape
    n = n_tiles * tn
    ndt = wdt_t.shape[0]
    half = d // 2
    tiles_per_seq = seq_len // tm
    pos_rows = tm // GRID_W if with_pos else posr.shape[0]
    split = tn_rest is not None and m // tm > 1
    common, wdt_spec = _in_proj_specs(tm, d, half, pos_rows, cemb, ndt, tiles_per_seq, with_pos, lambda i: i, split)
    w_spec = pl.BlockSpec((pl.Element(tn), pl.Element(d)), lambda i, j: (pl.multiple_of(row_off(j), 8), 0))
    out_specs = [pl.BlockSpec((tm, tn), lambda i, j: (i, j)), pl.BlockSpec((tm, ndt), lambda i, j: (i, 0))]
    m_first = tm if split else m
    out_shape = [jax.ShapeDtypeStruct((m_first, n), BF16), jax.ShapeDtypeStruct((m_first, ndt), F32)]
    if split:
        out_specs.append(pl.BlockSpec((d, tn), lambda i, j: (0, j)))
        out_shape.append(jax.ShapeDtypeStruct((d, n), BF16))
    res = pl.pallas_call(
        functools.partial(_inproj_t_kernel, with_pos=with_pos, emit_w=split),
        grid=(1 if split else m // tm, n_tiles),
        in_specs=common + [w_spec, wdt_spec],
        out_specs=out_specs,
        out_shape=out_shape,
        scratch_shapes=[pltpu.VMEM((tm, d), BF16)],
        compiler_params=_params(2, 58),
        name="in_proj_t",
    )(x2, posr, cemb, nw, sh, sc, w_t, wdt_t)
    if not split:
        return res
    proj0, dt0, w_bf = res
    nj = n // tn_rest
    common, wdt_spec = _in_proj_specs(tm, d, half, pos_rows, cemb, ndt, tiles_per_seq, with_pos,
                                      lambda i: jnp.maximum(i, 1), False)
    return pl.pallas_call(
        _inproj_rest_kernel,
        grid=(m // tm, nj),
        in_specs=common + [
            pl.BlockSpec((d, tn_rest), lambda i, j: (0, j)),
            wdt_spec,
            pl.BlockSpec((tm, tn_rest), lambda i, j: (0, jnp.where(i == 0, j, nj - 1))),
            pl.BlockSpec((tm, ndt), lambda i, j: (0, 0), pipeline_mode=ONE_BUFFER),
        ],
        out_specs=[pl.BlockSpec((tm, tn_rest), lambda i, j: (i, j)),
                   pl.BlockSpec((tm, ndt), lambda i, j: (i, 0))],
        out_shape=[jax.ShapeDtypeStruct((m, n), BF16), jax.ShapeDtypeStruct((m, ndt), F32)],
        scratch_shapes=[pltpu.VMEM((tm, d), BF16)],
        compiler_params=_params(2, 58),
        name="in_proj",
    )(x2, posr, cemb, nw, sh, sc, w_bf, wdt_t, proj0, dt0)


def _conv3_chunk(x_ref, r0, rows, seq_len, w, b):
    cur = x_ref[r0:r0 + rows, :].astype(F32)
    tc = cur.shape[1]
    rid = lax.broadcasted_iota(jnp.int32, (rows, tc), 0)
    if r0 > 0:
        prev_row = x_ref[r0 - 16:r0, :].astype(F32)[15:16, :]
    else:
        prev_row = jnp.zeros((1, tc), F32)
    if r0 + rows < seq_len:
        next_row = x_ref[r0 + rows:r0 + rows + 16, :].astype(F32)[0:1, :]
    else:
        next_row = jnp.zeros((1, tc), F32)
    up = jnp.where(rid == 0, prev_row, pltpu.roll(cur, 1, 0))
    down = jnp.where(rid == rows - 1, next_row, pltpu.roll(cur, rows - 1, 0))
    return up * w[0:1, :] + cur * w[1:2, :] + down * w[2:3, :] + b


def _ssd_prep_kernel(x_ref, w_ref, b_ref, o_ref, *, rows):
    seq_len = x_ref.shape[0]
    w = w_ref[...]
    b = b_ref[...]
    for r0 in range(0, seq_len, rows):
        v = _conv3_chunk(x_ref, r0, rows, seq_len, w, b)
        o_ref[r0:r0 + rows, :] = _silu(v).astype(o_ref.dtype)


def _ssd_prep(proj3, col0, width, w8, b1, tc):
    nb, seq_len, _ = proj3.shape
    rows = min(512, seq_len)
    j0 = col0 // tc
    return pl.pallas_call(
        functools.partial(_ssd_prep_kernel, rows=rows),
        grid=(nb, width // tc),
        in_specs=[
            pl.BlockSpec((None, seq_len, tc), lambda b, j: (b, 0, j0 + j)),
            pl.BlockSpec((8, tc), lambda b, j: (0, j)),
            pl.BlockSpec((1, tc), lambda b, j: (0, j)),
        ],
        out_specs=pl.BlockSpec((None, seq_len, tc), lambda b, j: (b, 0, j)),
        out_shape=jax.ShapeDtypeStruct((nb, seq_len, width), BF16),
        compiler_params=_params(2, 40),
        name="ssd_prep",
    )(proj3, w8, b1)


def _hy_prep_kernel(x0_ref, x1_ref, v_ref, w0_ref, w1_ref, wv_ref, b0_ref, b1_ref, bv_ref,
                    x0c_ref, u_ref, *, rows):
    seq_len = x0_ref.shape[0]
    w0, w1, wv = w0_ref[...], w1_ref[...], wv_ref[...]
    b0, b1, bv = b0_ref[...], b1_ref[...], bv_ref[...]
    for r0 in range(0, seq_len, rows):
        x0c_ref[r0:r0 + rows, :] = _conv3_chunk(x0_ref, r0, rows, seq_len, w0, b0).astype(x0c_ref.dtype)
        x1c = _conv3_chunk(x1_ref, r0, rows, seq_len, w1, b1)
        vc = _conv3_chunk(v_ref, r0, rows, seq_len, wv, bv)
        u_ref[r0:r0 + rows, :] = (vc * x1c).astype(u_ref.dtype)


def _hy_prep(proj3, col0, dh, w8, b1, tc):
    nb, seq_len, _ = proj3.shape
    rows = min(512, seq_len)
    j0 = col0 // tc
    nj = dh // tc
    x_spec = lambda k: pl.BlockSpec((None, seq_len, tc), lambda b, j: (b, 0, j0 + k * nj + j))
    w_spec = lambda k: pl.BlockSpec((8, tc), lambda b, j: (0, k * nj + j))
    b_spec = lambda k: pl.BlockSpec((1, tc), lambda b, j: (0, k * nj + j))
    o_spec = pl.BlockSpec((None, seq_len, tc), lambda b, j: (b, 0, j))
    return pl.pallas_call(
        functools.partial(_hy_prep_kernel, rows=rows),
        grid=(nb, nj),
        in_specs=[x_spec(0), x_spec(1), x_spec(2), w_spec(0), w_spec(1), w_spec(2),
                  b_spec(0), b_spec(1), b_spec(2)],
        out_specs=[o_spec, o_spec],
        out_shape=[jax.ShapeDtypeStruct((nb, seq_len, dh), BF16)] * 2,
        compiler_params=_params(2, 48),
        name="hy_prep",
    )(proj3, proj3, proj3, w8, w8, w8, b1, b1, b1)


def _ssd_kernel(u_ref, dt_ref, uc_ref, dtc_ref, z_ref, alog_ref, dtb_ref, dsk_ref, nw_ref, e_ref,
                o_ref, h_scr, yf_scr, *, n_ctx, n_lat):
    q = SSD_CHUNK
    d = pl.program_id(0)
    s = pl.program_id(1)
    is_ctx = s < n_ctx
    t = jnp.maximum(s - n_ctx, 0)
    cidx = jnp.where(d == 0, t, n_lat - 1 - t)
    fwd = d == 0

    @pl.when(s == 0)
    def _():
        h_scr[...] = jnp.zeros_like(h_scr)

    row = lax.broadcasted_iota(jnp.int32, (q, q), 0)
    col = lax.broadcasted_iota(jnp.int32, (q, q), 1)
    tri = (jnp.where(fwd, row, col) >= jnp.where(fwd, col, row)).astype(F32)
    nb = u_ref.shape[0]
    res = [_ssd_chunk(b, is_ctx, tri, u_ref, dt_ref, uc_ref, dtc_ref, alog_ref, dtb_ref, e_ref, h_scr)
           for b in range(nb)]

    @pl.when(jnp.logical_and(jnp.logical_not(is_ctx), fwd))
    def _():
        for b in range(nb):
            yf_scr[b, cidx] = res[b][0]

    @pl.when(jnp.logical_and(jnp.logical_not(is_ctx), jnp.logical_not(fwd)))
    def _():
        for b in range(nb):
            y, xs = res[b]
            ytot = yf_scr[b, cidx] + y + dsk_ref[...] * xs
            gated = ytot * _silu(z_ref[b].astype(F32))
            o_ref[b] = (_rms(gated) * nw_ref[...]).astype(o_ref.dtype)


def _ssd_chunk(b, is_ctx, tri, u_ref, dt_ref, uc_ref, dtc_ref, alog_ref, dtb_ref, e_ref, h_scr):
    q = SSD_CHUNK
    d_ssd = h_scr.shape[2]
    gw = d_ssd // SSD_GROUPS
    u = jnp.where(is_ctx, uc_ref[b], u_ref[b])
    dtr = jnp.where(is_ctx, dtc_ref[b], dt_ref[b])

    dt = _softplus(dtr + dtb_ref[...])
    a = dt * (-jnp.exp(alog_ref[...]))
    cum = jnp.dot(tri, a, precision=HIGHEST, preferred_element_type=F32)
    mask_add = ((1.0 - tri) * _NEG_BIG).astype(BF16)
    cum_t = cum.T
    tot = jnp.sum(a, axis=0, keepdims=True)

    stack = jnp.concatenate(
        [dt, jnp.exp(cum), jnp.exp(tot - cum), jnp.broadcast_to(jnp.exp(tot), (8, LANES))], axis=0)
    ex = jnp.dot(stack.astype(BF16), e_ref[...], preferred_element_type=F32)
    dt_x = ex[0:q]
    ecum_x = ex[q:2 * q]
    edec_x = ex[2 * q:3 * q]
    etot_x = ex[3 * q:3 * q + 1]

    xs = u[:, :d_ssd].astype(F32)
    xdt = xs * dt_x
    xdt_b = xdt.astype(BF16)
    xdw_b = (xdt * edec_x).astype(BF16)
    lane_lo = lax.broadcasted_iota(jnp.int32, (q, LANES), 1) < SSD_HEAD_DIM
    heads_per_group = gw // SSD_HEAD_DIM

    y_parts = []
    for g in range(SSD_GROUPS):
        bg = u[:, d_ssd + g * SSD_STATE:d_ssd + (g + 1) * SSD_STATE]
        cg = u[:, d_ssd + (SSD_GROUPS + g) * SSD_STATE:d_ssd + (SSD_GROUPS + g + 1) * SSD_STATE]
        scores = lax.dot_general(cg, bg, (((1,), (1,)), ((), ())), preferred_element_type=F32).astype(BF16)
        h_prev = h_scr[b, :, g * gw:(g + 1) * gw]
        y_off = jnp.dot(cg, h_prev.astype(BF16), preferred_element_type=F32)
        for j in range(heads_per_group // 2):
            c0 = g * gw + j * LANES
            xp = xdt_b[:, c0:c0 + LANES]
            acc = None
            for hh in range(2):
                h = g * heads_per_group + 2 * j + hh
                diff = (cum[:, h:h + 1] - cum_t[h:h + 1, :]).astype(BF16)
                m_h = scores * jnp.exp(diff + mask_add)
                x_h = jnp.where(lane_lo if hh == 0 else jnp.logical_not(lane_lo), xp, jnp.zeros_like(xp))
                part = jnp.dot(m_h, x_h, preferred_element_type=F32)
                acc = part if acc is None else acc + part
            y_parts.append(acc + y_off[:, j * LANES:(j + 1) * LANES] * ecum_x[:, c0:c0 + LANES])
        upd = lax.dot_general(bg, xdw_b[:, g * gw:(g + 1) * gw], (((0,), (0,)), ((), ())),
                              preferred_element_type=F32)
        h_scr[b, :, g * gw:(g + 1) * gw] = etot_x[:, g * gw:(g + 1) * gw] * h_prev + upd
    return jnp.concatenate(y_parts, axis=1), xs


def _ssd(u_lat, dt_lat, u_ctx, dt_ctx, proj3, alog, dtb, dsk, nw, expand):
    nb, seq_len, d_xbc = u_lat.shape
    ctx_len = u_ctx.shape[1]
    q = SSD_CHUNK
    n_lat = seq_len // q
    n_ctx = ctx_len // q
    d_ssd = dsk.shape[1]

    def lat_idx(d, s):
        t = jnp.maximum(s - n_ctx, 0)
        return jnp.where(d == 0, t, n_lat - 1 - t)

    def ctx_idx(d, s):
        t = jnp.minimum(s, n_ctx - 1)
        return jnp.where(d == 0, t, n_ctx - 1 - t)

    def out_idx(d, s):
        return jnp.where(d == 0, n_lat - 1, lat_idx(d, s))

    return pl.pallas_call(
        functools.partial(_ssd_kernel, n_ctx=n_ctx, n_lat=n_lat),
        grid=(2, n_ctx + n_lat),
        in_specs=[
            pl.BlockSpec((nb, q, d_xbc), lambda d, s: (0, lat_idx(d, s), 0)),
            pl.BlockSpec((nb, q, LANES), lambda d, s: (0, lat_idx(d, s), d)),
            pl.BlockSpec((nb, q, d_xbc), lambda d, s: (0, ctx_idx(d, s), 0)),
            pl.BlockSpec((nb, q, LANES), lambda d, s: (0, ctx_idx(d, s), d)),
            pl.BlockSpec((nb, q, d_ssd), lambda d, s: (0, lat_idx(d, s), 0)),
            pl.BlockSpec((None, 1, LANES), lambda d, s: (d, 0, 0)),
            pl.BlockSpec((None, 1, LANES), lambda d, s: (d, 0, 0)),
            pl.BlockSpec((1, d_ssd), lambda d, s: (0, 0)),
            pl.BlockSpec((1, d_ssd), lambda d, s: (0, 0)),
            pl.BlockSpec((LANES, d_ssd), lambda d, s: (0, 0)),
        ],
        out_specs=pl.BlockSpec((nb, q, d_ssd), lambda d, s: (0, out_idx(d, s), 0)),
        out_shape=jax.ShapeDtypeStruct((nb, seq_len, d_ssd), BF16),
        scratch_shapes=[
            pltpu.VMEM((nb, SSD_STATE, d_ssd), F32),
            pltpu.VMEM((nb, n_lat, q, d_ssd), F32),
        ],
        compiler_params=_params(2, 56),
        name="ssd",
    )(u_lat, dt_lat, u_ctx, dt_ctx, proj3, alog, dtb, dsk, nw, expand)


def _hy_mlp_kernel(zt_ref, w1_ref, w2_ref, w3_ref, b_ref, fr_ref, o_ref):
    fr = fr_ref[...]
    b = b_ref[...]
    h = jnp.sin(fr * (jnp.dot(w1_ref[...], zt_ref[...], precision=HIGHEST, preferred_element_type=F32)
                      + b[:, 0:1]))
    h = jnp.sin(fr * (jnp.dot(w2_ref[...], h, precision=HIGHEST, preferred_element_type=F32) + b[:, 1:2]))
    h = jnp.sin(fr * (jnp.dot(w3_ref[...], h, precision=HIGHEST, preferred_element_type=F32) + b[:, 2:3]))
    hid, seq_len = h.shape
    hp = jnp.concatenate([h, jnp.zeros((LANES - hid, seq_len), F32)], axis=0)
    o_ref[...] = hp.T


def _hy_mlp(zt, w1t, w2t, w3t, b3, fr):
    seq_len = zt.shape[1]
    return pl.pallas_call(
        _hy_mlp_kernel,
        out_shape=jax.ShapeDtypeStruct((seq_len, LANES), F32),
        compiler_params=pltpu.CompilerParams(vmem_limit_bytes=40 * MIB),
        name="hy_mlp",
    )(zt, w1t, w2t, w3t, b3, fr)


def _fft_tables(seq_len):
    n_fft = 2 * seq_len
    n2 = FFT_N2
    n1 = n_fft // n2
    n1h = n1 // 2
    k1n = n1h + 1
    k1p = -(-k1n // 4) * 4
    k1 = np.arange(k1n, dtype=np.int64)
    th = (2.0 * np.pi / n1) * ((k1[:, None] * np.arange(n1h, dtype=np.int64)[None, :]) % n1)
    f1 = np.zeros((2 * k1p, n1h))
    f1[0:2 * k1n:2] = np.cos(th)
    f1[1:2 * k1n:2] = -np.sin(th)
    idx = np.arange(n2, dtype=np.int64)
    kk = k1[:, None, None] + n1 * idx[None, :, None]
    ph = (2.0 * np.pi / n_fft) * ((kk * idx[None, None, :]) % n_fft)
    g_re, g_im = np.cos(ph), -np.sin(ph)

    def blocks(re, im):
        out = np.zeros((k1p, 2 * n2, 2 * n2))
        out[:k1n, :n2, :n2] = re
        out[:k1n, :n2, n2:] = -im
        out[:k1n, n2:, :n2] = im
        out[:k1n, n2:, n2:] = re
        return out

    wgt = np.where((k1 == 0) | (k1 == n1h), 1.0, 2.0) / n_fft
    gf = blocks(g_re, g_im)
    gi = blocks(np.transpose(g_re, (0, 2, 1)) * wgt[:, None, None],
                -np.transpose(g_im, (0, 2, 1)) * wgt[:, None, None])
    as_bf16 = lambda t: jnp.asarray(t.astype(np.float32)).astype(BF16)
    return as_bf16(f1), as_bf16(np.ascontiguousarray(f1.T)), as_bf16(gf), as_bf16(gi)


_BATCHED = (((2,), (1,)), ((0,), (0,)))


def _fft_forward(x, f1, gf):
    seq_len, c = x.shape
    rows, n1h = f1.shape
    xt = jnp.swapaxes(x.reshape(n1h, FFT_N2, c), 0, 1).astype(BF16)
    a = lax.dot_general(jnp.broadcast_to(f1[None], (FFT_N2, rows, n1h)), xt, _BATCHED,
                        preferred_element_type=F32)
    at = jnp.swapaxes(a, 0, 1).reshape(rows // 2, 2 * FFT_N2, c).astype(BF16)
    return lax.dot_general(gf, at, _BATCHED, preferred_element_type=F32)


def _fft_inverse(y, f1t, gi):
    c = y.shape[2]
    n1h, rows = f1t.shape
    bt = lax.dot_general(gi, y, _BATCHED, preferred_element_type=F32)
    btt = jnp.swapaxes(bt.reshape(rows, FFT_N2, c), 0, 1).astype(BF16)
    yv = lax.dot_general(jnp.broadcast_to(f1t[None], (FFT_N2, n1h, rows)), btt, _BATCHED,
                         preferred_element_type=F32)
    return jnp.swapaxes(yv, 0, 1).reshape(n1h * FFT_N2, c)


def _hy_spec_kernel(h_ref, wf_ref, wb_ref, dl_ref, f1_ref, gf_ref, k_ref):
    n2 = FFT_N2
    seq_len = h_ref.shape[0]
    tc = wf_ref.shape[1]
    h3 = h_ref[...].astype(BF16)
    rid = lax.broadcasted_iota(jnp.int32, (seq_len, tc), 0)
    decay = jnp.exp(-(rid.astype(F32) * (1.0 / (seq_len - 1))) * dl_ref[...])
    hf = jnp.dot(h3, wf_ref[...].astype(BF16), preferred_element_type=F32) * decay
    hb = jnp.dot(h3, wb_ref[...].astype(BF16), preferred_element_type=F32) * decay
    norm = jnp.sum(jnp.abs(hf) + jnp.abs(hb), axis=0, keepdims=True) + 1e-6
    inv = 1.0 / norm
    hb = jnp.where(rid == 0, 0.0, hb * inv)
    x = _fft_forward(jnp.concatenate([hf * inv, hb], axis=1), f1_ref[...], gf_ref[...])
    xf, xb = x[:, :, 0:tc], x[:, :, tc:2 * tc]
    k_ref[...] = jnp.concatenate([xf[:, :n2] + xb[:, :n2], xf[:, n2:] - xb[:, n2:]], axis=1)


def _hy_spec(h3, w4f, w4b, deltas, f1, gf, tc):
    seq_len = h3.shape[0]
    dh = w4f.shape[1]
    k1p = gf.shape[0]
    rows, n1h = f1.shape
    return pl.pallas_call(
        _hy_spec_kernel,
        grid=(dh // tc,),
        in_specs=[
            pl.BlockSpec((seq_len, LANES), lambda j: (0, 0)),
            pl.BlockSpec((LANES, tc), lambda j: (0, j)),
            pl.BlockSpec((LANES, tc), lambda j: (0, j)),
            pl.BlockSpec((1, tc), lambda j: (0, j)),
            pl.BlockSpec((rows, n1h), lambda j: (0, 0)),
            pl.BlockSpec((k1p, 2 * FFT_N2, 2 * FFT_N2), lambda j: (0, 0, 0)),
        ],
        out_specs=pl.BlockSpec((k1p, 2 * FFT_N2, tc), lambda j: (0, 0, j)),
        out_shape=jax.ShapeDtypeStruct((k1p, 2 * FFT_N2, dh), F32),
        compiler_params=_params(1, 56),
        name="hy_spec",
    )(h3, w4f, w4b, deltas, f1, gf)


def _hy_conv_kernel(u_ref, x0_ref, k_ref, f1_ref, f1t_ref, gf_ref, gi_ref, bias_ref, o_ref):
    n2 = FFT_N2
    u = u_ref[...].astype(F32)
    x = _fft_forward(u, f1_ref[...], gf_ref[...])
    kk = k_ref[...]
    xr, xi, kr, ki = x[:, :n2], x[:, n2:], kk[:, :n2], kk[:, n2:]
    y = jnp.concatenate([xr * kr - xi * ki, xr * ki + xi * kr], axis=1).astype(BF16)
    conv = _fft_inverse(y, f1t_ref[...], gi_ref[...])
    o_ref[...] = (x0_ref[...].astype(F32) * (conv + u * bias_ref[...])).astype(o_ref.dtype)


def _hy_conv(u3, x0c, kspec, f1, f1t, gf, gi, bias, tc):
    nb, seq_len, dh = u3.shape
    k1p = gf.shape[0]
    rows, n1h = f1.shape
    blk = pl.BlockSpec((None, seq_len, tc), lambda j, b: (b, 0, j))
    const = lambda shape: pl.BlockSpec(shape, lambda j, b: (0,) * len(shape), pipeline_mode=ONE_BUFFER)
    return pl.pallas_call(
        _hy_conv_kernel,
        grid=(dh // tc, nb),
        in_specs=[
            blk, blk,
            pl.BlockSpec((k1p, 2 * FFT_N2, tc), lambda j, b: (0, 0, j)),
            const((rows, n1h)), const((n1h, rows)),
            const((k1p, 2 * FFT_N2, 2 * FFT_N2)), const((k1p, 2 * FFT_N2, 2 * FFT_N2)),
            pl.BlockSpec((1, tc), lambda j, b: (0, j)),
        ],
        out_specs=blk,
        out_shape=jax.ShapeDtypeStruct((nb, seq_len, dh), BF16),
        compiler_params=_params(2, 56),
        name="hy_conv",
    )(u3, x0c, kspec, f1, f1t, gf, gi, bias)


def _outproj_kernel(ys_ref, yh_ref, w_ref, x_ref, posr_ref, cemb_ref, nwp_ref, g_ref, nwf_ref, sh_ref, sc_ref,
                    xo_ref, h_ref):
    tm, ds = ys_ref.shape
    rows = min(256, tm)
    for r0 in range(0, tm, rows):
        rs = slice(r0, r0 + rows)
        y = jnp.dot(ys_ref[rs, :], w_ref[0:ds, :], preferred_element_type=F32)
        y = y + jnp.dot(yh_ref[rs, :], w_ref[ds:, :], preferred_element_type=F32)
        xn = _add_pos(x_ref[rs, :], posr_ref, r0 // GRID_W, cemb_ref[...]) + g_ref[...] * (_rms(y) * nwp_ref[...])
        xo_ref[rs, :] = xn
        h_ref[rs, :] = (_rms(xn) * nwf_ref[...] * (1.0 + sc_ref[...]) + sh_ref[...]).astype(h_ref.dtype)


def _out_proj(ys, yh, w, x2, posr, cemb, nwp, g1, nwf, sh2, sc2, seq_len, tm):
    m, d = x2.shape
    ds = ys.shape[1]
    dh = yh.shape[1]
    tiles_per_seq = seq_len // tm
    row = lambda i: (i, 0)
    fixed = lambda i: (0, 0)
    per_batch = lambda i: (i // tiles_per_seq, 0, 0)
    return pl.pallas_call(
        _outproj_kernel,
        grid=(m // tm,),
        in_specs=[
            pl.BlockSpec((tm, ds), row),
            pl.BlockSpec((tm, dh), row),
            pl.BlockSpec((ds + dh, d), fixed, pipeline_mode=ONE_BUFFER),
            pl.BlockSpec((tm, d), row),
            pl.BlockSpec((tm // GRID_W, d // 2), lambda i: (i % tiles_per_seq, 0)),
            pl.BlockSpec(cemb.shape, fixed),
            pl.BlockSpec((1, d), fixed),
            pl.BlockSpec((None, 1, d), per_batch),
            pl.BlockSpec((1, d), fixed),
            pl.BlockSpec((None, 1, d), per_batch),
            pl.BlockSpec((None, 1, d), per_batch),
        ],
        out_specs=[pl.BlockSpec((tm, d), row), pl.BlockSpec((tm, d), row)],
        out_shape=[jax.ShapeDtypeStruct((m, d), F32), jax.ShapeDtypeStruct((m, d), BF16)],
        compiler_params=_params(1, 56),
        name="out_proj",
    )(ys, yh, w, x2, posr, cemb, nwp, g1, nwf, sh2, sc2)


def _ffn_step(h_ref, wg, wu, wd, x_ref, nw_ref, g_ref, o_ref):
    f = pl.program_id(1)
    h = h_ref[...]
    gate = jnp.dot(h, wg, preferred_element_type=F32)
    up = jnp.dot(h, wu, preferred_element_type=F32)
    act = (_silu(gate) * up).astype(BF16)

    @pl.when(f == 0)
    def _():
        o_ref[...] = jnp.zeros_like(o_ref)

    d = o_ref.shape[1]
    cw = min(512, d)
    for n0 in range(0, d, cw):
        o_ref[:, n0:n0 + cw] += jnp.dot(act, wd[:, n0:n0 + cw], preferred_element_type=F32)

    @pl.when(f == pl.num_programs(1) - 1)
    def _():
        tm = o_ref.shape[0]
        rows = min(256, tm)
        for r0 in range(0, tm, rows):
            y = o_ref[r0:r0 + rows, :]
            o_ref[r0:r0 + rows, :] = x_ref[r0:r0 + rows, :] + g_ref[...] * (_rms(y) * nw_ref[...])


def _ffn_first_kernel(h_ref, wg_ref, wu_ref, wd_ref, x_ref, nw_ref, g_ref, o_ref, wgb_ref, wub_ref, wdb_ref):
    wg = wg_ref[...].astype(BF16)
    wu = wu_ref[...].astype(BF16)
    wd = wd_ref[...].astype(BF16)
    wgb_ref[...] = wg
    wub_ref[...] = wu
    wdb_ref[...] = wd
    _ffn_step(h_ref, wg, wu, wd, x_ref, nw_ref, g_ref, o_ref)


def _ffn_rest_kernel(h_ref, wg_ref, wu_ref, wd_ref, x_ref, nw_ref, g_ref, o_ref):
    _ffn_step(h_ref, wg_ref[...], wu_ref[...], wd_ref[...], x_ref, nw_ref, g_ref, o_ref)


def _ffn(h2, wg, wu, wd, xn, nw, g2, seq_len, tm, tf_first, tf):
    m, d = xn.shape
    dff = wg.shape[1]
    tiles_per_seq = seq_len // tm
    n_rows = m // tm
    out, wgb, wub, wdb = pl.pallas_call(
        _ffn_first_kernel,
        grid=(1, dff // tf_first),
        in_specs=[
            pl.BlockSpec((tm, d), lambda i, f: (0, 0), pipeline_mode=ONE_BUFFER),
            pl.BlockSpec((d, tf_first), lambda i, f: (0, f)),
            pl.BlockSpec((d, tf_first), lambda i, f: (0, f)),
            pl.BlockSpec((tf_first, d), lambda i, f: (f, 0)),
            pl.BlockSpec((tm, d), lambda i, f: (0, 0), pipeline_mode=ONE_BUFFER),
            pl.BlockSpec((1, d), lambda i, f: (0, 0)),
            pl.BlockSpec((None, 1, d), lambda i, f: (0, 0, 0)),
        ],
        out_specs=[
            pl.BlockSpec((tm, d), lambda i, f: (0, 0)),
            pl.BlockSpec((d, tf_first), lambda i, f: (0, f)),
            pl.BlockSpec((d, tf_first), lambda i, f: (0, f)),
            pl.BlockSpec((tf_first, d), lambda i, f: (f, 0)),
        ],
        out_shape=[
            jax.ShapeDtypeStruct((m, d), F32),
            jax.ShapeDtypeStruct((d, dff), BF16),
            jax.ShapeDtypeStruct((d, dff), BF16),
            jax.ShapeDtypeStruct((dff, d), BF16),
        ],
        input_output_aliases={4: 0},
        compiler_params=_params(2, 58),
        name="ffn_first",
    )(h2, wg, wu, wd, xn, nw, g2)
    if n_rows == 1:
        return out
    return pl.pallas_call(
        _ffn_rest_kernel,
        grid=(n_rows - 1, dff // tf),
        in_specs=[
            pl.BlockSpec((tm, d), lambda i, f: (i + 1, 0)),
            pl.BlockSpec((d, tf), lambda i, f: (0, f)),
            pl.BlockSpec((d, tf), lambda i, f: (0, f)),
            pl.BlockSpec((tf, d), lambda i, f: (f, 0)),
            pl.BlockSpec((tm, d), lambda i, f: (i + 1, 0), pipeline_mode=ONE_BUFFER),
            pl.BlockSpec((1, d), lambda i, f: (0, 0)),
            pl.BlockSpec((None, 1, d), lambda i, f: ((i + 1) // tiles_per_seq, 0, 0)),
        ],
        out_specs=pl.BlockSpec((tm, d), lambda i, f: (i + 1, 0)),
        out_shape=jax.ShapeDtypeStruct((m, d), F32),
        input_output_aliases={4: 0},
        compiler_params=_params(2, 58),
        name="ffn",
    )(h2, wgb, wub, wdb, out, nw, g2)


def _sincos_tables(rows, cols, dim):
    qd = dim // 4
    omega = 1.0 / (POS_THETA ** (jnp.arange(qd, dtype=F32) / qd))
    r = jnp.arange(rows, dtype=F32)[:, None] * omega
    cc = jnp.arange(cols, dtype=F32)[:, None] * omega
    r_emb = jnp.concatenate([jnp.sin(r), jnp.cos(r)], -1)
    c_emb = jnp.concatenate([jnp.sin(cc), jnp.cos(cc)], -1)
    return r_emb, c_emb


def _filter_features_t(seq_len, n_bands):
    t = jnp.linspace(0.0, 1.0, seq_len, dtype=F32)[:, None]
    w = 2.0 * math.pi * jnp.arange(seq_len, dtype=F32)[:, None] / seq_len
    fb = jnp.linspace(1e-4, n_bands - 1, n_bands, dtype=F32)[None]
    zpos = jnp.concatenate([t, jnp.cos(fb * w), -jnp.sin(fb * w)], -1)
    emb = zpos.shape[1]
    return jnp.pad(zpos, ((0, 0), (0, LANES - emb))).T


def _pad_rows(a, rows):
    return jnp.pad(a, ((0, rows - a.shape[0]), (0, 0)))


def kernel(x, c, ctx, c_ctx, w_ada, b_ada, norm_mix_pre, norm_mix_post, norm_ffn_pre, norm_ffn_post,
           w_in, ssd_conv_w, ssd_conv_b, ssd_a_log, ssd_dt_bias, ssd_d, ssd_norm,
           hy_conv_w, hy_conv_b, hy_w1, hy_b1, hy_w2, hy_b2, hy_w3, hy_b3, hy_w4, hy_freq, hy_bias,
           w_out, w_gate, w_up, w_down):
    nb, seq_len, d = x.shape
    ctx_len = ctx.shape[1]
    assert w_ada.shape[0] == 1, "single layer"
    n_heads = ssd_d.shape[1]
    d_ssd = n_heads * SSD_HEAD_DIM
    d_xbc = d_ssd + 2 * SSD_GROUPS * SSD_STATE
    dh = hy_bias.shape[1]
    assert w_in.shape[2] == d_ssd + d_xbc + 2 * n_heads + 3 * dh
    assert n_heads <= LANES and nb + 1 <= 8
    assert seq_len % (GRID_W * 8) == 0 and seq_len % FFT_N2 == 0
    m = nb * seq_len

    crows = jnp.concatenate([c, c_ctx[None, :]], axis=0)
    mod = _ada(crows, w_ada[0], b_ada[0])
    part = lambda r0, r1, k: mod[r0:r1, k * d:(k + 1) * d][:, None, :]
    sh1, sc1, g1, sh2, sc2, g2 = (part(0, nb, k) for k in range(6))
    csh1 = jnp.broadcast_to(part(nb, nb + 1, 0), (nb, 1, d))
    csc1 = jnp.broadcast_to(part(nb, nb + 1, 1), (nb, 1, d))

    w_t = jnp.transpose(w_in[0])
    o_xbc = d_ssd
    o_dt = o_xbc + d_xbc
    o_hy = o_dt + 2 * n_heads
    tn_in = 512
    assert o_dt % tn_in == 0 and (3 * dh) % tn_in == 0 and o_xbc % tn_in == 0 and o_hy % 16 == 0
    n_left = o_dt // tn_in
    main_off = lambda j: jnp.where(j < n_left, j * tn_in, o_hy + (j - n_left) * tn_in)
    ctx_off = lambda j: o_xbc + j * tn_in
    pad_dt = lambda rows: jnp.pad(rows, ((0, LANES - n_heads), (0, 0)))
    w_dt_t = jnp.concatenate([pad_dt(w_t[o_dt:o_dt + n_heads]), pad_dt(w_t[o_dt + n_heads:o_hy])],
                             axis=0)

    r_emb, c_emb = _sincos_tables(seq_len // GRID_W, GRID_W, d)
    posr = r_emb
    nmp = norm_mix_pre[0][None, :]

    tm_in = min(1024, seq_len)
    n_main = o_dt + 3 * dh
    tn_rest = n_main // 4 if n_main % (4 * LANES) == 0 else tn_in
    proj, dt_lat = _in_proj(x.reshape(m, d), posr, c_emb, nmp, sh1, sc1, w_t, main_off, n_main // tn_in,
                            w_dt_t, seq_len, tm_in, tn_in, True, tn_rest)
    tm_ctx = min(256, ctx_len)
    xbc_ctx, dt_ctx = _in_proj(ctx.reshape(nb * ctx_len, d), jnp.zeros((8, d // 2), F32), c_emb, nmp, csh1, csc1,
                               w_t, ctx_off, d_xbc // tn_in, w_dt_t, ctx_len, tm_ctx, tn_in, False)
    proj3 = proj.reshape(nb, seq_len, -1)

    cw8 = _pad_rows(ssd_conv_w[0], 8)
    cb1 = ssd_conv_b[0][None, :]
    u_lat = _ssd_prep(proj3, d_ssd, d_xbc, cw8, cb1, 512)
    u_ctx = _ssd_prep(xbc_ctx.reshape(nb, ctx_len, d_xbc), 0, d_xbc, cw8, cb1, 512)
    pad_heads = lambda a: jnp.pad(a, ((0, 0), (0, LANES - n_heads)))[:, None, :]
    expand = (jnp.arange(LANES)[:, None] == (jnp.arange(d_ssd)[None, :] // SSD_HEAD_DIM)).astype(BF16)
    y_ssd = _ssd(u_lat, dt_lat.reshape(nb, seq_len, 2 * LANES), u_ctx, dt_ctx.reshape(nb, ctx_len, 2 * LANES),
                 proj3, pad_heads(ssd_a_log[0]), pad_heads(ssd_dt_bias[0]),
                 jnp.repeat(ssd_d[0], SSD_HEAD_DIM)[None, :], ssd_norm[0][None, :], expand)

    x0c, u_hy = _hy_prep(proj3, d_ssd + d_xbc, dh, _pad_rows(hy_conv_w[0], 8), hy_conv_b[0][None, :], 256)
    n_bands = (hy_w1.shape[1] - 1) // 2
    zt = _filter_features_t(seq_len, n_bands)
    w1t = jnp.pad(hy_w1[0].T, ((0, 0), (0, LANES - hy_w1.shape[1])))
    b3 = jnp.stack([hy_b1[0], hy_b2[0], hy_b3[0]], axis=1)
    h3 = _hy_mlp(zt, w1t, hy_w2[0].T, hy_w3[0].T, b3, hy_freq[0][:, None])
    w4 = _pad_rows(hy_w4[0], LANES)
    max_decay = math.log(HY_TARGET) / HY_FAST_PCT
    min_decay = math.log(HY_TARGET) / HY_SLOW_PCT
    deltas = jnp.abs(jnp.linspace(min_decay, max_decay, dh, dtype=F32))[None, :]
    f1, f1t, gf, gi = _fft_tables(seq_len)
    kspec = _hy_spec(h3, w4[:, :dh], w4[:, dh:], deltas, f1, gf, LANES)
    y_hy = _hy_conv(u_hy, x0c, kspec, f1, f1t, gf, gi, hy_bias[0][None, :], LANES)

    xn, h2 = _out_proj(y_ssd.reshape(m, d_ssd), y_hy.reshape(m, dh), w_out[0].astype(BF16), x.reshape(m, d),
                       posr, c_emb, norm_mix_post[0][None, :], g1, norm_ffn_pre[0][None, :], sh2, sc2, seq_len,
                       min(512, seq_len))
    out = _ffn(h2, w_gate[0], w_up[0], w_down[0], xn, norm_ffn_post[0][None, :], g2, seq_len,
               min(1024, seq_len), 256, 512)
    return out.reshape(nb, seq_len, d)
```

```python
import functools
import math

import numpy as np
import jax
import jax.numpy as jnp
from jax import lax
from jax.experimental import pallas as pl
from jax.experimental.pallas import tpu as pltpu

F32 = jnp.float32
BF16 = jnp.bfloat16
HIGHEST = lax.Precision.HIGHEST

RMS_EPS = 1e-6
POS_THETA = 10000.0
GRID_W = 64
SSD_HEAD_DIM = 64
SSD_GROUPS = 2
SSD_STATE = 128
SSD_CHUNK = 128
SSD_STEP_CHUNKS = 2
HY_TARGET = 1e-2
HY_FAST_PCT = 0.3
HY_SLOW_PCT = 1.5
FFT_N2 = 64
LANES = 128
MIB = 1024 * 1024
ONE_BUFFER = pl.Buffered(1)
_NT_DIMS = (((1,), (1,)), ((), ()))
_NEG_BIG = -1e30


def _params(n_axes, vmem_mib):
    return pltpu.CompilerParams(
        dimension_semantics=("arbitrary",) * n_axes,
        vmem_limit_bytes=vmem_mib * MIB,
    )


def _silu(v):
    return v * (1.0 / (1.0 + jnp.exp(-v)))


def _softplus(v):
    return jnp.maximum(v, 0.0) + jnp.log(1.0 + jnp.exp(-jnp.abs(v)))


def _rms(v):
    return v * lax.rsqrt(jnp.mean(v * v, axis=-1, keepdims=True) + RMS_EPS)


def _add_pos(x, remb_ref, g0, cemb):
    rows, d = x.shape
    half = d // 2
    lo = jnp.concatenate(
        [x[g * GRID_W:(g + 1) * GRID_W, :half] + remb_ref[g0 + g:g0 + g + 1, :] for g in range(rows // GRID_W)],
        axis=0)
    hi = (x[:, half:].reshape(rows // GRID_W, GRID_W, half) + cemb[None]).reshape(rows, half)
    return jnp.concatenate([lo, hi], axis=1)


def _ada_kernel(c_ref, w_ref, b_ref, o_ref, s_scr):
    nr, d, _ = c_ref.shape
    tn = w_ref.shape[1]
    nt = tn // LANES
    kc = 64

    @pl.when(pl.program_id(0) == 0)
    def _():
        for r in range(nr):
            for k0 in range(0, d, 512):
                s_scr[r, k0:k0 + 512, :] = _silu(c_ref[r, k0:k0 + 512, :])

    acc = [[jnp.zeros((8, LANES), F32) for _ in range(nt)] for _ in range(nr)]
    for k0 in range(0, d, kc):
        s = [s_scr[r, k0:k0 + kc, :] for r in range(nr)]
        for t in range(nt):
            w = w_ref[k0:k0 + kc, t * LANES:(t + 1) * LANES]
            for r in range(nr):
                acc[r][t] = acc[r][t] + jnp.sum((w * s[r]).reshape(kc // 8, 8, LANES), axis=0)
    rows = [jnp.concatenate([jnp.sum(a, axis=0, keepdims=True) for a in acc[r]], axis=1) for r in range(nr)]
    rows.append(jnp.zeros((o_ref.shape[0] - nr, tn), F32))
    o_ref[...] = jnp.concatenate(rows, axis=0) + b_ref[...]


def _ada(crows, w_ada, b_ada):
    nr = crows.shape[0]
    d, n = w_ada.shape
    tn = 1024
    c_lanes = jnp.broadcast_to(crows[:, :, None], (nr, d, LANES))
    return pl.pallas_call(
        _ada_kernel,
        grid=(n // tn,),
        in_specs=[
            pl.BlockSpec((nr, d, LANES), lambda j: (0, 0, 0), pipeline_mode=ONE_BUFFER),
            pl.BlockSpec((d, tn), lambda j: (0, j)),
            pl.BlockSpec((1, tn), lambda j: (0, j)),
        ],
        out_specs=pl.BlockSpec((8, tn), lambda j: (0, j)),
        out_shape=jax.ShapeDtypeStruct((8, n), F32),
        scratch_shapes=[pltpu.VMEM((nr, d, LANES), F32)],
        compiler_params=_params(1, 40),
        name="ada",
    )(c_lanes, w_ada, b_ada.reshape(1, n))


def _inproj_body(x_ref, posr_ref, cemb_ref, nw_ref, sh_ref, sc_ref, get_w, wdt_ref, o_ref, dt_ref, h_scr, with_pos):
    first = pl.program_id(1) == 0

    @pl.when(first)
    def _():
        tm = x_ref.shape[0]
        rows = min(256, tm)
        wdt = wdt_ref[...].astype(BF16)
        w = get_w()
        for r0 in range(0, tm, rows):
            xf = x_ref[r0:r0 + rows, :]
            if with_pos:
                xf = _add_pos(xf, posr_ref, r0 // GRID_W, cemb_ref[...])
            h = _rms(xf) * nw_ref[...] * (1.0 + sc_ref[...]) + sh_ref[...]
            hb = h.astype(BF16)
            h_scr[r0:r0 + rows, :] = hb
            dt_ref[r0:r0 + rows, :] = lax.dot_general(hb, wdt, _NT_DIMS, preferred_element_type=F32)
            o_ref[r0:r0 + rows, :] = jnp.dot(hb, w, preferred_element_type=F32).astype(o_ref.dtype)

    @pl.when(jnp.logical_not(first))
    def _():
        o_ref[...] = jnp.dot(h_scr[...], get_w(), preferred_element_type=F32).astype(o_ref.dtype)


def _inproj_t_kernel(x_ref, posr_ref, cemb_ref, nw_ref, sh_ref, sc_ref, w_ref, wdt_ref, *rest, with_pos, emit_w):
    w = w_ref[...].T.astype(BF16)
    if emit_w:
        o_ref, dt_ref, wb_ref, h_scr = rest
        wb_ref[...] = w
    else:
        o_ref, dt_ref, h_scr = rest
    _inproj_body(x_ref, posr_ref, cemb_ref, nw_ref, sh_ref, sc_ref, lambda: w, wdt_ref, o_ref, dt_ref, h_scr,
                 with_pos)


def _inproj_rest_kernel(x_ref, posr_ref, cemb_ref, nw_ref, sh_ref, sc_ref, w_ref, wdt_ref, p0_ref, dt0_ref,
                        o_ref, dt_ref, h_scr):
    row_block = pl.program_id(0)

    @pl.when(row_block == 0)
    def _():
        o_ref[...] = p0_ref[...]

        @pl.when(pl.program_id(1) == 0)
        def _():
            dt_ref[...] = dt0_ref[...]

    @pl.when(row_block > 0)
    def _():
        _inproj_body(x_ref, posr_ref, cemb_ref, nw_ref, sh_ref, sc_ref, lambda: w_ref[...], wdt_ref, o_ref, dt_ref,
                     h_scr, True)


def _in_proj_specs(tm, d, half, pos_rows, cemb, ndt, tiles_per_seq, with_pos, blk, one_row_block):
    mode = ONE_BUFFER if one_row_block else None
    return [
        pl.BlockSpec((tm, d), lambda i, j: (blk(i), 0), pipeline_mode=mode),
        pl.BlockSpec((pos_rows, half), lambda i, j: ((blk(i) % tiles_per_seq) if with_pos else 0, 0),
                     pipeline_mode=mode),
        pl.BlockSpec(cemb.shape, lambda i, j: (0, 0), pipeline_mode=ONE_BUFFER),
        pl.BlockSpec((1, d), lambda i, j: (0, 0)),
        pl.BlockSpec((None, 1, d), lambda i, j: (blk(i) // tiles_per_seq, 0, 0)),
        pl.BlockSpec((None, 1, d), lambda i, j: (blk(i) // tiles_per_seq, 0, 0)),
    ], pl.BlockSpec((ndt, d), lambda i, j: (0, 0), pipeline_mode=ONE_BUFFER)


def _in_proj(x2, posr, cemb, nw, sh, sc, w_t, row_off, n_tiles, wdt_t, seq_len, tm, tn, with_pos, tn_rest=None):
    m, d = x2.shape
    n = n_tiles * tn
    ndt = wdt_t.shape[0]
    half = d // 2
    tiles_per_seq = seq_len // tm
    pos_rows = tm // GRID_W if with_pos else posr.shape[0]
    split = tn_rest is not None and m // tm > 1
    common, wdt_spec = _in_proj_specs(tm, d, half, pos_rows, cemb, ndt, tiles_per_seq, with_pos, lambda i: i, split)
    w_spec = pl.BlockSpec((pl.Element(tn), pl.Element(d)), lambda i, j: (pl.multiple_of(row_off(j), 8), 0))
    out_specs = [pl.BlockSpec((tm, tn), lambda i, j: (i, j)), pl.BlockSpec((tm, ndt), lambda i, j: (i, 0))]
    m_first = tm if split else m
    out_shape = [jax.ShapeDtypeStruct((m_first, n), BF16), jax.ShapeDtypeStruct((m_first, ndt), F32)]
    if split:
        out_specs.append(pl.BlockSpec((d, tn), lambda i, j: (0, j)))
        out_shape.append(jax.ShapeDtypeStruct((d, n), BF16))
    res = pl.pallas_call(
        functools.partial(_inproj_t_kernel, with_pos=with_pos, emit_w=split),
        grid=(1 if split else m // tm, n_tiles),
        in_specs=common + [w_spec, wdt_spec],
        out_specs=out_specs,
        out_shape=out_shape,
        scratch_shapes=[pltpu.VMEM((tm, d), BF16)],
        compiler_params=_params(2, 58),
        name="in_proj_t",
    )(x2, posr, cemb, nw, sh, sc, w_t, wdt_t)
    if not split:
        return res
    proj0, dt0, w_bf = res
    nj = n // tn_rest
    common, wdt_spec = _in_proj_specs(tm, d, half, pos_rows, cemb, ndt, tiles_per_seq, with_pos,
                                      lambda i: jnp.maximum(i, 1), False)
    return pl.pallas_call(
        _inproj_rest_kernel,
        grid=(m // tm, nj),
        in_specs=common + [
            pl.BlockSpec((d, tn_rest), lambda i, j: (0, j)),
            wdt_spec,
            pl.BlockSpec((tm, tn_rest), lambda i, j: (0, jnp.where(i == 0, j, nj - 1))),
            pl.BlockSpec((tm, ndt), lambda i, j: (0, 0), pipeline_mode=ONE_BUFFER),
        ],
        out_specs=[pl.BlockSpec((tm, tn_rest), lambda i, j: (i, j)),
                   pl.BlockSpec((tm, ndt), lambda i, j: (i, 0))],
        out_shape=[jax.ShapeDtypeStruct((m, n), BF16), jax.ShapeDtypeStruct((m, ndt), F32)],
        scratch_shapes=[pltpu.VMEM((tm, d), BF16)],
        compiler_params=_params(2, 58),
        name="in_proj",
    )(x2, posr, cemb, nw, sh, sc, w_bf, wdt_t, proj0, dt0)


def _conv3_chunk(x_ref, r0, rows, seq_len, w, b):
    cur = x_ref[r0:r0 + rows, :].astype(F32)
    tc = cur.shape[1]
    rid = lax.broadcasted_iota(jnp.int32, (rows, tc), 0)
    if r0 > 0:
        prev_row = x_ref[r0 - 16:r0, :].astype(F32)[15:16, :]
    else:
        prev_row = jnp.zeros((1, tc), F32)
    if r0 + rows < seq_len:
        next_row = x_ref[r0 + rows:r0 + rows + 16, :].astype(F32)[0:1, :]
    else:
        next_row = jnp.zeros((1, tc), F32)
    up = jnp.where(rid == 0, prev_row, pltpu.roll(cur, 1, 0))
    down = jnp.where(rid == rows - 1, next_row, pltpu.roll(cur, rows - 1, 0))
    return up * w[0:1, :] + cur * w[1:2, :] + down * w[2:3, :] + b


def _ssd_prep_kernel(x_ref, w_ref, b_ref, o_ref, *, rows):
    seq_len = x_ref.shape[0]
    w = w_ref[...]
    b = b_ref[...]
    for r0 in range(0, seq_len, rows):
        v = _conv3_chunk(x_ref, r0, rows, seq_len, w, b)
        o_ref[r0:r0 + rows, :] = _silu(v).astype(o_ref.dtype)


def _ssd_prep(proj3, col0, width, w8, b1, tc):
    nb, seq_len, _ = proj3.shape
    rows = min(512, seq_len)
    j0 = col0 // tc
    return pl.pallas_call(
        functools.partial(_ssd_prep_kernel, rows=rows),
        grid=(nb, width // tc),
        in_specs=[
            pl.BlockSpec((None, seq_len, tc), lambda b, j: (b, 0, j0 + j)),
            pl.BlockSpec((8, tc), lambda b, j: (0, j)),
            pl.BlockSpec((1, tc), lambda b, j: (0, j)),
        ],
        out_specs=pl.BlockSpec((None, seq_len, tc), lambda b, j: (b, 0, j)),
        out_shape=jax.ShapeDtypeStruct((nb, seq_len, width), BF16),
        compiler_params=_params(2, 40),
        name="ssd_prep",
    )(proj3, w8, b1)


def _hy_prep_kernel(x0_ref, x1_ref, v_ref, w0_ref, w1_ref, wv_ref, b0_ref, b1_ref, bv_ref,
                    x0c_ref, u_ref, *, rows):
    seq_len = x0_ref.shape[0]
    w0, w1, wv = w0_ref[...], w1_ref[...], wv_ref[...]
    b0, b1, bv = b0_ref[...], b1_ref[...], bv_ref[...]
    for r0 in range(0, seq_len, rows):
        x0c_ref[r0:r0 + rows, :] = _conv3_chunk(x0_ref, r0, rows, seq_len, w0, b0).astype(x0c_ref.dtype)
        x1c = _conv3_chunk(x1_ref, r0, rows, seq_len, w1, b1)
        vc = _conv3_chunk(v_ref, r0, rows, seq_len, wv, bv)
        u_ref[r0:r0 + rows, :] = (vc * x1c).astype(u_ref.dtype)


def _hy_prep(proj3, col0, dh, w8, b1, tc):
    nb, seq_len, _ = proj3.shape
    rows = min(512, seq_len)
    j0 = col0 // tc
    nj = dh // tc
    x_spec = lambda k: pl.BlockSpec((None, seq_len, tc), lambda b, j: (b, 0, j0 + k * nj + j))
    w_spec = lambda k: pl.BlockSpec((8, tc), lambda b, j: (0, k * nj + j))
    b_spec = lambda k: pl.BlockSpec((1, tc), lambda b, j: (0, k * nj + j))
    o_spec = pl.BlockSpec((None, seq_len, tc), lambda b, j: (b, 0, j))
    return pl.pallas_call(
        functools.partial(_hy_prep_kernel, rows=rows),
        grid=(nb, nj),
        in_specs=[x_spec(0), x_spec(1), x_spec(2), w_spec(0), w_spec(1), w_spec(2),
                  b_spec(0), b_spec(1), b_spec(2)],
        out_specs=[o_spec, o_spec],
        out_shape=[jax.ShapeDtypeStruct((nb, seq_len, dh), BF16)] * 2,
        compiler_params=_params(2, 48),
        name="hy_prep",
    )(proj3, proj3, proj3, w8, w8, w8, b1, b1, b1)


def _ssd_kernel(u_ref, dt_ref, uc_ref, dtc_ref, z_ref, alog_ref, dtb_ref, dsk_ref, nw_ref, e_ref,
                o_ref, h_scr, yf_scr, *, n_ctx, n_lat):
    q = SSD_CHUNK
    nsub = SSD_STEP_CHUNKS
    d = pl.program_id(0)
    s = pl.program_id(1)
    is_ctx = s < n_ctx
    t = jnp.maximum(s - n_ctx, 0)
    cidx = jnp.where(d == 0, t, n_lat - 1 - t)
    fwd = d == 0

    @pl.when(s == 0)
    def _():
        h_scr[...] = jnp.zeros_like(h_scr)

    row = lax.broadcasted_iota(jnp.int32, (q, q), 0)
    col = lax.broadcasted_iota(jnp.int32, (q, q), 1)
    tri = (jnp.where(fwd, row, col) >= jnp.where(fwd, col, row)).astype(F32)
    nb = u_ref.shape[0]
    ys = []
    for b in range(nb):
        u_blk = jnp.where(is_ctx, uc_ref[b], u_ref[b])
        dt_blk = jnp.where(is_ctx, dtc_ref[b], dt_ref[b])
        sub = lambda v, k: v[k * q:(k + 1) * q]
        y_sub = [None] * nsub
        for k in range(nsub):
            rk = nsub - 1 - k
            u = jnp.where(fwd, sub(u_blk, k), sub(u_blk, rk))
            dtr = jnp.where(fwd, sub(dt_blk, k), sub(dt_blk, rk))
            y_sub[k] = _ssd_chunk(b, u, dtr, tri, alog_ref, dtb_ref, e_ref, h_scr)
        ys.append(jnp.concatenate([jnp.where(fwd, y_sub[k], y_sub[nsub - 1 - k]) for k in range(nsub)], axis=0))

    @pl.when(jnp.logical_and(jnp.logical_not(is_ctx), fwd))
    def _():
        for b in range(nb):
            yf_scr[b, cidx] = ys[b]

    @pl.when(jnp.logical_and(jnp.logical_not(is_ctx), jnp.logical_not(fwd)))
    def _():
        d_ssd = o_ref.shape[2]
        for b in range(nb):
            xs = u_ref[b][:, :d_ssd].astype(F32)
            ytot = yf_scr[b, cidx] + ys[b] + dsk_ref[...] * xs
            gated = ytot * _silu(z_ref[b].astype(F32))
            o_ref[b] = (_rms(gated) * nw_ref[...]).astype(o_ref.dtype)


def _ssd_chunk(b, u, dtr, tri, alog_ref, dtb_ref, e_ref, h_scr):
    q = SSD_CHUNK
    d_ssd = h_scr.shape[2]
    gw = d_ssd // SSD_GROUPS

    dt = _softplus(dtr + dtb_ref[...])
    a = dt * (-jnp.exp(alog_ref[...]))
    cum = jnp.dot(tri, a, precision=HIGHEST, preferred_element_type=F32)
    mask_add = ((1.0 - tri) * _NEG_BIG).astype(BF16)
    cum_t = cum.T
    tot = jnp.sum(a, axis=0, keepdims=True)

    stack = jnp.concatenate(
        [dt, jnp.exp(cum), jnp.exp(tot - cum), jnp.broadcast_to(jnp.exp(tot), (8, LANES))], axis=0)
    ex = jnp.dot(stack.astype(BF16), e_ref[...], preferred_element_type=F32)
    dt_x = ex[0:q]
    ecum_x = ex[q:2 * q]
    edec_x = ex[2 * q:3 * q]
    etot_x = ex[3 * q:3 * q + 1]

    xs = u[:, :d_ssd].astype(F32)
    xdt = xs * dt_x
    xdt_b = xdt.astype(BF16)
    xdw_b = (xdt * edec_x).astype(BF16)
    lane_lo = lax.broadcasted_iota(jnp.int32, (q, LANES), 1) < SSD_HEAD_DIM
    heads_per_group = gw // SSD_HEAD_DIM

    y_parts = []
    for g in range(SSD_GROUPS):
        bg = u[:, d_ssd + g * SSD_STATE:d_ssd + (g + 1) * SSD_STATE]
        cg = u[:, d_ssd + (SSD_GROUPS + g) * SSD_STATE:d_ssd + (SSD_GROUPS + g + 1) * SSD_STATE]
        scores = lax.dot_general(cg, bg, (((1,), (1,)), ((), ())), preferred_element_type=F32).astype(BF16)
        h_prev = h_scr[b, :, g * gw:(g + 1) * gw]
        y_off = jnp.dot(cg, h_prev.astype(BF16), preferred_element_type=F32)
        for j in range(heads_per_group // 2):
            c0 = g * gw + j * LANES
            xp = xdt_b[:, c0:c0 + LANES]
            acc = None
            for hh in range(2):
                h = g * heads_per_group + 2 * j + hh
                diff = (cum[:, h:h + 1] - cum_t[h:h + 1, :]).astype(BF16)
                m_h = scores * jnp.exp(diff + mask_add)
                x_h = jnp.where(lane_lo if hh == 0 else jnp.logical_not(lane_lo), xp, jnp.zeros_like(xp))
                part = jnp.dot(m_h, x_h, preferred_element_type=F32)
                acc = part if acc is None else acc + part
            y_parts.append(acc + y_off[:, j * LANES:(j + 1) * LANES] * ecum_x[:, c0:c0 + LANES])
        upd = lax.dot_general(bg, xdw_b[:, g * gw:(g + 1) * gw], (((0,), (0,)), ((), ())),
                              preferred_element_type=F32)
        h_scr[b, :, g * gw:(g + 1) * gw] = etot_x[:, g * gw:(g + 1) * gw] * h_prev + upd
    return jnp.concatenate(y_parts, axis=1)


def _ssd(u_lat, dt_lat, u_ctx, dt_ctx, proj3, alog, dtb, dsk, nw, expand):
    nb, seq_len, d_xbc = u_lat.shape
    ctx_len = u_ctx.shape[1]
    q = SSD_CHUNK * SSD_STEP_CHUNKS
    assert seq_len % q == 0 and ctx_len % q == 0
    n_lat = seq_len // q
    n_ctx = ctx_len // q
    d_ssd = dsk.shape[1]

    def lat_idx(d, s):
        t = jnp.maximum(s - n_ctx, 0)
        return jnp.where(d == 0, t, n_lat - 1 - t)

    def ctx_idx(d, s):
        t = jnp.minimum(s, n_ctx - 1)
        return jnp.where(d == 0, t, n_ctx - 1 - t)

    def out_idx(d, s):
        return jnp.where(d == 0, n_lat - 1, lat_idx(d, s))

    return pl.pallas_call(
        functools.partial(_ssd_kernel, n_ctx=n_ctx, n_lat=n_lat),
        grid=(2, n_ctx + n_lat),
        in_specs=[
            pl.BlockSpec((nb, q, d_xbc), lambda d, s: (0, lat_idx(d, s), 0)),
            pl.BlockSpec((nb, q, LANES), lambda d, s: (0, lat_idx(d, s), d)),
            pl.BlockSpec((nb, q, d_xbc), lambda d, s: (0, ctx_idx(d, s), 0)),
            pl.BlockSpec((nb, q, LANES), lambda d, s: (0, ctx_idx(d, s), d)),
            pl.BlockSpec((nb, q, d_ssd), lambda d, s: (0, lat_idx(d, s), 0)),
            pl.BlockSpec((None, 1, LANES), lambda d, s: (d, 0, 0)),
            pl.BlockSpec((None, 1, LANES), lambda d, s: (d, 0, 0)),
            pl.BlockSpec((1, d_ssd), lambda d, s: (0, 0)),
            pl.BlockSpec((1, d_ssd), lambda d, s: (0, 0)),
            pl.BlockSpec((LANES, d_ssd), lambda d, s: (0, 0)),
        ],
        out_specs=pl.BlockSpec((nb, q, d_ssd), lambda d, s: (0, out_idx(d, s), 0)),
        out_shape=jax.ShapeDtypeStruct((nb, seq_len, d_ssd), BF16),
        scratch_shapes=[
            pltpu.VMEM((nb, SSD_STATE, d_ssd), F32),
            pltpu.VMEM((nb, n_lat, q, d_ssd), F32),
        ],
        compiler_params=_params(2, 56),
        name="ssd",
    )(u_lat, dt_lat, u_ctx, dt_ctx, proj3, alog, dtb, dsk, nw, expand)


def _hy_mlp_kernel(zt_ref, w1_ref, w2_ref, w3_ref, b_ref, fr_ref, o_ref):
    fr = fr_ref[...]
    b = b_ref[...]
    h = jnp.sin(fr * (jnp.dot(w1_ref[...], zt_ref[...], precision=HIGHEST, preferred_element_type=F32)
                      + b[:, 0:1]))
    h = jnp.sin(fr * (jnp.dot(w2_ref[...], h, precision=HIGHEST, preferred_element_type=F32) + b[:, 1:2]))
    h = jnp.sin(fr * (jnp.dot(w3_ref[...], h, precision=HIGHEST, preferred_element_type=F32) + b[:, 2:3]))
    hid, seq_len = h.shape
    hp = jnp.concatenate([h, jnp.zeros((LANES - hid, seq_len), F32)], axis=0)
    o_ref[...] = hp.T


def _hy_mlp(zt, w1t, w2t, w3t, b3, fr):
    seq_len = zt.shape[1]
    return pl.pallas_call(
        _hy_mlp_kernel,
        out_shape=jax.ShapeDtypeStruct((seq_len, LANES), F32),
        compiler_params=pltpu.CompilerParams(vmem_limit_bytes=40 * MIB),
        name="hy_mlp",
    )(zt, w1t, w2t, w3t, b3, fr)


def _fft_tables(seq_len):
    n_fft = 2 * seq_len
    n2 = FFT_N2
    n1 = n_fft // n2
    n1h = n1 // 2
    k1n = n1h + 1
    k1p = -(-k1n // 4) * 4
    k1 = np.arange(k1n, dtype=np.int64)
    th = (2.0 * np.pi / n1) * ((k1[:, None] * np.arange(n1h, dtype=np.int64)[None, :]) % n1)
    f1 = np.zeros((2 * k1p, n1h))
    f1[0:2 * k1n:2] = np.cos(th)
    f1[1:2 * k1n:2] = -np.sin(th)
    idx = np.arange(n2, dtype=np.int64)
    kk = k1[:, None, None] + n1 * idx[None, :, None]
    ph = (2.0 * np.pi / n_fft) * ((kk * idx[None, None, :]) % n_fft)
    g_re, g_im = np.cos(ph), -np.sin(ph)

    def blocks(re, im):
        out = np.zeros((k1p, 2 * n2, 2 * n2))
        out[:k1n, :n2, :n2] = re
        out[:k1n, :n2, n2:] = -im
        out[:k1n, n2:, :n2] = im
        out[:k1n, n2:, n2:] = re
        return out

    wgt = np.where((k1 == 0) | (k1 == n1h), 1.0, 2.0) / n_fft
    gf = blocks(g_re, g_im)
    gi = blocks(np.transpose(g_re, (0, 2, 1)) * wgt[:, None, None],
                -np.transpose(g_im, (0, 2, 1)) * wgt[:, None, None])
    as_bf16 = lambda t: jnp.asarray(t.astype(np.float32)).astype(BF16)
    return as_bf16(f1), as_bf16(np.ascontiguousarray(f1.T)), as_bf16(gf), as_bf16(gi)


_BATCHED = (((2,), (1,)), ((0,), (0,)))


def _fft_forward(x, f1, gf):
    seq_len, c = x.shape
    rows, n1h = f1.shape
    xt = jnp.swapaxes(x.reshape(n1h, FFT_N2, c), 0, 1).astype(BF16)
    a = lax.dot_general(jnp.broadcast_to(f1[None], (FFT_N2, rows, n1h)), xt, _BATCHED,
                        preferred_element_type=F32)
    at = jnp.swapaxes(a, 0, 1).reshape(rows // 2, 2 * FFT_N2, c).astype(BF16)
    return lax.dot_general(gf, at, _BATCHED, preferred_element_type=F32)


def _fft_inverse(y, f1t, gi):
    c = y.shape[2]
    n1h, rows = f1t.shape
    bt = lax.dot_general(gi, y, _BATCHED, preferred_element_type=F32)
    btt = jnp.swapaxes(bt.reshape(rows, FFT_N2, c), 0, 1).astype(BF16)
    yv = lax.dot_general(jnp.broadcast_to(f1t[None], (FFT_N2, n1h, rows)), btt, _BATCHED,
                         preferred_element_type=F32)
    return jnp.swapaxes(yv, 0, 1).reshape(n1h * FFT_N2, c)


def _hy_spec_kernel(h_ref, wf_ref, wb_ref, dl_ref, f1_ref, gf_ref, k_ref):
    n2 = FFT_N2
    seq_len = h_ref.shape[0]
    tc = wf_ref.shape[1]
    h3 = h_ref[...].astype(BF16)
    rid = lax.broadcasted_iota(jnp.int32, (seq_len, tc), 0)
    decay = jnp.exp(-(rid.astype(F32) * (1.0 / (seq_len - 1))) * dl_ref[...])
    hf = jnp.dot(h3, wf_ref[...].astype(BF16), preferred_element_type=F32) * decay
    hb = jnp.dot(h3, wb_ref[...].astype(BF16), preferred_element_type=F32) * decay
    norm = jnp.sum(jnp.abs(hf) + jnp.abs(hb), axis=0, keepdims=True) + 1e-6
    inv = 1.0 / norm
    hb = jnp.where(rid == 0, 0.0, hb * inv)
    x = _fft_forward(jnp.concatenate([hf * inv, hb], axis=1), f1_ref[...], gf_ref[...])
    xf, xb = x[:, :, 0:tc], x[:, :, tc:2 * tc]
    k_ref[...] = jnp.concatenate([xf[:, :n2] + xb[:, :n2], xf[:, n2:] - xb[:, n2:]], axis=1)


def _hy_spec(h3, w4f, w4b, deltas, f1, gf, tc):
    seq_len = h3.shape[0]
    dh = w4f.shape[1]
    k1p = gf.shape[0]
    rows, n1h = f1.shape
    return pl.pallas_call(
        _hy_spec_kernel,
        grid=(dh // tc,),
        in_specs=[
            pl.BlockSpec((seq_len, LANES), lambda j: (0, 0)),
            pl.BlockSpec((LANES, tc), lambda j: (0, j)),
            pl.BlockSpec((LANES, tc), lambda j: (0, j)),
            pl.BlockSpec((1, tc), lambda j: (0, j)),
            pl.BlockSpec((rows, n1h), lambda j: (0, 0)),
            pl.BlockSpec((k1p, 2 * FFT_N2, 2 * FFT_N2), lambda j: (0, 0, 0)),
        ],
        out_specs=pl.BlockSpec((k1p, 2 * FFT_N2, tc), lambda j: (0, 0, j)),
        out_shape=jax.ShapeDtypeStruct((k1p, 2 * FFT_N2, dh), F32),
        compiler_params=_params(1, 56),
        name="hy_spec",
    )(h3, w4f, w4b, deltas, f1, gf)


def _hy_conv_kernel(u_ref, x0_ref, k_ref, f1_ref, f1t_ref, gf_ref, gi_ref, bias_ref, o_ref):
    n2 = FFT_N2
    u = u_ref[...].astype(F32)
    x = _fft_forward(u, f1_ref[...], gf_ref[...])
    kk = k_ref[...]
    xr, xi, kr, ki = x[:, :n2], x[:, n2:], kk[:, :n2], kk[:, n2:]
    y = jnp.concatenate([xr * kr - xi * ki, xr * ki + xi * kr], axis=1).astype(BF16)
    conv = _fft_inverse(y, f1t_ref[...], gi_ref[...])
    o_ref[...] = (x0_ref[...].astype(F32) * (conv + u * bias_ref[...])).astype(o_ref.dtype)


def _hy_conv(u3, x0c, kspec, f1, f1t, gf, gi, bias, tc):
    nb, seq_len, dh = u3.shape
    k1p = gf.shape[0]
    rows, n1h = f1.shape
    blk = pl.BlockSpec((None, seq_len, tc), lambda j, b: (b, 0, j))
    const = lambda shape: pl.BlockSpec(shape, lambda j, b: (0,) * len(shape), pipeline_mode=ONE_BUFFER)
    return pl.pallas_call(
        _hy_conv_kernel,
        grid=(dh // tc, nb),
        in_specs=[
            blk, blk,
            pl.BlockSpec((k1p, 2 * FFT_N2, tc), lambda j, b: (0, 0, j)),
            const((rows, n1h)), const((n1h, rows)),
            const((k1p, 2 * FFT_N2, 2 * FFT_N2)), const((k1p, 2 * FFT_N2, 2 * FFT_N2)),
            pl.BlockSpec((1, tc), lambda j, b: (0, j)),
        ],
        out_specs=blk,
        out_shape=jax.ShapeDtypeStruct((nb, seq_len, dh), BF16),
        compiler_params=_params(2, 56),
        name="hy_conv",
    )(u3, x0c, kspec, f1, f1t, gf, gi, bias)


def _outproj_kernel(ys_ref, yh_ref, w_ref, x_ref, posr_ref, cemb_ref, nwp_ref, g_ref, nwf_ref, sh_ref, sc_ref,
                    xo_ref, h_ref):
    tm, ds = ys_ref.shape
    rows = min(256, tm)
    for r0 in range(0, tm, rows):
        rs = slice(r0, r0 + rows)
        y = jnp.dot(ys_ref[rs, :], w_ref[0:ds, :], preferred_element_type=F32)
        y = y + jnp.dot(yh_ref[rs, :], w_ref[ds:, :], preferred_element_type=F32)
        xn = _add_pos(x_ref[rs, :], posr_ref, r0 // GRID_W, cemb_ref[...]) + g_ref[...] * (_rms(y) * nwp_ref[...])
        xo_ref[rs, :] = xn
        h_ref[rs, :] = (_rms(xn) * nwf_ref[...] * (1.0 + sc_ref[...]) + sh_ref[...]).astype(h_ref.dtype)


def _out_proj(ys, yh, w, x2, posr, cemb, nwp, g1, nwf, sh2, sc2, seq_len, tm):
    m, d = x2.shape
    ds = ys.shape[1]
    dh = yh.shape[1]
    tiles_per_seq = seq_len // tm
    row = lambda i: (i, 0)
    fixed = lambda i: (0, 0)
    per_batch = lambda i: (i // tiles_per_seq, 0, 0)
    return pl.pallas_call(
        _outproj_kernel,
        grid=(m // tm,),
        in_specs=[
            pl.BlockSpec((tm, ds), row),
            pl.BlockSpec((tm, dh), row),
            pl.BlockSpec((ds + dh, d), fixed, pipeline_mode=ONE_BUFFER),
            pl.BlockSpec((tm, d), row),
            pl.BlockSpec((tm // GRID_W, d // 2), lambda i: (i % tiles_per_seq, 0)),
            pl.BlockSpec(cemb.shape, fixed),
            pl.BlockSpec((1, d), fixed),
            pl.BlockSpec((None, 1, d), per_batch),
            pl.BlockSpec((1, d), fixed),
            pl.BlockSpec((None, 1, d), per_batch),
            pl.BlockSpec((None, 1, d), per_batch),
        ],
        out_specs=[pl.BlockSpec((tm, d), row), pl.BlockSpec((tm, d), row)],
        out_shape=[jax.ShapeDtypeStruct((m, d), F32), jax.ShapeDtypeStruct((m, d), BF16)],
        compiler_params=_params(1, 56),
        name="out_proj",
    )(ys, yh, w, x2, posr, cemb, nwp, g1, nwf, sh2, sc2)


def _ffn_step(h_ref, wg, wu, wd, x_ref, nw_ref, g_ref, o_ref):
    f = pl.program_id(1)
    h = h_ref[...]
    gate = jnp.dot(h, wg, preferred_element_type=F32)
    up = jnp.dot(h, wu, preferred_element_type=F32)
    act = (_silu(gate) * up).astype(BF16)

    @pl.when(f == 0)
    def _():
        o_ref[...] = jnp.zeros_like(o_ref)

    d = o_ref.shape[1]
    cw = min(512, d)
    for n0 in range(0, d, cw):
        o_ref[:, n0:n0 + cw] += jnp.dot(act, wd[:, n0:n0 + cw], preferred_element_type=F32)

    @pl.when(f == pl.num_programs(1) - 1)
    def _():
        tm = o_ref.shape[0]
        rows = min(256, tm)
        for r0 in range(0, tm, rows):
            y = o_ref[r0:r0 + rows, :]
            o_ref[r0:r0 + rows, :] = x_ref[r0:r0 + rows, :] + g_ref[...] * (_rms(y) * nw_ref[...])


def _ffn_first_kernel(h_ref, wg_ref, wu_ref, wd_ref, x_ref, nw_ref, g_ref, o_ref, wgb_ref, wub_ref, wdb_ref):
    wg = wg_ref[...].astype(BF16)
    wu = wu_ref[...].astype(BF16)
    wd = wd_ref[...].astype(BF16)
    wgb_ref[...] = wg
    wub_ref[...] = wu
    wdb_ref[...] = wd
    _ffn_step(h_ref, wg, wu, wd, x_ref, nw_ref, g_ref, o_ref)


def _ffn_rest_kernel(h_ref, wg_ref, wu_ref, wd_ref, x_ref, nw_ref, g_ref, o_ref):
    _ffn_step(h_ref, wg_ref[...], wu_ref[...], wd_ref[...], x_ref, nw_ref, g_ref, o_ref)


def _ffn(h2, wg, wu, wd, xn, nw, g2, seq_len, tm, tf_first, tf):
    m, d = xn.shape
    dff = wg.shape[1]
    tiles_per_seq = seq_len // tm
    n_rows = m // tm
    out, wgb, wub, wdb = pl.pallas_call(
        _ffn_first_kernel,
        grid=(1, dff // tf_first),
        in_specs=[
            pl.BlockSpec((tm, d), lambda i, f: (0, 0), pipeline_mode=ONE_BUFFER),
            pl.BlockSpec((d, tf_first), lambda i, f: (0, f)),
            pl.BlockSpec((d, tf_first), lambda i, f: (0, f)),
            pl.BlockSpec((tf_first, d), lambda i, f: (f, 0)),
            pl.BlockSpec((tm, d), lambda i, f: (0, 0), pipeline_mode=ONE_BUFFER),
            pl.BlockSpec((1, d), lambda i, f: (0, 0)),
            pl.BlockSpec((None, 1, d), lambda i, f: (0, 0, 0)),
        ],
        out_specs=[
            pl.BlockSpec((tm, d), lambda i, f: (0, 0)),
            pl.BlockSpec((d, tf_first), lambda i, f: (0, f)),
            pl.BlockSpec((d, tf_first), lambda i, f: (0, f)),
            pl.BlockSpec((tf_first, d), lambda i, f: (f, 0)),
        ],
        out_shape=[
            jax.ShapeDtypeStruct((m, d), F32),
            jax.ShapeDtypeStruct((d, dff), BF16),
            jax.ShapeDtypeStruct((d, dff), BF16),
            jax.ShapeDtypeStruct((dff, d), BF16),
        ],
        input_output_aliases={4: 0},
        compiler_params=_params(2, 58),
        name="ffn_first",
    )(h2, wg, wu, wd, xn, nw, g2)
    if n_rows == 1:
        return out
    return pl.pallas_call(
        _ffn_rest_kernel,
        grid=(n_rows - 1, dff // tf),
        in_specs=[
            pl.BlockSpec((tm, d), lambda i, f: (i + 1, 0)),
            pl.BlockSpec((d, tf), lambda i, f: (0, f)),
            pl.BlockSpec((d, tf), lambda i, f: (0, f)),
            pl.BlockSpec((tf, d), lambda i, f: (f, 0)),
            pl.BlockSpec((tm, d), lambda i, f: (i + 1, 0), pipeline_mode=ONE_BUFFER),
            pl.BlockSpec((1, d), lambda i, f: (0, 0)),
            pl.BlockSpec((None, 1, d), lambda i, f: ((i + 1) // tiles_per_seq, 0, 0)),
        ],
        out_specs=pl.BlockSpec((tm, d), lambda i, f: (i + 1, 0)),
        out_shape=jax.ShapeDtypeStruct((m, d), F32),
        input_output_aliases={4: 0},
        compiler_params=_params(2, 58),
        name="ffn",
    )(h2, wgb, wub, wdb, out, nw, g2)


def _sincos_tables(rows, cols, dim):
    qd = dim // 4
    omega = 1.0 / (POS_THETA ** (jnp.arange(qd, dtype=F32) / qd))
    r = jnp.arange(rows, dtype=F32)[:, None] * omega
    cc = jnp.arange(cols, dtype=F32)[:, None] * omega
    r_emb = jnp.concatenate([jnp.sin(r), jnp.cos(r)], -1)
    c_emb = jnp.concatenate([jnp.sin(cc), jnp.cos(cc)], -1)
    return r_emb, c_emb


def _filter_features_t(seq_len, n_bands):
    t = jnp.linspace(0.0, 1.0, seq_len, dtype=F32)[:, None]
    w = 2.0 * math.pi * jnp.arange(seq_len, dtype=F32)[:, None] / seq_len
    fb = jnp.linspace(1e-4, n_bands - 1, n_bands, dtype=F32)[None]
    zpos = jnp.concatenate([t, jnp.cos(fb * w), -jnp.sin(fb * w)], -1)
    emb = zpos.shape[1]
    return jnp.pad(zpos, ((0, 0), (0, LANES - emb))).T


def _pad_rows(a, rows):
    return jnp.pad(a, ((0, rows - a.shape[0]), (0, 0)))


def kernel(x, c, ctx, c_ctx, w_ada, b_ada, norm_mix_pre, norm_mix_post, norm_ffn_pre, norm_ffn_post,
           w_in, ssd_conv_w, ssd_conv_b, ssd_a_log, ssd_dt_bias, ssd_d, ssd_norm,
           hy_conv_w, hy_conv_b, hy_w1, hy_b1, hy_w2, hy_b2, hy_w3, hy_b3, hy_w4, hy_freq, hy_bias,
           w_out, w_gate, w_up, w_down):
    nb, seq_len, d = x.shape
    ctx_len = ctx.shape[1]
    assert w_ada.shape[0] == 1, "single layer"
    n_heads = ssd_d.shape[1]
    d_ssd = n_heads * SSD_HEAD_DIM
    d_xbc = d_ssd + 2 * SSD_GROUPS * SSD_STATE
    dh = hy_bias.shape[1]
    assert w_in.shape[2] == d_ssd + d_xbc + 2 * n_heads + 3 * dh
    assert n_heads <= LANES and nb + 1 <= 8
    assert seq_len % (GRID_W * 8) == 0 and seq_len % FFT_N2 == 0
    m = nb * seq_len

    crows = jnp.concatenate([c, c_ctx[None, :]], axis=0)
    mod = _ada(crows, w_ada[0], b_ada[0])
    part = lambda r0, r1, k: mod[r0:r1, k * d:(k + 1) * d][:, None, :]
    sh1, sc1, g1, sh2, sc2, g2 = (part(0, nb, k) for k in range(6))
    csh1 = jnp.broadcast_to(part(nb, nb + 1, 0), (nb, 1, d))
    csc1 = jnp.broadcast_to(part(nb, nb + 1, 1), (nb, 1, d))

    w_t = jnp.transpose(w_in[0])
    o_xbc = d_ssd
    o_dt = o_xbc + d_xbc
    o_hy = o_dt + 2 * n_heads
    tn_in = 512
    assert o_dt % tn_in == 0 and (3 * dh) % tn_in == 0 and o_xbc % tn_in == 0 and o_hy % 16 == 0
    n_left = o_dt // tn_in
    main_off = lambda j: jnp.where(j < n_left, j * tn_in, o_hy + (j - n_left) * tn_in)
    ctx_off = lambda j: o_xbc + j * tn_in
    pad_dt = lambda rows: jnp.pad(rows, ((0, LANES - n_heads), (0, 0)))
    w_dt_t = jnp.concatenate([pad_dt(w_t[o_dt:o_dt + n_heads]), pad_dt(w_t[o_dt + n_heads:o_hy])],
                             axis=0)

    r_emb, c_emb = _sincos_tables(seq_len // GRID_W, GRID_W, d)
    posr = r_emb
    nmp = norm_mix_pre[0][None, :]

    tm_in = min(1024, seq_len)
    n_main = o_dt + 3 * dh
    tn_rest = n_main // 4 if n_main % (4 * LANES) == 0 else tn_in
    proj, dt_lat = _in_proj(x.reshape(m, d), posr, c_emb, nmp, sh1, sc1, w_t, main_off, n_main // tn_in,
                            w_dt_t, seq_len, tm_in, tn_in, True, tn_rest)
    tm_ctx = min(256, ctx_len)
    xbc_ctx, dt_ctx = _in_proj(ctx.reshape(nb * ctx_len, d), jnp.zeros((8, d // 2), F32), c_emb, nmp, csh1, csc1,
                               w_t, ctx_off, d_xbc // tn_in, w_dt_t, ctx_len, tm_ctx, tn_in, False)
    proj3 = proj.reshape(nb, seq_len, -1)

    cw8 = _pad_rows(ssd_conv_w[0], 8)
    cb1 = ssd_conv_b[0][None, :]
    u_lat = _ssd_prep(proj3, d_ssd, d_xbc, cw8, cb1, 512)
    u_ctx = _ssd_prep(xbc_ctx.reshape(nb, ctx_len, d_xbc), 0, d_xbc, cw8, cb1, 512)
    pad_heads = lambda a: jnp.pad(a, ((0, 0), (0, LANES - n_heads)))[:, None, :]
    expand = (jnp.arange(LANES)[:, None] == (jnp.arange(d_ssd)[None, :] // SSD_HEAD_DIM)).astype(BF16)
    y_ssd = _ssd(u_lat, dt_lat.reshape(nb, seq_len, 2 * LANES), u_ctx, dt_ctx.reshape(nb, ctx_len, 2 * LANES),
                 proj3, pad_heads(ssd_a_log[0]), pad_heads(ssd_dt_bias[0]),
                 jnp.repeat(ssd_d[0], SSD_HEAD_DIM)[None, :], ssd_norm[0][None, :], expand)

    x0c, u_hy = _hy_prep(proj3, d_ssd + d_xbc, dh, _pad_rows(hy_conv_w[0], 8), hy_conv_b[0][None, :], 256)
    n_bands = (hy_w1.shape[1] - 1) // 2
    zt = _filter_features_t(seq_len, n_bands)
    w1t = jnp.pad(hy_w1[0].T, ((0, 0), (0, LANES - hy_w1.shape[1])))
    b3 = jnp.stack([hy_b1[0], hy_b2[0], hy_b3[0]], axis=1)
    h3 = _hy_mlp(zt, w1t, hy_w2[0].T, hy_w3[0].T, b3, hy_freq[0][:, None])
    w4 = _pad_rows(hy_w4[0], LANES)
    max_decay = math.log(HY_TARGET) / HY_FAST_PCT
    min_decay = math.log(HY_TARGET) / HY_SLOW_PCT
    deltas = jnp.abs(jnp.linspace(min_decay, max_decay, dh, dtype=F32))[None, :]
    f1, f1t, gf, gi = _fft_tables(seq_len)
    kspec = _hy_spec(h3, w4[:, :dh], w4[:, dh:], deltas, f1, gf, LANES)
    y_hy = _hy_conv(u_hy, x0c, kspec, f1, f1t, gf, gi, hy_bias[0][None, :], LANES)

    xn, h2 = _out_proj(y_ssd.reshape(m, d_ssd), y_hy.reshape(m, dh), w_out[0].astype(BF16), x.reshape(m, d),
                       posr, c_emb, norm_mix_post[0][None, :], g1, norm_ffn_pre[0][None, :], sh2, sc2, seq_len,
                       min(512, seq_len))
    out = _ffn(h2, w_gate[0], w_up[0], w_down[0], xn, norm_ffn_post[0][None, :], g2, seq_len,
               min(1024, seq_len), 256, 512)
    return out.reshape(nb, seq_len, d)
```

```python
import functools
import math

import numpy as np
import jax
import jax.numpy as jnp
from jax import lax
from jax.experimental import pallas as pl
from jax.experimental.pallas import tpu as pltpu

F32 = jnp.float32
BF16 = jnp.bfloat16
HIGHEST = lax.Precision.HIGHEST

RMS_EPS = 1e-6
POS_THETA = 10000.0
GRID_W = 64
SSD_HEAD_DIM = 64
SSD_GROUPS = 2
SSD_STATE = 128
SSD_CHUNK = 128
SSD_STEP_CHUNKS = 2
HY_TARGET = 1e-2
HY_FAST_PCT = 0.3
HY_SLOW_PCT = 1.5
FFT_N2 = 64
LANES = 128
MIB = 1024 * 1024
ONE_BUFFER = pl.Buffered(1)
_NT_DIMS = (((1,), (1,)), ((), ()))
_NEG_BIG = -1e30


def _params(n_axes, vmem_mib):
    return pltpu.CompilerParams(
        dimension_semantics=("arbitrary",) * n_axes,
        vmem_limit_bytes=vmem_mib * MIB,
    )


def _silu(v):
    return v * (1.0 / (1.0 + jnp.exp(-v)))


def _softplus(v):
    return jnp.maximum(v, 0.0) + jnp.log(1.0 + jnp.exp(-jnp.abs(v)))


def _rms(v):
    return v * lax.rsqrt(jnp.mean(v * v, axis=-1, keepdims=True) + RMS_EPS)


def _add_pos(x, remb_ref, g0, cemb):
    rows, d = x.shape
    half = d // 2
    lo = jnp.concatenate(
        [x[g * GRID_W:(g + 1) * GRID_W, :half] + remb_ref[g0 + g:g0 + g + 1, :] for g in range(rows // GRID_W)],
        axis=0)
    hi = (x[:, half:].reshape(rows // GRID_W, GRID_W, half) + cemb[None]).reshape(rows, half)
    return jnp.concatenate([lo, hi], axis=1)


def _ada_kernel(c_ref, w_ref, b_ref, o_ref, s_scr):
    nr, d, _ = c_ref.shape
    tn = w_ref.shape[1]
    nt = tn // LANES
    kc = 64

    @pl.when(pl.program_id(0) == 0)
    def _():
        for r in range(nr):
            for k0 in range(0, d, 512):
                s_scr[r, k0:k0 + 512, :] = _silu(c_ref[r, k0:k0 + 512, :])

    acc = [[jnp.zeros((8, LANES), F32) for _ in range(nt)] for _ in range(nr)]
    for k0 in range(0, d, kc):
        s = [s_scr[r, k0:k0 + kc, :] for r in range(nr)]
        for t in range(nt):
            w = w_ref[k0:k0 + kc, t * LANES:(t + 1) * LANES]
            for r in range(nr):
                acc[r][t] = acc[r][t] + jnp.sum((w * s[r]).reshape(kc // 8, 8, LANES), axis=0)
    rows = [jnp.concatenate([jnp.sum(a, axis=0, keepdims=True) for a in acc[r]], axis=1) for r in range(nr)]
    rows.append(jnp.zeros((o_ref.shape[0] - nr, tn), F32))
    o_ref[...] = jnp.concatenate(rows, axis=0) + b_ref[...]


def _ada(crows, w_ada, b_ada):
    nr = crows.shape[0]
    d, n = w_ada.shape
    tn = 1024
    c_lanes = jnp.broadcast_to(crows[:, :, None], (nr, d, LANES))
    return pl.pallas_call(
        _ada_kernel,
        grid=(n // tn,),
        in_specs=[
            pl.BlockSpec((nr, d, LANES), lambda j: (0, 0, 0), pipeline_mode=ONE_BUFFER),
            pl.BlockSpec((d, tn), lambda j: (0, j)),
            pl.BlockSpec((1, tn), lambda j: (0, j)),
        ],
        out_specs=pl.BlockSpec((8, tn), lambda j: (0, j)),
        out_shape=jax.ShapeDtypeStruct((8, n), F32),
        scratch_shapes=[pltpu.VMEM((nr, d, LANES), F32)],
        compiler_params=_params(1, 40),
        name="ada",
    )(c_lanes, w_ada, b_ada.reshape(1, n))


def _inproj_body(x_ref, posr_ref, cemb_ref, nw_ref, sh_ref, sc_ref, get_w, wdt_ref, o_ref, dt_ref, h_scr, with_pos):
    first = pl.program_id(1) == 0

    @pl.when(first)
    def _():
        tm = x_ref.shape[0]
        rows = min(256, tm)
        wdt = wdt_ref[...].astype(BF16)
        w = get_w()
        for r0 in range(0, tm, rows):
            xf = x_ref[r0:r0 + rows, :]
            if with_pos:
                xf = _add_pos(xf, posr_ref, r0 // GRID_W, cemb_ref[...])
            h = _rms(xf) * nw_ref[...] * (1.0 + sc_ref[...]) + sh_ref[...]
            hb = h.astype(BF16)
            h_scr[r0:r0 + rows, :] = hb
            dt_ref[r0:r0 + rows, :] = lax.dot_general(hb, wdt, _NT_DIMS, preferred_element_type=F32)
            o_ref[r0:r0 + rows, :] = jnp.dot(hb, w, preferred_element_type=F32).astype(o_ref.dtype)

    @pl.when(jnp.logical_not(first))
    def _():
        o_ref[...] = jnp.dot(h_scr[...], get_w(), preferred_element_type=F32).astype(o_ref.dtype)


def _inproj_t_kernel(x_ref, posr_ref, cemb_ref, nw_ref, sh_ref, sc_ref, w_ref, wdt_ref, *rest, with_pos, emit_w):
    w = w_ref[...].T.astype(BF16)
    if emit_w:
        o_ref, dt_ref, wb_ref, h_scr = rest
        wb_ref[...] = w
    else:
        o_ref, dt_ref, h_scr = rest
    _inproj_body(x_ref, posr_ref, cemb_ref, nw_ref, sh_ref, sc_ref, lambda: w, wdt_ref, o_ref, dt_ref, h_scr,
                 with_pos)


def _inproj_rest_kernel(x_ref, posr_ref, cemb_ref, nw_ref, sh_ref, sc_ref, w_ref, wdt_ref, p0_ref, dt0_ref,
                        o_ref, dt_ref, h_scr):
    row_block = pl.program_id(0)

    @pl.when(row_block == 0)
    def _():
        o_ref[...] = p0_ref[...]

        @pl.when(pl.program_id(1) == 0)
        def _():
            dt_ref[...] = dt0_ref[...]

    @pl.when(row_block > 0)
    def _():
        _inproj_body(x_ref, posr_ref, cemb_ref, nw_ref, sh_ref, sc_ref, lambda: w_ref[...], wdt_ref, o_ref, dt_ref,
                     h_scr, True)


def _in_proj_specs(tm, d, half, pos_rows, cemb, ndt, tiles_per_seq, with_pos, blk, one_row_block):
    mode = ONE_BUFFER if one_row_block else None
    return [
        pl.BlockSpec((tm, d), lambda i, j: (blk(i), 0), pipeline_mode=mode),
        pl.BlockSpec((pos_rows, half), lambda i, j: ((blk(i) % tiles_per_seq) if with_pos else 0, 0),
                     pipeline_mode=mode),
        pl.BlockSpec(cemb.shape, lambda i, j: (0, 0), pipeline_mode=ONE_BUFFER),
        pl.BlockSpec((1, d), lambda i, j: (0, 0)),
        pl.BlockSpec((None, 1, d), lambda i, j: (blk(i) // tiles_per_seq, 0, 0)),
        pl.BlockSpec((None, 1, d), lambda i, j: (blk(i) // tiles_per_seq, 0, 0)),
    ], pl.BlockSpec((ndt, d), lambda i, j: (0, 0), pipeline_mode=ONE_BUFFER)


def _in_proj(x2, posr, cemb, nw, sh, sc, w_t, row_off, n_tiles, wdt_t, seq_len, tm, tn, with_pos, tn_rest=None):
    m, d = x2.shape
    n = n_tiles * tn
    ndt = wdt_t.shape[0]
    half = d // 2
    tiles_per_seq = seq_len // tm
    pos_rows = tm // GRID_W if with_pos else posr.shape[0]
    split = tn_rest is not None and m // tm > 1
    common, wdt_spec = _in_proj_specs(tm, d, half, pos_rows, cemb, ndt, tiles_per_seq, with_pos, lambda i: i, split)
    w_spec = pl.BlockSpec((pl.Element(tn), pl.Element(d)), lambda i, j: (pl.multiple_of(row_off(j), 8), 0))
    out_specs = [pl.BlockSpec((tm, tn), lambda i, j: (i, j)), pl.BlockSpec((tm, ndt), lambda i, j: (i, 0))]
    m_first = tm if split else m
    out_shape = [jax.ShapeDtypeStruct((m_first, n), BF16), jax.ShapeDtypeStruct((m_first, ndt), F32)]
    if split:
        out_specs.append(pl.BlockSpec((d, tn), lambda i, j: (0, j)))
        out_shape.append(jax.ShapeDtypeStruct((d, n), BF16))
    res = pl.pallas_call(
        functools.partial(_inproj_t_kernel, with_pos=with_pos, emit_w=split),
        grid=(1 if split else m // tm, n_tiles),
        in_specs=common + [w_spec, wdt_spec],
        out_specs=out_specs,
        out_shape=out_shape,
        scratch_shapes=[pltpu.VMEM((tm, d), BF16)],
        compiler_params=_params(2, 58),
        name="in_proj_t",
    )(x2, posr, cemb, nw, sh, sc, w_t, wdt_t)
    if not split:
        return res
    proj0, dt0, w_bf = res
    nj = n // tn_rest
    common, wdt_spec = _in_proj_specs(tm, d, half, pos_rows, cemb, ndt, tiles_per_seq, with_pos,
                                      lambda i: jnp.maximum(i, 1), False)
    return pl.pallas_call(
        _inproj_rest_kernel,
        grid=(m // tm, nj),
        in_specs=common + [
            pl.BlockSpec((d, tn_rest), lambda i, j: (0, j)),
            wdt_spec,
            pl.BlockSpec((tm, tn_rest), lambda i, j: (0, jnp.where(i == 0, j, nj - 1))),
            pl.BlockSpec((tm, ndt), lambda i, j: (0, 0), pipeline_mode=ONE_BUFFER),
        ],
        out_specs=[pl.BlockSpec((tm, tn_rest), lambda i, j: (i, j)),
                   pl.BlockSpec((tm, ndt), lambda i, j: (i, 0))],
        out_shape=[jax.ShapeDtypeStruct((m, n), BF16), jax.ShapeDtypeStruct((m, ndt), F32)],
        scratch_shapes=[pltpu.VMEM((tm, d), BF16)],
        compiler_params=_params(2, 58),
        name="in_proj",
    )(x2, posr, cemb, nw, sh, sc, w_bf, wdt_t, proj0, dt0)


def _conv3_chunk(x_ref, r0, rows, seq_len, w, b):
    cur = x_ref[r0:r0 + rows, :].astype(F32)
    tc = cur.shape[1]
    rid = lax.broadcasted_iota(jnp.int32, (rows, tc), 0)
    if r0 > 0:
        prev_row = x_ref[r0 - 16:r0, :].astype(F32)[15:16, :]
    else:
        prev_row = jnp.zeros((1, tc), F32)
    if r0 + rows < seq_len:
        next_row = x_ref[r0 + rows:r0 + rows + 16, :].astype(F32)[0:1, :]
    else:
        next_row = jnp.zeros((1, tc), F32)
    up = jnp.where(rid == 0, prev_row, pltpu.roll(cur, 1, 0))
    down = jnp.where(rid == rows - 1, next_row, pltpu.roll(cur, rows - 1, 0))
    return up * w[0:1, :] + cur * w[1:2, :] + down * w[2:3, :] + b


def _ssd_prep_kernel(x_ref, w_ref, b_ref, o_ref, *, rows):
    seq_len = x_ref.shape[0]
    w = w_ref[...]
    b = b_ref[...]
    for r0 in range(0, seq_len, rows):
        v = _conv3_chunk(x_ref, r0, rows, seq_len, w, b)
        o_ref[r0:r0 + rows, :] = _silu(v).astype(o_ref.dtype)


def _ssd_prep(proj3, col0, width, w8, b1, tc):
    nb, seq_len, _ = proj3.shape
    rows = min(512, seq_len)
    j0 = col0 // tc
    return pl.pallas_call(
        functools.partial(_ssd_prep_kernel, rows=rows),
        grid=(nb, width // tc),
        in_specs=[
            pl.BlockSpec((None, seq_len, tc), lambda b, j: (b, 0, j0 + j)),
            pl.BlockSpec((8, tc), lambda b, j: (0, j)),
            pl.BlockSpec((1, tc), lambda b, j: (0, j)),
        ],
        out_specs=pl.BlockSpec((None, seq_len, tc), lambda b, j: (b, 0, j)),
        out_shape=jax.ShapeDtypeStruct((nb, seq_len, width), BF16),
        compiler_params=_params(2, 40),
        name="ssd_prep",
    )(proj3, w8, b1)


def _hy_prep_kernel(x0_ref, x1_ref, v_ref, w0_ref, w1_ref, wv_ref, b0_ref, b1_ref, bv_ref,
                    x0c_ref, u_ref, *, rows):
    seq_len = x0_ref.shape[0]
    w0, w1, wv = w0_ref[...], w1_ref[...], wv_ref[...]
    b0, b1, bv = b0_ref[...], b1_ref[...], bv_ref[...]
    for r0 in range(0, seq_len, rows):
        x0c_ref[r0:r0 + rows, :] = _conv3_chunk(x0_ref, r0, rows, seq_len, w0, b0).astype(x0c_ref.dtype)
        x1c = _conv3_chunk(x1_ref, r0, rows, seq_len, w1, b1)
        vc = _conv3_chunk(v_ref, r0, rows, seq_len, wv, bv)
        u_ref[r0:r0 + rows, :] = (vc * x1c).astype(u_ref.dtype)


def _hy_prep(proj3, col0, dh, w8, b1, tc):
    nb, seq_len, _ = proj3.shape
    rows = min(512, seq_len)
    j0 = col0 // tc
    nj = dh // tc
    x_spec = lambda k: pl.BlockSpec((None, seq_len, tc), lambda b, j: (b, 0, j0 + k * nj + j))
    w_spec = lambda k: pl.BlockSpec((8, tc), lambda b, j: (0, k * nj + j))
    b_spec = lambda k: pl.BlockSpec((1, tc), lambda b, j: (0, k * nj + j))
    o_spec = pl.BlockSpec((None, seq_len, tc), lambda b, j: (b, 0, j))
    return pl.pallas_call(
        functools.partial(_hy_prep_kernel, rows=rows),
        grid=(nb, nj),
        in_specs=[x_spec(0), x_spec(1), x_spec(2), w_spec(0), w_spec(1), w_spec(2),
                  b_spec(0), b_spec(1), b_spec(2)],
        out_specs=[o_spec, o_spec],
        out_shape=[jax.ShapeDtypeStruct((nb, seq_len, dh), BF16)] * 2,
        compiler_params=_params(2, 48),
        name="hy_prep",
    )(proj3, proj3, proj3, w8, w8, w8, b1, b1, b1)


def _ssd_kernel(u_ref, dt_ref, uc_ref, dtc_ref, z_ref, alog_ref, dtb_ref, dsk_ref, nw_ref, e_ref,
                o_ref, h_scr, yf_scr, *, n_ctx, n_lat):
    q = SSD_CHUNK
    nsub = SSD_STEP_CHUNKS
    d = pl.program_id(0)
    s = pl.program_id(1)
    is_ctx = s < n_ctx
    t = jnp.maximum(s - n_ctx, 0)
    cidx = jnp.where(d == 0, t, n_lat - 1 - t)
    fwd = d == 0

    @pl.when(s == 0)
    def _():
        h_scr[...] = jnp.zeros_like(h_scr)

    row = lax.broadcasted_iota(jnp.int32, (q, q), 0)
    col = lax.broadcasted_iota(jnp.int32, (q, q), 1)
    tri = (jnp.where(fwd, row, col) >= jnp.where(fwd, col, row)).astype(F32)
    nb = u_ref.shape[0]

    def scan_block(b, xbc_ref, dtr_ref, need_y):
        sub = lambda v, k: v[k * q:(k + 1) * q]
        y_sub = [None] * nsub
        for k in range(nsub):
            rk = nsub - 1 - k
            u = jnp.where(fwd, sub(xbc_ref[b], k), sub(xbc_ref[b], rk))
            dtr = jnp.where(fwd, sub(dtr_ref[b], k), sub(dtr_ref[b], rk))
            y_sub[k] = _ssd_chunk(b, u, dtr, tri, alog_ref, dtb_ref, e_ref, h_scr, need_y)
        if need_y:
            return jnp.concatenate([jnp.where(fwd, y_sub[k], y_sub[nsub - 1 - k]) for k in range(nsub)], axis=0)

    @pl.when(is_ctx)
    def _():
        for b in range(nb):
            scan_block(b, uc_ref, dtc_ref, False)

    @pl.when(jnp.logical_and(jnp.logical_not(is_ctx), fwd))
    def _():
        for b in range(nb):
            yf_scr[b, cidx] = scan_block(b, u_ref, dt_ref, True)

    @pl.when(jnp.logical_and(jnp.logical_not(is_ctx), jnp.logical_not(fwd)))
    def _():
        d_ssd = o_ref.shape[2]
        for b in range(nb):
            y = scan_block(b, u_ref, dt_ref, True)
            xs = u_ref[b][:, :d_ssd].astype(F32)
            ytot = yf_scr[b, cidx] + y + dsk_ref[...] * xs
            gated = ytot * _silu(z_ref[b].astype(F32))
            o_ref[b] = (_rms(gated) * nw_ref[...]).astype(o_ref.dtype)


def _ssd_chunk(b, u, dtr, tri, alog_ref, dtb_ref, e_ref, h_scr, need_y):
    q = SSD_CHUNK
    d_ssd = h_scr.shape[2]
    gw = d_ssd // SSD_GROUPS

    dt = _softplus(dtr + dtb_ref[...])
    a = dt * (-jnp.exp(alog_ref[...]))
    cum = jnp.dot(tri, a, precision=HIGHEST, preferred_element_type=F32)
    mask_add = ((1.0 - tri) * _NEG_BIG).astype(BF16)
    cum_t = cum.T
    tot = jnp.sum(a, axis=0, keepdims=True)

    stack = jnp.concatenate(
        [dt, jnp.exp(cum), jnp.exp(tot - cum), jnp.broadcast_to(jnp.exp(tot), (8, LANES))], axis=0)
    ex = jnp.dot(stack.astype(BF16), e_ref[...], preferred_element_type=F32)
    dt_x = ex[0:q]
    ecum_x = ex[q:2 * q]
    edec_x = ex[2 * q:3 * q]
    etot_x = ex[3 * q:3 * q + 1]

    xs = u[:, :d_ssd].astype(F32)
    xdt = xs * dt_x
    xdt_b = xdt.astype(BF16)
    xdw_b = (xdt * edec_x).astype(BF16)
    lane_lo = lax.broadcasted_iota(jnp.int32, (q, LANES), 1) < SSD_HEAD_DIM
    heads_per_group = gw // SSD_HEAD_DIM

    y_parts = []
    for g in range(SSD_GROUPS):
        bg = u[:, d_ssd + g * SSD_STATE:d_ssd + (g + 1) * SSD_STATE]
        cg = u[:, d_ssd + (SSD_GROUPS + g) * SSD_STATE:d_ssd + (SSD_GROUPS + g + 1) * SSD_STATE]
        h_prev = h_scr[b, :, g * gw:(g + 1) * gw]
        if need_y:
            scores = lax.dot_general(cg, bg, _NT_DIMS, preferred_element_type=F32).astype(BF16)
            y_off = jnp.dot(cg, h_prev.astype(BF16), preferred_element_type=F32)
        for j in range(heads_per_group // 2 if need_y else 0):
            c0 = g * gw + j * LANES
            xp = xdt_b[:, c0:c0 + LANES]
            acc = None
            for hh in range(2):
                h = g * heads_per_group + 2 * j + hh
                diff = (cum[:, h:h + 1] - cum_t[h:h + 1, :]).astype(BF16)
                m_h = scores * jnp.exp(diff + mask_add)
                x_h = jnp.where(lane_lo if hh == 0 else jnp.logical_not(lane_lo), xp, jnp.zeros_like(xp))
                part = jnp.dot(m_h, x_h, preferred_element_type=F32)
                acc = part if acc is None else acc + part
            y_parts.append(acc + y_off[:, j * LANES:(j + 1) * LANES] * ecum_x[:, c0:c0 + LANES])
        upd = lax.dot_general(bg, xdw_b[:, g * gw:(g + 1) * gw], (((0,), (0,)), ((), ())),
                              preferred_element_type=F32)
        h_scr[b, :, g * gw:(g + 1) * gw] = etot_x[:, g * gw:(g + 1) * gw] * h_prev + upd
    return jnp.concatenate(y_parts, axis=1) if need_y else None


def _ssd(u_lat, dt_lat, u_ctx, dt_ctx, proj3, alog, dtb, dsk, nw, expand):
    nb, seq_len, d_xbc = u_lat.shape
    ctx_len = u_ctx.shape[1]
    q = SSD_CHUNK * SSD_STEP_CHUNKS
    assert seq_len % q == 0 and ctx_len % q == 0
    n_lat = seq_len // q
    n_ctx = ctx_len // q
    d_ssd = dsk.shape[1]

    def lat_idx(d, s):
        t = jnp.maximum(s - n_ctx, 0)
        return jnp.where(d == 0, t, n_lat - 1 - t)

    def ctx_idx(d, s):
        t = jnp.minimum(s, n_ctx - 1)
        return jnp.where(d == 0, t, n_ctx - 1 - t)

    def out_idx(d, s):
        return jnp.where(d == 0, n_lat - 1, lat_idx(d, s))

    return pl.pallas_call(
        functools.partial(_ssd_kernel, n_ctx=n_ctx, n_lat=n_lat),
        grid=(2, n_ctx + n_lat),
        in_specs=[
            pl.BlockSpec((nb, q, d_xbc), lambda d, s: (0, lat_idx(d, s), 0)),
            pl.BlockSpec((nb, q, LANES), lambda d, s: (0, lat_idx(d, s), d)),
            pl.BlockSpec((nb, q, d_xbc), lambda d, s: (0, ctx_idx(d, s), 0)),
            pl.BlockSpec((nb, q, LANES), lambda d, s: (0, ctx_idx(d, s), d)),
            pl.BlockSpec((nb, q, d_ssd), lambda d, s: (0, lat_idx(d, s), 0)),
            pl.BlockSpec((None, 1, LANES), lambda d, s: (d, 0, 0)),
            pl.BlockSpec((None, 1, LANES), lambda d, s: (d, 0, 0)),
            pl.BlockSpec((1, d_ssd), lambda d, s: (0, 0)),
            pl.BlockSpec((1, d_ssd), lambda d, s: (0, 0)),
            pl.BlockSpec((LANES, d_ssd), lambda d, s: (0, 0)),
        ],
        out_specs=pl.BlockSpec((nb, q, d_ssd), lambda d, s: (0, out_idx(d, s), 0)),
        out_shape=jax.ShapeDtypeStruct((nb, seq_len, d_ssd), BF16),
        scratch_shapes=[
            pltpu.VMEM((nb, SSD_STATE, d_ssd), F32),
            pltpu.VMEM((nb, n_lat, q, d_ssd), F32),
        ],
        compiler_params=_params(2, 56),
        name="ssd",
    )(u_lat, dt_lat, u_ctx, dt_ctx, proj3, alog, dtb, dsk, nw, expand)


def _hy_mlp_kernel(zt_ref, w1_ref, w2_ref, w3_ref, b_ref, fr_ref, o_ref):
    fr = fr_ref[...]
    b = b_ref[...]
    h = jnp.sin(fr * (jnp.dot(w1_ref[...], zt_ref[...], precision=HIGHEST, preferred_element_type=F32)
                      + b[:, 0:1]))
    h = jnp.sin(fr * (jnp.dot(w2_ref[...], h, precision=HIGHEST, preferred_element_type=F32) + b[:, 1:2]))
    h = jnp.sin(fr * (jnp.dot(w3_ref[...], h, precision=HIGHEST, preferred_element_type=F32) + b[:, 2:3]))
    hid, seq_len = h.shape
    hp = jnp.concatenate([h, jnp.zeros((LANES - hid, seq_len), F32)], axis=0)
    o_ref[...] = hp.T


def _hy_mlp(zt, w1t, w2t, w3t, b3, fr):
    seq_len = zt.shape[1]
    return pl.pallas_call(
        _hy_mlp_kernel,
        out_shape=jax.ShapeDtypeStruct((seq_len, LANES), F32),
        compiler_params=pltpu.CompilerParams(vmem_limit_bytes=40 * MIB),
        name="hy_mlp",
    )(zt, w1t, w2t, w3t, b3, fr)


def _fft_tables(seq_len):
    n_fft = 2 * seq_len
    n2 = FFT_N2
    n1 = n_fft // n2
    n1h = n1 // 2
    k1n = n1h + 1
    k1p = -(-k1n // 4) * 4
    k1 = np.arange(k1n, dtype=np.int64)
    th = (2.0 * np.pi / n1) * ((k1[:, None] * np.arange(n1h, dtype=np.int64)[None, :]) % n1)
    f1 = np.zeros((2 * k1p, n1h))
    f1[0:2 * k1n:2] = np.cos(th)
    f1[1:2 * k1n:2] = -np.sin(th)
    idx = np.arange(n2, dtype=np.int64)
    kk = k1[:, None, None] + n1 * idx[None, :, None]
    ph = (2.0 * np.pi / n_fft) * ((kk * idx[None, None, :]) % n_fft)
    g_re, g_im = np.cos(ph), -np.sin(ph)

    def blocks(re, im):
        out = np.zeros((k1p, 2 * n2, 2 * n2))
        out[:k1n, :n2, :n2] = re
        out[:k1n, :n2, n2:] = -im
        out[:k1n, n2:, :n2] = im
        out[:k1n, n2:, n2:] = re
        return out

    wgt = np.where((k1 == 0) | (k1 == n1h), 1.0, 2.0) / n_fft
    gf = blocks(g_re, g_im)
    gi = blocks(np.transpose(g_re, (0, 2, 1)) * wgt[:, None, None],
                -np.transpose(g_im, (0, 2, 1)) * wgt[:, None, None])
    as_bf16 = lambda t: jnp.asarray(t.astype(np.float32)).astype(BF16)
    return as_bf16(f1), as_bf16(np.ascontiguousarray(f1.T)), as_bf16(gf), as_bf16(gi)


_BATCHED = (((2,), (1,)), ((0,), (0,)))


def _fft_forward(x, f1, gf):
    seq_len, c = x.shape
    rows, n1h = f1.shape
    xt = jnp.swapaxes(x.reshape(n1h, FFT_N2, c), 0, 1).astype(BF16)
    a = lax.dot_general(jnp.broadcast_to(f1[None], (FFT_N2, rows, n1h)), xt, _BATCHED,
                        preferred_element_type=F32)
    at = jnp.swapaxes(a, 0, 1).reshape(rows // 2, 2 * FFT_N2, c).astype(BF16)
    return lax.dot_general(gf, at, _BATCHED, preferred_element_type=F32)


def _fft_inverse(y, f1t, gi):
    c = y.shape[2]
    n1h, rows = f1t.shape
    bt = lax.dot_general(gi, y, _BATCHED, preferred_element_type=F32)
    btt = jnp.swapaxes(bt.reshape(rows, FFT_N2, c), 0, 1).astype(BF16)
    yv = lax.dot_general(jnp.broadcast_to(f1t[None], (FFT_N2, n1h, rows)), btt, _BATCHED,
                         preferred_element_type=F32)
    return jnp.swapaxes(yv, 0, 1).reshape(n1h * FFT_N2, c)


def _hy_spec_kernel(h_ref, wf_ref, wb_ref, dl_ref, f1_ref, gf_ref, k_ref):
    n2 = FFT_N2
    seq_len = h_ref.shape[0]
    tc = wf_ref.shape[1]
    h3 = h_ref[...].astype(BF16)
    rid = lax.broadcasted_iota(jnp.int32, (seq_len, tc), 0)
    decay = jnp.exp(-(rid.astype(F32) * (1.0 / (seq_len - 1))) * dl_ref[...])
    hf = jnp.dot(h3, wf_ref[...].astype(BF16), preferred_element_type=F32) * decay
    hb = jnp.dot(h3, wb_ref[...].astype(BF16), preferred_element_type=F32) * decay
    norm = jnp.sum(jnp.abs(hf) + jnp.abs(hb), axis=0, keepdims=True) + 1e-6
    inv = 1.0 / norm
    hb = jnp.where(rid == 0, 0.0, hb * inv)
    x = _fft_forward(jnp.concatenate([hf * inv, hb], axis=1), f1_ref[...], gf_ref[...])
    xf, xb = x[:, :, 0:tc], x[:, :, tc:2 * tc]
    k_ref[...] = jnp.concatenate([xf[:, :n2] + xb[:, :n2], xf[:, n2:] - xb[:, n2:]], axis=1)


def _hy_spec(h3, w4f, w4b, deltas, f1, gf, tc):
    seq_len = h3.shape[0]
    dh = w4f.shape[1]
    k1p = gf.shape[0]
    rows, n1h = f1.shape
    return pl.pallas_call(
        _hy_spec_kernel,
        grid=(dh // tc,),
        in_specs=[
            pl.BlockSpec((seq_len, LANES), lambda j: (0, 0)),
            pl.BlockSpec((LANES, tc), lambda j: (0, j)),
            pl.BlockSpec((LANES, tc), lambda j: (0, j)),
            pl.BlockSpec((1, tc), lambda j: (0, j)),
            pl.BlockSpec((rows, n1h), lambda j: (0, 0)),
            pl.BlockSpec((k1p, 2 * FFT_N2, 2 * FFT_N2), lambda j: (0, 0, 0)),
        ],
        out_specs=pl.BlockSpec((k1p, 2 * FFT_N2, tc), lambda j: (0, 0, j)),
        out_shape=jax.ShapeDtypeStruct((k1p, 2 * FFT_N2, dh), F32),
        compiler_params=_params(1, 56),
        name="hy_spec",
    )(h3, w4f, w4b, deltas, f1, gf)


def _hy_conv_kernel(u_ref, x0_ref, k_ref, f1_ref, f1t_ref, gf_ref, gi_ref, bias_ref, o_ref):
    n2 = FFT_N2
    u = u_ref[...].astype(F32)
    x = _fft_forward(u, f1_ref[...], gf_ref[...])
    kk = k_ref[...]
    xr, xi, kr, ki = x[:, :n2], x[:, n2:], kk[:, :n2], kk[:, n2:]
    y = jnp.concatenate([xr * kr - xi * ki, xr * ki + xi * kr], axis=1).astype(BF16)
    conv = _fft_inverse(y, f1t_ref[...], gi_ref[...])
    o_ref[...] = (x0_ref[...].astype(F32) * (conv + u * bias_ref[...])).astype(o_ref.dtype)


def _hy_conv(u3, x0c, kspec, f1, f1t, gf, gi, bias, tc):
    nb, seq_len, dh = u3.shape
    k1p = gf.shape[0]
    rows, n1h = f1.shape
    blk = pl.BlockSpec((None, seq_len, tc), lambda j, b: (b, 0, j))
    const = lambda shape: pl.BlockSpec(shape, lambda j, b: (0,) * len(shape), pipeline_mode=ONE_BUFFER)
    return pl.pallas_call(
        _hy_conv_kernel,
        grid=(dh // tc, nb),
        in_specs=[
            blk, blk,
            pl.BlockSpec((k1p, 2 * FFT_N2, tc), lambda j, b: (0, 0, j)),
            const((rows, n1h)), const((n1h, rows)),
            const((k1p, 2 * FFT_N2, 2 * FFT_N2)), const((k1p, 2 * FFT_N2, 2 * FFT_N2)),
            pl.BlockSpec((1, tc), lambda j, b: (0, j)),
        ],
        out_specs=blk,
        out_shape=jax.ShapeDtypeStruct((nb, seq_len, dh), BF16),
        compiler_params=_params(2, 56),
        name="hy_conv",
    )(u3, x0c, kspec, f1, f1t, gf, gi, bias)


def _outproj_kernel(ys_ref, yh_ref, w_ref, x_ref, posr_ref, cemb_ref, nwp_ref, g_ref, nwf_ref, sh_ref, sc_ref,
                    xo_ref, h_ref):
    tm, ds = ys_ref.shape
    rows = min(256, tm)
    for r0 in range(0, tm, rows):
        rs = slice(r0, r0 + rows)
        y = jnp.dot(ys_ref[rs, :], w_ref[0:ds, :], preferred_element_type=F32)
        y = y + jnp.dot(yh_ref[rs, :], w_ref[ds:, :], preferred_element_type=F32)
        xn = _add_pos(x_ref[rs, :], posr_ref, r0 // GRID_W, cemb_ref[...]) + g_ref[...] * (_rms(y) * nwp_ref[...])
        xo_ref[rs, :] = xn
        h_ref[rs, :] = (_rms(xn) * nwf_ref[...] * (1.0 + sc_ref[...]) + sh_ref[...]).astype(h_ref.dtype)


def _out_proj(ys, yh, w, x2, posr, cemb, nwp, g1, nwf, sh2, sc2, seq_len, tm):
    m, d = x2.shape
    ds = ys.shape[1]
    dh = yh.shape[1]
    tiles_per_seq = seq_len // tm
    row = lambda i: (i, 0)
    fixed = lambda i: (0, 0)
    per_batch = lambda i: (i // tiles_per_seq, 0, 0)
    return pl.pallas_call(
        _outproj_kernel,
        grid=(m // tm,),
        in_specs=[
            pl.BlockSpec((tm, ds), row),
            pl.BlockSpec((tm, dh), row),
            pl.BlockSpec((ds + dh, d), fixed, pipeline_mode=ONE_BUFFER),
            pl.BlockSpec((tm, d), row),
            pl.BlockSpec((tm // GRID_W, d // 2), lambda i: (i % tiles_per_seq, 0)),
            pl.BlockSpec(cemb.shape, fixed),
            pl.BlockSpec((1, d), fixed),
            pl.BlockSpec((None, 1, d), per_batch),
            pl.BlockSpec((1, d), fixed),
            pl.BlockSpec((None, 1, d), per_batch),
            pl.BlockSpec((None, 1, d), per_batch),
        ],
        out_specs=[pl.BlockSpec((tm, d), row), pl.BlockSpec((tm, d), row)],
        out_shape=[jax.ShapeDtypeStruct((m, d), F32), jax.ShapeDtypeStruct((m, d), BF16)],
        compiler_params=_params(1, 56),
        name="out_proj",
    )(ys, yh, w, x2, posr, cemb, nwp, g1, nwf, sh2, sc2)


def _ffn_step(h_ref, wg, wu, wd, x_ref, nw_ref, g_ref, o_ref):
    f = pl.program_id(1)
    h = h_ref[...]
    gate = jnp.dot(h, wg, preferred_element_type=F32)
    up = jnp.dot(h, wu, preferred_element_type=F32)
    act = (_silu(gate) * up).astype(BF16)

    @pl.when(f == 0)
    def _():
        o_ref[...] = jnp.zeros_like(o_ref)

    d = o_ref.shape[1]
    cw = min(512, d)
    for n0 in range(0, d, cw):
        o_ref[:, n0:n0 + cw] += jnp.dot(act, wd[:, n0:n0 + cw], preferred_element_type=F32)

    @pl.when(f == pl.num_programs(1) - 1)
    def _():
        tm = o_ref.shape[0]
        rows = min(256, tm)
        for r0 in range(0, tm, rows):
            y = o_ref[r0:r0 + rows, :]
            o_ref[r0:r0 + rows, :] = x_ref[r0:r0 + rows, :] + g_ref[...] * (_rms(y) * nw_ref[...])


def _ffn_first_kernel(h_ref, wg_ref, wu_ref, wd_ref, x_ref, nw_ref, g_ref, o_ref, wgb_ref, wub_ref, wdb_ref):
    wg = wg_ref[...].astype(BF16)
    wu = wu_ref[...].astype(BF16)
    wd = wd_ref[...].astype(BF16)
    wgb_ref[...] = wg
    wub_ref[...] = wu
    wdb_ref[...] = wd
    _ffn_step(h_ref, wg, wu, wd, x_ref, nw_ref, g_ref, o_ref)


def _ffn_rest_kernel(h_ref, wg_ref, wu_ref, wd_ref, x_ref, nw_ref, g_ref, o_ref):
    _ffn_step(h_ref, wg_ref[...], wu_ref[...], wd_ref[...], x_ref, nw_ref, g_ref, o_ref)


def _ffn(h2, wg, wu, wd, xn, nw, g2, seq_len, tm, tf_first, tf):
    m, d = xn.shape
    dff = wg.shape[1]
    tiles_per_seq = seq_len // tm
    n_rows = m // tm
    out, wgb, wub, wdb = pl.pallas_call(
        _ffn_first_kernel,
        grid=(1, dff // tf_first),
        in_specs=[
            pl.BlockSpec((tm, d), lambda i, f: (0, 0), pipeline_mode=ONE_BUFFER),
            pl.BlockSpec((d, tf_first), lambda i, f: (0, f)),
            pl.BlockSpec((d, tf_first), lambda i, f: (0, f)),
            pl.BlockSpec((tf_first, d), lambda i, f: (f, 0)),
            pl.BlockSpec((tm, d), lambda i, f: (0, 0), pipeline_mode=ONE_BUFFER),
            pl.BlockSpec((1, d), lambda i, f: (0, 0)),
            pl.BlockSpec((None, 1, d), lambda i, f: (0, 0, 0)),
        ],
        out_specs=[
            pl.BlockSpec((tm, d), lambda i, f: (0, 0)),
            pl.BlockSpec((d, tf_first), lambda i, f: (0, f)),
            pl.BlockSpec((d, tf_first), lambda i, f: (0, f)),
            pl.BlockSpec((tf_first, d), lambda i, f: (f, 0)),
        ],
        out_shape=[
            jax.ShapeDtypeStruct((m, d), F32),
            jax.ShapeDtypeStruct((d, dff), BF16),
            jax.ShapeDtypeStruct((d, dff), BF16),
            jax.ShapeDtypeStruct((dff, d), BF16),
        ],
        input_output_aliases={4: 0},
        compiler_params=_params(2, 58),
        name="ffn_first",
    )(h2, wg, wu, wd, xn, nw, g2)
    if n_rows == 1:
        return out
    return pl.pallas_call(
        _ffn_rest_kernel,
        grid=(n_rows - 1, dff // tf),
        in_specs=[
            pl.BlockSpec((tm, d), lambda i, f: (i + 1, 0)),
            pl.BlockSpec((d, tf), lambda i, f: (0, f)),
            pl.BlockSpec((d, tf), lambda i, f: (0, f)),
            pl.BlockSpec((tf, d), lambda i, f: (f, 0)),
            pl.BlockSpec((tm, d), lambda i, f: (i + 1, 0), pipeline_mode=ONE_BUFFER),
            pl.BlockSpec((1, d), lambda i, f: (0, 0)),
            pl.BlockSpec((None, 1, d), lambda i, f: ((i + 1) // tiles_per_seq, 0, 0)),
        ],
        out_specs=pl.BlockSpec((tm, d), lambda i, f: (i + 1, 0)),
        out_shape=jax.ShapeDtypeStruct((m, d), F32),
        input_output_aliases={4: 0},
        compiler_params=_params(2, 58),
        name="ffn",
    )(h2, wgb, wub, wdb, out, nw, g2)


def _sincos_tables(rows, cols, dim):
    qd = dim // 4
    omega = 1.0 / (POS_THETA ** (jnp.arange(qd, dtype=F32) / qd))
    r = jnp.arange(rows, dtype=F32)[:, None] * omega
    cc = jnp.arange(cols, dtype=F32)[:, None] * omega
    r_emb = jnp.concatenate([jnp.sin(r), jnp.cos(r)], -1)
    c_emb = jnp.concatenate([jnp.sin(cc), jnp.cos(cc)], -1)
    return r_emb, c_emb


def _filter_features_t(seq_len, n_bands):
    t = jnp.linspace(0.0, 1.0, seq_len, dtype=F32)[:, None]
    w = 2.0 * math.pi * jnp.arange(seq_len, dtype=F32)[:, None] / seq_len
    fb = jnp.linspace(1e-4, n_bands - 1, n_bands, dtype=F32)[None]
    zpos = jnp.concatenate([t, jnp.cos(fb * w), -jnp.sin(fb * w)], -1)
    emb = zpos.shape[1]
    return jnp.pad(zpos, ((0, 0), (0, LANES - emb))).T


def _pad_rows(a, rows):
    return jnp.pad(a, ((0, rows - a.shape[0]), (0, 0)))


def kernel(x, c, ctx, c_ctx, w_ada, b_ada, norm_mix_pre, norm_mix_post, norm_ffn_pre, norm_ffn_post,
           w_in, ssd_conv_w, ssd_conv_b, ssd_a_log, ssd_dt_bias, ssd_d, ssd_norm,
           hy_conv_w, hy_conv_b, hy_w1, hy_b1, hy_w2, hy_b2, hy_w3, hy_b3, hy_w4, hy_freq, hy_bias,
           w_out, w_gate, w_up, w_down):
    nb, seq_len, d = x.shape
    ctx_len = ctx.shape[1]
    assert w_ada.shape[0] == 1, "single layer"
    n_heads = ssd_d.shape[1]
    d_ssd = n_heads * SSD_HEAD_DIM
    d_xbc = d_ssd + 2 * SSD_GROUPS * SSD_STATE
    dh = hy_bias.shape[1]
    assert w_in.shape[2] == d_ssd + d_xbc + 2 * n_heads + 3 * dh
    assert n_heads <= LANES and nb + 1 <= 8
    assert seq_len % (GRID_W * 8) == 0 and seq_len % FFT_N2 == 0
    m = nb * seq_len

    crows = jnp.concatenate([c, c_ctx[None, :]], axis=0)
    mod = _ada(crows, w_ada[0], b_ada[0])
    part = lambda r0, r1, k: mod[r0:r1, k * d:(k + 1) * d][:, None, :]
    sh1, sc1, g1, sh2, sc2, g2 = (part(0, nb, k) for k in range(6))
    csh1 = jnp.broadcast_to(part(nb, nb + 1, 0), (nb, 1, d))
    csc1 = jnp.broadcast_to(part(nb, nb + 1, 1), (nb, 1, d))

    w_t = jnp.transpose(w_in[0])
    o_xbc = d_ssd
    o_dt = o_xbc + d_xbc
    o_hy = o_dt + 2 * n_heads
    tn_in = 512
    assert o_dt % tn_in == 0 and (3 * dh) % tn_in == 0 and o_xbc % tn_in == 0 and o_hy % 16 == 0
    n_left = o_dt // tn_in
    main_off = lambda j: jnp.where(j < n_left, j * tn_in, o_hy + (j - n_left) * tn_in)
    ctx_off = lambda j: o_xbc + j * tn_in
    pad_dt = lambda rows: jnp.pad(rows, ((0, LANES - n_heads), (0, 0)))
    w_dt_t = jnp.concatenate([pad_dt(w_t[o_dt:o_dt + n_heads]), pad_dt(w_t[o_dt + n_heads:o_hy])],
                             axis=0)

    r_emb, c_emb = _sincos_tables(seq_len // GRID_W, GRID_W, d)
    posr = r_emb
    nmp = norm_mix_pre[0][None, :]

    tm_in = min(1024, seq_len)
    n_main = o_dt + 3 * dh
    tn_rest = n_main // 4 if n_main % (4 * LANES) == 0 else tn_in
    proj, dt_lat = _in_proj(x.reshape(m, d), posr, c_emb, nmp, sh1, sc1, w_t, main_off, n_main // tn_in,
                            w_dt_t, seq_len, tm_in, tn_in, True, tn_rest)
    tm_ctx = min(256, ctx_len)
    xbc_ctx, dt_ctx = _in_proj(ctx.reshape(nb * ctx_len, d), jnp.zeros((8, d // 2), F32), c_emb, nmp, csh1, csc1,
                               w_t, ctx_off, d_xbc // tn_in, w_dt_t, ctx_len, tm_ctx, tn_in, False)
    proj3 = proj.reshape(nb, seq_len, -1)

    cw8 = _pad_rows(ssd_conv_w[0], 8)
    cb1 = ssd_conv_b[0][None, :]
    u_lat = _ssd_prep(proj3, d_ssd, d_xbc, cw8, cb1, 512)
    u_ctx = _ssd_prep(xbc_ctx.reshape(nb, ctx_len, d_xbc), 0, d_xbc, cw8, cb1, 512)
    pad_heads = lambda a: jnp.pad(a, ((0, 0), (0, LANES - n_heads)))[:, None, :]
    expand = (jnp.arange(LANES)[:, None] == (jnp.arange(d_ssd)[None, :] // SSD_HEAD_DIM)).astype(BF16)
    y_ssd = _ssd(u_lat, dt_lat.reshape(nb, seq_len, 2 * LANES), u_ctx, dt_ctx.reshape(nb, ctx_len, 2 * LANES),
                 proj3, pad_heads(ssd_a_log[0]), pad_heads(ssd_dt_bias[0]),
                 jnp.repeat(ssd_d[0], SSD_HEAD_DIM)[None, :], ssd_norm[0][None, :], expand)

    x0c, u_hy = _hy_prep(proj3, d_ssd + d_xbc, dh, _pad_rows(hy_conv_w[0], 8), hy_conv_b[0][None, :], 256)
    n_bands = (hy_w1.shape[1] - 1) // 2
    zt = _filter_features_t(seq_len, n_bands)
    w1t = jnp.pad(hy_w1[0].T, ((0, 0), (0, LANES - hy_w1.shape[1])))
    b3 = jnp.stack([hy_b1[0], hy_b2[0], hy_b3[0]], axis=1)
    h3 = _hy_mlp(zt, w1t, hy_w2[0].T, hy_w3[0].T, b3, hy_freq[0][:, None])
    w4 = _pad_rows(hy_w4[0], LANES)
    max_decay = math.log(HY_TARGET) / HY_FAST_PCT
    min_decay = math.log(HY_TARGET) / HY_SLOW_PCT
    deltas = jnp.abs(jnp.linspace(min_decay, max_decay, dh, dtype=F32))[None, :]
    f1, f1t, gf, gi = _fft_tables(seq_len)
    kspec = _hy_spec(h3, w4[:, :dh], w4[:, dh:], deltas, f1, gf, LANES)
    y_hy = _hy_conv(u_hy, x0c, kspec, f1, f1t, gf, gi, hy_bias[0][None, :], LANES)

    xn, h2 = _out_proj(y_ssd.reshape(m, d_ssd), y_hy.reshape(m, dh), w_out[0].astype(BF16), x.reshape(m, d),
                       posr, c_emb, norm_mix_post[0][None, :], g1, norm_ffn_pre[0][None, :], sh2, sc2, seq_len,
                       min(512, seq_len))
    out = _ffn(h2, w_gate[0], w_up[0], w_down[0], xn, norm_ffn_post[0][None, :], g2, seq_len,
               min(1024, seq_len), 256, 512)
    return out.reshape(nb, seq_len, d)
```

```python
import functools
import math

import numpy as np
import jax
import jax.numpy as jnp
from jax import lax
from jax.experimental import pallas as pl
from jax.experimental.pallas import tpu as pltpu

F32 = jnp.float32
BF16 = jnp.bfloat16
HIGHEST = lax.Precision.HIGHEST

RMS_EPS = 1e-6
POS_THETA = 10000.0
GRID_W = 64
SSD_HEAD_DIM = 64
SSD_GROUPS = 2
SSD_STATE = 128
SSD_CHUNK = 128
SSD_STEP_CHUNKS = 2
HY_TARGET = 1e-2
HY_FAST_PCT = 0.3
HY_SLOW_PCT = 1.5
FFT_N2 = 64
LANES = 128
MIB = 1024 * 1024
ONE_BUFFER = pl.Buffered(1)
_NT_DIMS = (((1,), (1,)), ((), ()))
_NEG_BIG = -1e30


def _params(n_axes, vmem_mib):
    return pltpu.CompilerParams(
        dimension_semantics=("arbitrary",) * n_axes,
        vmem_limit_bytes=vmem_mib * MIB,
    )


def _silu(v):
    return v * (1.0 / (1.0 + jnp.exp(-v)))


def _softplus(v):
    return jnp.maximum(v, 0.0) + jnp.log(1.0 + jnp.exp(-jnp.abs(v)))


def _rms(v):
    return v * lax.rsqrt(jnp.mean(v * v, axis=-1, keepdims=True) + RMS_EPS)


def _add_pos(x, remb_ref, g0, cemb):
    rows, d = x.shape
    half = d // 2
    lo = jnp.concatenate(
        [x[g * GRID_W:(g + 1) * GRID_W, :half] + remb_ref[g0 + g:g0 + g + 1, :] for g in range(rows // GRID_W)],
        axis=0)
    hi = (x[:, half:].reshape(rows // GRID_W, GRID_W, half) + cemb[None]).reshape(rows, half)
    return jnp.concatenate([lo, hi], axis=1)


def _ada_kernel(c_ref, w_ref, b_ref, o_ref, s_scr):
    nr, d, _ = c_ref.shape
    tn = w_ref.shape[1]
    nt = tn // LANES
    kc = 64

    @pl.when(pl.program_id(0) == 0)
    def _():
        for r in range(nr):
            for k0 in range(0, d, 512):
                s_scr[r, k0:k0 + 512, :] = _silu(c_ref[r, k0:k0 + 512, :])

    acc = [[jnp.zeros((8, LANES), F32) for _ in range(nt)] for _ in range(nr)]
    for k0 in range(0, d, kc):
        s = [s_scr[r, k0:k0 + kc, :] for r in range(nr)]
        for t in range(nt):
            w = w_ref[k0:k0 + kc, t * LANES:(t + 1) * LANES]
            for r in range(nr):
                acc[r][t] = acc[r][t] + jnp.sum((w * s[r]).reshape(kc // 8, 8, LANES), axis=0)
    rows = [jnp.concatenate([jnp.sum(a, axis=0, keepdims=True) for a in acc[r]], axis=1) for r in range(nr)]
    rows.append(jnp.zeros((o_ref.shape[0] - nr, tn), F32))
    o_ref[...] = jnp.concatenate(rows, axis=0) + b_ref[...]


def _ada(crows, w_ada, b_ada):
    nr = crows.shape[0]
    d, n = w_ada.shape
    tn = 1024
    c_lanes = jnp.broadcast_to(crows[:, :, None], (nr, d, LANES))
    return pl.pallas_call(
        _ada_kernel,
        grid=(n // tn,),
        in_specs=[
            pl.BlockSpec((nr, d, LANES), lambda j: (0, 0, 0), pipeline_mode=ONE_BUFFER),
            pl.BlockSpec((d, tn), lambda j: (0, j)),
            pl.BlockSpec((1, tn), lambda j: (0, j)),
        ],
        out_specs=pl.BlockSpec((8, tn), lambda j: (0, j)),
        out_shape=jax.ShapeDtypeStruct((8, n), F32),
        scratch_shapes=[pltpu.VMEM((nr, d, LANES), F32)],
        compiler_params=_params(1, 40),
        name="ada",
    )(c_lanes, w_ada, b_ada.reshape(1, n))


def _inproj_body(x_ref, posr_ref, cemb_ref, nw_ref, sh_ref, sc_ref, get_w, wdt_ref, o_ref, dt_ref, h_scr, with_pos):
    first = pl.program_id(1) == 0

    @pl.when(first)
    def _():
        tm = x_ref.shape[0]
        rows = min(256, tm)
        wdt = wdt_ref[...].astype(BF16)
        w = get_w()
        for r0 in range(0, tm, rows):
            xf = x_ref[r0:r0 + rows, :]
            if with_pos:
                xf = _add_pos(xf, posr_ref, r0 // GRID_W, cemb_ref[...])
            h = _rms(xf) * nw_ref[...] * (1.0 + sc_ref[...]) + sh_ref[...]
            hb = h.astype(BF16)
            h_scr[r0:r0 + rows, :] = hb
            dt_ref[r0:r0 + rows, :] = lax.dot_general(hb, wdt, _NT_DIMS, preferred_element_type=F32)
            o_ref[r0:r0 + rows, :] = jnp.dot(hb, w, preferred_element_type=F32).astype(o_ref.dtype)

    @pl.when(jnp.logical_not(first))
    def _():
        o_ref[...] = jnp.dot(h_scr[...], get_w(), preferred_element_type=F32).astype(o_ref.dtype)


def _inproj_t_kernel(x_ref, posr_ref, cemb_ref, nw_ref, sh_ref, sc_ref, w_ref, wdt_ref, *rest, with_pos, emit_w):
    w = w_ref[...].T.astype(BF16)
    if emit_w:
        o_ref, dt_ref, wb_ref, h_scr = rest
        wb_ref[...] = w
    else:
        o_ref, dt_ref, h_scr = rest
    _inproj_body(x_ref, posr_ref, cemb_ref, nw_ref, sh_ref, sc_ref, lambda: w, wdt_ref, o_ref, dt_ref, h_scr,
                 with_pos)


def _inproj_rest_kernel(x_ref, posr_ref, cemb_ref, nw_ref, sh_ref, sc_ref, w_ref, wdt_ref, p0_ref, dt0_ref,
                        o_ref, dt_ref, h_scr):
    row_block = pl.program_id(0)

    @pl.when(row_block == 0)
    def _():
        o_ref[...] = p0_ref[...]

        @pl.when(pl.program_id(1) == 0)
        def _():
            dt_ref[...] = dt0_ref[...]

    @pl.when(row_block > 0)
    def _():
        _inproj_body(x_ref, posr_ref, cemb_ref, nw_ref, sh_ref, sc_ref, lambda: w_ref[...], wdt_ref, o_ref, dt_ref,
                     h_scr, True)


def _in_proj_specs(tm, d, half, pos_rows, cemb, ndt, tiles_per_seq, with_pos, blk, one_row_block):
    mode = ONE_BUFFER if one_row_block else None
    return [
        pl.BlockSpec((tm, d), lambda i, j: (blk(i), 0), pipeline_mode=mode),
        pl.BlockSpec((pos_rows, half), lambda i, j: ((blk(i) % tiles_per_seq) if with_pos else 0, 0),
                     pipeline_mode=mode),
        pl.BlockSpec(cemb.shape, lambda i, j: (0, 0), pipeline_mode=ONE_BUFFER),
        pl.BlockSpec((1, d), lambda i, j: (0, 0)),
        pl.BlockSpec((None, 1, d), lambda i, j: (blk(i) // tiles_per_seq, 0, 0)),
        pl.BlockSpec((None, 1, d), lambda i, j: (blk(i) // tiles_per_seq, 0, 0)),
    ], pl.BlockSpec((ndt, d), lambda i, j: (0, 0), pipeline_mode=ONE_BUFFER)


def _in_proj(x2, posr, cemb, nw, sh, sc, w_t, row_off, n_tiles, wdt_t, seq_len, tm, tn, with_pos, tn_rest=None):
    m, d = x2.shape
    n = n_tiles * tn
    ndt = wdt_t.shape[0]
    half = d // 2
    tiles_per_seq = seq_len // tm
    pos_rows = tm // GRID_W if with_pos else posr.shape[0]
    split = tn_rest is not None and m // tm > 1
    common, wdt_spec = _in_proj_specs(tm, d, half, pos_rows, cemb, ndt, tiles_per_seq, with_pos, lambda i: i, split)
    w_spec = pl.BlockSpec((pl.Element(tn), pl.Element(d)), lambda i, j: (pl.multiple_of(row_off(j), 8), 0))
    out_specs = [pl.BlockSpec((tm, tn), lambda i, j: (i, j)), pl.BlockSpec((tm, ndt), lambda i, j: (i, 0))]
    m_first = tm if split else m
    out_shape = [jax.ShapeDtypeStruct((m_first, n), BF16), jax.ShapeDtypeStruct((m_first, ndt), F32)]
    if split:
        out_specs.append(pl.BlockSpec((d, tn), lambda i, j: (0, j)))
        out_shape.append(jax.ShapeDtypeStruct((d, n), BF16))
    res = pl.pallas_call(
        functools.partial(_inproj_t_kernel, with_pos=with_pos, emit_w=split),
        grid=(1 if split else m // tm, n_tiles),
        in_specs=common + [w_spec, wdt_spec],
        out_specs=out_specs,
        out_shape=out_shape,
        scratch_shapes=[pltpu.VMEM((tm, d), BF16)],
        compiler_params=_params(2, 58),
        name="in_proj_t",
    )(x2, posr, cemb, nw, sh, sc, w_t, wdt_t)
    if not split:
        return res
    proj0, dt0, w_bf = res
    nj = n // tn_rest
    common, wdt_spec = _in_proj_specs(tm, d, half, pos_rows, cemb, ndt, tiles_per_seq, with_pos,
                                      lambda i: jnp.maximum(i, 1), False)
    return pl.pallas_call(
        _inproj_rest_kernel,
        grid=(m // tm, nj),
        in_specs=common + [
            pl.BlockSpec((d, tn_rest), lambda i, j: (0, j)),
            wdt_spec,
            pl.BlockSpec((tm, tn_rest), lambda i, j: (0, jnp.where(i == 0, j, nj - 1))),
            pl.BlockSpec((tm, ndt), lambda i, j: (0, 0), pipeline_mode=ONE_BUFFER),
        ],
        out_specs=[pl.BlockSpec((tm, tn_rest), lambda i, j: (i, j)),
                   pl.BlockSpec((tm, ndt), lambda i, j: (i, 0))],
        out_shape=[jax.ShapeDtypeStruct((m, n), BF16), jax.ShapeDtypeStruct((m, ndt), F32)],
        scratch_shapes=[pltpu.VMEM((tm, d), BF16)],
        compiler_params=_params(2, 58),
        name="in_proj",
    )(x2, posr, cemb, nw, sh, sc, w_bf, wdt_t, proj0, dt0)


CONV_ROWS = 256


def _shift_matrices(rows):
    return jnp.stack([jnp.eye(rows, k=-1, dtype=BF16), jnp.eye(rows, k=1, dtype=BF16)])


def _conv3_chunk(x_ref, r0, rows, seq_len, w, b, shift_ref):
    cur_b = x_ref[r0:r0 + rows, :]
    tc = cur_b.shape[1]
    up = jnp.dot(shift_ref[0], cur_b, preferred_element_type=F32)
    down = jnp.dot(shift_ref[1], cur_b, preferred_element_type=F32)
    rid = lax.broadcasted_iota(jnp.int32, (8, tc), 0)
    if r0 > 0:
        prev_row = x_ref[r0 - 16:r0, :].astype(F32)[15:16, :]
        up = jnp.concatenate([jnp.where(rid == 0, prev_row, up[0:8]), up[8:]], axis=0)
    if r0 + rows < seq_len:
        next_row = x_ref[r0 + rows:r0 + rows + 16, :].astype(F32)[0:1, :]
        down = jnp.concatenate([down[:rows - 8], jnp.where(rid == 7, next_row, down[rows - 8:])], axis=0)
    return up * w[0:1, :] + cur_b.astype(F32) * w[1:2, :] + down * w[2:3, :] + b


def _ssd_prep_kernel(x_ref, w_ref, b_ref, shift_ref, o_ref):
    seq_len = x_ref.shape[0]
    rows = shift_ref.shape[1]
    w = w_ref[...]
    b = b_ref[...]
    for r0 in range(0, seq_len, rows):
        v = _conv3_chunk(x_ref, r0, rows, seq_len, w, b, shift_ref)
        o_ref[r0:r0 + rows, :] = _silu(v).astype(o_ref.dtype)


def _ssd_prep(proj3, col0, width, w8, b1, tc):
    nb, seq_len, _ = proj3.shape
    rows = min(CONV_ROWS, seq_len)
    j0 = col0 // tc
    return pl.pallas_call(
        _ssd_prep_kernel,
        grid=(nb, width // tc),
        in_specs=[
            pl.BlockSpec((None, seq_len, tc), lambda b, j: (b, 0, j0 + j)),
            pl.BlockSpec((8, tc), lambda b, j: (0, j)),
            pl.BlockSpec((1, tc), lambda b, j: (0, j)),
            pl.BlockSpec((2, rows, rows), lambda b, j: (0, 0, 0)),
        ],
        out_specs=pl.BlockSpec((None, seq_len, tc), lambda b, j: (b, 0, j)),
        out_shape=jax.ShapeDtypeStruct((nb, seq_len, width), BF16),
        compiler_params=_params(2, 40),
        name="ssd_prep",
    )(proj3, w8, b1, _shift_matrices(rows))


def _hy_prep_kernel(x0_ref, x1_ref, v_ref, w0_ref, w1_ref, wv_ref, b0_ref, b1_ref, bv_ref, shift_ref,
                    x0c_ref, u_ref):
    seq_len = x0_ref.shape[0]
    rows = shift_ref.shape[1]
    w0, w1, wv = w0_ref[...], w1_ref[...], wv_ref[...]
    b0, b1, bv = b0_ref[...], b1_ref[...], bv_ref[...]
    for r0 in range(0, seq_len, rows):
        x0c_ref[r0:r0 + rows, :] = _conv3_chunk(x0_ref, r0, rows, seq_len, w0, b0, shift_ref).astype(x0c_ref.dtype)
        x1c = _conv3_chunk(x1_ref, r0, rows, seq_len, w1, b1, shift_ref)
        vc = _conv3_chunk(v_ref, r0, rows, seq_len, wv, bv, shift_ref)
        u_ref[r0:r0 + rows, :] = (vc * x1c).astype(u_ref.dtype)


def _hy_prep(proj3, col0, dh, w8, b1, tc):
    nb, seq_len, _ = proj3.shape
    rows = min(CONV_ROWS, seq_len)
    j0 = col0 // tc
    nj = dh // tc
    x_spec = lambda k: pl.BlockSpec((None, seq_len, tc), lambda b, j: (b, 0, j0 + k * nj + j))
    w_spec = lambda k: pl.BlockSpec((8, tc), lambda b, j: (0, k * nj + j))
    b_spec = lambda k: pl.BlockSpec((1, tc), lambda b, j: (0, k * nj + j))
    o_spec = pl.BlockSpec((None, seq_len, tc), lambda b, j: (b, 0, j))
    return pl.pallas_call(
        _hy_prep_kernel,
        grid=(nb, nj),
        in_specs=[x_spec(0), x_spec(1), x_spec(2), w_spec(0), w_spec(1), w_spec(2),
                  b_spec(0), b_spec(1), b_spec(2),
                  pl.BlockSpec((2, rows, rows), lambda b, j: (0, 0, 0))],
        out_specs=[o_spec, o_spec],
        out_shape=[jax.ShapeDtypeStruct((nb, seq_len, dh), BF16)] * 2,
        compiler_params=_params(2, 48),
        name="hy_prep",
    )(proj3, proj3, proj3, w8, w8, w8, b1, b1, b1, _shift_matrices(rows))


def _ssd_kernel(u_ref, dt_ref, uc_ref, dtc_ref, z_ref, alog_ref, dtb_ref, dsk_ref, nw_ref, e_ref,
                o_ref, h_scr, yf_scr, *, n_ctx, n_lat):
    q = SSD_CHUNK
    nsub = SSD_STEP_CHUNKS
    d = pl.program_id(0)
    s = pl.program_id(1)
    is_ctx = s < n_ctx
    t = jnp.maximum(s - n_ctx, 0)
    cidx = jnp.where(d == 0, t, n_lat - 1 - t)
    fwd = d == 0

    @pl.when(s == 0)
    def _():
        h_scr[...] = jnp.zeros_like(h_scr)

    row = lax.broadcasted_iota(jnp.int32, (q, q), 0)
    col = lax.broadcasted_iota(jnp.int32, (q, q), 1)
    tri = (jnp.where(fwd, row, col) >= jnp.where(fwd, col, row)).astype(F32)
    nb = u_ref.shape[0]

    def scan_block(b, xbc_ref, dtr_ref, need_y):
        sub = lambda v, k: v[k * q:(k + 1) * q]
        y_sub = [None] * nsub
        for k in range(nsub):
            rk = nsub - 1 - k
            u = jnp.where(fwd, sub(xbc_ref[b], k), sub(xbc_ref[b], rk))
            dtr = jnp.where(fwd, sub(dtr_ref[b], k), sub(dtr_ref[b], rk))
            y_sub[k] = _ssd_chunk(b, u, dtr, tri, alog_ref, dtb_ref, e_ref, h_scr, need_y)
        if need_y:
            return jnp.concatenate([jnp.where(fwd, y_sub[k], y_sub[nsub - 1 - k]) for k in range(nsub)], axis=0)

    @pl.when(is_ctx)
    def _():
        for b in range(nb):
            scan_block(b, uc_ref, dtc_ref, False)

    @pl.when(jnp.logical_and(jnp.logical_not(is_ctx), fwd))
    def _():
        for b in range(nb):
            yf_scr[b, cidx] = scan_block(b, u_ref, dt_ref, True)

    @pl.when(jnp.logical_and(jnp.logical_not(is_ctx), jnp.logical_not(fwd)))
    def _():
        d_ssd = o_ref.shape[2]
        for b in range(nb):
            y = scan_block(b, u_ref, dt_ref, True)
            xs = u_ref[b][:, :d_ssd].astype(F32)
            ytot = yf_scr[b, cidx] + y + dsk_ref[...] * xs
            gated = ytot * _silu(z_ref[b].astype(F32))
            o_ref[b] = (_rms(gated) * nw_ref[...]).astype(o_ref.dtype)


def _ssd_chunk(b, u, dtr, tri, alog_ref, dtb_ref, e_ref, h_scr, need_y):
    q = SSD_CHUNK
    d_ssd = h_scr.shape[2]
    gw = d_ssd // SSD_GROUPS

    dt = _softplus(dtr + dtb_ref[...])
    a = dt * (-jnp.exp(alog_ref[...]))
    cum = jnp.dot(tri, a, precision=HIGHEST, preferred_element_type=F32)
    mask_add = ((1.0 - tri) * _NEG_BIG).astype(BF16)
    cum_t = cum.T
    tot = jnp.sum(a, axis=0, keepdims=True)

    stack = jnp.concatenate(
        [dt, jnp.exp(cum), jnp.exp(tot - cum), jnp.broadcast_to(jnp.exp(tot), (8, LANES))], axis=0)
    ex = jnp.dot(stack.astype(BF16), e_ref[...], preferred_element_type=F32)
    dt_x = ex[0:q]
    ecum_x = ex[q:2 * q]
    edec_x = ex[2 * q:3 * q]
    etot_x = ex[3 * q:3 * q + 1]

    xs = u[:, :d_ssd].astype(F32)
    xdt = xs * dt_x
    xdt_b = xdt.astype(BF16)
    xdw_b = (xdt * edec_x).astype(BF16)
    lane_lo = lax.broadcasted_iota(jnp.int32, (q, LANES), 1) < SSD_HEAD_DIM
    heads_per_group = gw // SSD_HEAD_DIM

    y_parts = []
    for g in range(SSD_GROUPS):
        bg = u[:, d_ssd + g * SSD_STATE:d_ssd + (g + 1) * SSD_STATE]
        cg = u[:, d_ssd + (SSD_GROUPS + g) * SSD_STATE:d_ssd + (SSD_GROUPS + g + 1) * SSD_STATE]
        h_prev = h_scr[b, :, g * gw:(g + 1) * gw]
        if need_y:
            scores = lax.dot_general(cg, bg, _NT_DIMS, preferred_element_type=F32).astype(BF16)
            y_off = jnp.dot(cg, h_prev.astype(BF16), preferred_element_type=F32)
        for j in range(heads_per_group // 2 if need_y else 0):
            c0 = g * gw + j * LANES
            xp = xdt_b[:, c0:c0 + LANES]
            acc = None
            for hh in range(2):
                h = g * heads_per_group + 2 * j + hh
                diff = (cum[:, h:h + 1] - cum_t[h:h + 1, :]).astype(BF16)
                m_h = scores * jnp.exp(diff + mask_add)
                x_h = jnp.where(lane_lo if hh == 0 else jnp.logical_not(lane_lo), xp, jnp.zeros_like(xp))
                part = jnp.dot(m_h, x_h, preferred_element_type=F32)
                acc = part if acc is None else acc + part
            y_parts.append(acc + y_off[:, j * LANES:(j + 1) * LANES] * ecum_x[:, c0:c0 + LANES])
        upd = lax.dot_general(bg, xdw_b[:, g * gw:(g + 1) * gw], (((0,), (0,)), ((), ())),
                              preferred_element_type=F32)
        h_scr[b, :, g * gw:(g + 1) * gw] = etot_x[:, g * gw:(g + 1) * gw] * h_prev + upd
    return jnp.concatenate(y_parts, axis=1) if need_y else None


def _ssd(u_lat, dt_lat, u_ctx, dt_ctx, proj3, alog, dtb, dsk, nw, expand):
    nb, seq_len, d_xbc = u_lat.shape
    ctx_len = u_ctx.shape[1]
    q = SSD_CHUNK * SSD_STEP_CHUNKS
    assert seq_len % q == 0 and ctx_len % q == 0
    n_lat = seq_len // q
    n_ctx = ctx_len // q
    d_ssd = dsk.shape[1]

    def lat_idx(d, s):
        t = jnp.maximum(s - n_ctx, 0)
        return jnp.where(d == 0, t, n_lat - 1 - t)

    def ctx_idx(d, s):
        t = jnp.minimum(s, n_ctx - 1)
        return jnp.where(d == 0, t, n_ctx - 1 - t)

    def out_idx(d, s):
        return jnp.where(d == 0, n_lat - 1, lat_idx(d, s))

    return pl.pallas_call(
        functools.partial(_ssd_kernel, n_ctx=n_ctx, n_lat=n_lat),
        grid=(2, n_ctx + n_lat),
        in_specs=[
            pl.BlockSpec((nb, q, d_xbc), lambda d, s: (0, lat_idx(d, s), 0)),
            pl.BlockSpec((nb, q, LANES), lambda d, s: (0, lat_idx(d, s), d)),
            pl.BlockSpec((nb, q, d_xbc), lambda d, s: (0, ctx_idx(d, s), 0)),
            pl.BlockSpec((nb, q, LANES), lambda d, s: (0, ctx_idx(d, s), d)),
            pl.BlockSpec((nb, q, d_ssd), lambda d, s: (0, lat_idx(d, s), 0)),
            pl.BlockSpec((None, 1, LANES), lambda d, s: (d, 0, 0)),
            pl.BlockSpec((None, 1, LANES), lambda d, s: (d, 0, 0)),
            pl.BlockSpec((1, d_ssd), lambda d, s: (0, 0)),
            pl.BlockSpec((1, d_ssd), lambda d, s: (0, 0)),
            pl.BlockSpec((LANES, d_ssd), lambda d, s: (0, 0)),
        ],
        out_specs=pl.BlockSpec((nb, q, d_ssd), lambda d, s: (0, out_idx(d, s), 0)),
        out_shape=jax.ShapeDtypeStruct((nb, seq_len, d_ssd), BF16),
        scratch_shapes=[
            pltpu.VMEM((nb, SSD_STATE, d_ssd), F32),
            pltpu.VMEM((nb, n_lat, q, d_ssd), F32),
        ],
        compiler_params=_params(2, 56),
        name="ssd",
    )(u_lat, dt_lat, u_ctx, dt_ctx, proj3, alog, dtb, dsk, nw, expand)


def _hy_mlp_kernel(zt_ref, w1_ref, w2_ref, w3_ref, b_ref, fr_ref, o_ref):
    fr = fr_ref[...]
    b = b_ref[...]
    h = jnp.sin(fr * (jnp.dot(w1_ref[...], zt_ref[...], precision=HIGHEST, preferred_element_type=F32)
                      + b[:, 0:1]))
    h = jnp.sin(fr * (jnp.dot(w2_ref[...], h, precision=HIGHEST, preferred_element_type=F32) + b[:, 1:2]))
    h = jnp.sin(fr * (jnp.dot(w3_ref[...], h, precision=HIGHEST, preferred_element_type=F32) + b[:, 2:3]))
    hid, seq_len = h.shape
    hp = jnp.concatenate([h, jnp.zeros((LANES - hid, seq_len), F32)], axis=0)
    o_ref[...] = hp.T


def _hy_mlp(zt, w1t, w2t, w3t, b3, fr):
    seq_len = zt.shape[1]
    return pl.pallas_call(
        _hy_mlp_kernel,
        out_shape=jax.ShapeDtypeStruct((seq_len, LANES), F32),
        compiler_params=pltpu.CompilerParams(vmem_limit_bytes=40 * MIB),
        name="hy_mlp",
    )(zt, w1t, w2t, w3t, b3, fr)


def _fft_tables(seq_len):
    n_fft = 2 * seq_len
    n2 = FFT_N2
    n1 = n_fft // n2
    n1h = n1 // 2
    k1n = n1h + 1
    k1p = -(-k1n // 4) * 4
    k1 = np.arange(k1n, dtype=np.int64)
    th = (2.0 * np.pi / n1) * ((k1[:, None] * np.arange(n1h, dtype=np.int64)[None, :]) % n1)
    f1 = np.zeros((2 * k1p, n1h))
    f1[0:2 * k1n:2] = np.cos(th)
    f1[1:2 * k1n:2] = -np.sin(th)
    idx = np.arange(n2, dtype=np.int64)
    kk = k1[:, None, None] + n1 * idx[None, :, None]
    ph = (2.0 * np.pi / n_fft) * ((kk * idx[None, None, :]) % n_fft)
    g_re, g_im = np.cos(ph), -np.sin(ph)

    def blocks(re, im):
        out = np.zeros((k1p, 2 * n2, 2 * n2))
        out[:k1n, :n2, :n2] = re
        out[:k1n, :n2, n2:] = -im
        out[:k1n, n2:, :n2] = im
        out[:k1n, n2:, n2:] = re
        return out

    wgt = np.where((k1 == 0) | (k1 == n1h), 1.0, 2.0) / n_fft
    gf = blocks(g_re, g_im)
    gi = blocks(np.transpose(g_re, (0, 2, 1)) * wgt[:, None, None],
                -np.transpose(g_im, (0, 2, 1)) * wgt[:, None, None])
    as_bf16 = lambda t: jnp.asarray(t.astype(np.float32)).astype(BF16)
    return as_bf16(f1), as_bf16(np.ascontiguousarray(f1.T)), as_bf16(gf), as_bf16(gi)


_BATCHED = (((2,), (1,)), ((0,), (0,)))


def _fft_forward(x, f1, gf):
    seq_len, c = x.shape
    rows, n1h = f1.shape
    xt = jnp.swapaxes(x.astype(BF16).reshape(n1h, FFT_N2, c), 0, 1)
    a = lax.dot_general(jnp.broadcast_to(f1[None], (FFT_N2, rows, n1h)), xt, _BATCHED,
                        preferred_element_type=F32)
    at = jnp.swapaxes(a.astype(BF16), 0, 1).reshape(rows // 2, 2 * FFT_N2, c)
    return lax.dot_general(gf, at, _BATCHED, preferred_element_type=F32)


def _fft_inverse(y, f1t, gi):
    c = y.shape[2]
    n1h, rows = f1t.shape
    bt = lax.dot_general(gi, y, _BATCHED, preferred_element_type=F32)
    btt = jnp.swapaxes(bt.astype(BF16).reshape(rows, FFT_N2, c), 0, 1)
    yv = lax.dot_general(jnp.broadcast_to(f1t[None], (FFT_N2, n1h, rows)), btt, _BATCHED,
                         preferred_element_type=F32)
    return jnp.swapaxes(yv, 0, 1).reshape(n1h * FFT_N2, c)


def _hy_spec_kernel(h_ref, wf_ref, wb_ref, dl_ref, f1_ref, gf_ref, k_ref):
    n2 = FFT_N2
    seq_len = h_ref.shape[0]
    tc = wf_ref.shape[1]
    h3 = h_ref[...].astype(BF16)
    rid = lax.broadcasted_iota(jnp.int32, (seq_len, tc), 0)
    decay = jnp.exp(-(rid.astype(F32) * (1.0 / (seq_len - 1))) * dl_ref[...])
    hf = jnp.dot(h3, wf_ref[...].astype(BF16), preferred_element_type=F32) * decay
    hb = jnp.dot(h3, wb_ref[...].astype(BF16), preferred_element_type=F32) * decay
    norm = jnp.sum(jnp.abs(hf) + jnp.abs(hb), axis=0, keepdims=True) + 1e-6
    inv = 1.0 / norm
    hb = jnp.where(rid == 0, 0.0, hb * inv)
    x = _fft_forward(jnp.concatenate([hf * inv, hb], axis=1), f1_ref[...], gf_ref[...])
    xf, xb = x[:, :, 0:tc], x[:, :, tc:2 * tc]
    k_ref[...] = jnp.concatenate([xf[:, :n2] + xb[:, :n2], xf[:, n2:] - xb[:, n2:]], axis=1)


def _hy_spec(h3, w4f, w4b, deltas, f1, gf, tc):
    seq_len = h3.shape[0]
    dh = w4f.shape[1]
    k1p = gf.shape[0]
    rows, n1h = f1.shape
    return pl.pallas_call(
        _hy_spec_kernel,
        grid=(dh // tc,),
        in_specs=[
            pl.BlockSpec((seq_len, LANES), lambda j: (0, 0)),
            pl.BlockSpec((LANES, tc), lambda j: (0, j)),
            pl.BlockSpec((LANES, tc), lambda j: (0, j)),
            pl.BlockSpec((1, tc), lambda j: (0, j)),
            pl.BlockSpec((rows, n1h), lambda j: (0, 0)),
            pl.BlockSpec((k1p, 2 * FFT_N2, 2 * FFT_N2), lambda j: (0, 0, 0)),
        ],
        out_specs=pl.BlockSpec((k1p, 2 * FFT_N2, tc), lambda j: (0, 0, j)),
        out_shape=jax.ShapeDtypeStruct((k1p, 2 * FFT_N2, dh), F32),
        compiler_params=_params(1, 56),
        name="hy_spec",
    )(h3, w4f, w4b, deltas, f1, gf)


def _hy_conv_kernel(u_ref, x0_ref, k_ref, f1_ref, f1t_ref, gf_ref, gi_ref, bias_ref, o_ref):
    n2 = FFT_N2
    x = _fft_forward(u_ref[...], f1_ref[...], gf_ref[...])
    u = u_ref[...].astype(F32)
    kk = k_ref[...]
    xr, xi, kr, ki = x[:, :n2], x[:, n2:], kk[:, :n2], kk[:, n2:]
    y = jnp.concatenate([xr * kr - xi * ki, xr * ki + xi * kr], axis=1).astype(BF16)
    conv = _fft_inverse(y, f1t_ref[...], gi_ref[...])
    o_ref[...] = (x0_ref[...].astype(F32) * (conv + u * bias_ref[...])).astype(o_ref.dtype)


def _hy_conv(u3, x0c, kspec, f1, f1t, gf, gi, bias, tc):
    nb, seq_len, dh = u3.shape
    k1p = gf.shape[0]
    rows, n1h = f1.shape
    blk = pl.BlockSpec((None, seq_len, tc), lambda j, b: (b, 0, j))
    const = lambda shape: pl.BlockSpec(shape, lambda j, b: (0,) * len(shape), pipeline_mode=ONE_BUFFER)
    return pl.pallas_call(
        _hy_conv_kernel,
        grid=(dh // tc, nb),
        in_specs=[
            blk, blk,
            pl.BlockSpec((k1p, 2 * FFT_N2, tc), lambda j, b: (0, 0, j)),
            const((rows, n1h)), const((n1h, rows)),
            const((k1p, 2 * FFT_N2, 2 * FFT_N2)), const((k1p, 2 * FFT_N2, 2 * FFT_N2)),
            pl.BlockSpec((1, tc), lambda j, b: (0, j)),
        ],
        out_specs=blk,
        out_shape=jax.ShapeDtypeStruct((nb, seq_len, dh), BF16),
        compiler_params=_params(2, 56),
        name="hy_conv",
    )(u3, x0c, kspec, f1, f1t, gf, gi, bias)


def _outproj_kernel(ys_ref, yh_ref, w_ref, x_ref, posr_ref, cemb_ref, nwp_ref, g_ref, nwf_ref, sh_ref, sc_ref,
                    xo_ref, h_ref):
    tm, ds = ys_ref.shape
    rows = min(256, tm)
    for r0 in range(0, tm, rows):
        rs = slice(r0, r0 + rows)
        y = jnp.dot(ys_ref[rs, :], w_ref[0:ds, :], preferred_element_type=F32)
        y = y + jnp.dot(yh_ref[rs, :], w_ref[ds:, :], preferred_element_type=F32)
        xn = _add_pos(x_ref[rs, :], posr_ref, r0 // GRID_W, cemb_ref[...]) + g_ref[...] * (_rms(y) * nwp_ref[...])
        xo_ref[rs, :] = xn
        h_ref[rs, :] = (_rms(xn) * nwf_ref[...] * (1.0 + sc_ref[...]) + sh_ref[...]).astype(h_ref.dtype)


def _out_proj(ys, yh, w, x2, posr, cemb, nwp, g1, nwf, sh2, sc2, seq_len, tm):
    m, d = x2.shape
    ds = ys.shape[1]
    dh = yh.shape[1]
    tiles_per_seq = seq_len // tm
    row = lambda i: (i, 0)
    fixed = lambda i: (0, 0)
    per_batch = lambda i: (i // tiles_per_seq, 0, 0)
    return pl.pallas_call(
        _outproj_kernel,
        grid=(m // tm,),
        in_specs=[
            pl.BlockSpec((tm, ds), row),
            pl.BlockSpec((tm, dh), row),
            pl.BlockSpec((ds + dh, d), fixed, pipeline_mode=ONE_BUFFER),
            pl.BlockSpec((tm, d), row),
            pl.BlockSpec((tm // GRID_W, d // 2), lambda i: (i % tiles_per_seq, 0)),
            pl.BlockSpec(cemb.shape, fixed),
            pl.BlockSpec((1, d), fixed),
            pl.BlockSpec((None, 1, d), per_batch),
            pl.BlockSpec((1, d), fixed),
            pl.BlockSpec((None, 1, d), per_batch),
            pl.BlockSpec((None, 1, d), per_batch),
        ],
        out_specs=[pl.BlockSpec((tm, d), row), pl.BlockSpec((tm, d), row)],
        out_shape=[jax.ShapeDtypeStruct((m, d), F32), jax.ShapeDtypeStruct((m, d), BF16)],
        compiler_params=_params(1, 56),
        name="out_proj",
    )(ys, yh, w, x2, posr, cemb, nwp, g1, nwf, sh2, sc2)


def _ffn_step(h_ref, wg, wu, wd, x_ref, nw_ref, g_ref, o_ref):
    f = pl.program_id(1)
    h = h_ref[...]
    gate = jnp.dot(h, wg, preferred_element_type=F32)
    up = jnp.dot(h, wu, preferred_element_type=F32)
    act = (_silu(gate) * up).astype(BF16)

    @pl.when(f == 0)
    def _():
        o_ref[...] = jnp.zeros_like(o_ref)

    d = o_ref.shape[1]
    cw = min(512, d)
    for n0 in range(0, d, cw):
        o_ref[:, n0:n0 + cw] += jnp.dot(act, wd[:, n0:n0 + cw], preferred_element_type=F32)

    @pl.when(f == pl.num_programs(1) - 1)
    def _():
        tm = o_ref.shape[0]
        rows = min(256, tm)
        for r0 in range(0, tm, rows):
            y = o_ref[r0:r0 + rows, :]
            o_ref[r0:r0 + rows, :] = x_ref[r0:r0 + rows, :] + g_ref[...] * (_rms(y) * nw_ref[...])


def _ffn_first_kernel(h_ref, wg_ref, wu_ref, wd_ref, x_ref, nw_ref, g_ref, o_ref, wgb_ref, wub_ref, wdb_ref):
    wg = wg_ref[...].astype(BF16)
    wu = wu_ref[...].astype(BF16)
    wd = wd_ref[...].astype(BF16)
    wgb_ref[...] = wg
    wub_ref[...] = wu
    wdb_ref[...] = wd
    _ffn_step(h_ref, wg, wu, wd, x_ref, nw_ref, g_ref, o_ref)


def _ffn_rest_kernel(h_ref, wg_ref, wu_ref, wd_ref, x_ref, nw_ref, g_ref, o_ref):
    _ffn_step(h_ref, wg_ref[...], wu_ref[...], wd_ref[...], x_ref, nw_ref, g_ref, o_ref)


def _ffn(h2, wg, wu, wd, xn, nw, g2, seq_len, tm, tf_first, tf):
    m, d = xn.shape
    dff = wg.shape[1]
    tiles_per_seq = seq_len // tm
    n_rows = m // tm
    out, wgb, wub, wdb = pl.pallas_call(
        _ffn_first_kernel,
        grid=(1, dff // tf_first),
        in_specs=[
            pl.BlockSpec((tm, d), lambda i, f: (0, 0), pipeline_mode=ONE_BUFFER),
            pl.BlockSpec((d, tf_first), lambda i, f: (0, f)),
            pl.BlockSpec((d, tf_first), lambda i, f: (0, f)),
            pl.BlockSpec((tf_first, d), lambda i, f: (f, 0)),
            pl.BlockSpec((tm, d), lambda i, f: (0, 0), pipeline_mode=ONE_BUFFER),
            pl.BlockSpec((1, d), lambda i, f: (0, 0)),
            pl.BlockSpec((None, 1, d), lambda i, f: (0, 0, 0)),
        ],
        out_specs=[
            pl.BlockSpec((tm, d), lambda i, f: (0, 0)),
            pl.BlockSpec((d, tf_first), lambda i, f: (0, f)),
            pl.BlockSpec((d, tf_first), lambda i, f: (0, f)),
            pl.BlockSpec((tf_first, d), lambda i, f: (f, 0)),
        ],
        out_shape=[
            jax.ShapeDtypeStruct((m, d), F32),
            jax.ShapeDtypeStruct((d, dff), BF16),
            jax.ShapeDtypeStruct((d, dff), BF16),
            jax.ShapeDtypeStruct((dff, d), BF16),
        ],
        input_output_aliases={4: 0},
        compiler_params=_params(2, 58),
        name="ffn_first",
    )(h2, wg, wu, wd, xn, nw, g2)
    if n_rows == 1:
        return out
    return pl.pallas_call(
        _ffn_rest_kernel,
        grid=(n_rows - 1, dff // tf),
        in_specs=[
            pl.BlockSpec((tm, d), lambda i, f: (i + 1, 0)),
            pl.BlockSpec((d, tf), lambda i, f: (0, f)),
            pl.BlockSpec((d, tf), lambda i, f: (0, f)),
            pl.BlockSpec((tf, d), lambda i, f: (f, 0)),
            pl.BlockSpec((tm, d), lambda i, f: (i + 1, 0), pipeline_mode=ONE_BUFFER),
            pl.BlockSpec((1, d), lambda i, f: (0, 0)),
            pl.BlockSpec((None, 1, d), lambda i, f: ((i + 1) // tiles_per_seq, 0, 0)),
        ],
        out_specs=pl.BlockSpec((tm, d), lambda i, f: (i + 1, 0)),
        out_shape=jax.ShapeDtypeStruct((m, d), F32),
        input_output_aliases={4: 0},
        compiler_params=_params(2, 58),
        name="ffn",
    )(h2, wgb, wub, wdb, out, nw, g2)


def _sincos_tables(rows, cols, dim):
    qd = dim // 4
    omega = 1.0 / (POS_THETA ** (jnp.arange(qd, dtype=F32) / qd))
    r = jnp.arange(rows, dtype=F32)[:, None] * omega
    cc = jnp.arange(cols, dtype=F32)[:, None] * omega
    r_emb = jnp.concatenate([jnp.sin(r), jnp.cos(r)], -1)
    c_emb = jnp.concatenate([jnp.sin(cc), jnp.cos(cc)], -1)
    return r_emb, c_emb


def _filter_features_t(seq_len, n_bands):
    t = jnp.linspace(0.0, 1.0, seq_len, dtype=F32)[:, None]
    w = 2.0 * math.pi * jnp.arange(seq_len, dtype=F32)[:, None] / seq_len
    fb = jnp.linspace(1e-4, n_bands - 1, n_bands, dtype=F32)[None]
    zpos = jnp.concatenate([t, jnp.cos(fb * w), -jnp.sin(fb * w)], -1)
    emb = zpos.shape[1]
    return jnp.pad(zpos, ((0, 0), (0, LANES - emb))).T


def _pad_rows(a, rows):
    return jnp.pad(a, ((0, rows - a.shape[0]), (0, 0)))


def kernel(x, c, ctx, c_ctx, w_ada, b_ada, norm_mix_pre, norm_mix_post, norm_ffn_pre, norm_ffn_post,
           w_in, ssd_conv_w, ssd_conv_b, ssd_a_log, ssd_dt_bias, ssd_d, ssd_norm,
           hy_conv_w, hy_conv_b, hy_w1, hy_b1, hy_w2, hy_b2, hy_w3, hy_b3, hy_w4, hy_freq, hy_bias,
           w_out, w_gate, w_up, w_down):
    nb, seq_len, d = x.shape
    ctx_len = ctx.shape[1]
    assert w_ada.shape[0] == 1, "single layer"
    n_heads = ssd_d.shape[1]
    d_ssd = n_heads * SSD_HEAD_DIM
    d_xbc = d_ssd + 2 * SSD_GROUPS * SSD_STATE
    dh = hy_bias.shape[1]
    assert w_in.shape[2] == d_ssd + d_xbc + 2 * n_heads + 3 * dh
    assert n_heads <= LANES and nb + 1 <= 8
    assert seq_len % (GRID_W * 8) == 0 and seq_len % FFT_N2 == 0
    m = nb * seq_len

    crows = jnp.concatenate([c, c_ctx[None, :]], axis=0)
    mod = _ada(crows, w_ada[0], b_ada[0])
    part = lambda r0, r1, k: mod[r0:r1, k * d:(k + 1) * d][:, None, :]
    sh1, sc1, g1, sh2, sc2, g2 = (part(0, nb, k) for k in range(6))
    csh1 = jnp.broadcast_to(part(nb, nb + 1, 0), (nb, 1, d))
    csc1 = jnp.broadcast_to(part(nb, nb + 1, 1), (nb, 1, d))

    w_t = jnp.transpose(w_in[0])
    o_xbc = d_ssd
    o_dt = o_xbc + d_xbc
    o_hy = o_dt + 2 * n_heads
    tn_in = 512
    assert o_dt % tn_in == 0 and (3 * dh) % tn_in == 0 and o_xbc % tn_in == 0 and o_hy % 16 == 0
    n_left = o_dt // tn_in
    main_off = lambda j: jnp.where(j < n_left, j * tn_in, o_hy + (j - n_left) * tn_in)
    ctx_off = lambda j: o_xbc + j * tn_in
    pad_dt = lambda rows: jnp.pad(rows, ((0, LANES - n_heads), (0, 0)))
    w_dt_t = jnp.concatenate([pad_dt(w_t[o_dt:o_dt + n_heads]), pad_dt(w_t[o_dt + n_heads:o_hy])],
                             axis=0)

    r_emb, c_emb = _sincos_tables(seq_len // GRID_W, GRID_W, d)
    posr = r_emb
    nmp = norm_mix_pre[0][None, :]

    tm_in = min(1024, seq_len)
    n_main = o_dt + 3 * dh
    tn_rest = n_main // 4 if n_main % (4 * LANES) == 0 else tn_in
    proj, dt_lat = _in_proj(x.reshape(m, d), posr, c_emb, nmp, sh1, sc1, w_t, main_off, n_main // tn_in,
                            w_dt_t, seq_len, tm_in, tn_in, True, tn_rest)
    tm_ctx = min(256, ctx_len)
    xbc_ctx, dt_ctx = _in_proj(ctx.reshape(nb * ctx_len, d), jnp.zeros((8, d // 2), F32), c_emb, nmp, csh1, csc1,
                               w_t, ctx_off, d_xbc // tn_in, w_dt_t, ctx_len, tm_ctx, tn_in, False)
    proj3 = proj.reshape(nb, seq_len, -1)

    cw8 = _pad_rows(ssd_conv_w[0], 8)
    cb1 = ssd_conv_b[0][None, :]
    u_lat = _ssd_prep(proj3, d_ssd, d_xbc, cw8, cb1, 512)
    u_ctx = _ssd_prep(xbc_ctx.reshape(nb, ctx_len, d_xbc), 0, d_xbc, cw8, cb1, 512)
    pad_heads = lambda a: jnp.pad(a, ((0, 0), (0, LANES - n_heads)))[:, None, :]
    expand = (jnp.arange(LANES)[:, None] == (jnp.arange(d_ssd)[None, :] // SSD_HEAD_DIM)).astype(BF16)
    y_ssd = _ssd(u_lat, dt_lat.reshape(nb, seq_len, 2 * LANES), u_ctx, dt_ctx.reshape(nb, ctx_len, 2 * LANES),
                 proj3, pad_heads(ssd_a_log[0]), pad_heads(ssd_dt_bias[0]),
                 jnp.repeat(ssd_d[0], SSD_HEAD_DIM)[None, :], ssd_norm[0][None, :], expand)

    x0c, u_hy = _hy_prep(proj3, d_ssd + d_xbc, dh, _pad_rows(hy_conv_w[0], 8), hy_conv_b[0][None, :], 256)
    n_bands = (hy_w1.shape[1] - 1) // 2
    zt = _filter_features_t(seq_len, n_bands)
    w1t = jnp.pad(hy_w1[0].T, ((0, 0), (0, LANES - hy_w1.shape[1])))
    b3 = jnp.stack([hy_b1[0], hy_b2[0], hy_b3[0]], axis=1)
    h3 = _hy_mlp(zt, w1t, hy_w2[0].T, hy_w3[0].T, b3, hy_freq[0][:, None])
    w4 = _pad_rows(hy_w4[0], LANES)
    max_decay = math.log(HY_TARGET) / HY_FAST_PCT
    min_decay = math.log(HY_TARGET) / HY_SLOW_PCT
    deltas = jnp.abs(jnp.linspace(min_decay, max_decay, dh, dtype=F32))[None, :]
    f1, f1t, gf, gi = _fft_tables(seq_len)
    kspec = _hy_spec(h3, w4[:, :dh], w4[:, dh:], deltas, f1, gf, LANES)
    y_hy = _hy_conv(u_hy, x0c, kspec, f1, f1t, gf, gi, hy_bias[0][None, :], LANES)

    xn, h2 = _out_proj(y_ssd.reshape(m, d_ssd), y_hy.reshape(m, dh), w_out[0].astype(BF16), x.reshape(m, d),
                       posr, c_emb, norm_mix_post[0][None, :], g1, norm_ffn_pre[0][None, :], sh2, sc2, seq_len,
                       min(512, seq_len))
    out = _ffn(h2, w_gate[0], w_up[0], w_down[0], xn, norm_ffn_post[0][None, :], g2, seq_len,
               min(1024, seq_len), 256, 512)
    return out.reshape(nb, seq_len, d)
```

```python
import functools
import math

import numpy as np
import jax
import jax.numpy as jnp
from jax import lax
from jax.experimental import pallas as pl
from jax.experimental.pallas import tpu as pltpu

F32 = jnp.float32
BF16 = jnp.bfloat16
HIGHEST = lax.Precision.HIGHEST

RMS_EPS = 1e-6
POS_THETA = 10000.0
GRID_W = 64
SSD_HEAD_DIM = 64
SSD_GROUPS = 2
SSD_STATE = 128
SSD_CHUNK = 128
SSD_STEP_CHUNKS = 2
HY_TARGET = 1e-2
HY_FAST_PCT = 0.3
HY_SLOW_PCT = 1.5
FFT_N2 = 64
LANES = 128
MIB = 1024 * 1024
ONE_BUFFER = pl.Buffered(1)
_NT_DIMS = (((1,), (1,)), ((), ()))
_NEG_BIG = -1e30


def _params(n_axes, vmem_mib):
    return pltpu.CompilerParams(
        dimension_semantics=("arbitrary",) * n_axes,
        vmem_limit_bytes=vmem_mib * MIB,
    )


def _silu(v):
    return v * (1.0 / (1.0 + jnp.exp(-v)))


def _softplus(v):
    return jnp.maximum(v, 0.0) + jnp.log(1.0 + jnp.exp(-jnp.abs(v)))


def _rms(v):
    return v * lax.rsqrt(jnp.mean(v * v, axis=-1, keepdims=True) + RMS_EPS)


def _add_pos(x, remb_ref, g0, cemb):
    rows, d = x.shape
    half = d // 2
    lo = jnp.concatenate(
        [x[g * GRID_W:(g + 1) * GRID_W, :half] + remb_ref[g0 + g:g0 + g + 1, :] for g in range(rows // GRID_W)],
        axis=0)
    hi = (x[:, half:].reshape(rows // GRID_W, GRID_W, half) + cemb[None]).reshape(rows, half)
    return jnp.concatenate([lo, hi], axis=1)


def _ada_kernel(c_ref, w_ref, b_ref, o_ref, s_scr):
    nr, d, _ = c_ref.shape
    tn = w_ref.shape[1]
    nt = tn // LANES
    kc = 64

    @pl.when(pl.program_id(0) == 0)
    def _():
        for r in range(nr):
            for k0 in range(0, d, 512):
                s_scr[r, k0:k0 + 512, :] = _silu(c_ref[r, k0:k0 + 512, :])

    acc = [[jnp.zeros((8, LANES), F32) for _ in range(nt)] for _ in range(nr)]
    for k0 in range(0, d, kc):
        s = [s_scr[r, k0:k0 + kc, :] for r in range(nr)]
        for t in range(nt):
            w = w_ref[k0:k0 + kc, t * LANES:(t + 1) * LANES]
            for r in range(nr):
                acc[r][t] = acc[r][t] + jnp.sum((w * s[r]).reshape(kc // 8, 8, LANES), axis=0)
    rows = [jnp.concatenate([jnp.sum(a, axis=0, keepdims=True) for a in acc[r]], axis=1) for r in range(nr)]
    rows.append(jnp.zeros((o_ref.shape[0] - nr, tn), F32))
    o_ref[...] = jnp.concatenate(rows, axis=0) + b_ref[...]


def _ada(crows, w_ada, b_ada):
    nr = crows.shape[0]
    d, n = w_ada.shape
    tn = 1024
    c_lanes = jnp.broadcast_to(crows[:, :, None], (nr, d, LANES))
    return pl.pallas_call(
        _ada_kernel,
        grid=(n // tn,),
        in_specs=[
            pl.BlockSpec((nr, d, LANES), lambda j: (0, 0, 0), pipeline_mode=ONE_BUFFER),
            pl.BlockSpec((d, tn), lambda j: (0, j)),
            pl.BlockSpec((1, tn), lambda j: (0, j)),
        ],
        out_specs=pl.BlockSpec((8, tn), lambda j: (0, j)),
        out_shape=jax.ShapeDtypeStruct((8, n), F32),
        scratch_shapes=[pltpu.VMEM((nr, d, LANES), F32)],
        compiler_params=_params(1, 40),
        name="ada",
    )(c_lanes, w_ada, b_ada.reshape(1, n))


def _inproj_body(x_ref, posr_ref, cemb_ref, nw_ref, sh_ref, sc_ref, get_w, wdt_ref, o_ref, dt_ref, h_scr, with_pos):
    first = pl.program_id(1) == 0

    @pl.when(first)
    def _():
        tm = x_ref.shape[0]
        rows = min(256, tm)
        wdt = wdt_ref[...].astype(BF16)
        w = get_w()
        for r0 in range(0, tm, rows):
            xf = x_ref[r0:r0 + rows, :]
            if with_pos:
                xf = _add_pos(xf, posr_ref, r0 // GRID_W, cemb_ref[...])
            h = _rms(xf) * nw_ref[...] * (1.0 + sc_ref[...]) + sh_ref[...]
            hb = h.astype(BF16)
            h_scr[r0:r0 + rows, :] = hb
            dt_ref[r0:r0 + rows, :] = lax.dot_general(hb, wdt, _NT_DIMS, preferred_element_type=F32)
            o_ref[r0:r0 + rows, :] = jnp.dot(hb, w, preferred_element_type=F32).astype(o_ref.dtype)

    @pl.when(jnp.logical_not(first))
    def _():
        o_ref[...] = jnp.dot(h_scr[...], get_w(), preferred_element_type=F32).astype(o_ref.dtype)


def _inproj_kernel(x_ref, posr_ref, cemb_ref, nw_ref, sh_ref, sc_ref, w_ref, wdt_ref, o_ref, dt_ref, h_scr,
                   *, with_pos):
    _inproj_body(x_ref, posr_ref, cemb_ref, nw_ref, sh_ref, sc_ref, lambda: w_ref[...], wdt_ref, o_ref, dt_ref,
                 h_scr, with_pos)


def _w_in_cast_kernel(w_ref, o_ref):
    o_ref[...] = w_ref[...].T.astype(BF16)


def _w_in_cast(w_t, row_off, n_tiles, tn):
    d = w_t.shape[1]
    return pl.pallas_call(
        _w_in_cast_kernel,
        grid=(n_tiles,),
        in_specs=[pl.BlockSpec((pl.Element(tn), pl.Element(d)), lambda j: (pl.multiple_of(row_off(j), 8), 0))],
        out_specs=pl.BlockSpec((d, tn), lambda j: (0, j)),
        out_shape=jax.ShapeDtypeStruct((d, n_tiles * tn), BF16),
        compiler_params=_params(1, 40),
        name="w_in_cast",
    )(w_t)


def _in_proj(x2, posr, cemb, nw, sh, sc, w_bf, col0, n, wdt_t, seq_len, tm, tn, with_pos):
    m, d = x2.shape
    ndt = wdt_t.shape[0]
    half = d // 2
    tiles_per_seq = seq_len // tm
    pos_rows = tm // GRID_W if with_pos else posr.shape[0]
    j0 = col0 // tn
    return pl.pallas_call(
        functools.partial(_inproj_kernel, with_pos=with_pos),
        grid=(m // tm, n // tn),
        in_specs=[
            pl.BlockSpec((tm, d), lambda i, j: (i, 0)),
            pl.BlockSpec((pos_rows, half), lambda i, j: ((i % tiles_per_seq) if with_pos else 0, 0)),
            pl.BlockSpec(cemb.shape, lambda i, j: (0, 0), pipeline_mode=ONE_BUFFER),
            pl.BlockSpec((1, d), lambda i, j: (0, 0)),
            pl.BlockSpec((None, 1, d), lambda i, j: (i // tiles_per_seq, 0, 0)),
            pl.BlockSpec((None, 1, d), lambda i, j: (i // tiles_per_seq, 0, 0)),
            pl.BlockSpec((d, tn), lambda i, j: (0, j0 + j)),
            pl.BlockSpec((ndt, d), lambda i, j: (0, 0), pipeline_mode=ONE_BUFFER),
        ],
        out_specs=[pl.BlockSpec((tm, tn), lambda i, j: (i, j)), pl.BlockSpec((tm, ndt), lambda i, j: (i, 0))],
        out_shape=[jax.ShapeDtypeStruct((m, n), BF16), jax.ShapeDtypeStruct((m, ndt), F32)],
        scratch_shapes=[pltpu.VMEM((tm, d), BF16)],
        compiler_params=_params(2, 58),
        name="in_proj",
    )(x2, posr, cemb, nw, sh, sc, w_bf, wdt_t)


CONV_ROWS = 256


def _shift_matrices(rows):
    return jnp.stack([jnp.eye(rows, k=-1, dtype=BF16), jnp.eye(rows, k=1, dtype=BF16)])


def _conv3_chunk(x_ref, r0, rows, seq_len, w, b, shift_ref):
    cur_b = x_ref[r0:r0 + rows, :]
    tc = cur_b.shape[1]
    up = jnp.dot(shift_ref[0], cur_b, preferred_element_type=F32)
    down = jnp.dot(shift_ref[1], cur_b, preferred_element_type=F32)
    rid = lax.broadcasted_iota(jnp.int32, (8, tc), 0)
    if r0 > 0:
        prev_row = x_ref[r0 - 16:r0, :].astype(F32)[15:16, :]
        up = jnp.concatenate([jnp.where(rid == 0, prev_row, up[0:8]), up[8:]], axis=0)
    if r0 + rows < seq_len:
        next_row = x_ref[r0 + rows:r0 + rows + 16, :].astype(F32)[0:1, :]
        down = jnp.concatenate([down[:rows - 8], jnp.where(rid == 7, next_row, down[rows - 8:])], axis=0)
    return up * w[0:1, :] + cur_b.astype(F32) * w[1:2, :] + down * w[2:3, :] + b


def _ssd_prep_kernel(x_ref, w_ref, b_ref, shift_ref, o_ref):
    seq_len = x_ref.shape[0]
    rows = shift_ref.shape[1]
    w = w_ref[...]
    b = b_ref[...]
    for r0 in range(0, seq_len, rows):
        v = _conv3_chunk(x_ref, r0, rows, seq_len, w, b, shift_ref)
        o_ref[r0:r0 + rows, :] = _silu(v).astype(o_ref.dtype)


def _ssd_prep(proj3, col0, width, w8, b1, tc):
    nb, seq_len, _ = proj3.shape
    rows = min(CONV_ROWS, seq_len)
    j0 = col0 // tc
    return pl.pallas_call(
        _ssd_prep_kernel,
        grid=(nb, width // tc),
        in_specs=[
            pl.BlockSpec((None, seq_len, tc), lambda b, j: (b, 0, j0 + j)),
            pl.BlockSpec((8, tc), lambda b, j: (0, j)),
            pl.BlockSpec((1, tc), lambda b, j: (0, j)),
            pl.BlockSpec((2, rows, rows), lambda b, j: (0, 0, 0)),
        ],
        out_specs=pl.BlockSpec((None, seq_len, tc), lambda b, j: (b, 0, j)),
        out_shape=jax.ShapeDtypeStruct((nb, seq_len, width), BF16),
        compiler_params=_params(2, 40),
        name="ssd_prep",
    )(proj3, w8, b1, _shift_matrices(rows))


def _hy_prep_kernel(x0_ref, x1_ref, v_ref, w0_ref, w1_ref, wv_ref, b0_ref, b1_ref, bv_ref, shift_ref,
                    x0c_ref, u_ref):
    seq_len = x0_ref.shape[0]
    rows = shift_ref.shape[1]
    w0, w1, wv = w0_ref[...], w1_ref[...], wv_ref[...]
    b0, b1, bv = b0_ref[...], b1_ref[...], bv_ref[...]
    for r0 in range(0, seq_len, rows):
        x0c_ref[r0:r0 + rows, :] = _conv3_chunk(x0_ref, r0, rows, seq_len, w0, b0, shift_ref).astype(x0c_ref.dtype)
        x1c = _conv3_chunk(x1_ref, r0, rows, seq_len, w1, b1, shift_ref)
        vc = _conv3_chunk(v_ref, r0, rows, seq_len, wv, bv, shift_ref)
        u_ref[r0:r0 + rows, :] = (vc * x1c).astype(u_ref.dtype)


def _hy_prep(proj3, col0, dh, w8, b1, tc):
    nb, seq_len, _ = proj3.shape
    rows = min(CONV_ROWS, seq_len)
    j0 = col0 // tc
    nj = dh // tc
    x_spec = lambda k: pl.BlockSpec((None, seq_len, tc), lambda b, j: (b, 0, j0 + k * nj + j))
    w_spec = lambda k: pl.BlockSpec((8, tc), lambda b, j: (0, k * nj + j))
    b_spec = lambda k: pl.BlockSpec((1, tc), lambda b, j: (0, k * nj + j))
    o_spec = pl.BlockSpec((None, seq_len, tc), lambda b, j: (b, 0, j))
    return pl.pallas_call(
        _hy_prep_kernel,
        grid=(nb, nj),
        in_specs=[x_spec(0), x_spec(1), x_spec(2), w_spec(0), w_spec(1), w_spec(2),
                  b_spec(0), b_spec(1), b_spec(2),
                  pl.BlockSpec((2, rows, rows), lambda b, j: (0, 0, 0))],
        out_specs=[o_spec, o_spec],
        out_shape=[jax.ShapeDtypeStruct((nb, seq_len, dh), BF16)] * 2,
        compiler_params=_params(2, 48),
        name="hy_prep",
    )(proj3, proj3, proj3, w8, w8, w8, b1, b1, b1, _shift_matrices(rows))


def _ssd_kernel(u_ref, dt_ref, uc_ref, dtc_ref, z_ref, alog_ref, dtb_ref, dsk_ref, nw_ref, e_ref,
                o_ref, h_scr, yf_scr, *, n_ctx, n_lat):
    q = SSD_CHUNK
    nsub = SSD_STEP_CHUNKS
    d = pl.program_id(0)
    s = pl.program_id(1)
    is_ctx = s < n_ctx
    t = jnp.maximum(s - n_ctx, 0)
    cidx = jnp.where(d == 0, t, n_lat - 1 - t)
    fwd = d == 0

    @pl.when(s == 0)
    def _():
        h_scr[...] = jnp.zeros_like(h_scr)

    row = lax.broadcasted_iota(jnp.int32, (q, q), 0)
    col = lax.broadcasted_iota(jnp.int32, (q, q), 1)
    tri = (jnp.where(fwd, row, col) >= jnp.where(fwd, col, row)).astype(F32)
    nb = u_ref.shape[0]

    def scan_block(b, xbc_ref, dtr_ref, need_y):
        sub = lambda v, k: v[k * q:(k + 1) * q]
        y_sub = [None] * nsub
        for k in range(nsub):
            rk = nsub - 1 - k
            u = jnp.where(fwd, sub(xbc_ref[b], k), sub(xbc_ref[b], rk))
            dtr = jnp.where(fwd, sub(dtr_ref[b], k), sub(dtr_ref[b], rk))
            y_sub[k] = _ssd_chunk(b, u, dtr, tri, alog_ref, dtb_ref, e_ref, h_scr, need_y)
        if need_y:
            return jnp.concatenate([jnp.where(fwd, y_sub[k], y_sub[nsub - 1 - k]) for k in range(nsub)], axis=0)

    @pl.when(is_ctx)
    def _():
        for b in range(nb):
            scan_block(b, uc_ref, dtc_ref, False)

    @pl.when(jnp.logical_and(jnp.logical_not(is_ctx), fwd))
    def _():
        for b in range(nb):
            yf_scr[b, cidx] = scan_block(b, u_ref, dt_ref, True)

    @pl.when(jnp.logical_and(jnp.logical_not(is_ctx), jnp.logical_not(fwd)))
    def _():
        d_ssd = o_ref.shape[2]
        for b in range(nb):
            y = scan_block(b, u_ref, dt_ref, True)
            xs = u_ref[b][:, :d_ssd].astype(F32)
            ytot = yf_scr[b, cidx] + y + dsk_ref[...] * xs
            gated = ytot * _silu(z_ref[b].astype(F32))
            o_ref[b] = (_rms(gated) * nw_ref[...]).astype(o_ref.dtype)


def _ssd_chunk(b, u, dtr, tri, alog_ref, dtb_ref, e_ref, h_scr, need_y):
    q = SSD_CHUNK
    d_ssd = h_scr.shape[2]
    gw = d_ssd // SSD_GROUPS

    dt = _softplus(dtr + dtb_ref[...])
    a = dt * (-jnp.exp(alog_ref[...]))
    cum = jnp.dot(tri, a, precision=HIGHEST, preferred_element_type=F32)
    mask_add = ((1.0 - tri) * _NEG_BIG).astype(BF16)
    cum_t = cum.T
    tot = jnp.sum(a, axis=0, keepdims=True)

    stack = jnp.concatenate(
        [dt, jnp.exp(cum), jnp.exp(tot - cum), jnp.broadcast_to(jnp.exp(tot), (8, LANES))], axis=0)
    ex = jnp.dot(stack.astype(BF16), e_ref[...], preferred_element_type=F32)
    dt_x = ex[0:q]
    ecum_x = ex[q:2 * q]
    edec_x = ex[2 * q:3 * q]
    etot_x = ex[3 * q:3 * q + 1]

    xs = u[:, :d_ssd].astype(F32)
    xdt = xs * dt_x
    xdt_b = xdt.astype(BF16)
    xdw_b = (xdt * edec_x).astype(BF16)
    lane_lo = lax.broadcasted_iota(jnp.int32, (q, LANES), 1) < SSD_HEAD_DIM
    heads_per_group = gw // SSD_HEAD_DIM

    y_parts = []
    for g in range(SSD_GROUPS):
        bg = u[:, d_ssd + g * SSD_STATE:d_ssd + (g + 1) * SSD_STATE]
        cg = u[:, d_ssd + (SSD_GROUPS + g) * SSD_STATE:d_ssd + (SSD_GROUPS + g + 1) * SSD_STATE]
        h_prev = h_scr[b, :, g * gw:(g + 1) * gw]
        if need_y:
            scores = lax.dot_general(cg, bg, _NT_DIMS, preferred_element_type=F32).astype(BF16)
            y_off = jnp.dot(cg, h_prev.astype(BF16), preferred_element_type=F32)
        for j in range(heads_per_group // 2 if need_y else 0):
            c0 = g * gw + j * LANES
            xp = xdt_b[:, c0:c0 + LANES]
            acc = None
            for hh in range(2):
                h = g * heads_per_group + 2 * j + hh
                diff = (cum[:, h:h + 1] - cum_t[h:h + 1, :]).astype(BF16)
                m_h = scores * jnp.exp(diff + mask_add)
                x_h = jnp.where(lane_lo if hh == 0 else jnp.logical_not(lane_lo), xp, jnp.zeros_like(xp))
                part = jnp.dot(m_h, x_h, preferred_element_type=F32)
                acc = part if acc is None else acc + part
            y_parts.append(acc + y_off[:, j * LANES:(j + 1) * LANES] * ecum_x[:, c0:c0 + LANES])
        upd = lax.dot_general(bg, xdw_b[:, g * gw:(g + 1) * gw], (((0,), (0,)), ((), ())),
                              preferred_element_type=F32)
        h_scr[b, :, g * gw:(g + 1) * gw] = etot_x[:, g * gw:(g + 1) * gw] * h_prev + upd
    return jnp.concatenate(y_parts, axis=1) if need_y else None


def _ssd(u_lat, dt_lat, u_ctx, dt_ctx, proj3, alog, dtb, dsk, nw, expand):
    nb, seq_len, d_xbc = u_lat.shape
    ctx_len = u_ctx.shape[1]
    q = SSD_CHUNK * SSD_STEP_CHUNKS
    assert seq_len % q == 0 and ctx_len % q == 0
    n_lat = seq_len // q
    n_ctx = ctx_len // q
    d_ssd = dsk.shape[1]

    def lat_idx(d, s):
        t = jnp.maximum(s - n_ctx, 0)
        return jnp.where(d == 0, t, n_lat - 1 - t)

    def ctx_idx(d, s):
        t = jnp.minimum(s, n_ctx - 1)
        return jnp.where(d == 0, t, n_ctx - 1 - t)

    def out_idx(d, s):
        return jnp.where(d == 0, n_lat - 1, lat_idx(d, s))

    return pl.pallas_call(
        functools.partial(_ssd_kernel, n_ctx=n_ctx, n_lat=n_lat),
        grid=(2, n_ctx + n_lat),
        in_specs=[
            pl.BlockSpec((nb, q, d_xbc), lambda d, s: (0, lat_idx(d, s), 0)),
            pl.BlockSpec((nb, q, LANES), lambda d, s: (0, lat_idx(d, s), d)),
            pl.BlockSpec((nb, q, d_xbc), lambda d, s: (0, ctx_idx(d, s), 0)),
            pl.BlockSpec((nb, q, LANES), lambda d, s: (0, ctx_idx(d, s), d)),
            pl.BlockSpec((nb, q, d_ssd), lambda d, s: (0, lat_idx(d, s), 0)),
            pl.BlockSpec((None, 1, LANES), lambda d, s: (d, 0, 0)),
            pl.BlockSpec((None, 1, LANES), lambda d, s: (d, 0, 0)),
            pl.BlockSpec((1, d_ssd), lambda d, s: (0, 0)),
            pl.BlockSpec((1, d_ssd), lambda d, s: (0, 0)),
            pl.BlockSpec((LANES, d_ssd), lambda d, s: (0, 0)),
        ],
        out_specs=pl.BlockSpec((nb, q, d_ssd), lambda d, s: (0, out_idx(d, s), 0)),
        out_shape=jax.ShapeDtypeStruct((nb, seq_len, d_ssd), BF16),
        scratch_shapes=[
            pltpu.VMEM((nb, SSD_STATE, d_ssd), F32),
            pltpu.VMEM((nb, n_lat, q, d_ssd), F32),
        ],
        compiler_params=_params(2, 56),
        name="ssd",
    )(u_lat, dt_lat, u_ctx, dt_ctx, proj3, alog, dtb, dsk, nw, expand)


def _hy_mlp_kernel(zt_ref, w1_ref, w2_ref, w3_ref, b_ref, fr_ref, o_ref):
    fr = fr_ref[...]
    b = b_ref[...]
    h = jnp.sin(fr * (jnp.dot(w1_ref[...], zt_ref[...], precision=HIGHEST, preferred_element_type=F32)
                      + b[:, 0:1]))
    h = jnp.sin(fr * (jnp.dot(w2_ref[...], h, precision=HIGHEST, preferred_element_type=F32) + b[:, 1:2]))
    h = jnp.sin(fr * (jnp.dot(w3_ref[...], h, precision=HIGHEST, preferred_element_type=F32) + b[:, 2:3]))
    hid, seq_len = h.shape
    hp = jnp.concatenate([h, jnp.zeros((LANES - hid, seq_len), F32)], axis=0)
    o_ref[...] = hp.T


def _hy_mlp(zt, w1t, w2t, w3t, b3, fr):
    seq_len = zt.shape[1]
    return pl.pallas_call(
        _hy_mlp_kernel,
        out_shape=jax.ShapeDtypeStruct((seq_len, LANES), F32),
        compiler_params=pltpu.CompilerParams(vmem_limit_bytes=40 * MIB),
        name="hy_mlp",
    )(zt, w1t, w2t, w3t, b3, fr)


def _fft_tables(seq_len):
    n_fft = 2 * seq_len
    n2 = FFT_N2
    n1 = n_fft // n2
    n1h = n1 // 2
    k1n = n1h + 1
    k1p = -(-k1n // 4) * 4
    k1 = np.arange(k1n, dtype=np.int64)
    th = (2.0 * np.pi / n1) * ((k1[:, None] * np.arange(n1h, dtype=np.int64)[None, :]) % n1)
    f1 = np.zeros((2 * k1p, n1h))
    f1[0:2 * k1n:2] = np.cos(th)
    f1[1:2 * k1n:2] = -np.sin(th)
    idx = np.arange(n2, dtype=np.int64)
    kk = k1[:, None, None] + n1 * idx[None, :, None]
    ph = (2.0 * np.pi / n_fft) * ((kk * idx[None, None, :]) % n_fft)
    g_re, g_im = np.cos(ph), -np.sin(ph)

    def blocks(re, im):
        out = np.zeros((k1p, 2 * n2, 2 * n2))
        out[:k1n, :n2, :n2] = re
        out[:k1n, :n2, n2:] = -im
        out[:k1n, n2:, :n2] = im
        out[:k1n, n2:, n2:] = re
        return out

    wgt = np.where((k1 == 0) | (k1 == n1h), 1.0, 2.0) / n_fft
    gf = blocks(g_re, g_im)
    gi = blocks(np.transpose(g_re, (0, 2, 1)) * wgt[:, None, None],
                -np.transpose(g_im, (0, 2, 1)) * wgt[:, None, None])
    as_bf16 = lambda t: jnp.asarray(t.astype(np.float32)).astype(BF16)
    return as_bf16(f1), as_bf16(np.ascontiguousarray(f1.T)), as_bf16(gf), as_bf16(gi)


_BATCHED = (((2,), (1,)), ((0,), (0,)))


def _fft_forward(x, f1, gf):
    seq_len, c = x.shape
    rows, n1h = f1.shape
    xt = jnp.swapaxes(x.astype(BF16).reshape(n1h, FFT_N2, c), 0, 1)
    a = lax.dot_general(jnp.broadcast_to(f1[None], (FFT_N2, rows, n1h)), xt, _BATCHED,
                        preferred_element_type=F32)
    at = jnp.swapaxes(a.astype(BF16), 0, 1).reshape(rows // 2, 2 * FFT_N2, c)
    return lax.dot_general(gf, at, _BATCHED, preferred_element_type=F32)


def _fft_inverse(y, f1t, gi):
    c = y.shape[2]
    n1h, rows = f1t.shape
    bt = lax.dot_general(gi, y, _BATCHED, preferred_element_type=F32)
    btt = jnp.swapaxes(bt.astype(BF16).reshape(rows, FFT_N2, c), 0, 1)
    yv = lax.dot_general(jnp.broadcast_to(f1t[None], (FFT_N2, n1h, rows)), btt, _BATCHED,
                         preferred_element_type=F32)
    return jnp.swapaxes(yv, 0, 1).reshape(n1h * FFT_N2, c)


def _hy_spec_kernel(h_ref, wf_ref, wb_ref, dl_ref, f1_ref, gf_ref, k_ref):
    n2 = FFT_N2
    seq_len = h_ref.shape[0]
    tc = wf_ref.shape[1]
    h3 = h_ref[...].astype(BF16)
    rid = lax.broadcasted_iota(jnp.int32, (seq_len, tc), 0)
    decay = jnp.exp(-(rid.astype(F32) * (1.0 / (seq_len - 1))) * dl_ref[...])
    hf = jnp.dot(h3, wf_ref[...].astype(BF16), preferred_element_type=F32) * decay
    hb = jnp.dot(h3, wb_ref[...].astype(BF16), preferred_element_type=F32) * decay
    norm = jnp.sum(jnp.abs(hf) + jnp.abs(hb), axis=0, keepdims=True) + 1e-6
    inv = 1.0 / norm
    hb = jnp.where(rid == 0, 0.0, hb * inv)
    x = _fft_forward(jnp.concatenate([hf * inv, hb], axis=1), f1_ref[...], gf_ref[...])
    xf, xb = x[:, :, 0:tc], x[:, :, tc:2 * tc]
    k_ref[...] = jnp.concatenate([xf[:, :n2] + xb[:, :n2], xf[:, n2:] - xb[:, n2:]], axis=1)


def _hy_spec(h3, w4f, w4b, deltas, f1, gf, tc):
    seq_len = h3.shape[0]
    dh = w4f.shape[1]
    k1p = gf.shape[0]
    rows, n1h = f1.shape
    return pl.pallas_call(
        _hy_spec_kernel,
        grid=(dh // tc,),
        in_specs=[
            pl.BlockSpec((seq_len, LANES), lambda j: (0, 0)),
            pl.BlockSpec((LANES, tc), lambda j: (0, j)),
            pl.BlockSpec((LANES, tc), lambda j: (0, j)),
            pl.BlockSpec((1, tc), lambda j: (0, j)),
            pl.BlockSpec((rows, n1h), lambda j: (0, 0)),
            pl.BlockSpec((k1p, 2 * FFT_N2, 2 * FFT_N2), lambda j: (0, 0, 0)),
        ],
        out_specs=pl.BlockSpec((k1p, 2 * FFT_N2, tc), lambda j: (0, 0, j)),
        out_shape=jax.ShapeDtypeStruct((k1p, 2 * FFT_N2, dh), F32),
        compiler_params=_params(1, 56),
        name="hy_spec",
    )(h3, w4f, w4b, deltas, f1, gf)


def _hy_conv_kernel(u_ref, x0_ref, k_ref, f1_ref, f1t_ref, gf_ref, gi_ref, bias_ref, o_ref):
    n2 = FFT_N2
    x = _fft_forward(u_ref[...], f1_ref[...], gf_ref[...])
    u = u_ref[...].astype(F32)
    kk = k_ref[...]
    xr, xi, kr, ki = x[:, :n2], x[:, n2:], kk[:, :n2], kk[:, n2:]
    y = jnp.concatenate([xr * kr - xi * ki, xr * ki + xi * kr], axis=1).astype(BF16)
    conv = _fft_inverse(y, f1t_ref[...], gi_ref[...])
    o_ref[...] = (x0_ref[...].astype(F32) * (conv + u * bias_ref[...])).astype(o_ref.dtype)


def _hy_conv(u3, x0c, kspec, f1, f1t, gf, gi, bias, tc):
    nb, seq_len, dh = u3.shape
    k1p = gf.shape[0]
    rows, n1h = f1.shape
    blk = pl.BlockSpec((None, seq_len, tc), lambda j, b: (b, 0, j))
    const = lambda shape: pl.BlockSpec(shape, lambda j, b: (0,) * len(shape), pipeline_mode=ONE_BUFFER)
    return pl.pallas_call(
        _hy_conv_kernel,
        grid=(dh // tc, nb),
        in_specs=[
            blk, blk,
            pl.BlockSpec((k1p, 2 * FFT_N2, tc), lambda j, b: (0, 0, j)),
            const((rows, n1h)), const((n1h, rows)),
            const((k1p, 2 * FFT_N2, 2 * FFT_N2)), const((k1p, 2 * FFT_N2, 2 * FFT_N2)),
            pl.BlockSpec((1, tc), lambda j, b: (0, j)),
        ],
        out_specs=blk,
        out_shape=jax.ShapeDtypeStruct((nb, seq_len, dh), BF16),
        compiler_params=_params(2, 56),
        name="hy_conv",
    )(u3, x0c, kspec, f1, f1t, gf, gi, bias)


def _outproj_kernel(ys_ref, yh_ref, w_ref, x_ref, posr_ref, cemb_ref, nwp_ref, g_ref, nwf_ref, sh_ref, sc_ref,
                    xo_ref, h_ref):
    tm, ds = ys_ref.shape
    rows = min(256, tm)
    for r0 in range(0, tm, rows):
        rs = slice(r0, r0 + rows)
        y = jnp.dot(ys_ref[rs, :], w_ref[0:ds, :], preferred_element_type=F32)
        y = y + jnp.dot(yh_ref[rs, :], w_ref[ds:, :], preferred_element_type=F32)
        xn = _add_pos(x_ref[rs, :], posr_ref, r0 // GRID_W, cemb_ref[...]) + g_ref[...] * (_rms(y) * nwp_ref[...])
        xo_ref[rs, :] = xn
        h_ref[rs, :] = (_rms(xn) * nwf_ref[...] * (1.0 + sc_ref[...]) + sh_ref[...]).astype(h_ref.dtype)


def _out_proj(ys, yh, w, x2, posr, cemb, nwp, g1, nwf, sh2, sc2, seq_len, tm):
    m, d = x2.shape
    ds = ys.shape[1]
    dh = yh.shape[1]
    tiles_per_seq = seq_len // tm
    row = lambda i: (i, 0)
    fixed = lambda i: (0, 0)
    per_batch = lambda i: (i // tiles_per_seq, 0, 0)
    return pl.pallas_call(
        _outproj_kernel,
        grid=(m // tm,),
        in_specs=[
            pl.BlockSpec((tm, ds), row),
            pl.BlockSpec((tm, dh), row),
            pl.BlockSpec((ds + dh, d), fixed, pipeline_mode=ONE_BUFFER),
            pl.BlockSpec((tm, d), row),
            pl.BlockSpec((tm // GRID_W, d // 2), lambda i: (i % tiles_per_seq, 0)),
            pl.BlockSpec(cemb.shape, fixed),
            pl.BlockSpec((1, d), fixed),
            pl.BlockSpec((None, 1, d), per_batch),
            pl.BlockSpec((1, d), fixed),
            pl.BlockSpec((None, 1, d), per_batch),
            pl.BlockSpec((None, 1, d), per_batch),
        ],
        out_specs=[pl.BlockSpec((tm, d), row), pl.BlockSpec((tm, d), row)],
        out_shape=[jax.ShapeDtypeStruct((m, d), F32), jax.ShapeDtypeStruct((m, d), BF16)],
        compiler_params=_params(1, 56),
        name="out_proj",
    )(ys, yh, w, x2, posr, cemb, nwp, g1, nwf, sh2, sc2)


def _ffn_step(h_ref, wg, wu, wd, x_ref, nw_ref, g_ref, o_ref):
    f = pl.program_id(1)
    h = h_ref[...]
    gate = jnp.dot(h, wg, preferred_element_type=F32)
    up = jnp.dot(h, wu, preferred_element_type=F32)
    act = (_silu(gate) * up).astype(BF16)

    @pl.when(f == 0)
    def _():
        o_ref[...] = jnp.zeros_like(o_ref)

    d = o_ref.shape[1]
    cw = min(512, d)
    for n0 in range(0, d, cw):
        o_ref[:, n0:n0 + cw] += jnp.dot(act, wd[:, n0:n0 + cw], preferred_element_type=F32)

    @pl.when(f == pl.num_programs(1) - 1)
    def _():
        tm = o_ref.shape[0]
        rows = min(256, tm)
        for r0 in range(0, tm, rows):
            y = o_ref[r0:r0 + rows, :]
            o_ref[r0:r0 + rows, :] = x_ref[r0:r0 + rows, :] + g_ref[...] * (_rms(y) * nw_ref[...])


def _ffn_first_kernel(h_ref, wg_ref, wu_ref, wd_ref, x_ref, nw_ref, g_ref, o_ref, wgb_ref, wub_ref, wdb_ref):
    wg = wg_ref[...].astype(BF16)
    wu = wu_ref[...].astype(BF16)
    wd = wd_ref[...].astype(BF16)
    wgb_ref[...] = wg
    wub_ref[...] = wu
    wdb_ref[...] = wd
    _ffn_step(h_ref, wg, wu, wd, x_ref, nw_ref, g_ref, o_ref)


def _ffn_rest_kernel(h_ref, wg_ref, wu_ref, wd_ref, x_ref, nw_ref, g_ref, o_ref):
    _ffn_step(h_ref, wg_ref[...], wu_ref[...], wd_ref[...], x_ref, nw_ref, g_ref, o_ref)


def _ffn(h2, wg, wu, wd, xn, nw, g2, seq_len, tm, tf_first, tf):
    m, d = xn.shape
    dff = wg.shape[1]
    tiles_per_seq = seq_len // tm
    n_rows = m // tm
    out, wgb, wub, wdb = pl.pallas_call(
        _ffn_first_kernel,
        grid=(1, dff // tf_first),
        in_specs=[
            pl.BlockSpec((tm, d), lambda i, f: (0, 0), pipeline_mode=ONE_BUFFER),
            pl.BlockSpec((d, tf_first), lambda i, f: (0, f)),
            pl.BlockSpec((d, tf_first), lambda i, f: (0, f)),
            pl.BlockSpec((tf_first, d), lambda i, f: (f, 0)),
            pl.BlockSpec((tm, d), lambda i, f: (0, 0), pipeline_mode=ONE_BUFFER),
            pl.BlockSpec((1, d), lambda i, f: (0, 0)),
            pl.BlockSpec((None, 1, d), lambda i, f: (0, 0, 0)),
        ],
        out_specs=[
            pl.BlockSpec((tm, d), lambda i, f: (0, 0)),
            pl.BlockSpec((d, tf_first), lambda i, f: (0, f)),
            pl.BlockSpec((d, tf_first), lambda i, f: (0, f)),
            pl.BlockSpec((tf_first, d), lambda i, f: (f, 0)),
        ],
        out_shape=[
            jax.ShapeDtypeStruct((m, d), F32),
            jax.ShapeDtypeStruct((d, dff), BF16),
            jax.ShapeDtypeStruct((d, dff), BF16),
            jax.ShapeDtypeStruct((dff, d), BF16),
        ],
        input_output_aliases={4: 0},
        compiler_params=_params(2, 58),
        name="ffn_first",
    )(h2, wg, wu, wd, xn, nw, g2)
    if n_rows == 1:
        return out
    return pl.pallas_call(
        _ffn_rest_kernel,
        grid=(n_rows - 1, dff // tf),
        in_specs=[
            pl.BlockSpec((tm, d), lambda i, f: (i + 1, 0)),
            pl.BlockSpec((d, tf), lambda i, f: (0, f)),
            pl.BlockSpec((d, tf), lambda i, f: (0, f)),
            pl.BlockSpec((tf, d), lambda i, f: (f, 0)),
            pl.BlockSpec((tm, d), lambda i, f: (i + 1, 0), pipeline_mode=ONE_BUFFER),
            pl.BlockSpec((1, d), lambda i, f: (0, 0)),
            pl.BlockSpec((None, 1, d), lambda i, f: ((i + 1) // tiles_per_seq, 0, 0)),
        ],
        out_specs=pl.BlockSpec((tm, d), lambda i, f: (i + 1, 0)),
        out_shape=jax.ShapeDtypeStruct((m, d), F32),
        input_output_aliases={4: 0},
        compiler_params=_params(2, 58),
        name="ffn",
    )(h2, wgb, wub, wdb, out, nw, g2)


def _sincos_tables(rows, cols, dim):
    qd = dim // 4
    omega = 1.0 / (POS_THETA ** (jnp.arange(qd, dtype=F32) / qd))
    r = jnp.arange(rows, dtype=F32)[:, None] * omega
    cc = jnp.arange(cols, dtype=F32)[:, None] * omega
    r_emb = jnp.concatenate([jnp.sin(r), jnp.cos(r)], -1)
    c_emb = jnp.concatenate([jnp.sin(cc), jnp.cos(cc)], -1)
    return r_emb, c_emb


def _filter_features_t(seq_len, n_bands):
    t = jnp.linspace(0.0, 1.0, seq_len, dtype=F32)[:, None]
    w = 2.0 * math.pi * jnp.arange(seq_len, dtype=F32)[:, None] / seq_len
    fb = jnp.linspace(1e-4, n_bands - 1, n_bands, dtype=F32)[None]
    zpos = jnp.concatenate([t, jnp.cos(fb * w), -jnp.sin(fb * w)], -1)
    emb = zpos.shape[1]
    return jnp.pad(zpos, ((0, 0), (0, LANES - emb))).T


def _pad_rows(a, rows):
    return jnp.pad(a, ((0, rows - a.shape[0]), (0, 0)))


def kernel(x, c, ctx, c_ctx, w_ada, b_ada, norm_mix_pre, norm_mix_post, norm_ffn_pre, norm_ffn_post,
           w_in, ssd_conv_w, ssd_conv_b, ssd_a_log, ssd_dt_bias, ssd_d, ssd_norm,
           hy_conv_w, hy_conv_b, hy_w1, hy_b1, hy_w2, hy_b2, hy_w3, hy_b3, hy_w4, hy_freq, hy_bias,
           w_out, w_gate, w_up, w_down):
    nb, seq_len, d = x.shape
    ctx_len = ctx.shape[1]
    assert w_ada.shape[0] == 1, "single layer"
    n_heads = ssd_d.shape[1]
    d_ssd = n_heads * SSD_HEAD_DIM
    d_xbc = d_ssd + 2 * SSD_GROUPS * SSD_STATE
    dh = hy_bias.shape[1]
    assert w_in.shape[2] == d_ssd + d_xbc + 2 * n_heads + 3 * dh
    assert n_heads <= LANES and nb + 1 <= 8
    assert seq_len % (GRID_W * 8) == 0 and seq_len % FFT_N2 == 0
    m = nb * seq_len

    crows = jnp.concatenate([c, c_ctx[None, :]], axis=0)
    mod = _ada(crows, w_ada[0], b_ada[0])
    part = lambda r0, r1, k: mod[r0:r1, k * d:(k + 1) * d][:, None, :]
    sh1, sc1, g1, sh2, sc2, g2 = (part(0, nb, k) for k in range(6))
    csh1 = jnp.broadcast_to(part(nb, nb + 1, 0), (nb, 1, d))
    csc1 = jnp.broadcast_to(part(nb, nb + 1, 1), (nb, 1, d))

    w_t = jnp.transpose(w_in[0])
    o_xbc = d_ssd
    o_dt = o_xbc + d_xbc
    o_hy = o_dt + 2 * n_heads
    tn_in = 512
    assert o_dt % tn_in == 0 and (3 * dh) % tn_in == 0 and o_xbc % tn_in == 0 and o_hy % 16 == 0
    n_left = o_dt // tn_in
    n_main = o_dt + 3 * dh
    main_off = lambda j: jnp.where(j < n_left, j * tn_in, o_hy + (j - n_left) * tn_in)
    w_main = _w_in_cast(w_t, main_off, n_main // tn_in, tn_in)
    pad_dt = lambda rows: jnp.pad(rows, ((0, LANES - n_heads), (0, 0)))
    w_dt_t = jnp.concatenate([pad_dt(w_t[o_dt:o_dt + n_heads]), pad_dt(w_t[o_dt + n_heads:o_hy])],
                             axis=0)

    r_emb, c_emb = _sincos_tables(seq_len // GRID_W, GRID_W, d)
    posr = r_emb
    nmp = norm_mix_pre[0][None, :]

    tm_in = min(1024, seq_len)
    tn_main = n_main // 4 if n_main % (4 * LANES) == 0 else tn_in
    proj, dt_lat = _in_proj(x.reshape(m, d), posr, c_emb, nmp, sh1, sc1, w_main, 0, n_main, w_dt_t, seq_len,
                            tm_in, tn_main, True)
    tm_ctx = min(256, ctx_len)
    xbc_ctx, dt_ctx = _in_proj(ctx.reshape(nb * ctx_len, d), jnp.zeros((8, d // 2), F32), c_emb, nmp, csh1, csc1,
                               w_main, o_xbc, d_xbc, w_dt_t, ctx_len, tm_ctx, tn_in, False)
    proj3 = proj.reshape(nb, seq_len, -1)

    cw8 = _pad_rows(ssd_conv_w[0], 8)
    cb1 = ssd_conv_b[0][None, :]
    u_lat = _ssd_prep(proj3, d_ssd, d_xbc, cw8, cb1, 512)
    u_ctx = _ssd_prep(xbc_ctx.reshape(nb, ctx_len, d_xbc), 0, d_xbc, cw8, cb1, 512)
    pad_heads = lambda a: jnp.pad(a, ((0, 0), (0, LANES - n_heads)))[:, None, :]
    expand = (jnp.arange(LANES)[:, None] == (jnp.arange(d_ssd)[None, :] // SSD_HEAD_DIM)).astype(BF16)
    y_ssd = _ssd(u_lat, dt_lat.reshape(nb, seq_len, 2 * LANES), u_ctx, dt_ctx.reshape(nb, ctx_len, 2 * LANES),
                 proj3, pad_heads(ssd_a_log[0]), pad_heads(ssd_dt_bias[0]),
                 jnp.repeat(ssd_d[0], SSD_HEAD_DIM)[None, :], ssd_norm[0][None, :], expand)

    x0c, u_hy = _hy_prep(proj3, d_ssd + d_xbc, dh, _pad_rows(hy_conv_w[0], 8), hy_conv_b[0][None, :], 256)
    n_bands = (hy_w1.shape[1] - 1) // 2
    zt = _filter_features_t(seq_len, n_bands)
    w1t = jnp.pad(hy_w1[0].T, ((0, 0), (0, LANES - hy_w1.shape[1])))
    b3 = jnp.stack([hy_b1[0], hy_b2[0], hy_b3[0]], axis=1)
    h3 = _hy_mlp(zt, w1t, hy_w2[0].T, hy_w3[0].T, b3, hy_freq[0][:, None])
    w4 = _pad_rows(hy_w4[0], LANES)
    max_decay = math.log(HY_TARGET) / HY_FAST_PCT
    min_decay = math.log(HY_TARGET) / HY_SLOW_PCT
    deltas = jnp.abs(jnp.linspace(min_decay, max_decay, dh, dtype=F32))[None, :]
    f1, f1t, gf, gi = _fft_tables(seq_len)
    kspec = _hy_spec(h3, w4[:, :dh], w4[:, dh:], deltas, f1, gf, LANES)
    y_hy = _hy_conv(u_hy, x0c, kspec, f1, f1t, gf, gi, hy_bias[0][None, :], LANES)

    xn, h2 = _out_proj(y_ssd.reshape(m, d_ssd), y_hy.reshape(m, dh), w_out[0].astype(BF16), x.reshape(m, d),
                       posr, c_emb, norm_mix_post[0][None, :], g1, norm_ffn_pre[0][None, :], sh2, sc2, seq_len,
                       min(512, seq_len))
    out = _ffn(h2, w_gate[0], w_up[0], w_down[0], xn, norm_ffn_post[0][None, :], g2, seq_len,
               min(1024, seq_len), 256, 512)
    return out.reshape(nb, seq_len, d)
```

```python
import functools
import math

import numpy as np
import jax
import jax.numpy as jnp
from jax import lax
from jax.experimental import pallas as pl
from jax.experimental.pallas import tpu as pltpu

F32 = jnp.float32
BF16 = jnp.bfloat16
HIGHEST = lax.Precision.HIGHEST

RMS_EPS = 1e-6
POS_THETA = 10000.0
GRID_W = 64
SSD_HEAD_DIM = 64
SSD_GROUPS = 2
SSD_STATE = 128
SSD_CHUNK = 128
SSD_STEP_CHUNKS = 2
HY_TARGET = 1e-2
HY_FAST_PCT = 0.3
HY_SLOW_PCT = 1.5
FFT_N2 = 64
LANES = 128
MIB = 1024 * 1024

TM_IN = 1024
TN_W_IN = 512
TM_CTX = 256
TC_SSD_PREP = 512
TC_HY_PREP = 256
TC_FFT = LANES
TM_OUT = 512
TM_FFN = 1024
TF_FFN_FIRST = 256
TF_FFN = 512
FFN_COL_CHUNK = 512
TN_ADA = 1024
ROW_CHUNK = 256
ONE_BUFFER = pl.Buffered(1)
_NT_DIMS = (((1,), (1,)), ((), ()))
_NEG_BIG = -1e30


def _params(n_axes, vmem_mib):
    return pltpu.CompilerParams(
        dimension_semantics=("arbitrary",) * n_axes,
        vmem_limit_bytes=vmem_mib * MIB,
    )


def _silu(v):
    return v * (1.0 / (1.0 + jnp.exp(-v)))


def _softplus(v):
    return jnp.maximum(v, 0.0) + jnp.log(1.0 + jnp.exp(-jnp.abs(v)))


def _rms(v):
    return v * lax.rsqrt(jnp.mean(v * v, axis=-1, keepdims=True) + RMS_EPS)


def _add_pos(x, remb_ref, g0, cemb):
    rows, d = x.shape
    half = d // 2
    lo = jnp.concatenate(
        [x[g * GRID_W:(g + 1) * GRID_W, :half] + remb_ref[g0 + g:g0 + g + 1, :] for g in range(rows // GRID_W)],
        axis=0)
    hi = (x[:, half:].reshape(rows // GRID_W, GRID_W, half) + cemb[None]).reshape(rows, half)
    return jnp.concatenate([lo, hi], axis=1)


def _ada_kernel(c_ref, w_ref, b_ref, o_ref, s_scr):
    nr, d, _ = c_ref.shape
    tn = w_ref.shape[1]
    nt = tn // LANES
    kc = 64

    @pl.when(pl.program_id(0) == 0)
    def _():
        for r in range(nr):
            for k0 in range(0, d, ROW_CHUNK):
                s_scr[r, k0:k0 + ROW_CHUNK, :] = _silu(c_ref[r, k0:k0 + ROW_CHUNK, :])

    acc = [[jnp.zeros((8, LANES), F32) for _ in range(nt)] for _ in range(nr)]
    for k0 in range(0, d, kc):
        s = [s_scr[r, k0:k0 + kc, :] for r in range(nr)]
        for t in range(nt):
            w = w_ref[k0:k0 + kc, t * LANES:(t + 1) * LANES]
            for r in range(nr):
                acc[r][t] = acc[r][t] + jnp.sum((w * s[r]).reshape(kc // 8, 8, LANES), axis=0)
    rows = [jnp.concatenate([jnp.sum(a, axis=0, keepdims=True) for a in acc[r]], axis=1) for r in range(nr)]
    rows.append(jnp.zeros((o_ref.shape[0] - nr, tn), F32))
    o_ref[...] = jnp.concatenate(rows, axis=0) + b_ref[...]


def _ada(crows, w_ada, b_ada):
    nr = crows.shape[0]
    d, n = w_ada.shape
    tn = TN_ADA
    c_lanes = jnp.broadcast_to(crows[:, :, None], (nr, d, LANES))
    return pl.pallas_call(
        _ada_kernel,
        grid=(n // tn,),
        in_specs=[
            pl.BlockSpec((nr, d, LANES), lambda j: (0, 0, 0), pipeline_mode=ONE_BUFFER),
            pl.BlockSpec((d, tn), lambda j: (0, j)),
            pl.BlockSpec((1, tn), lambda j: (0, j)),
        ],
        out_specs=pl.BlockSpec((8, tn), lambda j: (0, j)),
        out_shape=jax.ShapeDtypeStruct((8, n), F32),
        scratch_shapes=[pltpu.VMEM((nr, d, LANES), F32)],
        compiler_params=_params(1, 40),
        name="ada",
    )(c_lanes, w_ada, b_ada.reshape(1, n))


def _inproj_body(x_ref, posr_ref, cemb_ref, nw_ref, sh_ref, sc_ref, get_w, wdt_ref, o_ref, dt_ref, h_scr, with_pos):
    first = pl.program_id(1) == 0

    @pl.when(first)
    def _():
        tm = x_ref.shape[0]
        rows = min(ROW_CHUNK, tm)
        wdt = wdt_ref[...].astype(BF16)
        w = get_w()
        for r0 in range(0, tm, rows):
            xf = x_ref[r0:r0 + rows, :]
            if with_pos:
                xf = _add_pos(xf, posr_ref, r0 // GRID_W, cemb_ref[...])
            h = _rms(xf) * nw_ref[...] * (1.0 + sc_ref[...]) + sh_ref[...]
            hb = h.astype(BF16)
            h_scr[r0:r0 + rows, :] = hb
            dt_ref[r0:r0 + rows, :] = lax.dot_general(hb, wdt, _NT_DIMS, preferred_element_type=F32)
            o_ref[r0:r0 + rows, :] = jnp.dot(hb, w, preferred_element_type=F32).astype(o_ref.dtype)

    @pl.when(jnp.logical_not(first))
    def _():
        o_ref[...] = jnp.dot(h_scr[...], get_w(), preferred_element_type=F32).astype(o_ref.dtype)


def _inproj_kernel(x_ref, posr_ref, cemb_ref, nw_ref, sh_ref, sc_ref, w_ref, wdt_ref, o_ref, dt_ref, h_scr,
                   *, with_pos):
    _inproj_body(x_ref, posr_ref, cemb_ref, nw_ref, sh_ref, sc_ref, lambda: w_ref[...], wdt_ref, o_ref, dt_ref,
                 h_scr, with_pos)


def _w_in_cast_kernel(w_ref, o_ref):
    o_ref[...] = w_ref[...].T.astype(BF16)


def _w_in_cast(w_t, row_off, n_tiles, tn):
    d = w_t.shape[1]
    return pl.pallas_call(
        _w_in_cast_kernel,
        grid=(n_tiles,),
        in_specs=[pl.BlockSpec((pl.Element(tn), pl.Element(d)), lambda j: (pl.multiple_of(row_off(j), 8), 0))],
        out_specs=pl.BlockSpec((d, tn), lambda j: (0, j)),
        out_shape=jax.ShapeDtypeStruct((d, n_tiles * tn), BF16),
        compiler_params=_params(1, 40),
        name="w_in_cast",
    )(w_t)


def _in_proj(x2, posr, cemb, nw, sh, sc, w_bf, col0, n, wdt_t, seq_len, tm, tn, with_pos):
    m, d = x2.shape
    ndt = wdt_t.shape[0]
    half = d // 2
    tiles_per_seq = seq_len // tm
    pos_rows = tm // GRID_W if with_pos else posr.shape[0]
    j0 = col0 // tn
    return pl.pallas_call(
        functools.partial(_inproj_kernel, with_pos=with_pos),
        grid=(m // tm, n // tn),
        in_specs=[
            pl.BlockSpec((tm, d), lambda i, j: (i, 0)),
            pl.BlockSpec((pos_rows, half), lambda i, j: ((i % tiles_per_seq) if with_pos else 0, 0)),
            pl.BlockSpec(cemb.shape, lambda i, j: (0, 0), pipeline_mode=ONE_BUFFER),
            pl.BlockSpec((1, d), lambda i, j: (0, 0)),
            pl.BlockSpec((None, 1, d), lambda i, j: (i // tiles_per_seq, 0, 0)),
            pl.BlockSpec((None, 1, d), lambda i, j: (i // tiles_per_seq, 0, 0)),
            pl.BlockSpec((d, tn), lambda i, j: (0, j0 + j)),
            pl.BlockSpec((ndt, d), lambda i, j: (0, 0), pipeline_mode=ONE_BUFFER),
        ],
        out_specs=[pl.BlockSpec((tm, tn), lambda i, j: (i, j)), pl.BlockSpec((tm, ndt), lambda i, j: (i, 0))],
        out_shape=[jax.ShapeDtypeStruct((m, n), BF16), jax.ShapeDtypeStruct((m, ndt), F32)],
        scratch_shapes=[pltpu.VMEM((tm, d), BF16)],
        compiler_params=_params(2, 58),
        name="in_proj",
    )(x2, posr, cemb, nw, sh, sc, w_bf, wdt_t)


CONV_ROWS = 256


def _shift_matrices(rows):
    return jnp.stack([jnp.eye(rows, k=-1, dtype=BF16), jnp.eye(rows, k=1, dtype=BF16)])


def _conv3_chunk(x_ref, r0, rows, seq_len, w, b, shift_ref):
    cur_b = x_ref[r0:r0 + rows, :]
    tc = cur_b.shape[1]
    up = jnp.dot(shift_ref[0], cur_b, preferred_element_type=F32)
    down = jnp.dot(shift_ref[1], cur_b, preferred_element_type=F32)
    rid = lax.broadcasted_iota(jnp.int32, (8, tc), 0)
    if r0 > 0:
        prev_row = x_ref[r0 - 16:r0, :].astype(F32)[15:16, :]
        up = jnp.concatenate([jnp.where(rid == 0, prev_row, up[0:8]), up[8:]], axis=0)
    if r0 + rows < seq_len:
        next_row = x_ref[r0 + rows:r0 + rows + 16, :].astype(F32)[0:1, :]
        down = jnp.concatenate([down[:rows - 8], jnp.where(rid == 7, next_row, down[rows - 8:])], axis=0)
    return up * w[0:1, :] + cur_b.astype(F32) * w[1:2, :] + down * w[2:3, :] + b


def _ssd_prep_kernel(x_ref, w_ref, b_ref, shift_ref, o_ref):
    seq_len = x_ref.shape[0]
    rows = shift_ref.shape[1]
    w = w_ref[...]
    b = b_ref[...]
    for r0 in range(0, seq_len, rows):
        v = _conv3_chunk(x_ref, r0, rows, seq_len, w, b, shift_ref)
        o_ref[r0:r0 + rows, :] = _silu(v).astype(o_ref.dtype)


def _ssd_prep(proj3, col0, width, w8, b1, tc):
    nb, seq_len, _ = proj3.shape
    rows = min(CONV_ROWS, seq_len)
    j0 = col0 // tc
    return pl.pallas_call(
        _ssd_prep_kernel,
        grid=(nb, width // tc),
        in_specs=[
            pl.BlockSpec((None, seq_len, tc), lambda b, j: (b, 0, j0 + j)),
            pl.BlockSpec((8, tc), lambda b, j: (0, j)),
            pl.BlockSpec((1, tc), lambda b, j: (0, j)),
            pl.BlockSpec((2, rows, rows), lambda b, j: (0, 0, 0)),
        ],
        out_specs=pl.BlockSpec((None, seq_len, tc), lambda b, j: (b, 0, j)),
        out_shape=jax.ShapeDtypeStruct((nb, seq_len, width), BF16),
        compiler_params=_params(2, 40),
        name="ssd_prep",
    )(proj3, w8, b1, _shift_matrices(rows))


def _hy_prep_kernel(x0_ref, x1_ref, v_ref, w0_ref, w1_ref, wv_ref, b0_ref, b1_ref, bv_ref, shift_ref,
                    x0c_ref, u_ref):
    seq_len = x0_ref.shape[0]
    rows = shift_ref.shape[1]
    w0, w1, wv = w0_ref[...], w1_ref[...], wv_ref[...]
    b0, b1, bv = b0_ref[...], b1_ref[...], bv_ref[...]
    for r0 in range(0, seq_len, rows):
        x0c_ref[r0:r0 + rows, :] = _conv3_chunk(x0_ref, r0, rows, seq_len, w0, b0, shift_ref).astype(x0c_ref.dtype)
        x1c = _conv3_chunk(x1_ref, r0, rows, seq_len, w1, b1, shift_ref)
        vc = _conv3_chunk(v_ref, r0, rows, seq_len, wv, bv, shift_ref)
        u_ref[r0:r0 + rows, :] = (vc * x1c).astype(u_ref.dtype)


def _hy_prep(proj3, col0, dh, w8, b1, tc):
    nb, seq_len, _ = proj3.shape
    rows = min(CONV_ROWS, seq_len)
    j0 = col0 // tc
    nj = dh // tc
    x_spec = lambda k: pl.BlockSpec((None, seq_len, tc), lambda b, j: (b, 0, j0 + k * nj + j))
    w_spec = lambda k: pl.BlockSpec((8, tc), lambda b, j: (0, k * nj + j))
    b_spec = lambda k: pl.BlockSpec((1, tc), lambda b, j: (0, k * nj + j))
    o_spec = pl.BlockSpec((None, seq_len, tc), lambda b, j: (b, 0, j))
    return pl.pallas_call(
        _hy_prep_kernel,
        grid=(nb, nj),
        in_specs=[x_spec(0), x_spec(1), x_spec(2), w_spec(0), w_spec(1), w_spec(2),
                  b_spec(0), b_spec(1), b_spec(2),
                  pl.BlockSpec((2, rows, rows), lambda b, j: (0, 0, 0))],
        out_specs=[o_spec, o_spec],
        out_shape=[jax.ShapeDtypeStruct((nb, seq_len, dh), BF16)] * 2,
        compiler_params=_params(2, 48),
        name="hy_prep",
    )(proj3, proj3, proj3, w8, w8, w8, b1, b1, b1, _shift_matrices(rows))


def _ssd_kernel(u_ref, dt_ref, uc_ref, dtc_ref, z_ref, alog_ref, dtb_ref, dsk_ref, nw_ref, e_ref,
                o_ref, h_scr, yf_scr, *, n_ctx, n_lat):
    q = SSD_CHUNK
    nsub = SSD_STEP_CHUNKS
    d = pl.program_id(0)
    s = pl.program_id(1)
    is_ctx = s < n_ctx
    t = jnp.maximum(s - n_ctx, 0)
    cidx = jnp.where(d == 0, t, n_lat - 1 - t)
    fwd = d == 0

    @pl.when(s == 0)
    def _():
        h_scr[...] = jnp.zeros_like(h_scr)

    row = lax.broadcasted_iota(jnp.int32, (q, q), 0)
    col = lax.broadcasted_iota(jnp.int32, (q, q), 1)
    tri = (jnp.where(fwd, row, col) >= jnp.where(fwd, col, row)).astype(F32)
    nb = u_ref.shape[0]

    def scan_block(b, xbc_ref, dtr_ref, need_y):
        sub = lambda v, k: v[k * q:(k + 1) * q]
        y_sub = [None] * nsub
        for k in range(nsub):
            rk = nsub - 1 - k
            u = jnp.where(fwd, sub(xbc_ref[b], k), sub(xbc_ref[b], rk))
            dtr = jnp.where(fwd, sub(dtr_ref[b], k), sub(dtr_ref[b], rk))
            y_sub[k] = _ssd_chunk(b, u, dtr, tri, alog_ref, dtb_ref, e_ref, h_scr, need_y)
        if need_y:
            return jnp.concatenate([jnp.where(fwd, y_sub[k], y_sub[nsub - 1 - k]) for k in range(nsub)], axis=0)

    @pl.when(is_ctx)
    def _():
        for b in range(nb):
            scan_block(b, uc_ref, dtc_ref, False)

    @pl.when(jnp.logical_and(jnp.logical_not(is_ctx), fwd))
    def _():
        for b in range(nb):
            yf_scr[b, cidx] = scan_block(b, u_ref, dt_ref, True)

    @pl.when(jnp.logical_and(jnp.logical_not(is_ctx), jnp.logical_not(fwd)))
    def _():
        d_ssd = o_ref.shape[2]
        for b in range(nb):
            y = scan_block(b, u_ref, dt_ref, True)
            xs = u_ref[b][:, :d_ssd].astype(F32)
            ytot = yf_scr[b, cidx] + y + dsk_ref[...] * xs
            gated = ytot * _silu(z_ref[b].astype(F32))
            o_ref[b] = (_rms(gated) * nw_ref[...]).astype(o_ref.dtype)


def _ssd_chunk(b, u, dtr, tri, alog_ref, dtb_ref, e_ref, h_scr, need_y):
    q = SSD_CHUNK
    d_ssd = h_scr.shape[2]
    gw = d_ssd // SSD_GROUPS

    dt = _softplus(dtr + dtb_ref[...])
    a = dt * (-jnp.exp(alog_ref[...]))
    a_hi = a.astype(BF16)
    r_hi = a - a_hi.astype(F32)
    a_mid = r_hi.astype(BF16)
    a_lo = (r_hi - a_mid.astype(F32)).astype(BF16)
    tri_b = tri.astype(BF16)
    cum = (jnp.dot(tri_b, a_hi, preferred_element_type=F32) + jnp.dot(tri_b, a_mid, preferred_element_type=F32)
           + jnp.dot(tri_b, a_lo, preferred_element_type=F32))
    mask_add = ((1.0 - tri) * _NEG_BIG).astype(BF16)
    cum_t = cum.T
    tot = jnp.sum(a, axis=0, keepdims=True)

    stack = jnp.concatenate(
        [dt, jnp.exp(cum), jnp.exp(tot - cum), jnp.broadcast_to(jnp.exp(tot), (8, LANES))], axis=0)
    ex = jnp.dot(stack.astype(BF16), e_ref[...], preferred_element_type=F32)
    dt_x = ex[0:q]
    ecum_x = ex[q:2 * q]
    edec_x = ex[2 * q:3 * q]
    etot_x = ex[3 * q:3 * q + 1]

    xs = u[:, :d_ssd].astype(F32)
    xdt = xs * dt_x
    xdt_b = xdt.astype(BF16)
    xdw_b = (xdt * edec_x).astype(BF16)
    lane_lo = lax.broadcasted_iota(jnp.int32, (q, LANES), 1) < SSD_HEAD_DIM
    heads_per_group = gw // SSD_HEAD_DIM

    y_parts = []
    for g in range(SSD_GROUPS):
        bg = u[:, d_ssd + g * SSD_STATE:d_ssd + (g + 1) * SSD_STATE]
        cg = u[:, d_ssd + (SSD_GROUPS + g) * SSD_STATE:d_ssd + (SSD_GROUPS + g + 1) * SSD_STATE]
        h_prev = h_scr[b, :, g * gw:(g + 1) * gw]
        if need_y:
            scores = lax.dot_general(cg, bg, _NT_DIMS, preferred_element_type=F32).astype(BF16)
            y_off = jnp.dot(cg, h_prev.astype(BF16), preferred_element_type=F32)
        for j in range(heads_per_group // 2 if need_y else 0):
            c0 = g * gw + j * LANES
            xp = xdt_b[:, c0:c0 + LANES]
            acc = None
            for hh in range(2):
                h = g * heads_per_group + 2 * j + hh
                diff = (cum[:, h:h + 1] - cum_t[h:h + 1, :]).astype(BF16)
                m_h = scores * jnp.exp(diff + mask_add)
                x_h = jnp.where(lane_lo if hh == 0 else jnp.logical_not(lane_lo), xp, jnp.zeros_like(xp))
                part = jnp.dot(m_h, x_h, preferred_element_type=F32)
                acc = part if acc is None else acc + part
            y_parts.append(acc + y_off[:, j * LANES:(j + 1) * LANES] * ecum_x[:, c0:c0 + LANES])
        upd = lax.dot_general(bg, xdw_b[:, g * gw:(g + 1) * gw], (((0,), (0,)), ((), ())),
                              preferred_element_type=F32)
        h_scr[b, :, g * gw:(g + 1) * gw] = etot_x[:, g * gw:(g + 1) * gw] * h_prev + upd
    return jnp.concatenate(y_parts, axis=1) if need_y else None


def _ssd(u_lat, dt_lat, u_ctx, dt_ctx, proj3, alog, dtb, dsk, nw, expand):
    nb, seq_len, d_xbc = u_lat.shape
    ctx_len = u_ctx.shape[1]
    q = SSD_CHUNK * SSD_STEP_CHUNKS
    assert seq_len % q == 0 and ctx_len % q == 0
    n_lat = seq_len // q
    n_ctx = ctx_len // q
    d_ssd = dsk.shape[1]

    def lat_idx(d, s):
        t = jnp.maximum(s - n_ctx, 0)
        return jnp.where(d == 0, t, n_lat - 1 - t)

    def ctx_idx(d, s):
        t = jnp.minimum(s, n_ctx - 1)
        return jnp.where(d == 0, t, n_ctx - 1 - t)

    def out_idx(d, s):
        return jnp.where(d == 0, n_lat - 1, lat_idx(d, s))

    return pl.pallas_call(
        functools.partial(_ssd_kernel, n_ctx=n_ctx, n_lat=n_lat),
        grid=(2, n_ctx + n_lat),
        in_specs=[
            pl.BlockSpec((nb, q, d_xbc), lambda d, s: (0, lat_idx(d, s), 0)),
            pl.BlockSpec((nb, q, LANES), lambda d, s: (0, lat_idx(d, s), d)),
            pl.BlockSpec((nb, q, d_xbc), lambda d, s: (0, ctx_idx(d, s), 0)),
            pl.BlockSpec((nb, q, LANES), lambda d, s: (0, ctx_idx(d, s), d)),
            pl.BlockSpec((nb, q, d_ssd), lambda d, s: (0, lat_idx(d, s), 0)),
            pl.BlockSpec((None, 1, LANES), lambda d, s: (d, 0, 0)),
            pl.BlockSpec((None, 1, LANES), lambda d, s: (d, 0, 0)),
            pl.BlockSpec((1, d_ssd), lambda d, s: (0, 0)),
            pl.BlockSpec((1, d_ssd), lambda d, s: (0, 0)),
            pl.BlockSpec((LANES, d_ssd), lambda d, s: (0, 0)),
        ],
        out_specs=pl.BlockSpec((nb, q, d_ssd), lambda d, s: (0, out_idx(d, s), 0)),
        out_shape=jax.ShapeDtypeStruct((nb, seq_len, d_ssd), BF16),
        scratch_shapes=[
            pltpu.VMEM((nb, SSD_STATE, d_ssd), F32),
            pltpu.VMEM((nb, n_lat, q, d_ssd), F32),
        ],
        compiler_params=_params(2, 56),
        name="ssd",
    )(u_lat, dt_lat, u_ctx, dt_ctx, proj3, alog, dtb, dsk, nw, expand)


def _hy_mlp_kernel(zt_ref, w1_ref, w2_ref, w3_ref, b_ref, fr_ref, o_ref):
    fr = fr_ref[...]
    b = b_ref[...]
    h = jnp.sin(fr * (jnp.dot(w1_ref[...], zt_ref[...], precision=HIGHEST, preferred_element_type=F32)
                      + b[:, 0:1]))
    h = jnp.sin(fr * (jnp.dot(w2_ref[...], h, precision=HIGHEST, preferred_element_type=F32) + b[:, 1:2]))
    h = jnp.sin(fr * (jnp.dot(w3_ref[...], h, precision=HIGHEST, preferred_element_type=F32) + b[:, 2:3]))
    hid, seq_len = h.shape
    hp = jnp.concatenate([h, jnp.zeros((LANES - hid, seq_len), F32)], axis=0)
    o_ref[...] = hp.T


def _hy_mlp(zt, w1t, w2t, w3t, b3, fr):
    seq_len = zt.shape[1]
    return pl.pallas_call(
        _hy_mlp_kernel,
        out_shape=jax.ShapeDtypeStruct((seq_len, LANES), F32),
        compiler_params=pltpu.CompilerParams(vmem_limit_bytes=40 * MIB),
        name="hy_mlp",
    )(zt, w1t, w2t, w3t, b3, fr)


def _fft_tables(seq_len):
    n_fft = 2 * seq_len
    n2 = FFT_N2
    n1 = n_fft // n2
    n1h = n1 // 2
    k1n = n1h + 1
    k1p = -(-k1n // 4) * 4
    k1 = np.arange(k1n, dtype=np.int64)
    th = (2.0 * np.pi / n1) * ((k1[:, None] * np.arange(n1h, dtype=np.int64)[None, :]) % n1)
    f1 = np.zeros((2 * k1p, n1h))
    f1[0:2 * k1n:2] = np.cos(th)
    f1[1:2 * k1n:2] = -np.sin(th)
    idx = np.arange(n2, dtype=np.int64)
    kk = k1[:, None, None] + n1 * idx[None, :, None]
    ph = (2.0 * np.pi / n_fft) * ((kk * idx[None, None, :]) % n_fft)
    g_re, g_im = np.cos(ph), -np.sin(ph)

    def blocks(re, im):
        out = np.zeros((k1p, 2 * n2, 2 * n2))
        out[:k1n, :n2, :n2] = re
        out[:k1n, :n2, n2:] = -im
        out[:k1n, n2:, :n2] = im
        out[:k1n, n2:, n2:] = re
        return out

    wgt = np.where((k1 == 0) | (k1 == n1h), 1.0, 2.0) / n_fft
    gf = blocks(g_re, g_im)
    gi = blocks(np.transpose(g_re, (0, 2, 1)) * wgt[:, None, None],
                -np.transpose(g_im, (0, 2, 1)) * wgt[:, None, None])
    as_bf16 = lambda t: jnp.asarray(t.astype(np.float32)).astype(BF16)
    return as_bf16(f1), as_bf16(np.ascontiguousarray(f1.T)), as_bf16(gf), as_bf16(gi)


_BATCHED = (((2,), (1,)), ((0,), (0,)))


def _fft_forward(x, f1, gf):
    seq_len, c = x.shape
    rows, n1h = f1.shape
    xt = jnp.swapaxes(x.astype(BF16).reshape(n1h, FFT_N2, c), 0, 1)
    a = lax.dot_general(jnp.broadcast_to(f1[None], (FFT_N2, rows, n1h)), xt, _BATCHED,
                        preferred_element_type=F32)
    at = jnp.swapaxes(a.astype(BF16), 0, 1).reshape(rows // 2, 2 * FFT_N2, c)
    return lax.dot_general(gf, at, _BATCHED, preferred_element_type=F32)


def _fft_inverse(y, f1t, gi):
    c = y.shape[2]
    n1h, rows = f1t.shape
    bt = lax.dot_general(gi, y, _BATCHED, preferred_element_type=F32)
    btt = jnp.swapaxes(bt.astype(BF16).reshape(rows, FFT_N2, c), 0, 1)
    yv = lax.dot_general(jnp.broadcast_to(f1t[None], (FFT_N2, n1h, rows)), btt, _BATCHED,
                         preferred_element_type=F32)
    return jnp.swapaxes(yv, 0, 1).reshape(n1h * FFT_N2, c)


def _hy_spec_kernel(h_ref, wf_ref, wb_ref, dl_ref, f1_ref, gf_ref, k_ref):
    n2 = FFT_N2
    seq_len = h_ref.shape[0]
    tc = wf_ref.shape[1]
    h3 = h_ref[...].astype(BF16)
    rid = lax.broadcasted_iota(jnp.int32, (seq_len, tc), 0)
    decay = jnp.exp(-(rid.astype(F32) * (1.0 / (seq_len - 1))) * dl_ref[...])
    hf = jnp.dot(h3, wf_ref[...].astype(BF16), preferred_element_type=F32) * decay
    hb = jnp.dot(h3, wb_ref[...].astype(BF16), preferred_element_type=F32) * decay
    norm = jnp.sum(jnp.abs(hf) + jnp.abs(hb), axis=0, keepdims=True) + 1e-6
    inv = 1.0 / norm
    hb = jnp.where(rid == 0, 0.0, hb * inv)
    x = _fft_forward(jnp.concatenate([hf * inv, hb], axis=1), f1_ref[...], gf_ref[...])
    xf, xb = x[:, :, 0:tc], x[:, :, tc:2 * tc]
    k_ref[...] = jnp.concatenate([xf[:, :n2] + xb[:, :n2], xf[:, n2:] - xb[:, n2:]], axis=1)


def _hy_spec(h3, w4f, w4b, deltas, f1, gf, tc):
    seq_len = h3.shape[0]
    dh = w4f.shape[1]
    k1p = gf.shape[0]
    rows, n1h = f1.shape
    return pl.pallas_call(
        _hy_spec_kernel,
        grid=(dh // tc,),
        in_specs=[
            pl.BlockSpec((seq_len, LANES), lambda j: (0, 0)),
            pl.BlockSpec((LANES, tc), lambda j: (0, j)),
            pl.BlockSpec((LANES, tc), lambda j: (0, j)),
            pl.BlockSpec((1, tc), lambda j: (0, j)),
            pl.BlockSpec((rows, n1h), lambda j: (0, 0)),
            pl.BlockSpec((k1p, 2 * FFT_N2, 2 * FFT_N2), lambda j: (0, 0, 0)),
        ],
        out_specs=pl.BlockSpec((k1p, 2 * FFT_N2, tc), lambda j: (0, 0, j)),
        out_shape=jax.ShapeDtypeStruct((k1p, 2 * FFT_N2, dh), F32),
        compiler_params=_params(1, 56),
        name="hy_spec",
    )(h3, w4f, w4b, deltas, f1, gf)


def _hy_conv_kernel(u_ref, x0_ref, k_ref, f1_ref, f1t_ref, gf_ref, gi_ref, bias_ref, o_ref):
    n2 = FFT_N2
    x = _fft_forward(u_ref[...], f1_ref[...], gf_ref[...])
    u = u_ref[...].astype(F32)
    kk = k_ref[...]
    xr, xi, kr, ki = x[:, :n2], x[:, n2:], kk[:, :n2], kk[:, n2:]
    y = jnp.concatenate([xr * kr - xi * ki, xr * ki + xi * kr], axis=1).astype(BF16)
    conv = _fft_inverse(y, f1t_ref[...], gi_ref[...])
    o_ref[...] = (x0_ref[...].astype(F32) * (conv + u * bias_ref[...])).astype(o_ref.dtype)


def _hy_conv(u3, x0c, kspec, f1, f1t, gf, gi, bias, tc):
    nb, seq_len, dh = u3.shape
    k1p = gf.shape[0]
    rows, n1h = f1.shape
    blk = pl.BlockSpec((None, seq_len, tc), lambda j, b: (b, 0, j))
    const = lambda shape: pl.BlockSpec(shape, lambda j, b: (0,) * len(shape), pipeline_mode=ONE_BUFFER)
    return pl.pallas_call(
        _hy_conv_kernel,
        grid=(dh // tc, nb),
        in_specs=[
            blk, blk,
            pl.BlockSpec((k1p, 2 * FFT_N2, tc), lambda j, b: (0, 0, j)),
            const((rows, n1h)), const((n1h, rows)),
            const((k1p, 2 * FFT_N2, 2 * FFT_N2)), const((k1p, 2 * FFT_N2, 2 * FFT_N2)),
            pl.BlockSpec((1, tc), lambda j, b: (0, j)),
        ],
        out_specs=blk,
        out_shape=jax.ShapeDtypeStruct((nb, seq_len, dh), BF16),
        compiler_params=_params(2, 56),
        name="hy_conv",
    )(u3, x0c, kspec, f1, f1t, gf, gi, bias)


def _outproj_kernel(ys_ref, yh_ref, w_ref, x_ref, posr_ref, cemb_ref, nwp_ref, g_ref, nwf_ref, sh_ref, sc_ref,
                    xo_ref, h_ref):
    tm, ds = ys_ref.shape
    rows = min(ROW_CHUNK, tm)
    for r0 in range(0, tm, rows):
        rs = slice(r0, r0 + rows)
        y = jnp.dot(ys_ref[rs, :], w_ref[0:ds, :], preferred_element_type=F32)
        y = y + jnp.dot(yh_ref[rs, :], w_ref[ds:, :], preferred_element_type=F32)
        xn = _add_pos(x_ref[rs, :], posr_ref, r0 // GRID_W, cemb_ref[...]) + g_ref[...] * (_rms(y) * nwp_ref[...])
        xo_ref[rs, :] = xn
        h_ref[rs, :] = (_rms(xn) * nwf_ref[...] * (1.0 + sc_ref[...]) + sh_ref[...]).astype(h_ref.dtype)


def _out_proj(ys, yh, w, x2, posr, cemb, nwp, g1, nwf, sh2, sc2, seq_len, tm):
    m, d = x2.shape
    ds = ys.shape[1]
    dh = yh.shape[1]
    tiles_per_seq = seq_len // tm
    row = lambda i: (i, 0)
    fixed = lambda i: (0, 0)
    per_batch = lambda i: (i // tiles_per_seq, 0, 0)
    return pl.pallas_call(
        _outproj_kernel,
        grid=(m // tm,),
        in_specs=[
            pl.BlockSpec((tm, ds), row),
            pl.BlockSpec((tm, dh), row),
            pl.BlockSpec((ds + dh, d), fixed, pipeline_mode=ONE_BUFFER),
            pl.BlockSpec((tm, d), row),
            pl.BlockSpec((tm // GRID_W, d // 2), lambda i: (i % tiles_per_seq, 0)),
            pl.BlockSpec(cemb.shape, fixed),
            pl.BlockSpec((1, d), fixed),
            pl.BlockSpec((None, 1, d), per_batch),
            pl.BlockSpec((1, d), fixed),
            pl.BlockSpec((None, 1, d), per_batch),
            pl.BlockSpec((None, 1, d), per_batch),
        ],
        out_specs=[pl.BlockSpec((tm, d), row), pl.BlockSpec((tm, d), row)],
        out_shape=[jax.ShapeDtypeStruct((m, d), F32), jax.ShapeDtypeStruct((m, d), BF16)],
        compiler_params=_params(1, 56),
        name="out_proj",
    )(ys, yh, w, x2, posr, cemb, nwp, g1, nwf, sh2, sc2)


def _ffn_step(h_ref, wg, wu, wd, x_ref, nw_ref, g_ref, o_ref):
    f = pl.program_id(1)
    h = h_ref[...]
    gate = jnp.dot(h, wg, preferred_element_type=F32)
    up = jnp.dot(h, wu, preferred_element_type=F32)
    act = (_silu(gate) * up).astype(BF16)

    @pl.when(f == 0)
    def _():
        o_ref[...] = jnp.zeros_like(o_ref)

    d = o_ref.shape[1]
    cw = min(FFN_COL_CHUNK, d)
    for n0 in range(0, d, cw):
        o_ref[:, n0:n0 + cw] += jnp.dot(act, wd[:, n0:n0 + cw], preferred_element_type=F32)

    @pl.when(f == pl.num_programs(1) - 1)
    def _():
        tm = o_ref.shape[0]
        rows = min(ROW_CHUNK, tm)
        for r0 in range(0, tm, rows):
            y = o_ref[r0:r0 + rows, :]
            o_ref[r0:r0 + rows, :] = x_ref[r0:r0 + rows, :] + g_ref[...] * (_rms(y) * nw_ref[...])


def _ffn_first_kernel(h_ref, wg_ref, wu_ref, wd_ref, x_ref, nw_ref, g_ref, o_ref, wgb_ref, wub_ref, wdb_ref):
    wg = wg_ref[...].astype(BF16)
    wu = wu_ref[...].astype(BF16)
    wd = wd_ref[...].astype(BF16)
    wgb_ref[...] = wg
    wub_ref[...] = wu
    wdb_ref[...] = wd
    _ffn_step(h_ref, wg, wu, wd, x_ref, nw_ref, g_ref, o_ref)


def _ffn_rest_kernel(h_ref, wg_ref, wu_ref, wd_ref, x_ref, nw_ref, g_ref, o_ref):
    _ffn_step(h_ref, wg_ref[...], wu_ref[...], wd_ref[...], x_ref, nw_ref, g_ref, o_ref)


def _ffn(h2, wg, wu, wd, xn, nw, g2, seq_len, tm, tf_first, tf):
    m, d = xn.shape
    dff = wg.shape[1]
    tiles_per_seq = seq_len // tm
    n_rows = m // tm
    out, wgb, wub, wdb = pl.pallas_call(
        _ffn_first_kernel,
        grid=(1, dff // tf_first),
        in_specs=[
            pl.BlockSpec((tm, d), lambda i, f: (0, 0), pipeline_mode=ONE_BUFFER),
            pl.BlockSpec((d, tf_first), lambda i, f: (0, f)),
            pl.BlockSpec((d, tf_first), lambda i, f: (0, f)),
            pl.BlockSpec((tf_first, d), lambda i, f: (f, 0)),
            pl.BlockSpec((tm, d), lambda i, f: (0, 0), pipeline_mode=ONE_BUFFER),
            pl.BlockSpec((1, d), lambda i, f: (0, 0)),
            pl.BlockSpec((None, 1, d), lambda i, f: (0, 0, 0)),
        ],
        out_specs=[
            pl.BlockSpec((tm, d), lambda i, f: (0, 0)),
            pl.BlockSpec((d, tf_first), lambda i, f: (0, f)),
            pl.BlockSpec((d, tf_first), lambda i, f: (0, f)),
            pl.BlockSpec((tf_first, d), lambda i, f: (f, 0)),
        ],
        out_shape=[
            jax.ShapeDtypeStruct((m, d), F32),
            jax.ShapeDtypeStruct((d, dff), BF16),
            jax.ShapeDtypeStruct((d, dff), BF16),
            jax.ShapeDtypeStruct((dff, d), BF16),
        ],
        input_output_aliases={4: 0},
        compiler_params=_params(2, 58),
        name="ffn_first",
    )(h2, wg, wu, wd, xn, nw, g2)
    if n_rows == 1:
        return out
    return pl.pallas_call(
        _ffn_rest_kernel,
        grid=(n_rows - 1, dff // tf),
        in_specs=[
            pl.BlockSpec((tm, d), lambda i, f: (i + 1, 0)),
            pl.BlockSpec((d, tf), lambda i, f: (0, f)),
            pl.BlockSpec((d, tf), lambda i, f: (0, f)),
            pl.BlockSpec((tf, d), lambda i, f: (f, 0)),
            pl.BlockSpec((tm, d), lambda i, f: (i + 1, 0), pipeline_mode=ONE_BUFFER),
            pl.BlockSpec((1, d), lambda i, f: (0, 0)),
            pl.BlockSpec((None, 1, d), lambda i, f: ((i + 1) // tiles_per_seq, 0, 0)),
        ],
        out_specs=pl.BlockSpec((tm, d), lambda i, f: (i + 1, 0)),
        out_shape=jax.ShapeDtypeStruct((m, d), F32),
        input_output_aliases={4: 0},
        compiler_params=_params(2, 58),
        name="ffn",
    )(h2, wgb, wub, wdb, out, nw, g2)


def _sincos_tables(rows, cols, dim):
    qd = dim // 4
    omega = 1.0 / (POS_THETA ** (jnp.arange(qd, dtype=F32) / qd))
    r = jnp.arange(rows, dtype=F32)[:, None] * omega
    cc = jnp.arange(cols, dtype=F32)[:, None] * omega
    r_emb = jnp.concatenate([jnp.sin(r), jnp.cos(r)], -1)
    c_emb = jnp.concatenate([jnp.sin(cc), jnp.cos(cc)], -1)
    return r_emb, c_emb


def _filter_features_t(seq_len, n_bands):
    t = jnp.linspace(0.0, 1.0, seq_len, dtype=F32)[:, None]
    w = 2.0 * math.pi * jnp.arange(seq_len, dtype=F32)[:, None] / seq_len
    fb = jnp.linspace(1e-4, n_bands - 1, n_bands, dtype=F32)[None]
    zpos = jnp.concatenate([t, jnp.cos(fb * w), -jnp.sin(fb * w)], -1)
    emb = zpos.shape[1]
    return jnp.pad(zpos, ((0, 0), (0, LANES - emb))).T


def _pad_rows(a, rows):
    return jnp.pad(a, ((0, rows - a.shape[0]), (0, 0)))


def kernel(x, c, ctx, c_ctx, w_ada, b_ada, norm_mix_pre, norm_mix_post, norm_ffn_pre, norm_ffn_post,
           w_in, ssd_conv_w, ssd_conv_b, ssd_a_log, ssd_dt_bias, ssd_d, ssd_norm,
           hy_conv_w, hy_conv_b, hy_w1, hy_b1, hy_w2, hy_b2, hy_w3, hy_b3, hy_w4, hy_freq, hy_bias,
           w_out, w_gate, w_up, w_down):
    nb, seq_len, d = x.shape
    ctx_len = ctx.shape[1]
    assert w_ada.shape[0] == 1, "single layer"
    n_heads = ssd_d.shape[1]
    d_ssd = n_heads * SSD_HEAD_DIM
    d_xbc = d_ssd + 2 * SSD_GROUPS * SSD_STATE
    dh = hy_bias.shape[1]
    assert w_in.shape[2] == d_ssd + d_xbc + 2 * n_heads + 3 * dh
    assert n_heads <= LANES and nb + 1 <= 8
    assert seq_len % (GRID_W * 8) == 0 and seq_len % FFT_N2 == 0
    m = nb * seq_len

    crows = jnp.concatenate([c, c_ctx[None, :]], axis=0)
    mod = _ada(crows, w_ada[0], b_ada[0])
    part = lambda r0, r1, k: mod[r0:r1, k * d:(k + 1) * d][:, None, :]
    sh1, sc1, g1, sh2, sc2, g2 = (part(0, nb, k) for k in range(6))
    csh1 = jnp.broadcast_to(part(nb, nb + 1, 0), (nb, 1, d))
    csc1 = jnp.broadcast_to(part(nb, nb + 1, 1), (nb, 1, d))

    w_t = jnp.transpose(w_in[0])
    o_xbc = d_ssd
    o_dt = o_xbc + d_xbc
    o_hy = o_dt + 2 * n_heads
    tn_in = TN_W_IN
    assert o_dt % tn_in == 0 and (3 * dh) % tn_in == 0 and o_xbc % tn_in == 0 and o_hy % 16 == 0
    n_left = o_dt // tn_in
    n_main = o_dt + 3 * dh
    main_off = lambda j: jnp.where(j < n_left, j * tn_in, o_hy + (j - n_left) * tn_in)
    w_main = _w_in_cast(w_t, main_off, n_main // tn_in, tn_in)
    pad_dt = lambda rows: jnp.pad(rows, ((0, LANES - n_heads), (0, 0)))
    w_dt_t = jnp.concatenate([pad_dt(w_t[o_dt:o_dt + n_heads]), pad_dt(w_t[o_dt + n_heads:o_hy])],
                             axis=0)

    r_emb, c_emb = _sincos_tables(seq_len // GRID_W, GRID_W, d)
    posr = r_emb
    nmp = norm_mix_pre[0][None, :]

    tm_in = min(TM_IN, seq_len)
    tn_main = n_main // 4 if n_main % (4 * LANES) == 0 else tn_in
    proj, dt_lat = _in_proj(x.reshape(m, d), posr, c_emb, nmp, sh1, sc1, w_main, 0, n_main, w_dt_t, seq_len,
                            tm_in, tn_main, True)
    tm_ctx = min(TM_CTX, ctx_len)
    xbc_ctx, dt_ctx = _in_proj(ctx.reshape(nb * ctx_len, d), jnp.zeros((8, d // 2), F32), c_emb, nmp, csh1, csc1,
                               w_main, o_xbc, d_xbc, w_dt_t, ctx_len, tm_ctx, tn_in, False)
    proj3 = proj.reshape(nb, seq_len, -1)

    cw8 = _pad_rows(ssd_conv_w[0], 8)
    cb1 = ssd_conv_b[0][None, :]
    u_lat = _ssd_prep(proj3, d_ssd, d_xbc, cw8, cb1, TC_SSD_PREP)
    u_ctx = _ssd_prep(xbc_ctx.reshape(nb, ctx_len, d_xbc), 0, d_xbc, cw8, cb1, TC_SSD_PREP)
    pad_heads = lambda a: jnp.pad(a, ((0, 0), (0, LANES - n_heads)))[:, None, :]
    expand = (jnp.arange(LANES)[:, None] == (jnp.arange(d_ssd)[None, :] // SSD_HEAD_DIM)).astype(BF16)
    y_ssd = _ssd(u_lat, dt_lat.reshape(nb, seq_len, 2 * LANES), u_ctx, dt_ctx.reshape(nb, ctx_len, 2 * LANES),
                 proj3, pad_heads(ssd_a_log[0]), pad_heads(ssd_dt_bias[0]),
                 jnp.repeat(ssd_d[0], SSD_HEAD_DIM)[None, :], ssd_norm[0][None, :], expand)

    x0c, u_hy = _hy_prep(proj3, d_ssd + d_xbc, dh, _pad_rows(hy_conv_w[0], 8), hy_conv_b[0][None, :], TC_HY_PREP)
    n_bands = (hy_w1.shape[1] - 1) // 2
    zt = _filter_features_t(seq_len, n_bands)
    w1t = jnp.pad(hy_w1[0].T, ((0, 0), (0, LANES - hy_w1.shape[1])))
    b3 = jnp.stack([hy_b1[0], hy_b2[0], hy_b3[0]], axis=1)
    h3 = _hy_mlp(zt, w1t, hy_w2[0].T, hy_w3[0].T, b3, hy_freq[0][:, None])
    w4 = _pad_rows(hy_w4[0], LANES)
    max_decay = math.log(HY_TARGET) / HY_FAST_PCT
    min_decay = math.log(HY_TARGET) / HY_SLOW_PCT
    deltas = jnp.abs(jnp.linspace(min_decay, max_decay, dh, dtype=F32))[None, :]
    f1, f1t, gf, gi = _fft_tables(seq_len)
    kspec = _hy_spec(h3, w4[:, :dh], w4[:, dh:], deltas, f1, gf, TC_FFT)
    y_hy = _hy_conv(u_hy, x0c, kspec, f1, f1t, gf, gi, hy_bias[0][None, :], TC_FFT)

    xn, h2 = _out_proj(y_ssd.reshape(m, d_ssd), y_hy.reshape(m, dh), w_out[0].astype(BF16), x.reshape(m, d),
                       posr, c_emb, norm_mix_post[0][None, :], g1, norm_ffn_pre[0][None, :], sh2, sc2, seq_len,
                       min(TM_OUT, seq_len))
    out = _ffn(h2, w_gate[0], w_up[0], w_down[0], xn, norm_ffn_post[0][None, :], g2, seq_len,
               min(TM_FFN, seq_len), TF_FFN_FIRST, TF_FFN)
    return out.reshape(nb, seq_len, d)
```

```python
import functools
import math

import numpy as np
import jax
import jax.numpy as jnp
from jax import lax
from jax.experimental import pallas as pl
from jax.experimental.pallas import tpu as pltpu

F32 = jnp.float32
BF16 = jnp.bfloat16
HIGHEST = lax.Precision.HIGHEST

RMS_EPS = 1e-6
POS_THETA = 10000.0
GRID_W = 64
SSD_HEAD_DIM = 64
SSD_GROUPS = 2
SSD_STATE = 128
SSD_CHUNK = 128
SSD_STEP_CHUNKS = 2
HY_TARGET = 1e-2
HY_FAST_PCT = 0.3
HY_SLOW_PCT = 1.5
FFT_N2 = 64
LANES = 128
MIB = 1024 * 1024

TM_IN = 1024
TN_W_IN = 512
TM_CTX = 256
TC_SSD_PREP = 512
TC_HY_PREP = 256
TC_FFT = 256
TM_OUT = 512
TM_FFN = 1024
TF_FFN_FIRST = 256
TF_FFN = 512
FFN_COL_CHUNK = 512
TN_ADA = 1024
ROW_CHUNK = 256
ONE_BUFFER = pl.Buffered(1)
_NT_DIMS = (((1,), (1,)), ((), ()))
_NEG_BIG = -1e30


def _params(n_axes, vmem_mib):
    return pltpu.CompilerParams(
        dimension_semantics=("arbitrary",) * n_axes,
        vmem_limit_bytes=vmem_mib * MIB,
    )


def _silu(v):
    return v * (1.0 / (1.0 + jnp.exp(-v)))


def _softplus(v):
    return jnp.maximum(v, 0.0) + jnp.log(1.0 + jnp.exp(-jnp.abs(v)))


def _rms(v):
    return v * lax.rsqrt(jnp.mean(v * v, axis=-1, keepdims=True) + RMS_EPS)


def _add_pos(x, remb_ref, g0, cemb):
    rows, d = x.shape
    half = d // 2
    lo = jnp.concatenate(
        [x[g * GRID_W:(g + 1) * GRID_W, :half] + remb_ref[g0 + g:g0 + g + 1, :] for g in range(rows // GRID_W)],
        axis=0)
    hi = (x[:, half:].reshape(rows // GRID_W, GRID_W, half) + cemb[None]).reshape(rows, half)
    return jnp.concatenate([lo, hi], axis=1)


def _ada_kernel(c_ref, w_ref, b_ref, o_ref, s_scr):
    nr, d, _ = c_ref.shape
    tn = w_ref.shape[1]
    nt = tn // LANES
    kc = 64

    @pl.when(pl.program_id(0) == 0)
    def _():
        for r in range(nr):
            for k0 in range(0, d, ROW_CHUNK):
                s_scr[r, k0:k0 + ROW_CHUNK, :] = _silu(c_ref[r, k0:k0 + ROW_CHUNK, :])

    acc = [[jnp.zeros((8, LANES), F32) for _ in range(nt)] for _ in range(nr)]
    for k0 in range(0, d, kc):
        s = [s_scr[r, k0:k0 + kc, :] for r in range(nr)]
        for t in range(nt):
            w = w_ref[k0:k0 + kc, t * LANES:(t + 1) * LANES]
            for r in range(nr):
                acc[r][t] = acc[r][t] + jnp.sum((w * s[r]).reshape(kc // 8, 8, LANES), axis=0)
    rows = [jnp.concatenate([jnp.sum(a, axis=0, keepdims=True) for a in acc[r]], axis=1) for r in range(nr)]
    rows.append(jnp.zeros((o_ref.shape[0] - nr, tn), F32))
    o_ref[...] = jnp.concatenate(rows, axis=0) + b_ref[...]


def _ada(crows, w_ada, b_ada):
    nr = crows.shape[0]
    d, n = w_ada.shape
    tn = TN_ADA
    c_lanes = jnp.broadcast_to(crows[:, :, None], (nr, d, LANES))
    return pl.pallas_call(
        _ada_kernel,
        grid=(n // tn,),
        in_specs=[
            pl.BlockSpec((nr, d, LANES), lambda j: (0, 0, 0), pipeline_mode=ONE_BUFFER),
            pl.BlockSpec((d, tn), lambda j: (0, j)),
            pl.BlockSpec((1, tn), lambda j: (0, j)),
        ],
        out_specs=pl.BlockSpec((8, tn), lambda j: (0, j)),
        out_shape=jax.ShapeDtypeStruct((8, n), F32),
        scratch_shapes=[pltpu.VMEM((nr, d, LANES), F32)],
        compiler_params=_params(1, 40),
        name="ada",
    )(c_lanes, w_ada, b_ada.reshape(1, n))


def _inproj_body(x_ref, posr_ref, cemb_ref, nw_ref, sh_ref, sc_ref, get_w, wdt_ref, o_ref, dt_ref, h_scr, with_pos):
    first = pl.program_id(1) == 0

    @pl.when(first)
    def _():
        tm = x_ref.shape[0]
        rows = min(ROW_CHUNK, tm)
        wdt = wdt_ref[...].astype(BF16)
        w = get_w()
        for r0 in range(0, tm, rows):
            xf = x_ref[r0:r0 + rows, :]
            if with_pos:
                xf = _add_pos(xf, posr_ref, r0 // GRID_W, cemb_ref[...])
            h = _rms(xf) * nw_ref[...] * (1.0 + sc_ref[...]) + sh_ref[...]
            hb = h.astype(BF16)
            h_scr[r0:r0 + rows, :] = hb
            dt_ref[r0:r0 + rows, :] = lax.dot_general(hb, wdt, _NT_DIMS, preferred_element_type=F32)
            o_ref[r0:r0 + rows, :] = jnp.dot(hb, w, preferred_element_type=F32).astype(o_ref.dtype)

    @pl.when(jnp.logical_not(first))
    def _():
        o_ref[...] = jnp.dot(h_scr[...], get_w(), preferred_element_type=F32).astype(o_ref.dtype)


def _inproj_kernel(x_ref, posr_ref, cemb_ref, nw_ref, sh_ref, sc_ref, w_ref, wdt_ref, o_ref, dt_ref, h_scr,
                   *, with_pos):
    _inproj_body(x_ref, posr_ref, cemb_ref, nw_ref, sh_ref, sc_ref, lambda: w_ref[...], wdt_ref, o_ref, dt_ref,
                 h_scr, with_pos)


def _w_in_cast_kernel(w_ref, o_ref):
    o_ref[...] = w_ref[...].T.astype(BF16)


def _w_in_cast(w_t, row_off, n_tiles, tn):
    d = w_t.shape[1]
    return pl.pallas_call(
        _w_in_cast_kernel,
        grid=(n_tiles,),
        in_specs=[pl.BlockSpec((pl.Element(tn), pl.Element(d)), lambda j: (pl.multiple_of(row_off(j), 8), 0))],
        out_specs=pl.BlockSpec((d, tn), lambda j: (0, j)),
        out_shape=jax.ShapeDtypeStruct((d, n_tiles * tn), BF16),
        compiler_params=_params(1, 40),
        name="w_in_cast",
    )(w_t)


def _in_proj(x2, posr, cemb, nw, sh, sc, w_bf, col0, n, wdt_t, seq_len, tm, tn, with_pos):
    m, d = x2.shape
    ndt = wdt_t.shape[0]
    half = d // 2
    tiles_per_seq = seq_len // tm
    pos_rows = tm // GRID_W if with_pos else posr.shape[0]
    j0 = col0 // tn
    return pl.pallas_call(
        functools.partial(_inproj_kernel, with_pos=with_pos),
        grid=(m // tm, n // tn),
        in_specs=[
            pl.BlockSpec((tm, d), lambda i, j: (i, 0)),
            pl.BlockSpec((pos_rows, half), lambda i, j: ((i % tiles_per_seq) if with_pos else 0, 0)),
            pl.BlockSpec(cemb.shape, lambda i, j: (0, 0), pipeline_mode=ONE_BUFFER),
            pl.BlockSpec((1, d), lambda i, j: (0, 0)),
            pl.BlockSpec((None, 1, d), lambda i, j: (i // tiles_per_seq, 0, 0)),
            pl.BlockSpec((None, 1, d), lambda i, j: (i // tiles_per_seq, 0, 0)),
            pl.BlockSpec((d, tn), lambda i, j: (0, j0 + j)),
            pl.BlockSpec((ndt, d), lambda i, j: (0, 0), pipeline_mode=ONE_BUFFER),
        ],
        out_specs=[pl.BlockSpec((tm, tn), lambda i, j: (i, j)), pl.BlockSpec((tm, ndt), lambda i, j: (i, 0))],
        out_shape=[jax.ShapeDtypeStruct((m, n), BF16), jax.ShapeDtypeStruct((m, ndt), F32)],
        scratch_shapes=[pltpu.VMEM((tm, d), BF16)],
        compiler_params=_params(2, 58),
        name="in_proj",
    )(x2, posr, cemb, nw, sh, sc, w_bf, wdt_t)


CONV_ROWS = 256


def _shift_matrices(rows):
    return jnp.stack([jnp.eye(rows, k=-1, dtype=BF16), jnp.eye(rows, k=1, dtype=BF16)])


def _conv3_chunk(x_ref, r0, rows, seq_len, w, b, shift_ref):
    cur_b = x_ref[r0:r0 + rows, :]
    tc = cur_b.shape[1]
    up = jnp.dot(shift_ref[0], cur_b, preferred_element_type=F32)
    down = jnp.dot(shift_ref[1], cur_b, preferred_element_type=F32)
    rid = lax.broadcasted_iota(jnp.int32, (8, tc), 0)
    if r0 > 0:
        prev_row = x_ref[r0 - 16:r0, :].astype(F32)[15:16, :]
        up = jnp.concatenate([jnp.where(rid == 0, prev_row, up[0:8]), up[8:]], axis=0)
    if r0 + rows < seq_len:
        next_row = x_ref[r0 + rows:r0 + rows + 16, :].astype(F32)[0:1, :]
        down = jnp.concatenate([down[:rows - 8], jnp.where(rid == 7, next_row, down[rows - 8:])], axis=0)
    return up * w[0:1, :] + cur_b.astype(F32) * w[1:2, :] + down * w[2:3, :] + b


def _ssd_prep_kernel(x_ref, w_ref, b_ref, shift_ref, o_ref):
    seq_len = x_ref.shape[0]
    rows = shift_ref.shape[1]
    w = w_ref[...]
    b = b_ref[...]
    for r0 in range(0, seq_len, rows):
        v = _conv3_chunk(x_ref, r0, rows, seq_len, w, b, shift_ref)
        o_ref[r0:r0 + rows, :] = _silu(v).astype(o_ref.dtype)


def _ssd_prep(proj3, col0, width, w8, b1, tc):
    nb, seq_len, _ = proj3.shape
    rows = min(CONV_ROWS, seq_len)
    j0 = col0 // tc
    return pl.pallas_call(
        _ssd_prep_kernel,
        grid=(nb, width // tc),
        in_specs=[
            pl.BlockSpec((None, seq_len, tc), lambda b, j: (b, 0, j0 + j)),
            pl.BlockSpec((8, tc), lambda b, j: (0, j)),
            pl.BlockSpec((1, tc), lambda b, j: (0, j)),
            pl.BlockSpec((2, rows, rows), lambda b, j: (0, 0, 0)),
        ],
        out_specs=pl.BlockSpec((None, seq_len, tc), lambda b, j: (b, 0, j)),
        out_shape=jax.ShapeDtypeStruct((nb, seq_len, width), BF16),
        compiler_params=_params(2, 40),
        name="ssd_prep",
    )(proj3, w8, b1, _shift_matrices(rows))


def _hy_prep_kernel(x0_ref, x1_ref, v_ref, w0_ref, w1_ref, wv_ref, b0_ref, b1_ref, bv_ref, shift_ref,
                    x0c_ref, u_ref):
    seq_len = x0_ref.shape[0]
    rows = shift_ref.shape[1]
    w0, w1, wv = w0_ref[...], w1_ref[...], wv_ref[...]
    b0, b1, bv = b0_ref[...], b1_ref[...], bv_ref[...]
    for r0 in range(0, seq_len, rows):
        x0c_ref[r0:r0 + rows, :] = _conv3_chunk(x0_ref, r0, rows, seq_len, w0, b0, shift_ref).astype(x0c_ref.dtype)
        x1c = _conv3_chunk(x1_ref, r0, rows, seq_len, w1, b1, shift_ref)
        vc = _conv3_chunk(v_ref, r0, rows, seq_len, wv, bv, shift_ref)
        u_ref[r0:r0 + rows, :] = (vc * x1c).astype(u_ref.dtype)


def _hy_prep(proj3, col0, dh, w8, b1, tc):
    nb, seq_len, _ = proj3.shape
    rows = min(CONV_ROWS, seq_len)
    j0 = col0 // tc
    nj = dh // tc
    x_spec = lambda k: pl.BlockSpec((None, seq_len, tc), lambda b, j: (b, 0, j0 + k * nj + j))
    w_spec = lambda k: pl.BlockSpec((8, tc), lambda b, j: (0, k * nj + j))
    b_spec = lambda k: pl.BlockSpec((1, tc), lambda b, j: (0, k * nj + j))
    o_spec = pl.BlockSpec((None, seq_len, tc), lambda b, j: (b, 0, j))
    return pl.pallas_call(
        _hy_prep_kernel,
        grid=(nb, nj),
        in_specs=[x_spec(0), x_spec(1), x_spec(2), w_spec(0), w_spec(1), w_spec(2),
                  b_spec(0), b_spec(1), b_spec(2),
                  pl.BlockSpec((2, rows, rows), lambda b, j: (0, 0, 0))],
        out_specs=[o_spec, o_spec],
        out_shape=[jax.ShapeDtypeStruct((nb, seq_len, dh), BF16)] * 2,
        compiler_params=_params(2, 48),
        name="hy_prep",
    )(proj3, proj3, proj3, w8, w8, w8, b1, b1, b1, _shift_matrices(rows))


def _ssd_kernel(u_ref, dt_ref, uc_ref, dtc_ref, z_ref, alog_ref, dtb_ref, dsk_ref, nw_ref, e_ref,
                o_ref, h_scr, yf_scr, *, n_ctx, n_lat):
    q = SSD_CHUNK
    nsub = SSD_STEP_CHUNKS
    d = pl.program_id(0)
    s = pl.program_id(1)
    is_ctx = s < n_ctx
    t = jnp.maximum(s - n_ctx, 0)
    cidx = jnp.where(d == 0, t, n_lat - 1 - t)
    fwd = d == 0

    @pl.when(s == 0)
    def _():
        h_scr[...] = jnp.zeros_like(h_scr)

    row = lax.broadcasted_iota(jnp.int32, (q, q), 0)
    col = lax.broadcasted_iota(jnp.int32, (q, q), 1)
    tri = (jnp.where(fwd, row, col) >= jnp.where(fwd, col, row)).astype(F32)
    nb = u_ref.shape[0]

    def scan_block(b, xbc_ref, dtr_ref, need_y):
        sub = lambda v, k: v[k * q:(k + 1) * q]
        y_sub = [None] * nsub
        for k in range(nsub):
            rk = nsub - 1 - k
            u = jnp.where(fwd, sub(xbc_ref[b], k), sub(xbc_ref[b], rk))
            dtr = jnp.where(fwd, sub(dtr_ref[b], k), sub(dtr_ref[b], rk))
            y_sub[k] = _ssd_chunk(b, u, dtr, tri, alog_ref, dtb_ref, e_ref, h_scr, need_y)
        if need_y:
            return jnp.concatenate([jnp.where(fwd, y_sub[k], y_sub[nsub - 1 - k]) for k in range(nsub)], axis=0)

    @pl.when(is_ctx)
    def _():
        for b in range(nb):
            scan_block(b, uc_ref, dtc_ref, False)

    @pl.when(jnp.logical_and(jnp.logical_not(is_ctx), fwd))
    def _():
        for b in range(nb):
            yf_scr[b, cidx] = scan_block(b, u_ref, dt_ref, True)

    @pl.when(jnp.logical_and(jnp.logical_not(is_ctx), jnp.logical_not(fwd)))
    def _():
        d_ssd = o_ref.shape[2]
        for b in range(nb):
            y = scan_block(b, u_ref, dt_ref, True)
            xs = u_ref[b][:, :d_ssd].astype(F32)
            ytot = yf_scr[b, cidx] + y + dsk_ref[...] * xs
            gated = ytot * _silu(z_ref[b].astype(F32))
            o_ref[b] = (_rms(gated) * nw_ref[...]).astype(o_ref.dtype)


def _ssd_chunk(b, u, dtr, tri, alog_ref, dtb_ref, e_ref, h_scr, need_y):
    q = SSD_CHUNK
    d_ssd = h_scr.shape[2]
    gw = d_ssd // SSD_GROUPS

    dt = _softplus(dtr + dtb_ref[...])
    a = dt * (-jnp.exp(alog_ref[...]))
    a_hi = a.astype(BF16)
    r_hi = a - a_hi.astype(F32)
    a_mid = r_hi.astype(BF16)
    a_lo = (r_hi - a_mid.astype(F32)).astype(BF16)
    tri_b = tri.astype(BF16)
    cum = (jnp.dot(tri_b, a_hi, preferred_element_type=F32) + jnp.dot(tri_b, a_mid, preferred_element_type=F32)
           + jnp.dot(tri_b, a_lo, preferred_element_type=F32))
    mask_add = ((1.0 - tri) * _NEG_BIG).astype(BF16)
    cum_t = cum.T
    tot = jnp.sum(a, axis=0, keepdims=True)

    stack = jnp.concatenate(
        [dt, jnp.exp(cum), jnp.exp(tot - cum), jnp.broadcast_to(jnp.exp(tot), (8, LANES))], axis=0)
    ex = jnp.dot(stack.astype(BF16), e_ref[...], preferred_element_type=F32)
    dt_x = ex[0:q]
    ecum_x = ex[q:2 * q]
    edec_x = ex[2 * q:3 * q]
    etot_x = ex[3 * q:3 * q + 1]

    xs = u[:, :d_ssd].astype(F32)
    xdt = xs * dt_x
    xdt_b = xdt.astype(BF16)
    xdw_b = (xdt * edec_x).astype(BF16)
    lane_lo = lax.broadcasted_iota(jnp.int32, (q, LANES), 1) < SSD_HEAD_DIM
    heads_per_group = gw // SSD_HEAD_DIM

    y_parts = []
    for g in range(SSD_GROUPS):
        bg = u[:, d_ssd + g * SSD_STATE:d_ssd + (g + 1) * SSD_STATE]
        cg = u[:, d_ssd + (SSD_GROUPS + g) * SSD_STATE:d_ssd + (SSD_GROUPS + g + 1) * SSD_STATE]
        h_prev = h_scr[b, :, g * gw:(g + 1) * gw]
        if need_y:
            scores = lax.dot_general(cg, bg, _NT_DIMS, preferred_element_type=F32).astype(BF16)
            y_off = jnp.dot(cg, h_prev.astype(BF16), preferred_element_type=F32)
        for j in range(heads_per_group // 2 if need_y else 0):
            c0 = g * gw + j * LANES
            xp = xdt_b[:, c0:c0 + LANES]
            acc = None
            for hh in range(2):
                h = g * heads_per_group + 2 * j + hh
                diff = (cum[:, h:h + 1] - cum_t[h:h + 1, :]).astype(BF16)
                m_h = scores * jnp.exp(diff + mask_add)
                x_h = jnp.where(lane_lo if hh == 0 else jnp.logical_not(lane_lo), xp, jnp.zeros_like(xp))
                part = jnp.dot(m_h, x_h, preferred_element_type=F32)
                acc = part if acc is None else acc + part
            y_parts.append(acc + y_off[:, j * LANES:(j + 1) * LANES] * ecum_x[:, c0:c0 + LANES])
        upd = lax.dot_general(bg, xdw_b[:, g * gw:(g + 1) * gw], (((0,), (0,)), ((), ())),
                              preferred_element_type=F32)
        h_scr[b, :, g * gw:(g + 1) * gw] = etot_x[:, g * gw:(g + 1) * gw] * h_prev + upd
    return jnp.concatenate(y_parts, axis=1) if need_y else None


def _ssd(u_lat, dt_lat, u_ctx, dt_ctx, proj3, alog, dtb, dsk, nw, expand):
    nb, seq_len, d_xbc = u_lat.shape
    ctx_len = u_ctx.shape[1]
    q = SSD_CHUNK * SSD_STEP_CHUNKS
    assert seq_len % q == 0 and ctx_len % q == 0
    n_lat = seq_len // q
    n_ctx = ctx_len // q
    d_ssd = dsk.shape[1]

    def lat_idx(d, s):
        t = jnp.maximum(s - n_ctx, 0)
        return jnp.where(d == 0, t, n_lat - 1 - t)

    def ctx_idx(d, s):
        t = jnp.minimum(s, n_ctx - 1)
        return jnp.where(d == 0, t, n_ctx - 1 - t)

    def out_idx(d, s):
        return jnp.where(d == 0, n_lat - 1, lat_idx(d, s))

    return pl.pallas_call(
        functools.partial(_ssd_kernel, n_ctx=n_ctx, n_lat=n_lat),
        grid=(2, n_ctx + n_lat),
        in_specs=[
            pl.BlockSpec((nb, q, d_xbc), lambda d, s: (0, lat_idx(d, s), 0)),
            pl.BlockSpec((nb, q, LANES), lambda d, s: (0, lat_idx(d, s), d)),
            pl.BlockSpec((nb, q, d_xbc), lambda d, s: (0, ctx_idx(d, s), 0)),
            pl.BlockSpec((nb, q, LANES), lambda d, s: (0, ctx_idx(d, s), d)),
            pl.BlockSpec((nb, q, d_ssd), lambda d, s: (0, lat_idx(d, s), 0)),
            pl.BlockSpec((None, 1, LANES), lambda d, s: (d, 0, 0)),
            pl.BlockSpec((None, 1, LANES), lambda d, s: (d, 0, 0)),
            pl.BlockSpec((1, d_ssd), lambda d, s: (0, 0)),
            pl.BlockSpec((1, d_ssd), lambda d, s: (0, 0)),
            pl.BlockSpec((LANES, d_ssd), lambda d, s: (0, 0)),
        ],
        out_specs=pl.BlockSpec((nb, q, d_ssd), lambda d, s: (0, out_idx(d, s), 0)),
        out_shape=jax.ShapeDtypeStruct((nb, seq_len, d_ssd), BF16),
        scratch_shapes=[
            pltpu.VMEM((nb, SSD_STATE, d_ssd), F32),
            pltpu.VMEM((nb, n_lat, q, d_ssd), F32),
        ],
        compiler_params=_params(2, 56),
        name="ssd",
    )(u_lat, dt_lat, u_ctx, dt_ctx, proj3, alog, dtb, dsk, nw, expand)


def _hy_mlp_kernel(zt_ref, w1_ref, w2_ref, w3_ref, b_ref, fr_ref, o_ref):
    fr = fr_ref[...]
    b = b_ref[...]
    h = jnp.sin(fr * (jnp.dot(w1_ref[...], zt_ref[...], precision=HIGHEST, preferred_element_type=F32)
                      + b[:, 0:1]))
    h = jnp.sin(fr * (jnp.dot(w2_ref[...], h, precision=HIGHEST, preferred_element_type=F32) + b[:, 1:2]))
    h = jnp.sin(fr * (jnp.dot(w3_ref[...], h, precision=HIGHEST, preferred_element_type=F32) + b[:, 2:3]))
    hid, seq_len = h.shape
    hp = jnp.concatenate([h, jnp.zeros((LANES - hid, seq_len), F32)], axis=0)
    o_ref[...] = hp.T


def _hy_mlp(zt, w1t, w2t, w3t, b3, fr):
    seq_len = zt.shape[1]
    return pl.pallas_call(
        _hy_mlp_kernel,
        out_shape=jax.ShapeDtypeStruct((seq_len, LANES), F32),
        compiler_params=pltpu.CompilerParams(vmem_limit_bytes=40 * MIB),
        name="hy_mlp",
    )(zt, w1t, w2t, w3t, b3, fr)


def _fft_tables(seq_len):
    n_fft = 2 * seq_len
    n2 = FFT_N2
    n1 = n_fft // n2
    n1h = n1 // 2
    k1n = n1h + 1
    k1p = -(-k1n // 4) * 4
    k1 = np.arange(k1n, dtype=np.int64)
    th = (2.0 * np.pi / n1) * ((k1[:, None] * np.arange(n1h, dtype=np.int64)[None, :]) % n1)
    f1 = np.zeros((2 * k1p, n1h))
    f1[0:2 * k1n:2] = np.cos(th)
    f1[1:2 * k1n:2] = -np.sin(th)
    idx = np.arange(n2, dtype=np.int64)
    kk = k1[:, None, None] + n1 * idx[None, :, None]
    ph = (2.0 * np.pi / n_fft) * ((kk * idx[None, None, :]) % n_fft)
    g_re, g_im = np.cos(ph), -np.sin(ph)

    def blocks(re, im):
        out = np.zeros((k1p, 2 * n2, 2 * n2))
        out[:k1n, :n2, :n2] = re
        out[:k1n, :n2, n2:] = -im
        out[:k1n, n2:, :n2] = im
        out[:k1n, n2:, n2:] = re
        return out

    wgt = np.where((k1 == 0) | (k1 == n1h), 1.0, 2.0) / n_fft
    gf = blocks(g_re, g_im)
    gi = blocks(np.transpose(g_re, (0, 2, 1)) * wgt[:, None, None],
                -np.transpose(g_im, (0, 2, 1)) * wgt[:, None, None])
    as_bf16 = lambda t: jnp.asarray(t.astype(np.float32)).astype(BF16)
    return as_bf16(f1), as_bf16(np.ascontiguousarray(f1.T)), as_bf16(gf), as_bf16(gi)


_BATCHED = (((2,), (1,)), ((0,), (0,)))


def _fft_forward(x, f1, gf):
    seq_len, c = x.shape
    rows, n1h = f1.shape
    xt = jnp.swapaxes(x.astype(BF16).reshape(n1h, FFT_N2, c), 0, 1)
    a = lax.dot_general(jnp.broadcast_to(f1[None], (FFT_N2, rows, n1h)), xt, _BATCHED,
                        preferred_element_type=F32)
    at = jnp.swapaxes(a.astype(BF16), 0, 1).reshape(rows // 2, 2 * FFT_N2, c)
    return lax.dot_general(gf, at, _BATCHED, preferred_element_type=F32)


def _fft_inverse(y, f1t, gi):
    c = y.shape[2]
    n1h, rows = f1t.shape
    bt = lax.dot_general(gi, y, _BATCHED, preferred_element_type=F32)
    btt = jnp.swapaxes(bt.astype(BF16).reshape(rows, FFT_N2, c), 0, 1)
    yv = lax.dot_general(jnp.broadcast_to(f1t[None], (FFT_N2, n1h, rows)), btt, _BATCHED,
                         preferred_element_type=F32)
    return jnp.swapaxes(yv, 0, 1).reshape(n1h * FFT_N2, c)


def _hy_spec_kernel(h_ref, wf_ref, wb_ref, dl_ref, f1_ref, gf_ref, k_ref):
    n2 = FFT_N2
    seq_len = h_ref.shape[0]
    tc = wf_ref.shape[1]
    h3 = h_ref[...].astype(BF16)
    rid = lax.broadcasted_iota(jnp.int32, (seq_len, tc), 0)
    decay = jnp.exp(-(rid.astype(F32) * (1.0 / (seq_len - 1))) * dl_ref[...])
    hf = jnp.dot(h3, wf_ref[...].astype(BF16), preferred_element_type=F32) * decay
    hb = jnp.dot(h3, wb_ref[...].astype(BF16), preferred_element_type=F32) * decay
    norm = jnp.sum(jnp.abs(hf) + jnp.abs(hb), axis=0, keepdims=True) + 1e-6
    inv = 1.0 / norm
    hb = jnp.where(rid == 0, 0.0, hb * inv)
    x = _fft_forward(jnp.concatenate([hf * inv, hb], axis=1), f1_ref[...], gf_ref[...])
    xf, xb = x[:, :, 0:tc], x[:, :, tc:2 * tc]
    k_ref[...] = jnp.concatenate([xf[:, :n2] + xb[:, :n2], xf[:, n2:] - xb[:, n2:]], axis=1)


def _hy_spec(h3, w4f, w4b, deltas, f1, gf, tc):
    seq_len = h3.shape[0]
    dh = w4f.shape[1]
    k1p = gf.shape[0]
    rows, n1h = f1.shape
    return pl.pallas_call(
        _hy_spec_kernel,
        grid=(dh // tc,),
        in_specs=[
            pl.BlockSpec((seq_len, LANES), lambda j: (0, 0)),
            pl.BlockSpec((LANES, tc), lambda j: (0, j)),
            pl.BlockSpec((LANES, tc), lambda j: (0, j)),
            pl.BlockSpec((1, tc), lambda j: (0, j)),
            pl.BlockSpec((rows, n1h), lambda j: (0, 0)),
            pl.BlockSpec((k1p, 2 * FFT_N2, 2 * FFT_N2), lambda j: (0, 0, 0)),
        ],
        out_specs=pl.BlockSpec((k1p, 2 * FFT_N2, tc), lambda j: (0, 0, j)),
        out_shape=jax.ShapeDtypeStruct((k1p, 2 * FFT_N2, dh), F32),
        compiler_params=_params(1, 56),
        name="hy_spec",
    )(h3, w4f, w4b, deltas, f1, gf)


def _hy_conv_kernel(u_ref, x0_ref, k_ref, f1_ref, f1t_ref, gf_ref, gi_ref, bias_ref, o_ref):
    n2 = FFT_N2
    x = _fft_forward(u_ref[...], f1_ref[...], gf_ref[...])
    u = u_ref[...].astype(F32)
    kk = k_ref[...]
    xr, xi, kr, ki = x[:, :n2], x[:, n2:], kk[:, :n2], kk[:, n2:]
    y = jnp.concatenate([xr * kr - xi * ki, xr * ki + xi * kr], axis=1).astype(BF16)
    conv = _fft_inverse(y, f1t_ref[...], gi_ref[...])
    o_ref[...] = (x0_ref[...].astype(F32) * (conv + u * bias_ref[...])).astype(o_ref.dtype)


def _hy_conv(u3, x0c, kspec, f1, f1t, gf, gi, bias, tc):
    nb, seq_len, dh = u3.shape
    k1p = gf.shape[0]
    rows, n1h = f1.shape
    blk = pl.BlockSpec((None, seq_len, tc), lambda j, b: (b, 0, j))
    const = lambda shape: pl.BlockSpec(shape, lambda j, b: (0,) * len(shape), pipeline_mode=ONE_BUFFER)
    return pl.pallas_call(
        _hy_conv_kernel,
        grid=(dh // tc, nb),
        in_specs=[
            blk, blk,
            pl.BlockSpec((k1p, 2 * FFT_N2, tc), lambda j, b: (0, 0, j)),
            const((rows, n1h)), const((n1h, rows)),
            const((k1p, 2 * FFT_N2, 2 * FFT_N2)), const((k1p, 2 * FFT_N2, 2 * FFT_N2)),
            pl.BlockSpec((1, tc), lambda j, b: (0, j)),
        ],
        out_specs=blk,
        out_shape=jax.ShapeDtypeStruct((nb, seq_len, dh), BF16),
        compiler_params=_params(2, 56),
        name="hy_conv",
    )(u3, x0c, kspec, f1, f1t, gf, gi, bias)


def _outproj_kernel(ys_ref, yh_ref, w_ref, x_ref, posr_ref, cemb_ref, nwp_ref, g_ref, nwf_ref, sh_ref, sc_ref,
                    xo_ref, h_ref):
    tm, ds = ys_ref.shape
    rows = min(ROW_CHUNK, tm)
    for r0 in range(0, tm, rows):
        rs = slice(r0, r0 + rows)
        y = jnp.dot(ys_ref[rs, :], w_ref[0:ds, :], preferred_element_type=F32)
        y = y + jnp.dot(yh_ref[rs, :], w_ref[ds:, :], preferred_element_type=F32)
        xn = _add_pos(x_ref[rs, :], posr_ref, r0 // GRID_W, cemb_ref[...]) + g_ref[...] * (_rms(y) * nwp_ref[...])
        xo_ref[rs, :] = xn
        h_ref[rs, :] = (_rms(xn) * nwf_ref[...] * (1.0 + sc_ref[...]) + sh_ref[...]).astype(h_ref.dtype)


def _out_proj(ys, yh, w, x2, posr, cemb, nwp, g1, nwf, sh2, sc2, seq_len, tm):
    m, d = x2.shape
    ds = ys.shape[1]
    dh = yh.shape[1]
    tiles_per_seq = seq_len // tm
    row = lambda i: (i, 0)
    fixed = lambda i: (0, 0)
    per_batch = lambda i: (i // tiles_per_seq, 0, 0)
    return pl.pallas_call(
        _outproj_kernel,
        grid=(m // tm,),
        in_specs=[
            pl.BlockSpec((tm, ds), row),
            pl.BlockSpec((tm, dh), row),
            pl.BlockSpec((ds + dh, d), fixed, pipeline_mode=ONE_BUFFER),
            pl.BlockSpec((tm, d), row),
            pl.BlockSpec((tm // GRID_W, d // 2), lambda i: (i % tiles_per_seq, 0)),
            pl.BlockSpec(cemb.shape, fixed),
            pl.BlockSpec((1, d), fixed),
            pl.BlockSpec((None, 1, d), per_batch),
            pl.BlockSpec((1, d), fixed),
            pl.BlockSpec((None, 1, d), per_batch),
            pl.BlockSpec((None, 1, d), per_batch),
        ],
        out_specs=[pl.BlockSpec((tm, d), row), pl.BlockSpec((tm, d), row)],
        out_shape=[jax.ShapeDtypeStruct((m, d), F32), jax.ShapeDtypeStruct((m, d), BF16)],
        compiler_params=_params(1, 56),
        name="out_proj",
    )(ys, yh, w, x2, posr, cemb, nwp, g1, nwf, sh2, sc2)


def _ffn_step(h_ref, wg, wu, wd, x_ref, nw_ref, g_ref, o_ref):
    f = pl.program_id(1)
    h = h_ref[...]
    gate = jnp.dot(h, wg, preferred_element_type=F32)
    up = jnp.dot(h, wu, preferred_element_type=F32)
    act = (_silu(gate) * up).astype(BF16)

    @pl.when(f == 0)
    def _():
        o_ref[...] = jnp.zeros_like(o_ref)

    d = o_ref.shape[1]
    cw = min(FFN_COL_CHUNK, d)
    for n0 in range(0, d, cw):
        o_ref[:, n0:n0 + cw] += jnp.dot(act, wd[:, n0:n0 + cw], preferred_element_type=F32)

    @pl.when(f == pl.num_programs(1) - 1)
    def _():
        tm = o_ref.shape[0]
        rows = min(ROW_CHUNK, tm)
        for r0 in range(0, tm, rows):
            y = o_ref[r0:r0 + rows, :]
            o_ref[r0:r0 + rows, :] = x_ref[r0:r0 + rows, :] + g_ref[...] * (_rms(y) * nw_ref[...])


def _ffn_first_kernel(h_ref, wg_ref, wu_ref, wd_ref, x_ref, nw_ref, g_ref, o_ref, wgb_ref, wub_ref, wdb_ref):
    wg = wg_ref[...].astype(BF16)
    wu = wu_ref[...].astype(BF16)
    wd = wd_ref[...].astype(BF16)
    wgb_ref[...] = wg
    wub_ref[...] = wu
    wdb_ref[...] = wd
    _ffn_step(h_ref, wg, wu, wd, x_ref, nw_ref, g_ref, o_ref)


def _ffn_rest_kernel(h_ref, wg_ref, wu_ref, wd_ref, x_ref, nw_ref, g_ref, o_ref):
    _ffn_step(h_ref, wg_ref[...], wu_ref[...], wd_ref[...], x_ref, nw_ref, g_ref, o_ref)


def _ffn(h2, wg, wu, wd, xn, nw, g2, seq_len, tm, tf_first, tf):
    m, d = xn.shape
    dff = wg.shape[1]
    tiles_per_seq = seq_len // tm
    n_rows = m // tm
    out, wgb, wub, wdb = pl.pallas_call(
        _ffn_first_kernel,
        grid=(1, dff // tf_first),
        in_specs=[
            pl.BlockSpec((tm, d), lambda i, f: (0, 0), pipeline_mode=ONE_BUFFER),
            pl.BlockSpec((d, tf_first), lambda i, f: (0, f)),
            pl.BlockSpec((d, tf_first), lambda i, f: (0, f)),
            pl.BlockSpec((tf_first, d), lambda i, f: (f, 0)),
            pl.BlockSpec((tm, d), lambda i, f: (0, 0), pipeline_mode=ONE_BUFFER),
            pl.BlockSpec((1, d), lambda i, f: (0, 0)),
            pl.BlockSpec((None, 1, d), lambda i, f: (0, 0, 0)),
        ],
        out_specs=[
            pl.BlockSpec((tm, d), lambda i, f: (0, 0)),
            pl.BlockSpec((d, tf_first), lambda i, f: (0, f)),
            pl.BlockSpec((d, tf_first), lambda i, f: (0, f)),
            pl.BlockSpec((tf_first, d), lambda i, f: (f, 0)),
        ],
        out_shape=[
            jax.ShapeDtypeStruct((m, d), F32),
            jax.ShapeDtypeStruct((d, dff), BF16),
            jax.ShapeDtypeStruct((d, dff), BF16),
            jax.ShapeDtypeStruct((dff, d), BF16),
        ],
        input_output_aliases={4: 0},
        compiler_params=_params(2, 58),
        name="ffn_first",
    )(h2, wg, wu, wd, xn, nw, g2)
    if n_rows == 1:
        return out
    return pl.pallas_call(
        _ffn_rest_kernel,
        grid=(n_rows - 1, dff // tf),
        in_specs=[
            pl.BlockSpec((tm, d), lambda i, f: (i + 1, 0)),
            pl.BlockSpec((d, tf), lambda i, f: (0, f)),
            pl.BlockSpec((d, tf), lambda i, f: (0, f)),
            pl.BlockSpec((tf, d), lambda i, f: (f, 0)),
            pl.BlockSpec((tm, d), lambda i, f: (i + 1, 0), pipeline_mode=ONE_BUFFER),
            pl.BlockSpec((1, d), lambda i, f: (0, 0)),
            pl.BlockSpec((None, 1, d), lambda i, f: ((i + 1) // tiles_per_seq, 0, 0)),
        ],
        out_specs=pl.BlockSpec((tm, d), lambda i, f: (i + 1, 0)),
        out_shape=jax.ShapeDtypeStruct((m, d), F32),
        input_output_aliases={4: 0},
        compiler_params=_params(2, 58),
        name="ffn",
    )(h2, wgb, wub, wdb, out, nw, g2)


def _sincos_tables(rows, cols, dim):
    qd = dim // 4
    omega = 1.0 / (POS_THETA ** (jnp.arange(qd, dtype=F32) / qd))
    r = jnp.arange(rows, dtype=F32)[:, None] * omega
    cc = jnp.arange(cols, dtype=F32)[:, None] * omega
    r_emb = jnp.concatenate([jnp.sin(r), jnp.cos(r)], -1)
    c_emb = jnp.concatenate([jnp.sin(cc), jnp.cos(cc)], -1)
    return r_emb, c_emb


def _filter_features_t(seq_len, n_bands):
    t = jnp.linspace(0.0, 1.0, seq_len, dtype=F32)[:, None]
    w = 2.0 * math.pi * jnp.arange(seq_len, dtype=F32)[:, None] / seq_len
    fb = jnp.linspace(1e-4, n_bands - 1, n_bands, dtype=F32)[None]
    zpos = jnp.concatenate([t, jnp.cos(fb * w), -jnp.sin(fb * w)], -1)
    emb = zpos.shape[1]
    return jnp.pad(zpos, ((0, 0), (0, LANES - emb))).T


def _pad_rows(a, rows):
    return jnp.pad(a, ((0, rows - a.shape[0]), (0, 0)))


def kernel(x, c, ctx, c_ctx, w_ada, b_ada, norm_mix_pre, norm_mix_post, norm_ffn_pre, norm_ffn_post,
           w_in, ssd_conv_w, ssd_conv_b, ssd_a_log, ssd_dt_bias, ssd_d, ssd_norm,
           hy_conv_w, hy_conv_b, hy_w1, hy_b1, hy_w2, hy_b2, hy_w3, hy_b3, hy_w4, hy_freq, hy_bias,
           w_out, w_gate, w_up, w_down):
    nb, seq_len, d = x.shape
    ctx_len = ctx.shape[1]
    assert w_ada.shape[0] == 1, "single layer"
    n_heads = ssd_d.shape[1]
    d_ssd = n_heads * SSD_HEAD_DIM
    d_xbc = d_ssd + 2 * SSD_GROUPS * SSD_STATE
    dh = hy_bias.shape[1]
    assert w_in.shape[2] == d_ssd + d_xbc + 2 * n_heads + 3 * dh
    assert n_heads <= LANES and nb + 1 <= 8
    assert seq_len % (GRID_W * 8) == 0 and seq_len % FFT_N2 == 0
    m = nb * seq_len

    crows = jnp.concatenate([c, c_ctx[None, :]], axis=0)
    mod = _ada(crows, w_ada[0], b_ada[0])
    part = lambda r0, r1, k: mod[r0:r1, k * d:(k + 1) * d][:, None, :]
    sh1, sc1, g1, sh2, sc2, g2 = (part(0, nb, k) for k in range(6))
    csh1 = jnp.broadcast_to(part(nb, nb + 1, 0), (nb, 1, d))
    csc1 = jnp.broadcast_to(part(nb, nb + 1, 1), (nb, 1, d))

    w_t = jnp.transpose(w_in[0])
    o_xbc = d_ssd
    o_dt = o_xbc + d_xbc
    o_hy = o_dt + 2 * n_heads
    tn_in = TN_W_IN
    assert o_dt % tn_in == 0 and (3 * dh) % tn_in == 0 and o_xbc % tn_in == 0 and o_hy % 16 == 0
    n_left = o_dt // tn_in
    n_main = o_dt + 3 * dh
    main_off = lambda j: jnp.where(j < n_left, j * tn_in, o_hy + (j - n_left) * tn_in)
    w_main = _w_in_cast(w_t, main_off, n_main // tn_in, tn_in)
    pad_dt = lambda rows: jnp.pad(rows, ((0, LANES - n_heads), (0, 0)))
    w_dt_t = jnp.concatenate([pad_dt(w_t[o_dt:o_dt + n_heads]), pad_dt(w_t[o_dt + n_heads:o_hy])],
                             axis=0)

    r_emb, c_emb = _sincos_tables(seq_len // GRID_W, GRID_W, d)
    posr = r_emb
    nmp = norm_mix_pre[0][None, :]

    tm_in = min(TM_IN, seq_len)
    tn_main = n_main // 4 if n_main % (4 * LANES) == 0 else tn_in
    proj, dt_lat = _in_proj(x.reshape(m, d), posr, c_emb, nmp, sh1, sc1, w_main, 0, n_main, w_dt_t, seq_len,
                            tm_in, tn_main, True)
    tm_ctx = min(TM_CTX, ctx_len)
    xbc_ctx, dt_ctx = _in_proj(ctx.reshape(nb * ctx_len, d), jnp.zeros((8, d // 2), F32), c_emb, nmp, csh1, csc1,
                               w_main, o_xbc, d_xbc, w_dt_t, ctx_len, tm_ctx, tn_in, False)
    proj3 = proj.reshape(nb, seq_len, -1)

    cw8 = _pad_rows(ssd_conv_w[0], 8)
    cb1 = ssd_conv_b[0][None, :]
    u_lat = _ssd_prep(proj3, d_ssd, d_xbc, cw8, cb1, TC_SSD_PREP)
    u_ctx = _ssd_prep(xbc_ctx.reshape(nb, ctx_len, d_xbc), 0, d_xbc, cw8, cb1, TC_SSD_PREP)
    pad_heads = lambda a: jnp.pad(a, ((0, 0), (0, LANES - n_heads)))[:, None, :]
    expand = (jnp.arange(LANES)[:, None] == (jnp.arange(d_ssd)[None, :] // SSD_HEAD_DIM)).astype(BF16)
    y_ssd = _ssd(u_lat, dt_lat.reshape(nb, seq_len, 2 * LANES), u_ctx, dt_ctx.reshape(nb, ctx_len, 2 * LANES),
                 proj3, pad_heads(ssd_a_log[0]), pad_heads(ssd_dt_bias[0]),
                 jnp.repeat(ssd_d[0], SSD_HEAD_DIM)[None, :], ssd_norm[0][None, :], expand)

    x0c, u_hy = _hy_prep(proj3, d_ssd + d_xbc, dh, _pad_rows(hy_conv_w[0], 8), hy_conv_b[0][None, :], TC_HY_PREP)
    n_bands = (hy_w1.shape[1] - 1) // 2
    zt = _filter_features_t(seq_len, n_bands)
    w1t = jnp.pad(hy_w1[0].T, ((0, 0), (0, LANES - hy_w1.shape[1])))
    b3 = jnp.stack([hy_b1[0], hy_b2[0], hy_b3[0]], axis=1)
    h3 = _hy_mlp(zt, w1t, hy_w2[0].T, hy_w3[0].T, b3, hy_freq[0][:, None])
    w4 = _pad_rows(hy_w4[0], LANES)
    max_decay = math.log(HY_TARGET) / HY_FAST_PCT
    min_decay = math.log(HY_TARGET) / HY_SLOW_PCT
    deltas = jnp.abs(jnp.linspace(min_decay, max_decay, dh, dtype=F32))[None, :]
    f1, f1t, gf, gi = _fft_tables(seq_len)
    kspec = _hy_spec(h3, w4[:, :dh], w4[:, dh:], deltas, f1, gf, TC_FFT)
    y_hy = _hy_conv(u_hy, x0c, kspec, f1, f1t, gf, gi, hy_bias[0][None, :], TC_FFT)

    xn, h2 = _out_proj(y_ssd.reshape(m, d_ssd), y_hy.reshape(m, dh), w_out[0].astype(BF16), x.reshape(m, d),
                       posr, c_emb, norm_mix_post[0][None, :], g1, norm_ffn_pre[0][None, :], sh2, sc2, seq_len,
                       min(TM_OUT, seq_len))
    out = _ffn(h2, w_gate[0], w_up[0], w_down[0], xn, norm_ffn_post[0][None, :], g2, seq_len,
               min(TM_FFN, seq_len), TF_FFN_FIRST, TF_FFN)
    return out.reshape(nb, seq_len, d)
```

```python
import functools
import math

import numpy as np
import jax
import jax.numpy as jnp
from jax import lax
from jax.experimental import pallas as pl
from jax.experimental.pallas import tpu as pltpu

F32 = jnp.float32
BF16 = jnp.bfloat16
HIGHEST = lax.Precision.HIGHEST

RMS_EPS = 1e-6
POS_THETA = 10000.0
GRID_W = 64
SSD_HEAD_DIM = 64
SSD_GROUPS = 2
SSD_STATE = 128
SSD_CHUNK = 128
SSD_STEP_CHUNKS = 2
HY_TARGET = 1e-2
HY_FAST_PCT = 0.3
HY_SLOW_PCT = 1.5
FFT_N2 = 64
LANES = 128
MIB = 1024 * 1024

TM_IN = 1024
TN_W_IN = 512
TM_CTX = 256
TC_SSD_PREP = 512
TC_HY_PREP = 256
TC_FFT = 256
TM_OUT = 512
TM_FFN = 1024
TF_FFN_FIRST = 256
TF_FFN = 512
FFN_COL_CHUNK = 512
TN_ADA = 1024
ROW_CHUNK = 256
ONE_BUFFER = pl.Buffered(1)
_NT_DIMS = (((1,), (1,)), ((), ()))
_NEG_BIG = -1e30


def _params(n_axes, vmem_mib):
    return pltpu.CompilerParams(
        dimension_semantics=("arbitrary",) * n_axes,
        vmem_limit_bytes=vmem_mib * MIB,
    )


def _silu(v):
    return v * (1.0 / (1.0 + jnp.exp(-v)))


def _softplus(v):
    return jnp.maximum(v, 0.0) + jnp.log(1.0 + jnp.exp(-jnp.abs(v)))


def _rms(v):
    return v * lax.rsqrt(jnp.mean(v * v, axis=-1, keepdims=True) + RMS_EPS)


def _add_pos(x, remb_ref, g0, cemb):
    rows, d = x.shape
    half = d // 2
    lo = jnp.concatenate(
        [x[g * GRID_W:(g + 1) * GRID_W, :half] + remb_ref[g0 + g:g0 + g + 1, :] for g in range(rows // GRID_W)],
        axis=0)
    hi = (x[:, half:].reshape(rows // GRID_W, GRID_W, half) + cemb[None]).reshape(rows, half)
    return jnp.concatenate([lo, hi], axis=1)


def _ada_kernel(c_ref, w_ref, b_ref, o_ref, s_scr):
    nr, d, _ = c_ref.shape
    tn = w_ref.shape[1]
    nt = tn // LANES
    kc = 64

    @pl.when(pl.program_id(0) == 0)
    def _():
        for r in range(nr):
            for k0 in range(0, d, ROW_CHUNK):
                s_scr[r, k0:k0 + ROW_CHUNK, :] = _silu(c_ref[r, k0:k0 + ROW_CHUNK, :])

    acc = [[jnp.zeros((8, LANES), F32) for _ in range(nt)] for _ in range(nr)]
    for k0 in range(0, d, kc):
        s = [s_scr[r, k0:k0 + kc, :] for r in range(nr)]
        for t in range(nt):
            w = w_ref[k0:k0 + kc, t * LANES:(t + 1) * LANES]
            for r in range(nr):
                acc[r][t] = acc[r][t] + jnp.sum((w * s[r]).reshape(kc // 8, 8, LANES), axis=0)
    rows = [jnp.concatenate([jnp.sum(a, axis=0, keepdims=True) for a in acc[r]], axis=1) for r in range(nr)]
    rows.append(jnp.zeros((o_ref.shape[0] - nr, tn), F32))
    o_ref[...] = jnp.concatenate(rows, axis=0) + b_ref[...]


def _ada(crows, w_ada, b_ada):
    nr = crows.shape[0]
    d, n = w_ada.shape
    tn = TN_ADA
    c_lanes = jnp.broadcast_to(crows[:, :, None], (nr, d, LANES))
    return pl.pallas_call(
        _ada_kernel,
        grid=(n // tn,),
        in_specs=[
            pl.BlockSpec((nr, d, LANES), lambda j: (0, 0, 0), pipeline_mode=ONE_BUFFER),
            pl.BlockSpec((d, tn), lambda j: (0, j)),
            pl.BlockSpec((1, tn), lambda j: (0, j)),
        ],
        out_specs=pl.BlockSpec((8, tn), lambda j: (0, j)),
        out_shape=jax.ShapeDtypeStruct((8, n), F32),
        scratch_shapes=[pltpu.VMEM((nr, d, LANES), F32)],
        compiler_params=_params(1, 40),
        name="ada",
    )(c_lanes, w_ada, b_ada.reshape(1, n))


def _inproj_body(x_ref, posr_ref, cemb_ref, nw_ref, sh_ref, sc_ref, get_w, wdt_ref, o_ref, dt_ref, h_scr, with_pos):
    first = pl.program_id(1) == 0

    @pl.when(first)
    def _():
        tm = x_ref.shape[0]
        rows = min(ROW_CHUNK, tm)
        wdt = wdt_ref[...].astype(BF16)
        w = get_w()
        for r0 in range(0, tm, rows):
            xf = x_ref[r0:r0 + rows, :]
            if with_pos:
                xf = _add_pos(xf, posr_ref, r0 // GRID_W, cemb_ref[...])
            h = _rms(xf) * nw_ref[...] * (1.0 + sc_ref[...]) + sh_ref[...]
            hb = h.astype(BF16)
            h_scr[r0:r0 + rows, :] = hb
            dt_ref[r0:r0 + rows, :] = lax.dot_general(hb, wdt, _NT_DIMS, preferred_element_type=F32)
            o_ref[r0:r0 + rows, :] = jnp.dot(hb, w, preferred_element_type=F32).astype(o_ref.dtype)

    @pl.when(jnp.logical_not(first))
    def _():
        o_ref[...] = jnp.dot(h_scr[...], get_w(), preferred_element_type=F32).astype(o_ref.dtype)


def _inproj_kernel(x_ref, posr_ref, cemb_ref, nw_ref, sh_ref, sc_ref, w_ref, wdt_ref, o_ref, dt_ref, h_scr,
                   *, with_pos):
    _inproj_body(x_ref, posr_ref, cemb_ref, nw_ref, sh_ref, sc_ref, lambda: w_ref[...], wdt_ref, o_ref, dt_ref,
                 h_scr, with_pos)


def _w_in_cast_kernel(w_ref, o_ref):
    o_ref[...] = w_ref[...].T.astype(BF16)


def _w_in_cast(w_t, row_off, n_tiles, tn):
    d = w_t.shape[1]
    return pl.pallas_call(
        _w_in_cast_kernel,
        grid=(n_tiles,),
        in_specs=[pl.BlockSpec((pl.Element(tn), pl.Element(d)), lambda j: (pl.multiple_of(row_off(j), 8), 0))],
        out_specs=pl.BlockSpec((d, tn), lambda j: (0, j)),
        out_shape=jax.ShapeDtypeStruct((d, n_tiles * tn), BF16),
        compiler_params=_params(1, 40),
        name="w_in_cast",
    )(w_t)


def _in_proj(x2, posr, cemb, nw, sh, sc, w_bf, col0, n, wdt_t, seq_len, tm, tn, with_pos):
    m, d = x2.shape
    ndt = wdt_t.shape[0]
    half = d // 2
    tiles_per_seq = seq_len // tm
    pos_rows = tm // GRID_W if with_pos else posr.shape[0]
    j0 = col0 // tn
    return pl.pallas_call(
        functools.partial(_inproj_kernel, with_pos=with_pos),
        grid=(m // tm, n // tn),
        in_specs=[
            pl.BlockSpec((tm, d), lambda i, j: (i, 0)),
            pl.BlockSpec((pos_rows, half), lambda i, j: ((i % tiles_per_seq) if with_pos else 0, 0)),
            pl.BlockSpec(cemb.shape, lambda i, j: (0, 0), pipeline_mode=ONE_BUFFER),
            pl.BlockSpec((1, d), lambda i, j: (0, 0)),
            pl.BlockSpec((None, 1, d), lambda i, j: (i // tiles_per_seq, 0, 0)),
            pl.BlockSpec((None, 1, d), lambda i, j: (i // tiles_per_seq, 0, 0)),
            pl.BlockSpec((d, tn), lambda i, j: (0, j0 + j)),
            pl.BlockSpec((ndt, d), lambda i, j: (0, 0), pipeline_mode=ONE_BUFFER),
        ],
        out_specs=[pl.BlockSpec((tm, tn), lambda i, j: (i, j)), pl.BlockSpec((tm, ndt), lambda i, j: (i, 0))],
        out_shape=[jax.ShapeDtypeStruct((m, n), BF16), jax.ShapeDtypeStruct((m, ndt), F32)],
        scratch_shapes=[pltpu.VMEM((tm, d), BF16)],
        compiler_params=_params(2, 58),
        name="in_proj",
    )(x2, posr, cemb, nw, sh, sc, w_bf, wdt_t)


CONV_ROWS = 256


def _shift_matrices(rows):
    return jnp.stack([jnp.eye(rows, k=-1, dtype=BF16), jnp.eye(rows, k=1, dtype=BF16)])


def _conv3_chunk(x_ref, r0, rows, seq_len, w, b, shift_ref):
    cur_b = x_ref[r0:r0 + rows, :]
    tc = cur_b.shape[1]
    up = jnp.dot(shift_ref[0], cur_b, preferred_element_type=F32)
    down = jnp.dot(shift_ref[1], cur_b, preferred_element_type=F32)
    rid = lax.broadcasted_iota(jnp.int32, (8, tc), 0)
    if r0 > 0:
        prev_row = x_ref[r0 - 16:r0, :].astype(F32)[15:16, :]
        up = jnp.concatenate([jnp.where(rid == 0, prev_row, up[0:8]), up[8:]], axis=0)
    if r0 + rows < seq_len:
        next_row = x_ref[r0 + rows:r0 + rows + 16, :].astype(F32)[0:1, :]
        down = jnp.concatenate([down[:rows - 8], jnp.where(rid == 7, next_row, down[rows - 8:])], axis=0)
    return up * w[0:1, :] + cur_b.astype(F32) * w[1:2, :] + down * w[2:3, :] + b


def _ssd_prep_kernel(x_ref, w_ref, b_ref, shift_ref, o_ref):
    seq_len = x_ref.shape[0]
    rows = shift_ref.shape[1]
    w = w_ref[...]
    b = b_ref[...]
    for r0 in range(0, seq_len, rows):
        v = _conv3_chunk(x_ref, r0, rows, seq_len, w, b, shift_ref)
        o_ref[r0:r0 + rows, :] = _silu(v).astype(o_ref.dtype)


def _ssd_prep(proj3, col0, width, w8, b1, tc):
    nb, seq_len, _ = proj3.shape
    rows = min(CONV_ROWS, seq_len)
    j0 = col0 // tc
    return pl.pallas_call(
        _ssd_prep_kernel,
        grid=(nb, width // tc),
        in_specs=[
            pl.BlockSpec((None, seq_len, tc), lambda b, j: (b, 0, j0 + j)),
            pl.BlockSpec((8, tc), lambda b, j: (0, j)),
            pl.BlockSpec((1, tc), lambda b, j: (0, j)),
            pl.BlockSpec((2, rows, rows), lambda b, j: (0, 0, 0)),
        ],
        out_specs=pl.BlockSpec((None, seq_len, tc), lambda b, j: (b, 0, j)),
        out_shape=jax.ShapeDtypeStruct((nb, seq_len, width), BF16),
        compiler_params=_params(2, 40),
        name="ssd_prep",
    )(proj3, w8, b1, _shift_matrices(rows))


def _hy_prep_kernel(x0_ref, x1_ref, v_ref, w0_ref, w1_ref, wv_ref, b0_ref, b1_ref, bv_ref, shift_ref,
                    x0c_ref, u_ref):
    seq_len = x0_ref.shape[0]
    rows = shift_ref.shape[1]
    w0, w1, wv = w0_ref[...], w1_ref[...], wv_ref[...]
    b0, b1, bv = b0_ref[...], b1_ref[...], bv_ref[...]
    for r0 in range(0, seq_len, rows):
        x0c_ref[r0:r0 + rows, :] = _conv3_chunk(x0_ref, r0, rows, seq_len, w0, b0, shift_ref).astype(x0c_ref.dtype)
        x1c = _conv3_chunk(x1_ref, r0, rows, seq_len, w1, b1, shift_ref)
        vc = _conv3_chunk(v_ref, r0, rows, seq_len, wv, bv, shift_ref)
        u_ref[r0:r0 + rows, :] = (vc * x1c).astype(u_ref.dtype)


def _hy_prep(proj3, col0, dh, w8, b1, tc):
    nb, seq_len, _ = proj3.shape
    rows = min(CONV_ROWS, seq_len)
    j0 = col0 // tc
    nj = dh // tc
    x_spec = lambda k: pl.BlockSpec((None, seq_len, tc), lambda b, j: (b, 0, j0 + k * nj + j))
    w_spec = lambda k: pl.BlockSpec((8, tc), lambda b, j: (0, k * nj + j))
    b_spec = lambda k: pl.BlockSpec((1, tc), lambda b, j: (0, k * nj + j))
    o_spec = pl.BlockSpec((None, seq_len, tc), lambda b, j: (b, 0, j))
    return pl.pallas_call(
        _hy_prep_kernel,
        grid=(nb, nj),
        in_specs=[x_spec(0), x_spec(1), x_spec(2), w_spec(0), w_spec(1), w_spec(2),
                  b_spec(0), b_spec(1), b_spec(2),
                  pl.BlockSpec((2, rows, rows), lambda b, j: (0, 0, 0))],
        out_specs=[o_spec, o_spec],
        out_shape=[jax.ShapeDtypeStruct((nb, seq_len, dh), BF16)] * 2,
        compiler_params=_params(2, 48),
        name="hy_prep",
    )(proj3, proj3, proj3, w8, w8, w8, b1, b1, b1, _shift_matrices(rows))


def _ssd_kernel(u_ref, dt_ref, uc_ref, dtc_ref, z_ref, alog_ref, dtb_ref, dsk_ref, nw_ref, e_ref,
                o_ref, h_scr, yf_scr, *, n_ctx, n_lat):
    q = SSD_CHUNK
    nsub = SSD_STEP_CHUNKS
    d = pl.program_id(0)
    s = pl.program_id(1)
    is_ctx = s < n_ctx
    t = jnp.maximum(s - n_ctx, 0)
    cidx = jnp.where(d == 0, t, n_lat - 1 - t)
    fwd = d == 0

    @pl.when(s == 0)
    def _():
        h_scr[...] = jnp.zeros_like(h_scr)

    row = lax.broadcasted_iota(jnp.int32, (q, q), 0)
    col = lax.broadcasted_iota(jnp.int32, (q, q), 1)
    tri = (jnp.where(fwd, row, col) >= jnp.where(fwd, col, row)).astype(F32)
    nb = u_ref.shape[0]

    def scan_block(b, xbc_ref, dtr_ref, need_y):
        sub = lambda v, k: v[k * q:(k + 1) * q]
        y_sub = [None] * nsub
        for k in range(nsub):
            rk = nsub - 1 - k
            u = jnp.where(fwd, sub(xbc_ref[b], k), sub(xbc_ref[b], rk))
            dtr = jnp.where(fwd, sub(dtr_ref[b], k), sub(dtr_ref[b], rk))
            y_sub[k] = _ssd_chunk(b, u, dtr, tri, alog_ref, dtb_ref, e_ref, h_scr, need_y)
        if need_y:
            return jnp.concatenate([jnp.where(fwd, y_sub[k], y_sub[nsub - 1 - k]) for k in range(nsub)], axis=0)

    @pl.when(is_ctx)
    def _():
        for b in range(nb):
            scan_block(b, uc_ref, dtc_ref, False)

    @pl.when(jnp.logical_and(jnp.logical_not(is_ctx), fwd))
    def _():
        for b in range(nb):
            yf_scr[b, cidx] = scan_block(b, u_ref, dt_ref, True)

    @pl.when(jnp.logical_and(jnp.logical_not(is_ctx), jnp.logical_not(fwd)))
    def _():
        d_ssd = o_ref.shape[2]
        for b in range(nb):
            y = scan_block(b, u_ref, dt_ref, True)
            xs = u_ref[b][:, :d_ssd].astype(F32)
            ytot = yf_scr[b, cidx] + y + dsk_ref[...] * xs
            gated = ytot * _silu(z_ref[b].astype(F32))
            o_ref[b] = (_rms(gated) * nw_ref[...]).astype(o_ref.dtype)


def _ssd_chunk(b, u, dtr, tri, alog_ref, dtb_ref, e_ref, h_scr, need_y):
    q = SSD_CHUNK
    d_ssd = h_scr.shape[2]
    gw = d_ssd // SSD_GROUPS

    dt = _softplus(dtr + dtb_ref[...])
    a = dt * (-jnp.exp(alog_ref[...]))
    a_hi = a.astype(BF16)
    r_hi = a - a_hi.astype(F32)
    a_mid = r_hi.astype(BF16)
    a_lo = (r_hi - a_mid.astype(F32)).astype(BF16)
    tri_b = tri.astype(BF16)
    cum = (jnp.dot(tri_b, a_hi, preferred_element_type=F32) + jnp.dot(tri_b, a_mid, preferred_element_type=F32)
           + jnp.dot(tri_b, a_lo, preferred_element_type=F32))
    mask_add = ((1.0 - tri) * _NEG_BIG).astype(BF16)
    cum_t = cum.T
    tot = jnp.sum(a, axis=0, keepdims=True)

    stack = jnp.concatenate(
        [dt, jnp.exp(cum), jnp.exp(tot - cum), jnp.broadcast_to(jnp.exp(tot), (8, LANES))], axis=0)
    ex = jnp.dot(stack.astype(BF16), e_ref[...], preferred_element_type=F32)
    dt_x = ex[0:q]
    ecum_x = ex[q:2 * q]
    edec_x = ex[2 * q:3 * q]
    etot_x = ex[3 * q:3 * q + 1]

    xs = u[:, :d_ssd].astype(F32)
    xdt = xs * dt_x
    xdt_b = xdt.astype(BF16)
    xdw_b = (xdt * edec_x).astype(BF16)
    lane_lo = lax.broadcasted_iota(jnp.int32, (q, LANES), 1) < SSD_HEAD_DIM
    heads_per_group = gw // SSD_HEAD_DIM

    y_parts = []
    for g in range(SSD_GROUPS):
        bg = u[:, d_ssd + g * SSD_STATE:d_ssd + (g + 1) * SSD_STATE]
        cg = u[:, d_ssd + (SSD_GROUPS + g) * SSD_STATE:d_ssd + (SSD_GROUPS + g + 1) * SSD_STATE]
        h_prev = h_scr[b, :, g * gw:(g + 1) * gw]
        if need_y:
            scores = lax.dot_general(cg, bg, _NT_DIMS, preferred_element_type=F32).astype(BF16)
            y_off = jnp.dot(cg, h_prev.astype(BF16), preferred_element_type=F32)
        for j in range(heads_per_group // 2 if need_y else 0):
            c0 = g * gw + j * LANES
            xp = xdt_b[:, c0:c0 + LANES]
            acc = None
            for hh in range(2):
                h = g * heads_per_group + 2 * j + hh
                diff = (cum[:, h:h + 1] - cum_t[h:h + 1, :]).astype(BF16)
                m_h = scores * jnp.exp(diff + mask_add)
                x_h = jnp.where(lane_lo if hh == 0 else jnp.logical_not(lane_lo), xp, jnp.zeros_like(xp))
                part = jnp.dot(m_h, x_h, preferred_element_type=F32)
                acc = part if acc is None else acc + part
            y_parts.append(acc + y_off[:, j * LANES:(j + 1) * LANES] * ecum_x[:, c0:c0 + LANES])
        upd = lax.dot_general(bg, xdw_b[:, g * gw:(g + 1) * gw], (((0,), (0,)), ((), ())),
                              preferred_element_type=F32)
        h_scr[b, :, g * gw:(g + 1) * gw] = etot_x[:, g * gw:(g + 1) * gw] * h_prev + upd
    return jnp.concatenate(y_parts, axis=1) if need_y else None


def _ssd(u_lat, dt_lat, u_ctx, dt_ctx, proj3, alog, dtb, dsk, nw, expand):
    nb, seq_len, d_xbc = u_lat.shape
    ctx_len = u_ctx.shape[1]
    q = SSD_CHUNK * SSD_STEP_CHUNKS
    assert seq_len % q == 0 and ctx_len % q == 0
    n_lat = seq_len // q
    n_ctx = ctx_len // q
    d_ssd = dsk.shape[1]

    def lat_idx(d, s):
        t = jnp.maximum(s - n_ctx, 0)
        return jnp.where(d == 0, t, n_lat - 1 - t)

    def ctx_idx(d, s):
        t = jnp.minimum(s, n_ctx - 1)
        return jnp.where(d == 0, t, n_ctx - 1 - t)

    def out_idx(d, s):
        return jnp.where(d == 0, n_lat - 1, lat_idx(d, s))

    return pl.pallas_call(
        functools.partial(_ssd_kernel, n_ctx=n_ctx, n_lat=n_lat),
        grid=(2, n_ctx + n_lat),
        in_specs=[
            pl.BlockSpec((nb, q, d_xbc), lambda d, s: (0, lat_idx(d, s), 0)),
            pl.BlockSpec((nb, q, LANES), lambda d, s: (0, lat_idx(d, s), d)),
            pl.BlockSpec((nb, q, d_xbc), lambda d, s: (0, ctx_idx(d, s), 0)),
            pl.BlockSpec((nb, q, LANES), lambda d, s: (0, ctx_idx(d, s), d)),
            pl.BlockSpec((nb, q, d_ssd), lambda d, s: (0, lat_idx(d, s), 0)),
            pl.BlockSpec((None, 1, LANES), lambda d, s: (d, 0, 0)),
            pl.BlockSpec((None, 1, LANES), lambda d, s: (d, 0, 0)),
            pl.BlockSpec((1, d_ssd), lambda d, s: (0, 0)),
            pl.BlockSpec((1, d_ssd), lambda d, s: (0, 0)),
            pl.BlockSpec((LANES, d_ssd), lambda d, s: (0, 0)),
        ],
        out_specs=pl.BlockSpec((nb, q, d_ssd), lambda d, s: (0, out_idx(d, s), 0)),
        out_shape=jax.ShapeDtypeStruct((nb, seq_len, d_ssd), BF16),
        scratch_shapes=[
            pltpu.VMEM((nb, SSD_STATE, d_ssd), F32),
            pltpu.VMEM((nb, n_lat, q, d_ssd), F32),
        ],
        compiler_params=_params(2, 56),
        name="ssd",
    )(u_lat, dt_lat, u_ctx, dt_ctx, proj3, alog, dtb, dsk, nw, expand)


def _hy_mlp_kernel(zt_ref, w1_ref, w2_ref, w3_ref, b_ref, fr_ref, o_ref):
    fr = fr_ref[...]
    b = b_ref[...]
    h = jnp.sin(fr * (jnp.dot(w1_ref[...], zt_ref[...], precision=HIGHEST, preferred_element_type=F32)
                      + b[:, 0:1]))
    h = jnp.sin(fr * (jnp.dot(w2_ref[...], h, precision=HIGHEST, preferred_element_type=F32) + b[:, 1:2]))
    h = jnp.sin(fr * (jnp.dot(w3_ref[...], h, precision=HIGHEST, preferred_element_type=F32) + b[:, 2:3]))
    hid, seq_len = h.shape
    hp = jnp.concatenate([h, jnp.zeros((LANES - hid, seq_len), F32)], axis=0)
    o_ref[...] = hp.T


def _hy_mlp(zt, w1t, w2t, w3t, b3, fr):
    seq_len = zt.shape[1]
    return pl.pallas_call(
        _hy_mlp_kernel,
        out_shape=jax.ShapeDtypeStruct((seq_len, LANES), F32),
        compiler_params=pltpu.CompilerParams(vmem_limit_bytes=40 * MIB),
        name="hy_mlp",
    )(zt, w1t, w2t, w3t, b3, fr)


def _fft_tables(seq_len):
    n_fft = 2 * seq_len
    n2 = FFT_N2
    n1 = n_fft // n2
    n1h = n1 // 2
    k1n = n1h + 1
    k1p = -(-k1n // 4) * 4
    k1 = np.arange(k1n, dtype=np.int64)
    th = (2.0 * np.pi / n1) * ((k1[:, None] * np.arange(n1h, dtype=np.int64)[None, :]) % n1)
    f1 = np.zeros((2 * k1p, n1h))
    f1[0:2 * k1n:2] = np.cos(th)
    f1[1:2 * k1n:2] = -np.sin(th)
    idx = np.arange(n2, dtype=np.int64)
    kk = k1[:, None, None] + n1 * idx[None, :, None]
    ph = (2.0 * np.pi / n_fft) * ((kk * idx[None, None, :]) % n_fft)
    g_re, g_im = np.cos(ph), -np.sin(ph)

    def blocks(re, im):
        out = np.zeros((k1p, 2 * n2, 2 * n2))
        out[:k1n, :n2, :n2] = re
        out[:k1n, :n2, n2:] = -im
        out[:k1n, n2:, :n2] = im
        out[:k1n, n2:, n2:] = re
        return out

    wgt = np.where((k1 == 0) | (k1 == n1h), 1.0, 2.0) / n_fft
    gf = blocks(g_re, g_im)
    gi = blocks(np.transpose(g_re, (0, 2, 1)) * wgt[:, None, None],
                -np.transpose(g_im, (0, 2, 1)) * wgt[:, None, None])
    as_bf16 = lambda t: jnp.asarray(t.astype(np.float32)).astype(BF16)
    return as_bf16(f1), as_bf16(np.ascontiguousarray(f1.T)), as_bf16(gf), as_bf16(gi)


_BATCHED = (((2,), (1,)), ((0,), (0,)))


def _fft_forward(x, f1, gf):
    seq_len, c = x.shape
    rows, n1h = f1.shape
    xt = jnp.swapaxes(x.astype(BF16).reshape(n1h, FFT_N2, c), 0, 1)
    a = lax.dot_general(jnp.broadcast_to(f1[None], (FFT_N2, rows, n1h)), xt, _BATCHED,
                        preferred_element_type=F32)
    at = jnp.swapaxes(a.astype(BF16), 0, 1).reshape(rows // 2, 2 * FFT_N2, c)
    return lax.dot_general(gf, at, _BATCHED, preferred_element_type=F32)


def _fft_inverse(y, f1t, gi):
    c = y.shape[2]
    n1h, rows = f1t.shape
    bt = lax.dot_general(gi, y, _BATCHED, preferred_element_type=F32)
    btt = jnp.swapaxes(bt.astype(BF16).reshape(rows, FFT_N2, c), 0, 1)
    yv = lax.dot_general(jnp.broadcast_to(f1t[None], (FFT_N2, n1h, rows)), btt, _BATCHED,
                         preferred_element_type=F32)
    return jnp.swapaxes(yv, 0, 1).reshape(n1h * FFT_N2, c)


def _hy_spec_kernel(h_ref, wf_ref, wb_ref, dl_ref, f1_ref, gf_ref, k_ref):
    n2 = FFT_N2
    seq_len = h_ref.shape[0]
    tc = wf_ref.shape[1]
    h3 = h_ref[...].astype(BF16)
    rid = lax.broadcasted_iota(jnp.int32, (seq_len, tc), 0)
    decay = jnp.exp(-(rid.astype(F32) * (1.0 / (seq_len - 1))) * dl_ref[...])
    hf = jnp.dot(h3, wf_ref[...].astype(BF16), preferred_element_type=F32) * decay
    hb = jnp.dot(h3, wb_ref[...].astype(BF16), preferred_element_type=F32) * decay
    norm = jnp.sum(jnp.abs(hf) + jnp.abs(hb), axis=0, keepdims=True) + 1e-6
    inv = 1.0 / norm
    hb = jnp.where(rid == 0, 0.0, hb * inv)
    x = _fft_forward(jnp.concatenate([hf * inv, hb], axis=1), f1_ref[...], gf_ref[...])
    xf, xb = x[:, :, 0:tc], x[:, :, tc:2 * tc]
    k_ref[...] = jnp.concatenate([xf[:, :n2] + xb[:, :n2], xf[:, n2:] - xb[:, n2:]], axis=1)


def _hy_spec(h3, w4f, w4b, deltas, f1, gf, tc):
    seq_len = h3.shape[0]
    dh = w4f.shape[1]
    k1p = gf.shape[0]
    rows, n1h = f1.shape
    return pl.pallas_call(
        _hy_spec_kernel,
        grid=(dh // tc,),
        in_specs=[
            pl.BlockSpec((seq_len, LANES), lambda j: (0, 0)),
            pl.BlockSpec((LANES, tc), lambda j: (0, j)),
            pl.BlockSpec((LANES, tc), lambda j: (0, j)),
            pl.BlockSpec((1, tc), lambda j: (0, j)),
            pl.BlockSpec((rows, n1h), lambda j: (0, 0)),
            pl.BlockSpec((k1p, 2 * FFT_N2, 2 * FFT_N2), lambda j: (0, 0, 0)),
        ],
        out_specs=pl.BlockSpec((k1p, 2 * FFT_N2, tc), lambda j: (0, 0, j)),
        out_shape=jax.ShapeDtypeStruct((k1p, 2 * FFT_N2, dh), F32),
        compiler_params=_params(1, 56),
        name="hy_spec",
    )(h3, w4f, w4b, deltas, f1, gf)


def _hy_conv_kernel(u_ref, x0_ref, k_ref, f1_ref, f1t_ref, gf_ref, gi_ref, bias_ref, o_ref):
    n2 = FFT_N2
    x = _fft_forward(u_ref[...], f1_ref[...], gf_ref[...])
    u = u_ref[...].astype(F32)
    kk = k_ref[...]
    xr, xi, kr, ki = x[:, :n2], x[:, n2:], kk[:, :n2], kk[:, n2:]
    y = jnp.concatenate([xr * kr - xi * ki, xr * ki + xi * kr], axis=1).astype(BF16)
    conv = _fft_inverse(y, f1t_ref[...], gi_ref[...])
    o_ref[...] = (x0_ref[...].astype(F32) * (conv + u * bias_ref[...])).astype(o_ref.dtype)


def _hy_conv(u3, x0c, kspec, f1, f1t, gf, gi, bias, tc):
    nb, seq_len, dh = u3.shape
    k1p = gf.shape[0]
    rows, n1h = f1.shape
    blk = pl.BlockSpec((None, seq_len, tc), lambda j, b: (b, 0, j))
    const = lambda shape: pl.BlockSpec(shape, lambda j, b: (0,) * len(shape), pipeline_mode=ONE_BUFFER)
    return pl.pallas_call(
        _hy_conv_kernel,
        grid=(dh // tc, nb),
        in_specs=[
            blk, blk,
            pl.BlockSpec((k1p, 2 * FFT_N2, tc), lambda j, b: (0, 0, j)),
            const((rows, n1h)), const((n1h, rows)),
            const((k1p, 2 * FFT_N2, 2 * FFT_N2)), const((k1p, 2 * FFT_N2, 2 * FFT_N2)),
            pl.BlockSpec((1, tc), lambda j, b: (0, j)),
        ],
        out_specs=blk,
        out_shape=jax.ShapeDtypeStruct((nb, seq_len, dh), BF16),
        compiler_params=_params(2, 56),
        name="hy_conv",
    )(u3, x0c, kspec, f1, f1t, gf, gi, bias)


def _outproj_kernel(ys_ref, yh_ref, w_ref, x_ref, posr_ref, cemb_ref, nwp_ref, g_ref, nwf_ref, sh_ref, sc_ref,
                    xo_ref, h_ref, w_scr):
    tm, ds = ys_ref.shape
    rows = min(ROW_CHUNK, tm)

    @pl.when(pl.program_id(0) == 0)
    def _():
        for k0 in range(0, w_ref.shape[0], ROW_CHUNK):
            w_scr[k0:k0 + ROW_CHUNK, :] = w_ref[k0:k0 + ROW_CHUNK, :].astype(BF16)

    for r0 in range(0, tm, rows):
        rs = slice(r0, r0 + rows)
        y = jnp.dot(ys_ref[rs, :], w_scr[0:ds, :], preferred_element_type=F32)
        y = y + jnp.dot(yh_ref[rs, :], w_scr[ds:, :], preferred_element_type=F32)
        xn = _add_pos(x_ref[rs, :], posr_ref, r0 // GRID_W, cemb_ref[...]) + g_ref[...] * (_rms(y) * nwp_ref[...])
        xo_ref[rs, :] = xn
        h_ref[rs, :] = (_rms(xn) * nwf_ref[...] * (1.0 + sc_ref[...]) + sh_ref[...]).astype(h_ref.dtype)


def _out_proj(ys, yh, w, x2, posr, cemb, nwp, g1, nwf, sh2, sc2, seq_len, tm):
    m, d = x2.shape
    ds = ys.shape[1]
    dh = yh.shape[1]
    tiles_per_seq = seq_len // tm
    row = lambda i: (i, 0)
    fixed = lambda i: (0, 0)
    per_batch = lambda i: (i // tiles_per_seq, 0, 0)
    return pl.pallas_call(
        _outproj_kernel,
        grid=(m // tm,),
        in_specs=[
            pl.BlockSpec((tm, ds), row),
            pl.BlockSpec((tm, dh), row),
            pl.BlockSpec((ds + dh, d), fixed, pipeline_mode=ONE_BUFFER),
            pl.BlockSpec((tm, d), row),
            pl.BlockSpec((tm // GRID_W, d // 2), lambda i: (i % tiles_per_seq, 0)),
            pl.BlockSpec(cemb.shape, fixed),
            pl.BlockSpec((1, d), fixed),
            pl.BlockSpec((None, 1, d), per_batch),
            pl.BlockSpec((1, d), fixed),
            pl.BlockSpec((None, 1, d), per_batch),
            pl.BlockSpec((None, 1, d), per_batch),
        ],
        out_specs=[pl.BlockSpec((tm, d), row), pl.BlockSpec((tm, d), row)],
        out_shape=[jax.ShapeDtypeStruct((m, d), F32), jax.ShapeDtypeStruct((m, d), BF16)],
        scratch_shapes=[pltpu.VMEM((ds + dh, d), BF16)],
        compiler_params=_params(1, 58),
        name="out_proj",
    )(ys, yh, w, x2, posr, cemb, nwp, g1, nwf, sh2, sc2)


def _ffn_step(h_ref, wg, wu, wd, x_ref, nw_ref, g_ref, o_ref):
    f = pl.program_id(1)
    h = h_ref[...]
    gate = jnp.dot(h, wg, preferred_element_type=F32)
    up = jnp.dot(h, wu, preferred_element_type=F32)
    act = (_silu(gate) * up).astype(BF16)

    @pl.when(f == 0)
    def _():
        o_ref[...] = jnp.zeros_like(o_ref)

    d = o_ref.shape[1]
    cw = min(FFN_COL_CHUNK, d)
    for n0 in range(0, d, cw):
        o_ref[:, n0:n0 + cw] += jnp.dot(act, wd[:, n0:n0 + cw], preferred_element_type=F32)

    @pl.when(f == pl.num_programs(1) - 1)
    def _():
        tm = o_ref.shape[0]
        rows = min(ROW_CHUNK, tm)
        for r0 in range(0, tm, rows):
            y = o_ref[r0:r0 + rows, :]
            o_ref[r0:r0 + rows, :] = x_ref[r0:r0 + rows, :] + g_ref[...] * (_rms(y) * nw_ref[...])


def _ffn_first_kernel(h_ref, wg_ref, wu_ref, wd_ref, x_ref, nw_ref, g_ref, o_ref, wgb_ref, wub_ref, wdb_ref):
    wg = wg_ref[...].astype(BF16)
    wu = wu_ref[...].astype(BF16)
    wd = wd_ref[...].astype(BF16)
    wgb_ref[...] = wg
    wub_ref[...] = wu
    wdb_ref[...] = wd
    _ffn_step(h_ref, wg, wu, wd, x_ref, nw_ref, g_ref, o_ref)


def _ffn_rest_kernel(h_ref, wg_ref, wu_ref, wd_ref, x_ref, nw_ref, g_ref, o_ref):
    _ffn_step(h_ref, wg_ref[...], wu_ref[...], wd_ref[...], x_ref, nw_ref, g_ref, o_ref)


def _ffn(h2, wg, wu, wd, xn, nw, g2, seq_len, tm, tf_first, tf):
    m, d = xn.shape
    dff = wg.shape[1]
    tiles_per_seq = seq_len // tm
    n_rows = m // tm
    out, wgb, wub, wdb = pl.pallas_call(
        _ffn_first_kernel,
        grid=(1, dff // tf_first),
        in_specs=[
            pl.BlockSpec((tm, d), lambda i, f: (0, 0), pipeline_mode=ONE_BUFFER),
            pl.BlockSpec((d, tf_first), lambda i, f: (0, f)),
            pl.BlockSpec((d, tf_first), lambda i, f: (0, f)),
            pl.BlockSpec((tf_first, d), lambda i, f: (f, 0)),
            pl.BlockSpec((tm, d), lambda i, f: (0, 0), pipeline_mode=ONE_BUFFER),
            pl.BlockSpec((1, d), lambda i, f: (0, 0)),
            pl.BlockSpec((None, 1, d), lambda i, f: (0, 0, 0)),
        ],
        out_specs=[
            pl.BlockSpec((tm, d), lambda i, f: (0, 0)),
            pl.BlockSpec((d, tf_first), lambda i, f: (0, f)),
            pl.BlockSpec((d, tf_first), lambda i, f: (0, f)),
            pl.BlockSpec((tf_first, d), lambda i, f: (f, 0)),
        ],
        out_shape=[
            jax.ShapeDtypeStruct((m, d), F32),
            jax.ShapeDtypeStruct((d, dff), BF16),
            jax.ShapeDtypeStruct((d, dff), BF16),
            jax.ShapeDtypeStruct((dff, d), BF16),
        ],
        input_output_aliases={4: 0},
        compiler_params=_params(2, 58),
        name="ffn_first",
    )(h2, wg, wu, wd, xn, nw, g2)
    if n_rows == 1:
        return out
    return pl.pallas_call(
        _ffn_rest_kernel,
        grid=(n_rows - 1, dff // tf),
        in_specs=[
            pl.BlockSpec((tm, d), lambda i, f: (i + 1, 0)),
            pl.BlockSpec((d, tf), lambda i, f: (0, f)),
            pl.BlockSpec((d, tf), lambda i, f: (0, f)),
            pl.BlockSpec((tf, d), lambda i, f: (f, 0)),
            pl.BlockSpec((tm, d), lambda i, f: (i + 1, 0), pipeline_mode=ONE_BUFFER),
            pl.BlockSpec((1, d), lambda i, f: (0, 0)),
            pl.BlockSpec((None, 1, d), lambda i, f: ((i + 1) // tiles_per_seq, 0, 0)),
        ],
        out_specs=pl.BlockSpec((tm, d), lambda i, f: (i + 1, 0)),
        out_shape=jax.ShapeDtypeStruct((m, d), F32),
        input_output_aliases={4: 0},
        compiler_params=_params(2, 58),
        name="ffn",
    )(h2, wgb, wub, wdb, out, nw, g2)


def _sincos_tables(rows, cols, dim):
    qd = dim // 4
    omega = 1.0 / (POS_THETA ** (jnp.arange(qd, dtype=F32) / qd))
    r = jnp.arange(rows, dtype=F32)[:, None] * omega
    cc = jnp.arange(cols, dtype=F32)[:, None] * omega
    r_emb = jnp.concatenate([jnp.sin(r), jnp.cos(r)], -1)
    c_emb = jnp.concatenate([jnp.sin(cc), jnp.cos(cc)], -1)
    return r_emb, c_emb


def _filter_features_t(seq_len, n_bands):
    t = jnp.linspace(0.0, 1.0, seq_len, dtype=F32)[:, None]
    w = 2.0 * math.pi * jnp.arange(seq_len, dtype=F32)[:, None] / seq_len
    fb = jnp.linspace(1e-4, n_bands - 1, n_bands, dtype=F32)[None]
    zpos = jnp.concatenate([t, jnp.cos(fb * w), -jnp.sin(fb * w)], -1)
    emb = zpos.shape[1]
    return jnp.pad(zpos, ((0, 0), (0, LANES - emb))).T


def _pad_rows(a, rows):
    return jnp.pad(a, ((0, rows - a.shape[0]), (0, 0)))


def kernel(x, c, ctx, c_ctx, w_ada, b_ada, norm_mix_pre, norm_mix_post, norm_ffn_pre, norm_ffn_post,
           w_in, ssd_conv_w, ssd_conv_b, ssd_a_log, ssd_dt_bias, ssd_d, ssd_norm,
           hy_conv_w, hy_conv_b, hy_w1, hy_b1, hy_w2, hy_b2, hy_w3, hy_b3, hy_w4, hy_freq, hy_bias,
           w_out, w_gate, w_up, w_down):
    nb, seq_len, d = x.shape
    ctx_len = ctx.shape[1]
    assert w_ada.shape[0] == 1, "single layer"
    n_heads = ssd_d.shape[1]
    d_ssd = n_heads * SSD_HEAD_DIM
    d_xbc = d_ssd + 2 * SSD_GROUPS * SSD_STATE
    dh = hy_bias.shape[1]
    assert w_in.shape[2] == d_ssd + d_xbc + 2 * n_heads + 3 * dh
    assert n_heads <= LANES and nb + 1 <= 8
    assert seq_len % (GRID_W * 8) == 0 and seq_len % FFT_N2 == 0
    m = nb * seq_len

    crows = jnp.concatenate([c, c_ctx[None, :]], axis=0)
    mod = _ada(crows, w_ada[0], b_ada[0])
    part = lambda r0, r1, k: mod[r0:r1, k * d:(k + 1) * d][:, None, :]
    sh1, sc1, g1, sh2, sc2, g2 = (part(0, nb, k) for k in range(6))
    csh1 = jnp.broadcast_to(part(nb, nb + 1, 0), (nb, 1, d))
    csc1 = jnp.broadcast_to(part(nb, nb + 1, 1), (nb, 1, d))

    w_t = jnp.transpose(w_in[0])
    o_xbc = d_ssd
    o_dt = o_xbc + d_xbc
    o_hy = o_dt + 2 * n_heads
    tn_in = TN_W_IN
    assert o_dt % tn_in == 0 and (3 * dh) % tn_in == 0 and o_xbc % tn_in == 0 and o_hy % 16 == 0
    n_left = o_dt // tn_in
    n_main = o_dt + 3 * dh
    main_off = lambda j: jnp.where(j < n_left, j * tn_in, o_hy + (j - n_left) * tn_in)
    w_main = _w_in_cast(w_t, main_off, n_main // tn_in, tn_in)
    pad_dt = lambda rows: jnp.pad(rows, ((0, LANES - n_heads), (0, 0)))
    w_dt_t = jnp.concatenate([pad_dt(w_t[o_dt:o_dt + n_heads]), pad_dt(w_t[o_dt + n_heads:o_hy])],
                             axis=0)

    r_emb, c_emb = _sincos_tables(seq_len // GRID_W, GRID_W, d)
    posr = r_emb
    nmp = norm_mix_pre[0][None, :]

    tm_in = min(TM_IN, seq_len)
    tn_main = n_main // 4 if n_main % (4 * LANES) == 0 else tn_in
    proj, dt_lat = _in_proj(x.reshape(m, d), posr, c_emb, nmp, sh1, sc1, w_main, 0, n_main, w_dt_t, seq_len,
                            tm_in, tn_main, True)
    tm_ctx = min(TM_CTX, ctx_len)
    xbc_ctx, dt_ctx = _in_proj(ctx.reshape(nb * ctx_len, d), jnp.zeros((8, d // 2), F32), c_emb, nmp, csh1, csc1,
                               w_main, o_xbc, d_xbc, w_dt_t, ctx_len, tm_ctx, tn_in, False)
    proj3 = proj.reshape(nb, seq_len, -1)

    cw8 = _pad_rows(ssd_conv_w[0], 8)
    cb1 = ssd_conv_b[0][None, :]
    u_lat = _ssd_prep(proj3, d_ssd, d_xbc, cw8, cb1, TC_SSD_PREP)
    u_ctx = _ssd_prep(xbc_ctx.reshape(nb, ctx_len, d_xbc), 0, d_xbc, cw8, cb1, TC_SSD_PREP)
    pad_heads = lambda a: jnp.pad(a, ((0, 0), (0, LANES - n_heads)))[:, None, :]
    expand = (jnp.arange(LANES)[:, None] == (jnp.arange(d_ssd)[None, :] // SSD_HEAD_DIM)).astype(BF16)
    y_ssd = _ssd(u_lat, dt_lat.reshape(nb, seq_len, 2 * LANES), u_ctx, dt_ctx.reshape(nb, ctx_len, 2 * LANES),
                 proj3, pad_heads(ssd_a_log[0]), pad_heads(ssd_dt_bias[0]),
                 jnp.repeat(ssd_d[0], SSD_HEAD_DIM)[None, :], ssd_norm[0][None, :], expand)

    x0c, u_hy = _hy_prep(proj3, d_ssd + d_xbc, dh, _pad_rows(hy_conv_w[0], 8), hy_conv_b[0][None, :], TC_HY_PREP)
    n_bands = (hy_w1.shape[1] - 1) // 2
    zt = _filter_features_t(seq_len, n_bands)
    w1t = jnp.pad(hy_w1[0].T, ((0, 0), (0, LANES - hy_w1.shape[1])))
    b3 = jnp.stack([hy_b1[0], hy_b2[0], hy_b3[0]], axis=1)
    h3 = _hy_mlp(zt, w1t, hy_w2[0].T, hy_w3[0].T, b3, hy_freq[0][:, None])
    w4 = _pad_rows(hy_w4[0], LANES)
    max_decay = math.log(HY_TARGET) / HY_FAST_PCT
    min_decay = math.log(HY_TARGET) / HY_SLOW_PCT
    deltas = jnp.abs(jnp.linspace(min_decay, max_decay, dh, dtype=F32))[None, :]
    f1, f1t, gf, gi = _fft_tables(seq_len)
    kspec = _hy_spec(h3, w4[:, :dh], w4[:, dh:], deltas, f1, gf, TC_FFT)
    y_hy = _hy_conv(u_hy, x0c, kspec, f1, f1t, gf, gi, hy_bias[0][None, :], TC_FFT)

    xn, h2 = _out_proj(y_ssd.reshape(m, d_ssd), y_hy.reshape(m, dh), w_out[0], x.reshape(m, d),
                       posr, c_emb, norm_mix_post[0][None, :], g1, norm_ffn_pre[0][None, :], sh2, sc2, seq_len,
                       min(TM_OUT, seq_len))
    out = _ffn(h2, w_gate[0], w_up[0], w_down[0], xn, norm_ffn_post[0][None, :], g2, seq_len,
               min(TM_FFN, seq_len), TF_FFN_FIRST, TF_FFN)
    return out.reshape(nb, seq_len, d)
```
